```python
import jax
import jax.numpy as jnp
from jax import lax
import numpy as np

D_MODEL = 1024
BATCH = 16
SEQ = 256
DEPTH = 1
DEC_BATCH = 4
DEC_SEQ = 4096
PAST_LEN = 256

GRID_W = 64
D_FOURIER = 512
N_FOURIER_GROUPS = 8
FOURIER_GROUP = D_FOURIER // N_FOURIER_GROUPS
D_RWKV = D_MODEL - D_FOURIER
HEAD_SIZE = 64
N_RWKV_HEADS = D_RWKV // HEAD_SIZE
DECAY_LORA = 64
ICLR_LORA = 64
GATE_LORA = 128
D_RWKV_IN = 3 * D_RWKV + 2 * DECAY_LORA + 2 * ICLR_LORA + GATE_LORA
D_IN = D_FOURIER + D_RWKV_IN
D_FF = 2816
N_MOD = 9
RMS_EPS = 1e-6
GN_EPS = 64e-5

kernel_name = 'hybrid_fnet_rwkv7_diffusion_step'


def _rmsnorm(x, g):
    xf = x.astype(jnp.float32)
    y = xf * lax.rsqrt(jnp.mean(xf * xf, axis=-1, keepdims=True) + RMS_EPS)
    return (y * g.astype(jnp.float32)).astype(x.dtype)


def _modulate(h, shift, scale):
    return h * (1 + scale[:, None, :]) + shift[:, None, :]


def _swiglu(h, w_gate, w_up, w_down):
    return (jax.nn.silu(h @ w_gate) * (h @ w_up)) @ w_down


def _shift(p, axis, offset):
    n = p.shape[axis]
    pad = [(0, 0)] * p.ndim
    if offset > 0:
        body = lax.slice_in_dim(p, 0, n - offset, axis=axis)
        pad[axis] = (offset, 0)
    else:
        body = lax.slice_in_dim(p, -offset, n, axis=axis)
        pad[axis] = (0, -offset)
    return jnp.pad(body, pad)


def _token_shift(z, grid):
    B, S, C = z.shape
    q = z.reshape(B, S, C // 4, 4)
    if grid:
        rows = S // GRID_W
        g = q.reshape(B, rows, GRID_W, C // 4, 4)
        parts = [_shift(g[..., 0], 2, 1), _shift(g[..., 1], 2, -1),
                 _shift(g[..., 2], 1, 1), _shift(g[..., 3], 1, -1)]
    else:
        parts = [_shift(q[..., 0], 1, 1), _shift(q[..., 1], 1, -1),
                 _shift(q[..., 2], 1, 1), _shift(q[..., 3], 1, -1)]
    return jnp.stack(parts, axis=-1).reshape(B, S, C)


def _wkv7_scan(r, decay, k, v, kk, a, s0, reverse):
    def step(state, inp):
        r_t, w_t, k_t, v_t, kk_t, a_t = inp
        sa = jnp.einsum('bhvk,bhk->bhv', state, -kk_t)
        state = (state * w_t[:, :, None, :]
                 + sa[..., None] * (kk_t * a_t)[:, :, None, :]
                 + v_t[..., None] * k_t[:, :, None, :])
        return state, jnp.einsum('bhvk,bhk->bhv', state, r_t)
    xs = tuple(jnp.moveaxis(t, 1, 0) for t in (r, decay, k, v, kk, a))
    s_final, ys = lax.scan(step, s0, xs, reverse=reverse)
    return jnp.moveaxis(ys, 0, 1), s_final


def _mixer(h, s0_f, s0_b, grid, w_in, mu, w0, w2, a0, a2, g2, k_k, k_a, r_k, ln_w, ln_b, w_out):
    B, S, _ = h.shape
    f32 = jnp.float32
    proj = h @ w_in
    u = proj[..., :D_FOURIER]
    z = proj[..., D_FOURIER:]
    z = (z + mu * (_token_shift(z, grid) - z)).astype(f32)

    uf = u.astype(f32).reshape(B, S, N_FOURIER_GROUPS, FOURIER_GROUP)
    y_four = jnp.fft.fft2(uf, axes=(1, 3), norm='ortho').real.reshape(B, S, D_FOURIER)

    o1, o2, o3 = D_RWKV, 2 * D_RWKV, 3 * D_RWKV
    o4 = o3 + 2 * DECAY_LORA
    o5 = o4 + 2 * ICLR_LORA
    r, k, v = z[..., :o1], z[..., o1:o2], z[..., o2:o3]
    wd = z[..., o3:o4].reshape(B, S, 2, DECAY_LORA)
    ad = z[..., o4:o5].reshape(B, S, 2, ICLR_LORA)
    gd = z[..., o5:]

    def heads(t):
        return t.reshape(B, S, N_RWKV_HEADS, HEAD_SIZE)

    kk = heads(k * k_k.astype(f32))
    kk = kk / jnp.maximum(jnp.linalg.norm(kk, axis=-1, keepdims=True), 1e-12)
    r_h, v_h = heads(r), heads(v)
    r_k_h = r_k.astype(f32).reshape(N_RWKV_HEADS, HEAD_SIZE)

    ys, bonuses, finals = [], [], []
    for d, s0 in enumerate((s0_f, s0_b)):
        w_raw = w0[d] + jnp.tanh(wd[:, :, d]) @ w2[d]
        decay = jnp.exp(-jnp.exp(-jax.nn.softplus(-w_raw) - 0.5))
        a = jax.nn.sigmoid(a0[d] + ad[:, :, d] @ a2[d])
        k_d = heads(k * (1 + (a - 1) * k_a))
        y_d, s_d = _wkv7_scan(r_h, heads(decay), k_d, v_h, kk, heads(a),
                              s0.astype(f32), reverse=(d == 1))
        ys.append(y_d)
        bonuses.append(jnp.sum(r_h * k_d * r_k_h, axis=-1, keepdims=True) * v_h)
        finals.append(s_d)
    y_sum = ys[0] + ys[1]
    mean = jnp.mean(y_sum, axis=-1, keepdims=True)
    var = jnp.mean(jnp.square(y_sum - mean), axis=-1, keepdims=True)
    y_gn = ((y_sum - mean) * lax.rsqrt(var + GN_EPS)).reshape(B, S, D_RWKV) * ln_w + ln_b
    gate = jax.nn.sigmoid(gd) @ g2
    y_rwkv = (y_gn + (bonuses[0] + bonuses[1]).reshape(B, S, D_RWKV)) * gate

    mixed = jnp.concatenate([y_four, y_rwkv], axis=-1).astype(h.dtype)
    return mixed @ w_out, finals[0], finals[1]


def _layer(x, mod, s0_f, s0_b, grid, norm_g, ffn_gate, ffn_up, ffn_down, mixer_params):
    sh1, sc1, g1, sh2, sc2, g2, sh3, sc3, g3 = jnp.split(mod, N_MOD, axis=-1)
    h = _modulate(_rmsnorm(x, norm_g[0]), sh1, sc1)
    x = x + 0.5 * g1[:, None, :] * _swiglu(h, ffn_gate[0], ffn_up[0], ffn_down[0])
    h = _modulate(_rmsnorm(x, norm_g[1]), sh2, sc2)
    mixed, s_f, s_b = _mixer(h, s0_f, s0_b, grid, *mixer_params)
    x = x + g2[:, None, :] * mixed
    h = _modulate(_rmsnorm(x, norm_g[2]), sh3, sc3)
    x = x + 0.5 * g3[:, None, :] * _swiglu(h, ffn_gate[1], ffn_up[1], ffn_down[1])
    return x, s_f, s_b


def setup_inputs(seed: int = 0) -> dict:
    key = jax.random.key(seed)
    ks = jax.random.split(key, 32)
    f32 = jnp.float32

    def nrm(k, shape, s):
        return jax.random.normal(k, shape, f32) * s

    def unif(k, shape, lo, hi):
        return jax.random.uniform(k, shape, f32, lo, hi)

    L = DEPTH
    st_shape = (DEC_BATCH, DEPTH, N_RWKV_HEADS, HEAD_SIZE, HEAD_SIZE)
    return {
        'x_prompt': nrm(ks[0], (BATCH, SEQ, D_MODEL), 1.0),
        'x_sample': nrm(ks[1], (DEC_BATCH, DEC_SEQ, D_MODEL), 1.0),
        'state_fwd': nrm(ks[2], st_shape, 0.5),
        'state_bwd': nrm(ks[3], st_shape, 0.5),
        'c': nrm(ks[4], (DEC_BATCH, D_MODEL), 1.0),
        'c_ctx': nrm(ks[5], (D_MODEL,), 1.0),
        'w_mod': nrm(ks[6], (L, D_MODEL, N_MOD * D_MODEL), 0.5 * D_MODEL ** -0.5),
        'b_mod': nrm(ks[7], (L, N_MOD * D_MODEL), 0.02),
        'norm_g': 1.0 + nrm(ks[8], (L, 3, D_MODEL), 0.05),
        'ffn_w_gate': nrm(ks[9], (L, 2, D_MODEL, D_FF), D_MODEL ** -0.5),
        'ffn_w_up': nrm(ks[10], (L, 2, D_MODEL, D_FF), D_MODEL ** -0.5),
        'ffn_w_down': nrm(ks[11], (L, 2, D_FF, D_MODEL), D_FF ** -0.5),
        'w_in': nrm(ks[12], (L, D_MODEL, D_IN), D_MODEL ** -0.5),
        'shift_mu': unif(ks[13], (L, D_RWKV_IN), 0.2, 0.8),
        'decay_w0': unif(ks[14], (L, 2, D_RWKV), -6.0, -1.0),
        'decay_w2': nrm(ks[15], (L, 2, DECAY_LORA, D_RWKV), 0.5 * DECAY_LORA ** -0.5),
        'iclr_a0': nrm(ks[16], (L, 2, D_RWKV), 0.1),
        'iclr_a2': nrm(ks[17], (L, 2, ICLR_LORA, D_RWKV), 0.5 * ICLR_LORA ** -0.5),
        'gate_g2': nrm(ks[18], (L, GATE_LORA, D_RWKV), GATE_LORA ** -0.5),
        'k_k': 0.85 + nrm(ks[19], (L, D_RWKV), 0.05),
        'k_a': 1.0 + nrm(ks[20], (L, D_RWKV), 0.05),
        'r_k': nrm(ks[21], (L, D_RWKV), 0.1),
        'ln_x_w': 1.0 + nrm(ks[22], (L, D_RWKV), 0.05),
        'ln_x_b': nrm(ks[23], (L, D_RWKV), 0.02),
        'w_out': nrm(ks[24], (L, D_MODEL, D_MODEL), D_MODEL ** -0.5),
        'final_norm': 1.0 + nrm(ks[25], (D_MODEL,), 0.05),
    }


def reference(x_prompt, x_sample, state_fwd, state_bwd, c, c_ctx, w_mod, b_mod, norm_g,
              ffn_w_gate, ffn_w_up, ffn_w_down, w_in, shift_mu, decay_w0, decay_w2,
              iclr_a0, iclr_a2, gate_g2, k_k, k_a, r_k, ln_x_w, ln_x_b, w_out, final_norm):
    xp, xs = x_prompt, x_sample
    bp = xp.shape[0]
    zero_state = jnp.zeros((bp, N_RWKV_HEADS, HEAD_SIZE, HEAD_SIZE), jnp.float32)
    new_f, new_b = [], []
    for l in range(DEPTH):
        mp = (w_in[l], shift_mu[l], decay_w0[l], decay_w2[l], iclr_a0[l], iclr_a2[l],
              gate_g2[l], k_k[l], k_a[l], r_k[l], ln_x_w[l], ln_x_b[l], w_out[l])
        mod_ctx = jnp.broadcast_to(jax.nn.silu(c_ctx) @ w_mod[l] + b_mod[l], (bp, N_MOD * D_MODEL))
        xp, s_f, s_b = _layer(xp, mod_ctx, zero_state, zero_state, False, norm_g[l],
                              ffn_w_gate[l], ffn_w_up[l], ffn_w_down[l], mp)
        new_f.append(s_f.astype(x_prompt.dtype))
        new_b.append(s_b.astype(x_prompt.dtype))
        mod_lat = jax.nn.silu(c) @ w_mod[l] + b_mod[l]
        xs, _, _ = _layer(xs, mod_lat, state_fwd[:, l], state_bwd[:, l], True, norm_g[l],
                          ffn_w_gate[l], ffn_w_up[l], ffn_w_down[l], mp)
    y_prompt = _rmsnorm(xp, final_norm)
    y_sample = _rmsnorm(xs, final_norm)
    return (y_prompt, y_sample, jnp.stack(new_f, axis=1), jnp.stack(new_b, axis=1))
```

```python
import functools

import numpy as np
import jax
import jax.numpy as jnp
from jax import lax
from jax.experimental import pallas as pl
from jax.experimental.pallas import tpu as pltpu

D_MODEL = 1024
GRID_W = 64
D_FOURIER = 512
N_FOURIER_GROUPS = 8
FOURIER_GROUP = D_FOURIER // N_FOURIER_GROUPS
D_RWKV = D_MODEL - D_FOURIER
HEAD_SIZE = 64
N_RWKV_HEADS = D_RWKV // HEAD_SIZE
N_PAIRS = N_RWKV_HEADS // 2
PAIR = 2 * HEAD_SIZE
DECAY_LORA = 64
ICLR_LORA = 64
GATE_LORA = 128
D_Z = 3 * D_RWKV + 2 * DECAY_LORA + 2 * ICLR_LORA + GATE_LORA
D_IN = D_FOURIER + D_Z
D_FF = 2816
N_MOD = 9
RMS_EPS = 1e-6
GN_EPS = 64e-5
CHUNK = 64
OFF_WD = 3 * D_RWKV
OFF_AD = OFF_WD + 2 * DECAY_LORA
OFF_GD = OFF_AD + 2 * ICLR_LORA
DECAY_SCALE = float(np.exp(-0.5))
V7X_VMEM_LIMIT = 56 * 1024 * 1024

F32 = jnp.float32
BF16 = jnp.bfloat16


def _dot32(a, b):
    return jnp.dot(a, b, precision=lax.Precision.HIGHEST, preferred_element_type=F32)


def _dot32_nt(a, b):
    return lax.dot_general(a, b, (((1,), (1,)), ((), ())), precision=lax.Precision.HIGHEST,
                           preferred_element_type=F32)


def _dot16(a, b):
    return jnp.dot(a.astype(BF16), b.astype(BF16), preferred_element_type=F32)


def _sigmoid(x):
    return 1.0 / (1.0 + jnp.exp(-x))


def _norm_mod(x, g, shift, scale):
    ms = jnp.mean(x * x, axis=-1, keepdims=True)
    return x * lax.rsqrt(ms + RMS_EPS) * g * (1.0 + scale) + shift


def _swiglu(h, wg_ref, wu_ref, wd_ref):
    hb = h.astype(BF16)
    gate = jnp.dot(hb, wg_ref[...], preferred_element_type=F32)
    up = jnp.dot(hb, wu_ref[...], preferred_element_type=F32)
    act = gate * _sigmoid(gate) * up
    return jnp.dot(act.astype(BF16), wd_ref[...], preferred_element_type=F32)


def _const_spec(shape):
    nd = len(shape)
    return pl.BlockSpec(shape, lambda *_: (0,) * nd, pipeline_mode=pl.Buffered(1))


def _params(sem):
    return pltpu.CompilerParams(dimension_semantics=sem, vmem_limit_bytes=V7X_VMEM_LIMIT)


def _mod_kernel(c_ref, w_ref, b_ref, o_ref):
    c = c_ref[...]
    o_ref[...] = _dot32(c * _sigmoid(c), w_ref[...]) + b_ref[...]


def _modulation(cvec, w_mod, b_mod):
    n = w_mod.shape[1]
    tn = n // 8
    return pl.pallas_call(
        _mod_kernel,
        grid=(n // tn,),
        in_specs=[pl.BlockSpec((8, D_MODEL), lambda j: (0, 0)),
                  pl.BlockSpec((D_MODEL, tn), lambda j: (0, j)),
                  pl.BlockSpec((1, tn), lambda j: (0, j))],
        out_specs=pl.BlockSpec((8, tn), lambda j: (0, j)),
        out_shape=jax.ShapeDtypeStruct((8, n), F32),
        compiler_params=_params(("arbitrary",)),
        name="modulation",
    )(cvec, w_mod, b_mod.reshape(1, n))


def _front_kernel(x_ref, mod_ref, g_ref, wg_ref, wu_ref, wd_ref, win_ref, x1_ref, u_ref, z_ref):
    x = x_ref[...]
    h = _norm_mod(x, g_ref[0:1, :], mod_ref[0:1, :], mod_ref[1:2, :])
    x1 = x + 0.5 * mod_ref[2:3, :] * _swiglu(h, wg_ref, wu_ref, wd_ref)
    x1_ref[...] = x1
    h2 = _norm_mod(x1, g_ref[1:2, :], mod_ref[3:4, :], mod_ref[4:5, :])
    proj = jnp.dot(h2.astype(BF16), win_ref[...], preferred_element_type=F32)
    u_ref[...] = proj[:, :D_FOURIER]
    z_ref[...] = proj[:, D_FOURIER:]


def _front(x, mod, mod_per_batch, norm_g, wg, wu, wd, w_in, tm):
    b, s, _ = x.shape
    nt = s // tm
    tok = lambda w: pl.BlockSpec((None, tm, w), lambda i, j: (i, j, 0))
    mod_map = (lambda i, j: (i, 0, 0)) if mod_per_batch else (lambda i, j: (0, 0, 0))
    return pl.pallas_call(
        _front_kernel,
        grid=(b, nt),
        in_specs=[tok(D_MODEL),
                  pl.BlockSpec((None, N_MOD, D_MODEL), mod_map),
                  _const_spec((3, D_MODEL)),
                  _const_spec((D_MODEL, D_FF)), _const_spec((D_MODEL, D_FF)),
                  _const_spec((D_FF, D_MODEL)), _const_spec((D_MODEL, D_IN))],
        out_specs=[tok(D_MODEL), tok(D_FOURIER), tok(D_Z)],
        out_shape=[jax.ShapeDtypeStruct((b, s, D_MODEL), F32),
                   jax.ShapeDtypeStruct((b, s, D_FOURIER), F32),
                   jax.ShapeDtypeStruct((b, s, D_Z), F32)],
        compiler_params=_params(("arbitrary", "arbitrary")),
        name="front",
    )(x, mod, norm_g, wg, wu, wd, w_in)


def _back_kernel(x1_ref, yf_ref, yr_ref, mod_ref, g_ref, fin_ref, wof_ref, wor_ref,
                 wg_ref, wu_ref, wd_ref, o_ref):
    mixed = (jnp.dot(yf_ref[...].astype(BF16), wof_ref[...], preferred_element_type=F32)
             + jnp.dot(yr_ref[...].astype(BF16), wor_ref[...], preferred_element_type=F32))
    x2 = x1_ref[...] + mod_ref[5:6, :] * mixed
    h = _norm_mod(x2, g_ref[2:3, :], mod_ref[6:7, :], mod_ref[7:8, :])
    x3 = x2 + 0.5 * mod_ref[8:9, :] * _swiglu(h, wg_ref, wu_ref, wd_ref)
    ms = jnp.mean(x3 * x3, axis=-1, keepdims=True)
    o_ref[...] = x3 * lax.rsqrt(ms + RMS_EPS) * fin_ref[...]


def _back(x1, yf, yr, mod, mod_per_batch, norm_g, final_norm, wo_f, wo_r, wg, wu, wd, tm):
    b, s, _ = x1.shape
    nt = s // tm
    tok = lambda w: pl.BlockSpec((None, tm, w), lambda i, j: (i, j, 0))
    mod_map = (lambda i, j: (i, 0, 0)) if mod_per_batch else (lambda i, j: (0, 0, 0))
    return pl.pallas_call(
        _back_kernel,
        grid=(b, nt),
        in_specs=[tok(D_MODEL), tok(D_FOURIER), tok(D_RWKV),
                  pl.BlockSpec((None, N_MOD, D_MODEL), mod_map),
                  _const_spec((3, D_MODEL)), _const_spec((1, D_MODEL)),
                  _const_spec((D_FOURIER, D_MODEL)), _const_spec((D_RWKV, D_MODEL)),
                  _const_spec((D_MODEL, D_FF)), _const_spec((D_MODEL, D_FF)),
                  _const_spec((D_FF, D_MODEL))],
        out_specs=tok(D_MODEL),
        out_shape=jax.ShapeDtypeStruct((b, s, D_MODEL), F32),
        compiler_params=_params(("arbitrary", "arbitrary")),
        name="back",
    )(x1, yf, yr, mod, norm_g, final_norm.reshape(1, D_MODEL), wo_f, wo_r, wg, wu, wd)


def _group_dft_tables(seq_len):
    q = np.arange(FOURIER_GROUP)
    ang = 2.0 * np.pi * ((q[:, None] * q[None, :]) % FOURIER_GROUP) / FOURIER_GROUP
    scale = 1.0 / np.sqrt(seq_len * FOURIER_GROUP)
    eye = np.eye(N_FOURIER_GROUPS)
    return (jnp.asarray(np.kron(eye, np.cos(ang) * scale), F32),
            jnp.asarray(np.kron(eye, np.sin(ang) * scale), F32))


def _dft_tables(n, rows=None, cols=None):
    r = np.arange(n) if rows is None else rows
    c = np.arange(n) if cols is None else cols
    ang = 2.0 * np.pi * ((r[:, None] * c[None, :]) % n) / n
    return np.cos(ang), np.sin(ang)


def _fourier_dense_kernel(u_ref, cbd_ref, sbd_ref, pc_ref, ps_ref, o_ref):
    x = u_ref[...]
    xc = _dot32(x, cbd_ref[...])
    xs = _dot32(x, sbd_ref[...])
    o_ref[...] = _dot32(pc_ref[...], xc) - _dot32(ps_ref[...], xs)


def _fourier_dense(u):
    b, s, _ = u.shape
    cbd, sbd = _group_dft_tables(s)
    pc, ps = _dft_tables(s)
    return pl.pallas_call(
        _fourier_dense_kernel,
        grid=(b,),
        in_specs=[pl.BlockSpec((None, s, D_FOURIER), lambda i: (i, 0, 0)),
                  _const_spec((D_FOURIER, D_FOURIER)), _const_spec((D_FOURIER, D_FOURIER)),
                  _const_spec((s, s)), _const_spec((s, s))],
        out_specs=pl.BlockSpec((None, s, D_FOURIER), lambda i: (i, 0, 0)),
        out_shape=jax.ShapeDtypeStruct((b, s, D_FOURIER), F32),
        compiler_params=_params(("arbitrary",)),
        name="fourier_dense",
    )(u, cbd, sbd, jnp.asarray(pc, F32), jnp.asarray(ps, F32))


FFT_COLS = 8


def _fourier_stage1_kernel(x_ref, cbd_ref, sbd_ref, fc_ref, fs_ref, zc_ref, zs_ref):
    fc = fc_ref[...]
    fs = fs_ref[...]
    for j in range(FFT_COLS):
        cols = slice(j * D_FOURIER, (j + 1) * D_FOURIER)
        xb = x_ref[:, cols]
        xc = _dot32(xb, cbd_ref[...])
        xs = _dot32(xb, sbd_ref[...])
        zc_ref[:, cols] = _dot32(fc, xc) - _dot32(fs, xs)
        zs_ref[:, cols] = _dot32(fc, xs) + _dot32(fs, xc)


def _fourier_stage2_kernel(zc_ref, zs_ref, gc_ref, gs_ref, o_ref):
    for j in range(FFT_COLS):
        cols = slice(j * D_FOURIER, (j + 1) * D_FOURIER)
        o_ref[:, cols] = _dot32(gc_ref[j], zc_ref[j]) - _dot32(gs_ref[j], zs_ref[j])


def _fourier_two_stage(u):
    b, s, _ = u.shape
    n = int(round(np.sqrt(s)))
    assert n * n == s and n % FFT_COLS == 0
    wide = n * D_FOURIER
    cbd, sbd = _group_dft_tables(s)
    fc, fs = _dft_tables(n)
    blk = FFT_COLS * D_FOURIER
    x2d = u.reshape(b, n, wide)
    zc, zs = pl.pallas_call(
        _fourier_stage1_kernel,
        grid=(b, n // FFT_COLS),
        in_specs=[pl.BlockSpec((None, n, blk), lambda i, j: (i, 0, j)),
                  _const_spec((D_FOURIER, D_FOURIER)), _const_spec((D_FOURIER, D_FOURIER)),
                  _const_spec((n, n)), _const_spec((n, n))],
        out_specs=[pl.BlockSpec((None, n, blk), lambda i, j: (i, 0, j))] * 2,
        out_shape=[jax.ShapeDtypeStruct((b, n, wide), F32)] * 2,
        compiler_params=_params(("arbitrary", "arbitrary")),
        name="fourier_stage1",
    )(x2d, cbd, sbd, jnp.asarray(fc, F32), jnp.asarray(fs, F32))
    bb, aa, s1 = np.arange(n)[:, None, None], np.arange(n)[None, :, None], np.arange(n)[None, None, :]
    ang = 2.0 * np.pi * ((s1 * (n * aa + bb)) % s) / s
    gc, gs = jnp.asarray(np.cos(ang), F32), jnp.asarray(np.sin(ang), F32)
    zc4 = zc.reshape(b, n, n, D_FOURIER)
    zs4 = zs.reshape(b, n, n, D_FOURIER)
    out = pl.pallas_call(
        _fourier_stage2_kernel,
        grid=(b, n // FFT_COLS),
        in_specs=[pl.BlockSpec((None, FFT_COLS, n, D_FOURIER), lambda i, j: (i, j, 0, 0))] * 2
                 + [pl.BlockSpec((FFT_COLS, n, n), lambda i, j: (j, 0, 0))] * 2,
        out_specs=pl.BlockSpec((None, n, blk), lambda i, j: (i, 0, j)),
        out_shape=jax.ShapeDtypeStruct((b, n, wide), F32),
        compiler_params=_params(("arbitrary", "arbitrary")),
        name="fourier_stage2",
    )(zc4, zs4, gc, gs)
    return out.reshape(b, s, D_FOURIER)


def _token_shift(z, prev_rows, next_rows, grid):
    tm = z.shape[0]
    row = lax.broadcasted_iota(jnp.int32, (tm, 1), 0)
    lane = lax.broadcasted_iota(jnp.int32, (1, z.shape[1]), 1) % 4
    back1 = pltpu.roll(z, 1, 0)
    fwd1 = pltpu.roll(z, tm - 1, 0)
    if grid:
        col = row % GRID_W
        left = jnp.where(col == 0, 0.0, back1)
        right = jnp.where(col == GRID_W - 1, 0.0, fwd1)
        up = jnp.concatenate([prev_rows, z[:tm - GRID_W]], axis=0)
        down = jnp.concatenate([z[GRID_W:], next_rows], axis=0)
        return jnp.where(lane == 0, left, jnp.where(lane == 1, right, jnp.where(lane == 2, up, down)))
    prev = jnp.where(row == 0, 0.0, back1)
    nxt = jnp.where(row == tm - 1, 0.0, fwd1)
    return jnp.where(lane % 2 == 0, prev, nxt)


def _stack_masked(x, head0):
    return jnp.concatenate([jnp.where(head0, x, 0.0), jnp.where(head0, 0.0, x)], axis=0)


def _unit_triangular_inverse(l):
    n = l.shape[0]
    eye = (lax.broadcasted_iota(jnp.int32, (n, n), 0) == lax.broadcasted_iota(jnp.int32, (n, n), 1)).astype(F32)
    t = eye + l
    p = l
    steps = int(np.log2(CHUNK)) - 1
    for _ in range(steps):
        p = _dot32(p, p)
        t = t + _dot32(p, t)
    return t


def _scan_kernel(*refs, grid, reverse, combine, n_tiles, tm):
    it = iter(refs)
    z_ref = next(it)
    zp_ref = next(it) if grid else None
    zn_ref = next(it) if grid else None
    h0_ref = next(it)
    if combine:
        yo_ref, bo_ref = next(it), next(it)
    mu_ref, kk_ref, ka_ref, rk_ref, w0_ref, a0_ref, w2_ref, a2_ref, ones_ref = (next(it) for _ in range(9))
    if combine:
        lnw_ref, lnb_ref, g2_ref = next(it), next(it), next(it)
    y_ref = next(it)
    b_ref = None if combine else next(it)
    hout_ref = next(it)
    r_s, v_s, al_s, be_s, kd_s, lw_s, y_s, g_s, h_s = (next(it) for _ in range(9))

    step = pl.program_id(1)
    tile = (n_tiles - 1 - step) if reverse else step

    @pl.when(step == 0)
    def _():
        h_s[...] = h0_ref[...]

    z = z_ref[...]
    if grid:
        prev_rows = jnp.where(tile == 0, 0.0, zp_ref[...])
        next_rows = jnp.where(tile == n_tiles - 1, 0.0, zn_ref[...])
    else:
        prev_rows = next_rows = None
    z = z + mu_ref[...] * (_token_shift(z, prev_rows, next_rows, grid) - z)
    r = z[:, 0:D_RWKV]
    k = z[:, D_RWKV:2 * D_RWKV]
    v = z[:, 2 * D_RWKV:3 * D_RWKV]
    ones_bd = ones_ref[...]
    kk = k * kk_ref[...]
    kk = kk / jnp.maximum(jnp.sqrt(_dot32(kk * kk, ones_bd)), 1e-12)
    w_raw = w0_ref[...] + _dot32(jnp.tanh(z[:, OFF_WD:OFF_AD]), w2_ref[...])
    a = _sigmoid(a0_ref[...] + _dot32(z[:, OFF_AD:OFF_GD], a2_ref[...]))
    kd = k * (1.0 + (a - 1.0) * ka_ref[...])
    bonus = _dot32(r * kd * rk_ref[...], ones_bd) * v
    if combine:
        y_ref[...] = bonus + bo_ref[...]
        g_s[...] = _dot32(_sigmoid(z[:, OFF_GD:]), g2_ref[...])
    else:
        b_ref[...] = bonus
    r_s[...] = r
    v_s[...] = v
    al_s[...] = -kk
    be_s[...] = a * kk
    kd_s[...] = kd
    lw_s[...] = -DECAY_SCALE * _sigmoid(w_raw)

    n_chunks = tm // CHUNK
    ri = lax.broadcasted_iota(jnp.int32, (CHUNK, CHUNK), 0)
    ci = lax.broadcasted_iota(jnp.int32, (CHUNK, CHUNK), 1)
    tri = ((ci >= ri) if reverse else (ci <= ri)).astype(F32)
    rj = lax.broadcasted_iota(jnp.int32, (PAIR, PAIR), 0) % CHUNK
    cj = lax.broadcasted_iota(jnp.int32, (PAIR, PAIR), 1) % CHUNK
    strict = (cj > rj) if reverse else (cj < rj)
    incl = (cj >= rj) if reverse else (cj <= rj)
    head0 = lax.broadcasted_iota(jnp.int32, (1, PAIR), 1) < HEAD_SIZE
    last = 0 if reverse else CHUNK - 1

    def chunk_body(c, carry):
        off = pl.multiple_of(((n_chunks - 1 - c) if reverse else c) * CHUNK, CHUNK)
        rows = pl.ds(off, CHUNK)
        lw = lw_s[rows, :]
        cum = _dot32(tri, lw)
        tot = cum[last:last + 1, :]
        g_in = jnp.exp(cum)
        g_ex = jnp.exp(cum - lw)
        g_inv = jnp.exp(-cum)
        g_end = jnp.exp(tot - cum)
        g_tot = jnp.exp(tot)
        abar = al_s[rows, :] * g_ex
        rbar = r_s[rows, :] * g_in
        be = be_s[rows, :]
        kdc = kd_s[rows, :]
        btil = be * g_inv
        ktil = kdc * g_inv
        bhat = be * g_end
        khat = kdc * g_end
        vc = v_s[rows, :]
        for p in range(N_PAIRS):
            lanes = slice(p * PAIR, (p + 1) * PAIR)
            a_m = _stack_masked(abar[:, lanes], head0)
            r_m = _stack_masked(rbar[:, lanes], head0)
            v_m = _stack_masked(vc[:, lanes], head0)
            g = _dot32_nt(jnp.concatenate([a_m, r_m], axis=0),
                          jnp.concatenate([_stack_masked(btil[:, lanes], head0),
                                           _stack_masked(ktil[:, lanes], head0)], axis=0))
            l_ab = jnp.where(strict, g[:PAIR, :PAIR], 0.0)
            l_ak = jnp.where(strict, g[:PAIR, PAIR:], 0.0)
            l_rb = jnp.where(incl, g[PAIR:, :PAIR], 0.0)
            l_rk = jnp.where(incl, g[PAIR:, PAIR:], 0.0)
            t_inv = _unit_triangular_inverse(l_ab)
            pu = _dot32(t_inv, jnp.concatenate([a_m, _dot32(l_ak, v_m)], axis=1))
            p_m, u_loc = pu[:, :PAIR], pu[:, PAIR:]
            q_m = r_m + _dot32(l_rb, p_m)
            y_loc = _dot32(l_rb, u_loc) + _dot32(l_rk, v_m)
            bh_t = _stack_masked(bhat[:, lanes], head0).T
            kh_t = _stack_masked(khat[:, lanes], head0).T
            diag = jnp.where(lax.broadcasted_iota(jnp.int32, (PAIR, PAIR), 0)
                             == lax.broadcasted_iota(jnp.int32, (PAIR, PAIR), 1), g_tot[:, lanes], 0.0)
            m_mat = diag + _dot32(bh_t, p_m)
            n_mat = _dot32(bh_t, u_loc) + _dot32(kh_t, v_m)
            h = h_s[p]
            y_m = _dot32(q_m, h) + y_loc
            y_s[rows, lanes] = y_m[:CHUNK] + y_m[CHUNK:]
            h_s[p] = _dot32(m_mat, h) + n_mat
        return carry

    lax.fori_loop(0, n_chunks, chunk_body, 0)

    @pl.when(step == n_tiles - 1)
    def _():
        hout_ref[...] = h_s[...]

    if combine:
        y_sum = y_s[...] + yo_ref[...]
        mean = _dot32(y_sum, ones_bd) * (1.0 / HEAD_SIZE)
        cen = y_sum - mean
        var = _dot32(cen * cen, ones_bd) * (1.0 / HEAD_SIZE)
        y_gn = cen * lax.rsqrt(var + GN_EPS) * lnw_ref[...] + lnb_ref[...]
        y_ref[...] = (y_gn + y_ref[...]) * g_s[...]
    else:
        y_ref[...] = y_s[...]


def _scan(z, h0, prm, d, grid, other=None):
    b, s, _ = z.shape
    tm = 512 if grid else s
    n_tiles = s // tm
    reverse = d == 1
    combine = other is not None
    tile_of = (lambda j: n_tiles - 1 - j) if reverse else (lambda j: j)
    tok = lambda w: pl.BlockSpec((None, tm, w), lambda i, j: (i, tile_of(j), 0))
    halo_per_tile = tm // GRID_W
    n_halo = s // GRID_W
    in_specs = [tok(D_Z)]
    args = [z]
    if grid:
        in_specs += [pl.BlockSpec((None, GRID_W, D_Z),
                                  lambda i, j: (i, jnp.maximum(tile_of(j) * halo_per_tile - 1, 0), 0)),
                     pl.BlockSpec((None, GRID_W, D_Z),
                                  lambda i, j: (i, jnp.minimum((tile_of(j) + 1) * halo_per_tile, n_halo - 1), 0))]
        args += [z, z]
    state_spec = pl.BlockSpec((None, N_PAIRS, PAIR, PAIR), lambda i, j: (i, 0, 0, 0))
    in_specs.append(state_spec)
    args.append(h0)
    if combine:
        in_specs += [tok(D_RWKV), tok(D_RWKV)]
        args += list(other)
    row = lambda x: x.reshape(1, -1)
    small = [row(prm["mu"]), row(prm["k_k"]), row(prm["k_a"]), row(prm["r_k"]),
             row(prm["w0"][d]), row(prm["a0"][d]),
             jnp.zeros((2 * DECAY_LORA, D_RWKV), F32).at[d * DECAY_LORA:(d + 1) * DECAY_LORA].set(prm["w2"][d]),
             jnp.zeros((2 * ICLR_LORA, D_RWKV), F32).at[d * ICLR_LORA:(d + 1) * ICLR_LORA].set(prm["a2"][d]),
             jnp.asarray(np.kron(np.eye(N_RWKV_HEADS), np.ones((HEAD_SIZE, HEAD_SIZE))), F32)]
    if combine:
        small += [row(prm["ln_w"]), row(prm["ln_b"]), prm["g2"]]
    in_specs += [_const_spec(x.shape) for x in small]
    args += small
    out_specs = [tok(D_RWKV)]
    out_shape = [jax.ShapeDtypeStruct((b, s, D_RWKV), F32)]
    if not combine:
        out_specs.append(tok(D_RWKV))
        out_shape.append(jax.ShapeDtypeStruct((b, s, D_RWKV), F32))
    out_specs.append(state_spec)
    out_shape.append(jax.ShapeDtypeStruct((b, N_PAIRS, PAIR, PAIR), F32))
    scratch = [pltpu.VMEM((tm, D_RWKV), F32) for _ in range(8)] + [pltpu.VMEM((N_PAIRS, PAIR, PAIR), F32)]
    return pl.pallas_call(
        functools.partial(_scan_kernel, grid=grid, reverse=reverse, combine=combine, n_tiles=n_tiles, tm=tm),
        grid=(b, n_tiles),
        in_specs=in_specs,
        out_specs=out_specs,
        out_shape=out_shape,
        scratch_shapes=scratch,
        compiler_params=_params(("arbitrary", "arbitrary")),
        name="scan_bwd" if reverse else "scan_fwd",
    )(*args)


def _pack_state(s):
    b = s.shape[0]
    h = jnp.swapaxes(s, -1, -2).reshape(b, N_PAIRS, 2, HEAD_SIZE, HEAD_SIZE)
    zero = jnp.zeros_like(h[:, :, 0])
    top = jnp.concatenate([h[:, :, 0], zero], axis=-1)
    bot = jnp.concatenate([zero, h[:, :, 1]], axis=-1)
    return jnp.concatenate([top, bot], axis=-2)


def _unpack_state(hp):
    b = hp.shape[0]
    h0 = hp[:, :, :HEAD_SIZE, :HEAD_SIZE]
    h1 = hp[:, :, HEAD_SIZE:, HEAD_SIZE:]
    h = jnp.stack([h0, h1], axis=2).reshape(b, N_RWKV_HEADS, HEAD_SIZE, HEAD_SIZE)
    return jnp.swapaxes(h, -1, -2)


def _mixer_heads(z, s0_f, s0_b, prm, grid):
    y_b, bonus_b, h_b = _scan(z, _pack_state(s0_b), prm, 1, grid)
    y, h_f = _scan(z, _pack_state(s0_f), prm, 0, grid, other=(y_b, bonus_b))
    return y, _unpack_state(h_f), _unpack_state(h_b)


def _path(x, mod, mod_per_batch, s0_f, s0_b, grid, w, prm, tm):
    x1, u, z = _front(x, mod, mod_per_batch, w["norm_g"], w["g0"], w["u0"], w["d0"], w["w_in"], tm)
    y_four = _fourier_two_stage(u) if grid else _fourier_dense(u)
    y_rwkv, s_f, s_b = _mixer_heads(z, s0_f, s0_b, prm, grid)
    y = _back(x1, y_four, y_rwkv, mod, mod_per_batch, w["norm_g"], w["final"], w["wo_f"], w["wo_r"],
              w["g1"], w["u1"], w["d1"], tm)
    return y, s_f, s_b


def kernel(x_prompt, x_sample, state_fwd, state_bwd, c, c_ctx, w_mod, b_mod, norm_g, ffn_w_gate, ffn_w_up,
           ffn_w_down, w_in, shift_mu, decay_w0, decay_w2, iclr_a0, iclr_a2, gate_g2, k_k, k_a, r_k,
           ln_x_w, ln_x_b, w_out, final_norm):
    depth = w_mod.shape[0]
    assert depth == 1, "the back kernel applies the final norm, so exactly one layer is supported"
    bp = x_prompt.shape[0]
    bs = x_sample.shape[0]
    xp, xs = x_prompt, x_sample
    new_f, new_b = [], []
    for l in range(depth):
        cvec = jnp.zeros((8, D_MODEL), F32).at[:bs].set(c).at[bs].set(c_ctx)
        mod = _modulation(cvec, w_mod[l], b_mod[l]).reshape(8, N_MOD, D_MODEL)
        w = {"norm_g": norm_g[l], "final": final_norm,
             "g0": ffn_w_gate[l, 0].astype(BF16), "u0": ffn_w_up[l, 0].astype(BF16),
             "d0": ffn_w_down[l, 0].astype(BF16),
             "g1": ffn_w_gate[l, 1].astype(BF16), "u1": ffn_w_up[l, 1].astype(BF16),
             "d1": ffn_w_down[l, 1].astype(BF16),
             "w_in": w_in[l].astype(BF16),
             "wo_f": w_out[l, :D_FOURIER].astype(BF16), "wo_r": w_out[l, D_FOURIER:].astype(BF16)}
        prm = {"mu": shift_mu[l], "w0": decay_w0[l], "w2": decay_w2[l], "a0": iclr_a0[l], "a2": iclr_a2[l],
               "g2": gate_g2[l], "k_k": k_k[l], "k_a": k_a[l], "r_k": r_k[l], "ln_w": ln_x_w[l], "ln_b": ln_x_b[l]}
        zero_state = jnp.zeros((bp, N_RWKV_HEADS, HEAD_SIZE, HEAD_SIZE), F32)
        xp, s_f, s_b = _path(xp, mod[bs:bs + 1], False, zero_state, zero_state, False, w, prm, 256)
        new_f.append(s_f)
        new_b.append(s_b)
        xs, _, _ = _path(xs, mod[:bs], True, state_fwd[:, l], state_bwd[:, l], True, w, prm, 256)
    return xp, xs, jnp.stack(new_f, axis=1), jnp.stack(new_b, axis=1)
```

```python
import functools

import numpy as np
import jax
import jax.numpy as jnp
from jax import lax
from jax.experimental import pallas as pl
from jax.experimental.pallas import tpu as pltpu

D_MODEL = 1024
GRID_W = 64
D_FOURIER = 512
N_FOURIER_GROUPS = 8
FOURIER_GROUP = D_FOURIER // N_FOURIER_GROUPS
D_RWKV = D_MODEL - D_FOURIER
HEAD_SIZE = 64
N_RWKV_HEADS = D_RWKV // HEAD_SIZE
N_PAIRS = N_RWKV_HEADS // 2
PAIR = 2 * HEAD_SIZE
DECAY_LORA = 64
ICLR_LORA = 64
GATE_LORA = 128
D_Z = 3 * D_RWKV + 2 * DECAY_LORA + 2 * ICLR_LORA + GATE_LORA
D_IN = D_FOURIER + D_Z
D_FF = 2816
N_MOD = 9
RMS_EPS = 1e-6
GN_EPS = 64e-5
CHUNK = 64
OFF_WD = 3 * D_RWKV
OFF_AD = OFF_WD + 2 * DECAY_LORA
OFF_GD = OFF_AD + 2 * ICLR_LORA
DECAY_SCALE = float(np.exp(-0.5))
V7X_VMEM_LIMIT = 56 * 1024 * 1024

F32 = jnp.float32
BF16 = jnp.bfloat16


def _dot32(a, b):
    return jnp.dot(a, b, precision=lax.Precision.HIGHEST, preferred_element_type=F32)


def _sigmoid(x):
    return 1.0 / (1.0 + jnp.exp(-x))


def _norm_mod(x, g, shift, scale):
    ms = jnp.mean(x * x, axis=-1, keepdims=True)
    return x * lax.rsqrt(ms + RMS_EPS) * g * (1.0 + scale) + shift


def _swiglu(h, wg_ref, wu_ref, wd_ref):
    hb = h.astype(BF16)
    gate = jnp.dot(hb, wg_ref[...], preferred_element_type=F32)
    up = jnp.dot(hb, wu_ref[...], preferred_element_type=F32)
    act = gate * _sigmoid(gate) * up
    return jnp.dot(act.astype(BF16), wd_ref[...], preferred_element_type=F32)


def _const_spec(shape):
    nd = len(shape)
    return pl.BlockSpec(shape, lambda *_: (0,) * nd, pipeline_mode=pl.Buffered(1))


def _params(sem):
    return pltpu.CompilerParams(dimension_semantics=sem, vmem_limit_bytes=V7X_VMEM_LIMIT)


def _mod_kernel(c_ref, w_ref, b_ref, o_ref):
    c = c_ref[...]
    o_ref[...] = _dot32(c * _sigmoid(c), w_ref[...]) + b_ref[...]


def _modulation(cvec, w_mod, b_mod):
    n = w_mod.shape[1]
    tn = n // 8
    return pl.pallas_call(
        _mod_kernel,
        grid=(n // tn,),
        in_specs=[pl.BlockSpec((8, D_MODEL), lambda j: (0, 0)),
                  pl.BlockSpec((D_MODEL, tn), lambda j: (0, j)),
                  pl.BlockSpec((1, tn), lambda j: (0, j))],
        out_specs=pl.BlockSpec((8, tn), lambda j: (0, j)),
        out_shape=jax.ShapeDtypeStruct((8, n), F32),
        compiler_params=_params(("arbitrary",)),
        name="modulation",
    )(cvec, w_mod, b_mod.reshape(1, n))


def _front_kernel(x_ref, mod_ref, g_ref, wg_ref, wu_ref, wd_ref, win_ref, x1_ref, u_ref, z_ref):
    x = x_ref[...]
    h = _norm_mod(x, g_ref[0:1, :], mod_ref[0:1, :], mod_ref[1:2, :])
    x1 = x + 0.5 * mod_ref[2:3, :] * _swiglu(h, wg_ref, wu_ref, wd_ref)
    x1_ref[...] = x1
    h2 = _norm_mod(x1, g_ref[1:2, :], mod_ref[3:4, :], mod_ref[4:5, :])
    proj = jnp.dot(h2.astype(BF16), win_ref[...], preferred_element_type=F32)
    u_ref[...] = proj[:, :D_FOURIER]
    z_ref[...] = proj[:, D_FOURIER:]


def _front(x, mod, mod_per_batch, norm_g, wg, wu, wd, w_in, tm):
    b, s, _ = x.shape
    nt = s // tm
    tok = lambda w: pl.BlockSpec((None, tm, w), lambda i, j: (i, j, 0))
    mod_map = (lambda i, j: (i, 0, 0)) if mod_per_batch else (lambda i, j: (0, 0, 0))
    return pl.pallas_call(
        _front_kernel,
        grid=(b, nt),
        in_specs=[tok(D_MODEL),
                  pl.BlockSpec((None, N_MOD, D_MODEL), mod_map),
                  _const_spec((3, D_MODEL)),
                  _const_spec((D_MODEL, D_FF)), _const_spec((D_MODEL, D_FF)),
                  _const_spec((D_FF, D_MODEL)), _const_spec((D_MODEL, D_IN))],
        out_specs=[tok(D_MODEL), tok(D_FOURIER), tok(D_Z)],
        out_shape=[jax.ShapeDtypeStruct((b, s, D_MODEL), F32),
                   jax.ShapeDtypeStruct((b, s, D_FOURIER), F32),
                   jax.ShapeDtypeStruct((b, s, D_Z), F32)],
        compiler_params=_params(("arbitrary", "arbitrary")),
        name="front",
    )(x, mod, norm_g, wg, wu, wd, w_in)


def _back_kernel(x1_ref, yf_ref, yr_ref, mod_ref, g_ref, fin_ref, wof_ref, wor_ref,
                 wg_ref, wu_ref, wd_ref, o_ref):
    mixed = (jnp.dot(yf_ref[...].astype(BF16), wof_ref[...], preferred_element_type=F32)
             + jnp.dot(yr_ref[...].astype(BF16), wor_ref[...], preferred_element_type=F32))
    x2 = x1_ref[...] + mod_ref[5:6, :] * mixed
    h = _norm_mod(x2, g_ref[2:3, :], mod_ref[6:7, :], mod_ref[7:8, :])
    x3 = x2 + 0.5 * mod_ref[8:9, :] * _swiglu(h, wg_ref, wu_ref, wd_ref)
    ms = jnp.mean(x3 * x3, axis=-1, keepdims=True)
    o_ref[...] = x3 * lax.rsqrt(ms + RMS_EPS) * fin_ref[...]


def _back(x1, yf, yr, mod, mod_per_batch, norm_g, final_norm, wo_f, wo_r, wg, wu, wd, tm):
    b, s, _ = x1.shape
    nt = s // tm
    tok = lambda w: pl.BlockSpec((None, tm, w), lambda i, j: (i, j, 0))
    mod_map = (lambda i, j: (i, 0, 0)) if mod_per_batch else (lambda i, j: (0, 0, 0))
    return pl.pallas_call(
        _back_kernel,
        grid=(b, nt),
        in_specs=[tok(D_MODEL), tok(D_FOURIER), tok(D_RWKV),
                  pl.BlockSpec((None, N_MOD, D_MODEL), mod_map),
                  _const_spec((3, D_MODEL)), _const_spec((1, D_MODEL)),
                  _const_spec((D_FOURIER, D_MODEL)), _const_spec((D_RWKV, D_MODEL)),
                  _const_spec((D_MODEL, D_FF)), _const_spec((D_MODEL, D_FF)),
                  _const_spec((D_FF, D_MODEL))],
        out_specs=tok(D_MODEL),
        out_shape=jax.ShapeDtypeStruct((b, s, D_MODEL), F32),
        compiler_params=_params(("arbitrary", "arbitrary")),
        name="back",
    )(x1, yf, yr, mod, norm_g, final_norm.reshape(1, D_MODEL), wo_f, wo_r, wg, wu, wd)


def _group_dft_tables(seq_len):
    q = np.arange(FOURIER_GROUP)
    ang = 2.0 * np.pi * ((q[:, None] * q[None, :]) % FOURIER_GROUP) / FOURIER_GROUP
    scale = 1.0 / np.sqrt(seq_len * FOURIER_GROUP)
    eye = np.eye(N_FOURIER_GROUPS)
    return (jnp.asarray(np.kron(eye, np.cos(ang) * scale), F32),
            jnp.asarray(np.kron(eye, np.sin(ang) * scale), F32))


def _dft_tables(n, rows=None, cols=None):
    r = np.arange(n) if rows is None else rows
    c = np.arange(n) if cols is None else cols
    ang = 2.0 * np.pi * ((r[:, None] * c[None, :]) % n) / n
    return np.cos(ang), np.sin(ang)


def _fourier_dense_kernel(u_ref, cbd_ref, sbd_ref, pc_ref, ps_ref, o_ref):
    x = u_ref[...]
    xc = _dot32(x, cbd_ref[...])
    xs = _dot32(x, sbd_ref[...])
    o_ref[...] = _dot32(pc_ref[...], xc) - _dot32(ps_ref[...], xs)


def _fourier_dense(u):
    b, s, _ = u.shape
    cbd, sbd = _group_dft_tables(s)
    pc, ps = _dft_tables(s)
    return pl.pallas_call(
        _fourier_dense_kernel,
        grid=(b,),
        in_specs=[pl.BlockSpec((None, s, D_FOURIER), lambda i: (i, 0, 0)),
                  _const_spec((D_FOURIER, D_FOURIER)), _const_spec((D_FOURIER, D_FOURIER)),
                  _const_spec((s, s)), _const_spec((s, s))],
        out_specs=pl.BlockSpec((None, s, D_FOURIER), lambda i: (i, 0, 0)),
        out_shape=jax.ShapeDtypeStruct((b, s, D_FOURIER), F32),
        compiler_params=_params(("arbitrary",)),
        name="fourier_dense",
    )(u, cbd, sbd, jnp.asarray(pc, F32), jnp.asarray(ps, F32))


FFT_COLS = 8


def _fourier_stage1_kernel(x_ref, cbd_ref, sbd_ref, fc_ref, fs_ref, zc_ref, zs_ref):
    fc = fc_ref[...]
    fs = fs_ref[...]
    for j in range(FFT_COLS):
        cols = slice(j * D_FOURIER, (j + 1) * D_FOURIER)
        xb = x_ref[:, cols]
        xc = _dot32(xb, cbd_ref[...])
        xs = _dot32(xb, sbd_ref[...])
        zc_ref[:, cols] = _dot32(fc, xc) - _dot32(fs, xs)
        zs_ref[:, cols] = _dot32(fc, xs) + _dot32(fs, xc)


def _fourier_stage2_kernel(zc_ref, zs_ref, gc_ref, gs_ref, o_ref):
    for j in range(FFT_COLS):
        cols = slice(j * D_FOURIER, (j + 1) * D_FOURIER)
        o_ref[:, cols] = _dot32(gc_ref[j], zc_ref[j]) - _dot32(gs_ref[j], zs_ref[j])


def _fourier_two_stage(u):
    b, s, _ = u.shape
    n = int(round(np.sqrt(s)))
    assert n * n == s and n % FFT_COLS == 0
    wide = n * D_FOURIER
    cbd, sbd = _group_dft_tables(s)
    fc, fs = _dft_tables(n)
    blk = FFT_COLS * D_FOURIER
    x2d = u.reshape(b, n, wide)
    zc, zs = pl.pallas_call(
        _fourier_stage1_kernel,
        grid=(b, n // FFT_COLS),
        in_specs=[pl.BlockSpec((None, n, blk), lambda i, j: (i, 0, j)),
                  _const_spec((D_FOURIER, D_FOURIER)), _const_spec((D_FOURIER, D_FOURIER)),
                  _const_spec((n, n)), _const_spec((n, n))],
        out_specs=[pl.BlockSpec((None, n, blk), lambda i, j: (i, 0, j))] * 2,
        out_shape=[jax.ShapeDtypeStruct((b, n, wide), F32)] * 2,
        compiler_params=_params(("arbitrary", "arbitrary")),
        name="fourier_stage1",
    )(x2d, cbd, sbd, jnp.asarray(fc, F32), jnp.asarray(fs, F32))
    bb, aa, s1 = np.arange(n)[:, None, None], np.arange(n)[None, :, None], np.arange(n)[None, None, :]
    ang = 2.0 * np.pi * ((s1 * (n * aa + bb)) % s) / s
    gc, gs = jnp.asarray(np.cos(ang), F32), jnp.asarray(np.sin(ang), F32)
    zc4 = zc.reshape(b, n, n, D_FOURIER)
    zs4 = zs.reshape(b, n, n, D_FOURIER)
    out = pl.pallas_call(
        _fourier_stage2_kernel,
        grid=(b, n // FFT_COLS),
        in_specs=[pl.BlockSpec((None, FFT_COLS, n, D_FOURIER), lambda i, j: (i, j, 0, 0))] * 2
                 + [pl.BlockSpec((FFT_COLS, n, n), lambda i, j: (j, 0, 0))] * 2,
        out_specs=pl.BlockSpec((None, n, blk), lambda i, j: (i, 0, j)),
        out_shape=jax.ShapeDtypeStruct((b, n, wide), F32),
        compiler_params=_params(("arbitrary", "arbitrary")),
        name="fourier_stage2",
    )(zc4, zs4, gc, gs)
    return out.reshape(b, s, D_FOURIER)


def _token_shift(z, prev_rows, next_rows, grid):
    tm = z.shape[0]
    row = lax.broadcasted_iota(jnp.int32, (tm, 1), 0)
    lane = lax.broadcasted_iota(jnp.int32, (1, z.shape[1]), 1) % 4
    back1 = pltpu.roll(z, 1, 0)
    fwd1 = pltpu.roll(z, tm - 1, 0)
    if grid:
        col = row % GRID_W
        left = jnp.where(col == 0, 0.0, back1)
        right = jnp.where(col == GRID_W - 1, 0.0, fwd1)
        up = jnp.concatenate([prev_rows, z[:tm - GRID_W]], axis=0)
        down = jnp.concatenate([z[GRID_W:], next_rows], axis=0)
        return jnp.where(lane == 0, left, jnp.where(lane == 1, right, jnp.where(lane == 2, up, down)))
    prev = jnp.where(row == 0, 0.0, back1)
    nxt = jnp.where(row == tm - 1, 0.0, fwd1)
    return jnp.where(lane % 2 == 0, prev, nxt)


def _stack_masked(x, head0):
    return jnp.concatenate([jnp.where(head0, x, 0.0), jnp.where(head0, 0.0, x)], axis=0)


NT = (((1,), (1,)), ((), ()))
CUM_PIECES = 3
CHUNKS_PER_STEP = 2


def _split(x, pieces):
    out = []
    for i in range(pieces):
        hi = x.astype(BF16)
        out.append(hi)
        if i + 1 < pieces:
            x = x - hi.astype(F32)
    return out


def _bdot(a, b, dims=None):
    if dims is None:
        return jnp.dot(a, b, preferred_element_type=F32)
    return lax.dot_general(a, b, dims, preferred_element_type=F32)


HEADS_PER_SUM = 4


def _head_sums(x, ones_ref):
    w = HEADS_PER_SUM * HEAD_SIZE
    return jnp.concatenate([_bdot(x[:, i:i + w].astype(BF16), ones_ref[...]) for i in range(0, D_RWKV, w)], axis=1)


def _scan_kernel(*refs, grid, reverse, combine, n_tiles, tm):
    it = iter(refs)
    z_ref = next(it)
    zp_ref = next(it) if grid else None
    zn_ref = next(it) if grid else None
    h0_ref = next(it)
    if combine:
        yo_ref, bo_ref = next(it), next(it)
    mu_ref, kk_ref, ka_ref, rk_ref, w0_ref, a0_ref, w2_ref, a2_ref, ones_ref = (next(it) for _ in range(9))
    if combine:
        lnw_ref, lnb_ref, g2_ref = next(it), next(it), next(it)
    y_ref = next(it)
    b_ref = None if combine else next(it)
    hout_ref = next(it)
    r_s, v_s, al_s, be_s, kd_s, lw_s, y_s, g_s, h_s, qm_s, yn_s = (next(it) for _ in range(11))

    step = pl.program_id(1)
    tile = (n_tiles - 1 - step) if reverse else step

    @pl.when(step == 0)
    def _():
        h_s[...] = h0_ref[...]

    z = z_ref[...]
    if grid:
        prev_rows = jnp.where(tile == 0, 0.0, zp_ref[...])
        next_rows = jnp.where(tile == n_tiles - 1, 0.0, zn_ref[...])
    else:
        prev_rows = next_rows = None
    z = z + mu_ref[...] * (_token_shift(z, prev_rows, next_rows, grid) - z)
    r = z[:, 0:D_RWKV]
    k = z[:, D_RWKV:2 * D_RWKV]
    v = z[:, 2 * D_RWKV:3 * D_RWKV]
    kk = k * kk_ref[...]
    kk = kk / jnp.maximum(jnp.sqrt(_head_sums(kk * kk, ones_ref)), 1e-12)
    w_raw = w0_ref[...] + _bdot(jnp.tanh(z[:, OFF_WD:OFF_AD]).astype(BF16), w2_ref[...])
    a = _sigmoid(a0_ref[...] + _bdot(z[:, OFF_AD:OFF_GD].astype(BF16), a2_ref[...]))
    kd = k * (1.0 + (a - 1.0) * ka_ref[...])
    bonus = _head_sums(r * kd * rk_ref[...], ones_ref) * v
    if combine:
        y_ref[...] = bonus + bo_ref[...]
        g_s[...] = _bdot(_sigmoid(z[:, OFF_GD:]).astype(BF16), g2_ref[...])
    else:
        b_ref[...] = bonus
    r_s[...] = r
    v_s[...] = v
    al_s[...] = -kk
    be_s[...] = a * kk
    kd_s[...] = kd
    lw_s[...] = -DECAY_SCALE * _sigmoid(w_raw)

    n_chunks = tm // CHUNK
    ri = lax.broadcasted_iota(jnp.int32, (CHUNK, CHUNK), 0)
    ci = lax.broadcasted_iota(jnp.int32, (CHUNK, CHUNK), 1)
    tri = ((ci >= ri) if reverse else (ci <= ri)).astype(BF16)
    rj = lax.broadcasted_iota(jnp.int32, (PAIR, PAIR), 0)
    cj = lax.broadcasted_iota(jnp.int32, (PAIR, PAIR), 1)
    eye = rj == cj
    strict = (cj % CHUNK > rj % CHUNK) if reverse else (cj % CHUNK < rj % CHUNK)
    rj2 = lax.broadcasted_iota(jnp.int32, (PAIR, 2 * PAIR), 0) % CHUNK
    cj2 = lax.broadcasted_iota(jnp.int32, (PAIR, 2 * PAIR), 1) % CHUNK
    incl2 = (cj2 >= rj2) if reverse else (cj2 <= rj2)
    head0 = lax.broadcasted_iota(jnp.int32, (1, PAIR), 1) < HEAD_SIZE
    last = 0 if reverse else CHUNK - 1
    same_block = lambda n: (rj & -n) == (cj & -n)
    levels = [2 ** i for i in range(1, int(np.log2(CHUNK)))]
    base_mask = same_block(2)
    pair_masks = {n: same_block(2 * n) & ~same_block(n) for n in levels}

    def chunk_off(c):
        return pl.multiple_of(((n_chunks - 1 - c) if reverse else c) * CHUNK, CHUNK)

    def local_body(step_i, carry):
        units = []
        for cc in range(CHUNKS_PER_STEP):
            c = step_i * CHUNKS_PER_STEP + cc
            rows = pl.ds(chunk_off(c), CHUNK)
            lw = lw_s[rows, :]
            cum = sum(_bdot(tri, piece) for piece in _split(lw, CUM_PIECES))
            tot = cum[last:last + 1, :]
            g_inv = jnp.exp(-cum)
            g_end = jnp.exp(tot - cum)
            g_tot = jnp.exp(tot)
            abar = al_s[rows, :] * jnp.exp(cum - lw)
            rbar = r_s[rows, :] * jnp.exp(cum)
            be = be_s[rows, :]
            kdc = kd_s[rows, :]
            btil, ktil, bhat, khat = be * g_inv, kdc * g_inv, be * g_end, kdc * g_end
            vc = v_s[rows, :]
            for p in range(N_PAIRS):
                lanes = slice(p * PAIR, (p + 1) * PAIR)
                stack = lambda x: _stack_masked(x[:, lanes], head0)
                units.append(dict(c=c, p=p, a=stack(abar).astype(BF16), r=stack(rbar), v=stack(vc).astype(BF16),
                                  btk=jnp.concatenate([stack(btil), stack(ktil)], axis=0).astype(BF16),
                                  bkh=jnp.concatenate([stack(bhat), stack(khat)], axis=0),
                                  g_tot=g_tot[:, lanes]))
        for u in units:
            u["g"] = _bdot(jnp.concatenate([u["a"], u["r"].astype(BF16)], axis=0), u["btk"], NT)
        for u in units:
            g = u.pop("g")
            u["l_ab"] = jnp.where(strict, g[:PAIR, :PAIR], 0.0)
            u["l_ak"] = jnp.where(strict, g[:PAIR, PAIR:], 0.0).astype(BF16)
            u["l_rbk"] = jnp.where(incl2, g[PAIR:], 0.0).astype(BF16)
            u["t"] = jnp.where(eye, 1.0, jnp.where(base_mask, u["l_ab"], 0.0))
        for n in levels:
            for u in units:
                e = jnp.where(pair_masks[n], u["l_ab"], 0.0)
                u["et"] = _bdot(e.astype(BF16), u["t"].astype(BF16))
            for u in units:
                u["t"] = u["t"] + _bdot(u["t"].astype(BF16), u["et"].astype(BF16))
        for u in units:
            u["x1"] = _bdot(u["l_ak"], u["v"])
        for u in units:
            u["pu"] = _bdot(u["t"].astype(BF16), jnp.concatenate([u["a"], u["x1"].astype(BF16)], axis=1))
        for u in units:
            lhs = jnp.concatenate([u["l_rbk"], u["bkh"].T.astype(BF16)], axis=0)
            rhs = jnp.concatenate([u["pu"].astype(BF16),
                                   jnp.concatenate([jnp.zeros_like(u["v"]), u["v"]], axis=1)], axis=0)
            o = _bdot(lhs, rhs)
            q_m = u["r"] + o[:PAIR, :PAIR]
            m_mat = jnp.where(eye, u["g_tot"], 0.0) + o[PAIR:, :PAIR]
            qm_s[u["c"], u["p"]] = jnp.concatenate([q_m, m_mat], axis=0).astype(BF16)
            yn_s[u["c"], u["p"]] = o[:, PAIR:]
        return carry

    lax.fori_loop(0, n_chunks // CHUNKS_PER_STEP, local_body, 0)

    def state_body(c, carry):
        rows = pl.ds(chunk_off(c), CHUNK)
        for p in range(N_PAIRS):
            oh = _bdot(qm_s[c, p], h_s[p].astype(BF16)) + yn_s[c, p]
            y_s[rows, p * PAIR:(p + 1) * PAIR] = oh[:CHUNK] + oh[CHUNK:PAIR]
            h_s[p] = oh[PAIR:]
        return carry

    lax.fori_loop(0, n_chunks, state_body, 0)

    @pl.when(step == n_tiles - 1)
    def _():
        hout_ref[...] = h_s[...]

    if combine:
        y_sum = y_s[...] + yo_ref[...]
        mean = _head_sums(y_sum, ones_ref) * (1.0 / HEAD_SIZE)
        cen = y_sum - mean
        var = _head_sums(cen * cen, ones_ref) * (1.0 / HEAD_SIZE)
        y_gn = cen * lax.rsqrt(var + GN_EPS) * lnw_ref[...] + lnb_ref[...]
        y_ref[...] = (y_gn + y_ref[...]) * g_s[...]
    else:
        y_ref[...] = y_s[...]


def _scan(z, h0, prm, d, grid, other=None):
    b, s, _ = z.shape
    tm = 512 if grid else s
    n_tiles = s // tm
    reverse = d == 1
    combine = other is not None
    tile_of = (lambda j: n_tiles - 1 - j) if reverse else (lambda j: j)
    tok = lambda w: pl.BlockSpec((None, tm, w), lambda i, j: (i, tile_of(j), 0))
    halo_per_tile = tm // GRID_W
    n_halo = s // GRID_W
    in_specs = [tok(D_Z)]
    args = [z]
    if grid:
        in_specs += [pl.BlockSpec((None, GRID_W, D_Z),
                                  lambda i, j: (i, jnp.maximum(tile_of(j) * halo_per_tile - 1, 0), 0)),
                     pl.BlockSpec((None, GRID_W, D_Z),
                                  lambda i, j: (i, jnp.minimum((tile_of(j) + 1) * halo_per_tile, n_halo - 1), 0))]
        args += [z, z]
    state_spec = pl.BlockSpec((None, N_PAIRS, PAIR, PAIR), lambda i, j: (i, 0, 0, 0))
    in_specs.append(state_spec)
    args.append(h0)
    if combine:
        in_specs += [tok(D_RWKV), tok(D_RWKV)]
        args += list(other)
    row = lambda x: x.reshape(1, -1)
    small = [row(prm["mu"]), row(prm["k_k"]), row(prm["k_a"]), row(prm["r_k"]),
             row(prm["w0"][d]), row(prm["a0"][d]),
             jnp.zeros((2 * DECAY_LORA, D_RWKV), BF16).at[d * DECAY_LORA:(d + 1) * DECAY_LORA].set(
                 prm["w2"][d].astype(BF16)),
             jnp.zeros((2 * ICLR_LORA, D_RWKV), BF16).at[d * ICLR_LORA:(d + 1) * ICLR_LORA].set(
                 prm["a2"][d].astype(BF16)),
             jnp.asarray(np.kron(np.eye(HEADS_PER_SUM), np.ones((HEAD_SIZE, HEAD_SIZE))), BF16)]
    if combine:
        small += [row(prm["ln_w"]), row(prm["ln_b"]), prm["g2"].astype(BF16)]
    in_specs += [_const_spec(x.shape) for x in small]
    args += small
    out_specs = [tok(D_RWKV)]
    out_shape = [jax.ShapeDtypeStruct((b, s, D_RWKV), F32)]
    if not combine:
        out_specs.append(tok(D_RWKV))
        out_shape.append(jax.ShapeDtypeStruct((b, s, D_RWKV), F32))
    out_specs.append(state_spec)
    out_shape.append(jax.ShapeDtypeStruct((b, N_PAIRS, PAIR, PAIR), F32))
    n_chunks = tm // CHUNK
    scratch = ([pltpu.VMEM((tm, D_RWKV), F32) for _ in range(8)]
               + [pltpu.VMEM((N_PAIRS, PAIR, PAIR), F32),
                  pltpu.VMEM((n_chunks, N_PAIRS, 2 * PAIR, PAIR), BF16),
                  pltpu.VMEM((n_chunks, N_PAIRS, 2 * PAIR, PAIR), F32)])
    return pl.pallas_call(
        functools.partial(_scan_kernel, grid=grid, reverse=reverse, combine=combine, n_tiles=n_tiles, tm=tm),
        grid=(b, n_tiles),
        in_specs=in_specs,
        out_specs=out_specs,
        out_shape=out_shape,
        scratch_shapes=scratch,
        compiler_params=_params(("arbitrary", "arbitrary")),
        name="scan_bwd" if reverse else "scan_fwd",
    )(*args)


def _pack_state(s):
    b = s.shape[0]
    h = jnp.swapaxes(s, -1, -2).reshape(b, N_PAIRS, 2, HEAD_SIZE, HEAD_SIZE)
    zero = jnp.zeros_like(h[:, :, 0])
    top = jnp.concatenate([h[:, :, 0], zero], axis=-1)
    bot = jnp.concatenate([zero, h[:, :, 1]], axis=-1)
    return jnp.concatenate([top, bot], axis=-2)


def _unpack_state(hp):
    b = hp.shape[0]
    h0 = hp[:, :, :HEAD_SIZE, :HEAD_SIZE]
    h1 = hp[:, :, HEAD_SIZE:, HEAD_SIZE:]
    h = jnp.stack([h0, h1], axis=2).reshape(b, N_RWKV_HEADS, HEAD_SIZE, HEAD_SIZE)
    return jnp.swapaxes(h, -1, -2)


def _mixer_heads(z, s0_f, s0_b, prm, grid):
    y_b, bonus_b, h_b = _scan(z, _pack_state(s0_b), prm, 1, grid)
    y, h_f = _scan(z, _pack_state(s0_f), prm, 0, grid, other=(y_b, bonus_b))
    return y, _unpack_state(h_f), _unpack_state(h_b)


def _path(x, mod, mod_per_batch, s0_f, s0_b, grid, w, prm, tm):
    x1, u, z = _front(x, mod, mod_per_batch, w["norm_g"], w["g0"], w["u0"], w["d0"], w["w_in"], tm)
    y_four = _fourier_two_stage(u) if grid else _fourier_dense(u)
    y_rwkv, s_f, s_b = _mixer_heads(z, s0_f, s0_b, prm, grid)
    y = _back(x1, y_four, y_rwkv, mod, mod_per_batch, w["norm_g"], w["final"], w["wo_f"], w["wo_r"],
              w["g1"], w["u1"], w["d1"], tm)
    return y, s_f, s_b


def kernel(x_prompt, x_sample, state_fwd, state_bwd, c, c_ctx, w_mod, b_mod, norm_g, ffn_w_gate, ffn_w_up,
           ffn_w_down, w_in, shift_mu, decay_w0, decay_w2, iclr_a0, iclr_a2, gate_g2, k_k, k_a, r_k,
           ln_x_w, ln_x_b, w_out, final_norm):
    depth = w_mod.shape[0]
    assert depth == 1, "the back kernel applies the final norm, so exactly one layer is supported"
    bp = x_prompt.shape[0]
    bs = x_sample.shape[0]
    xp, xs = x_prompt, x_sample
    new_f, new_b = [], []
    for l in range(depth):
        cvec = jnp.zeros((8, D_MODEL), F32).at[:bs].set(c).at[bs].set(c_ctx)
        mod = _modulation(cvec, w_mod[l], b_mod[l]).reshape(8, N_MOD, D_MODEL)
        w = {"norm_g": norm_g[l], "final": final_norm,
             "g0": ffn_w_gate[l, 0].astype(BF16), "u0": ffn_w_up[l, 0].astype(BF16),
             "d0": ffn_w_down[l, 0].astype(BF16),
             "g1": ffn_w_gate[l, 1].astype(BF16), "u1": ffn_w_up[l, 1].astype(BF16),
             "d1": ffn_w_down[l, 1].astype(BF16),
             "w_in": w_in[l].astype(BF16),
             "wo_f": w_out[l, :D_FOURIER].astype(BF16), "wo_r": w_out[l, D_FOURIER:].astype(BF16)}
        prm = {"mu": shift_mu[l], "w0": decay_w0[l], "w2": decay_w2[l], "a0": iclr_a0[l], "a2": iclr_a2[l],
               "g2": gate_g2[l], "k_k": k_k[l], "k_a": k_a[l], "r_k": r_k[l], "ln_w": ln_x_w[l], "ln_b": ln_x_b[l]}
        zero_state = jnp.zeros((bp, N_RWKV_HEADS, HEAD_SIZE, HEAD_SIZE), F32)
        xp, s_f, s_b = _path(xp, mod[bs:bs + 1], False, zero_state, zero_state, False, w, prm, 256)
        new_f.append(s_f)
        new_b.append(s_b)
        xs, _, _ = _path(xs, mod[:bs], True, state_fwd[:, l], state_bwd[:, l], True, w, prm, 256)
    return xp, xs, jnp.stack(new_f, axis=1), jnp.stack(new_b, axis=1)
```

```python
import functools

import numpy as np
import jax
import jax.numpy as jnp
from jax import lax
from jax.experimental import pallas as pl
from jax.experimental.pallas import tpu as pltpu

D_MODEL = 1024
GRID_W = 64
D_FOURIER = 512
N_FOURIER_GROUPS = 8
FOURIER_GROUP = D_FOURIER // N_FOURIER_GROUPS
D_RWKV = D_MODEL - D_FOURIER
HEAD_SIZE = 64
N_RWKV_HEADS = D_RWKV // HEAD_SIZE
N_PAIRS = N_RWKV_HEADS // 2
PAIR = 2 * HEAD_SIZE
DECAY_LORA = 64
ICLR_LORA = 64
GATE_LORA = 128
D_Z = 3 * D_RWKV + 2 * DECAY_LORA + 2 * ICLR_LORA + GATE_LORA
D_FOLD = 2 * D_FOURIER
D_IN = D_FOLD + D_Z
D_FF = 2816
N_MOD = 9
RMS_EPS = 1e-6
GN_EPS = 64e-5
CHUNK = 64
OFF_WD = 3 * D_RWKV
OFF_AD = OFF_WD + 2 * DECAY_LORA
OFF_GD = OFF_AD + 2 * ICLR_LORA
DECAY_SCALE = float(np.exp(-0.5))
V7X_VMEM_LIMIT = 56 * 1024 * 1024
F32_SUBLANES = 8

F32 = jnp.float32
BF16 = jnp.bfloat16


def _dot32(a, b):
    return jnp.dot(a, b, precision=lax.Precision.HIGHEST, preferred_element_type=F32)


def _sigmoid(x):
    return 1.0 / (1.0 + jnp.exp(-x))


def _norm_mod(x, g, shift, scale):
    ms = jnp.mean(x * x, axis=-1, keepdims=True)
    return x * lax.rsqrt(ms + RMS_EPS) * g * (1.0 + scale) + shift


def _swiglu(h, wg_ref, wu_ref, wd_ref):
    hb = h.astype(BF16)
    gate = jnp.dot(hb, wg_ref[...], preferred_element_type=F32)
    up = jnp.dot(hb, wu_ref[...], preferred_element_type=F32)
    act = gate * _sigmoid(gate) * up
    return jnp.dot(act.astype(BF16), wd_ref[...], preferred_element_type=F32)


def _const_spec(shape):
    nd = len(shape)
    return pl.BlockSpec(shape, lambda *_: (0,) * nd, pipeline_mode=pl.Buffered(1))


def _params(sem):
    return pltpu.CompilerParams(dimension_semantics=sem, vmem_limit_bytes=V7X_VMEM_LIMIT)


def _mod_kernel(c_ref, w_ref, b_ref, o_ref):
    c = c_ref[...]
    o_ref[...] = _dot32(c * _sigmoid(c), w_ref[...]) + b_ref[...]


def _modulation(cvec, w_mod, b_mod):
    n = w_mod.shape[1]
    tn = n // 8
    return pl.pallas_call(
        _mod_kernel,
        grid=(n // tn,),
        in_specs=[pl.BlockSpec((8, D_MODEL), lambda j: (0, 0)),
                  pl.BlockSpec((D_MODEL, tn), lambda j: (0, j)),
                  pl.BlockSpec((1, tn), lambda j: (0, j))],
        out_specs=pl.BlockSpec((8, tn), lambda j: (0, j)),
        out_shape=jax.ShapeDtypeStruct((8, n), F32),
        compiler_params=_params(("arbitrary",)),
        name="modulation",
    )(cvec, w_mod, b_mod.reshape(1, n))


def _front_kernel(x_ref, mod_ref, g_ref, wg_ref, wu_ref, wd_ref, win_ref, x1_ref, u_ref, z_ref):
    x = x_ref[...]
    h = _norm_mod(x, g_ref[0:1, :], mod_ref[0:1, :], mod_ref[1:2, :])
    x1 = x + 0.5 * mod_ref[2:3, :] * _swiglu(h, wg_ref, wu_ref, wd_ref)
    x1_ref[...] = x1
    h2 = _norm_mod(x1, g_ref[1:2, :], mod_ref[3:4, :], mod_ref[4:5, :])
    proj = jnp.dot(h2.astype(BF16), win_ref[...], preferred_element_type=F32)
    u_ref[...] = proj[:, :D_FOLD]
    z_ref[...] = proj[:, D_FOLD:]


def _front(x, mod, mod_per_batch, norm_g, wg, wu, wd, w_in, tm):
    b, s, _ = x.shape
    nt = s // tm
    tok = lambda w: pl.BlockSpec((None, tm, w), lambda i, j: (i, j, 0))
    mod_map = (lambda i, j: (i, 0, 0)) if mod_per_batch else (lambda i, j: (0, 0, 0))
    return pl.pallas_call(
        _front_kernel,
        grid=(b, nt),
        in_specs=[tok(D_MODEL),
                  pl.BlockSpec((None, N_MOD, D_MODEL), mod_map),
                  _const_spec((3, D_MODEL)),
                  _const_spec((D_MODEL, D_FF)), _const_spec((D_MODEL, D_FF)),
                  _const_spec((D_FF, D_MODEL)), _const_spec((D_MODEL, D_IN))],
        out_specs=[tok(D_MODEL), tok(D_FOLD), tok(D_Z)],
        out_shape=[jax.ShapeDtypeStruct((b, s, D_MODEL), F32),
                   jax.ShapeDtypeStruct((b, s, D_FOLD), F32),
                   jax.ShapeDtypeStruct((b, s, D_Z), F32)],
        compiler_params=_params(("arbitrary", "arbitrary")),
        name="front",
    )(x, mod, norm_g, wg, wu, wd, w_in)


def _back_kernel(x1_ref, yf_ref, yr_ref, mod_ref, g_ref, fin_ref, wof_ref, wor_ref,
                 wg_ref, wu_ref, wd_ref, o_ref):
    mixed = (jnp.dot(yf_ref[...].astype(BF16), wof_ref[...], preferred_element_type=F32)
             + jnp.dot(yr_ref[...].astype(BF16), wor_ref[...], preferred_element_type=F32))
    x2 = x1_ref[...] + mod_ref[5:6, :] * mixed
    h = _norm_mod(x2, g_ref[2:3, :], mod_ref[6:7, :], mod_ref[7:8, :])
    x3 = x2 + 0.5 * mod_ref[8:9, :] * _swiglu(h, wg_ref, wu_ref, wd_ref)
    ms = jnp.mean(x3 * x3, axis=-1, keepdims=True)
    o_ref[...] = x3 * lax.rsqrt(ms + RMS_EPS) * fin_ref[...]


def _back(x1, yf, yr, mod, mod_per_batch, norm_g, final_norm, wo_f, wo_r, wg, wu, wd, tm):
    b, s, _ = x1.shape
    nt = s // tm
    tok = lambda w: pl.BlockSpec((None, tm, w), lambda i, j: (i, j, 0))
    mod_map = (lambda i, j: (i, 0, 0)) if mod_per_batch else (lambda i, j: (0, 0, 0))
    return pl.pallas_call(
        _back_kernel,
        grid=(b, nt),
        in_specs=[tok(D_MODEL), tok(D_FOURIER), tok(D_RWKV),
                  pl.BlockSpec((None, N_MOD, D_MODEL), mod_map),
                  _const_spec((3, D_MODEL)), _const_spec((1, D_MODEL)),
                  _const_spec((D_FOURIER, D_MODEL)), _const_spec((D_RWKV, D_MODEL)),
                  _const_spec((D_MODEL, D_FF)), _const_spec((D_MODEL, D_FF)),
                  _const_spec((D_FF, D_MODEL))],
        out_specs=tok(D_MODEL),
        out_shape=jax.ShapeDtypeStruct((b, s, D_MODEL), F32),
        compiler_params=_params(("arbitrary", "arbitrary")),
        name="back",
    )(x1, yf, yr, mod, norm_g, final_norm.reshape(1, D_MODEL), wo_f, wo_r, wg, wu, wd)


def _fold_kernel(w_ref, cs_ref, o_ref):
    o_ref[...] = _dot32(w_ref[...], cs_ref[...])


def _fold_group_dft(w_u):
    q = np.arange(FOURIER_GROUP)
    ang = 2.0 * np.pi * ((q[:, None] * q[None, :]) % FOURIER_GROUP) / FOURIER_GROUP
    eye = np.eye(N_FOURIER_GROUPS)
    cs = jnp.asarray(np.concatenate([np.kron(eye, np.cos(ang)), np.kron(eye, np.sin(ang))], axis=1), F32)
    return pl.pallas_call(
        _fold_kernel,
        out_shape=jax.ShapeDtypeStruct((D_MODEL, D_FOLD), F32),
        compiler_params=pltpu.CompilerParams(vmem_limit_bytes=V7X_VMEM_LIMIT),
        name="fold_group_dft",
    )(w_u, cs)


def _stack_cos_sin(x):
    return jnp.concatenate([x[:, :D_FOURIER], x[:, D_FOURIER:]], axis=0).astype(BF16)


def _fourier_dense_kernel(x_ref, p_ref, o_ref):
    o_ref[...] = _bdot(p_ref[...], _stack_cos_sin(x_ref[...]))


def _fourier_dense(xcs):
    b, s, _ = xcs.shape
    pos = np.arange(s)
    ang = 2.0 * np.pi * ((pos[:, None] * pos[None, :]) % s) / s
    table = np.concatenate([np.cos(ang), -np.sin(ang)], axis=1) / np.sqrt(s * FOURIER_GROUP)
    return pl.pallas_call(
        _fourier_dense_kernel,
        grid=(b,),
        in_specs=[pl.BlockSpec((None, s, D_FOLD), lambda i: (i, 0, 0)), _const_spec((s, 2 * s))],
        out_specs=pl.BlockSpec((None, s, D_FOURIER), lambda i: (i, 0, 0)),
        out_shape=jax.ShapeDtypeStruct((b, s, D_FOURIER), F32),
        compiler_params=_params(("arbitrary",)),
        name="fourier_dense",
    )(xcs, jnp.asarray(table, F32).astype(BF16))


FFT_ROWS = 16


def _fourier_stage1_kernel(x_ref, f_ref, z_ref):
    for j in range(FFT_ROWS):
        z = _bdot(f_ref[...], _stack_cos_sin(x_ref[:, j, :]))
        n = z.shape[0] // 2
        z_ref[:, j, :] = jnp.concatenate([z[:n], z[n:]], axis=1)


def _fourier_stage2_kernel(z_ref, g_ref, o_ref):
    for j in range(FFT_ROWS):
        o_ref[:, j, :] = _bdot(g_ref[j], _stack_cos_sin(z_ref[j]))


def _fourier_two_stage(xcs):
    b, s, _ = xcs.shape
    n = int(round(np.sqrt(s)))
    assert n * n == s and n % FFT_ROWS == 0
    idx = np.arange(n)
    ang = 2.0 * np.pi * ((idx[:, None] * idx[None, :]) % n) / n
    fc, fs = np.cos(ang), np.sin(ang)
    f2 = jnp.asarray(np.block([[fc, -fs], [fs, fc]]), F32).astype(BF16)
    z = pl.pallas_call(
        _fourier_stage1_kernel,
        grid=(b, n // FFT_ROWS),
        in_specs=[pl.BlockSpec((None, n, FFT_ROWS, D_FOLD), lambda i, j: (i, 0, j, 0)),
                  _const_spec((2 * n, 2 * n))],
        out_specs=pl.BlockSpec((None, n, FFT_ROWS, D_FOLD), lambda i, j: (i, 0, j, 0)),
        out_shape=jax.ShapeDtypeStruct((b, n, n, D_FOLD), F32),
        compiler_params=_params(("arbitrary", "arbitrary")),
        name="fourier_stage1",
    )(xcs.reshape(b, n, n, D_FOLD), f2)
    bb, aa, s1 = idx[:, None, None], idx[None, :, None], idx[None, None, :]
    ang2 = 2.0 * np.pi * ((s1 * (n * aa + bb)) % s) / s
    g2 = np.concatenate([np.cos(ang2), -np.sin(ang2)], axis=2) / np.sqrt(s * FOURIER_GROUP)
    out = pl.pallas_call(
        _fourier_stage2_kernel,
        grid=(b, n // FFT_ROWS),
        in_specs=[pl.BlockSpec((None, FFT_ROWS, n, D_FOLD), lambda i, j: (i, j, 0, 0)),
                  pl.BlockSpec((FFT_ROWS, n, 2 * n), lambda i, j: (j, 0, 0))],
        out_specs=pl.BlockSpec((None, n, FFT_ROWS, D_FOURIER), lambda i, j: (i, 0, j, 0)),
        out_shape=jax.ShapeDtypeStruct((b, n, n, D_FOURIER), F32),
        compiler_params=_params(("arbitrary", "arbitrary")),
        name="fourier_stage2",
    )(z, jnp.asarray(g2, F32).astype(BF16))
    return out.reshape(b, s, D_FOURIER)


def _token_shift(zf_ref, tm, pad, grid):
    lane = lax.broadcasted_iota(jnp.int32, (1, zf_ref.shape[1]), 1) % 4
    back1 = zf_ref[pad - 1:pad - 1 + tm, :]
    fwd1 = zf_ref[pad + 1:pad + 1 + tm, :]
    if grid:
        col = lax.broadcasted_iota(jnp.int32, (tm, 1), 0) % GRID_W
        left = jnp.where(col == 0, 0.0, back1)
        right = jnp.where(col == GRID_W - 1, 0.0, fwd1)
        up = zf_ref[pad - GRID_W:pad - GRID_W + tm, :]
        down = zf_ref[pad + GRID_W:pad + GRID_W + tm, :]
        return jnp.where(lane == 0, left, jnp.where(lane == 1, right, jnp.where(lane == 2, up, down)))
    return jnp.where(lane % 2 == 0, back1, fwd1)


def _stack_masked(x, head0):
    return jnp.concatenate([jnp.where(head0, x, 0.0), jnp.where(head0, 0.0, x)], axis=0)


NT = (((1,), (1,)), ((), ()))
CUM_PIECES = 3
CHUNKS_PER_STEP = 4


def _split(x, pieces):
    out = []
    for i in range(pieces):
        hi = x.astype(BF16)
        out.append(hi)
        if i + 1 < pieces:
            x = x - hi.astype(F32)
    return out


def _bdot(a, b, dims=None):
    if dims is None:
        return jnp.dot(a, b, preferred_element_type=F32)
    return lax.dot_general(a, b, dims, preferred_element_type=F32)


HEADS_PER_SUM = 4


def _head_sums(x, ones_ref):
    w = HEADS_PER_SUM * HEAD_SIZE
    return jnp.concatenate([_bdot(x[:, i:i + w].astype(BF16), ones_ref[...]) for i in range(0, D_RWKV, w)], axis=1)


def _scan_kernel(*refs, grid, reverse, combine, n_tiles, tm):
    it = iter(refs)
    z_ref = next(it)
    zp_ref = next(it) if grid else None
    zn_ref = next(it) if grid else None
    h0_ref = next(it)
    if combine:
        yo_ref, bo_ref = next(it), next(it)
    mu_ref, kk_ref, ka_ref, rk_ref, w0_ref, a0_ref, w2_ref, a2_ref, ones_ref = (next(it) for _ in range(9))
    if combine:
        lnw_ref, lnb_ref, g2_ref = next(it), next(it), next(it)
    y_ref = next(it)
    b_ref = None if combine else next(it)
    hout_ref = next(it)
    r_s, v_s, al_s, be_s, kd_s, lw_s, y_s, g_s, h_s, qm_s, yn_s, zf_s = (next(it) for _ in range(12))

    step = pl.program_id(1)
    tile = (n_tiles - 1 - step) if reverse else step

    @pl.when(step == 0)
    def _():
        h_s[...] = h0_ref[...]

    z = z_ref[...]
    pad = (zf_s.shape[0] - tm) // 2
    zf_s[pad:pad + tm, :] = z
    if grid:
        zf_s[0:pad, :] = jnp.where(tile == 0, 0.0, zp_ref[...])
        zf_s[pad + tm:, :] = jnp.where(tile == n_tiles - 1, 0.0, zn_ref[...])
    else:
        zf_s[0:pad, :] = jnp.zeros((pad, D_Z), F32)
        zf_s[pad + tm:, :] = jnp.zeros((pad, D_Z), F32)
    z = z + mu_ref[...] * (_token_shift(zf_s, tm, pad, grid) - z)
    r = z[:, 0:D_RWKV]
    k = z[:, D_RWKV:2 * D_RWKV]
    v = z[:, 2 * D_RWKV:3 * D_RWKV]
    kk = k * kk_ref[...]
    kk = kk * lax.rsqrt(jnp.maximum(_head_sums(kk * kk, ones_ref), 1e-24))
    w_raw = w0_ref[...] + _bdot(jnp.tanh(z[:, OFF_WD:OFF_AD]).astype(BF16), w2_ref[...])
    a = _sigmoid(a0_ref[...] + _bdot(z[:, OFF_AD:OFF_GD].astype(BF16), a2_ref[...]))
    kd = k * (1.0 + (a - 1.0) * ka_ref[...])
    bonus = _head_sums(r * kd * rk_ref[...], ones_ref) * v
    if combine:
        y_ref[...] = bonus + bo_ref[...]
        g_s[...] = _bdot(_sigmoid(z[:, OFF_GD:]).astype(BF16), g2_ref[...])
    else:
        b_ref[...] = bonus
    r_s[...] = r
    v_s[...] = v
    al_s[...] = -kk
    be_s[...] = a * kk
    kd_s[...] = kd
    lw_s[...] = -DECAY_SCALE * _sigmoid(w_raw)

    n_chunks = tm // CHUNK
    ri = lax.broadcasted_iota(jnp.int32, (CHUNK, CHUNK), 0)
    ci = lax.broadcasted_iota(jnp.int32, (CHUNK, CHUNK), 1)
    tri = ((ci >= ri) if reverse else (ci <= ri)).astype(BF16)
    rj = lax.broadcasted_iota(jnp.int32, (CHUNK, PAIR), 0)
    cj = lax.broadcasted_iota(jnp.int32, (CHUNK, PAIR), 1) % CHUNK
    eye = rj == cj
    strict = (cj > rj) if reverse else (cj < rj)
    incl = (cj >= rj) if reverse else (cj <= rj)
    head0 = lax.broadcasted_iota(jnp.int32, (1, PAIR), 1) < HEAD_SIZE
    last = 0 if reverse else CHUNK - 1
    blockdiag = lambda x: _stack_masked(x, head0)

    def head_transpose(x):
        t = blockdiag(x).T
        return t[:CHUNK] + t[CHUNK:]

    same_block = lambda n: (rj & -n) == (cj & -n)
    levels = [2 ** i for i in range(1, int(np.log2(CHUNK)))]
    base_mask = same_block(2)
    pair_masks = {n: same_block(2 * n) & ~same_block(n) for n in levels}

    def chunk_off(c):
        return pl.multiple_of(((n_chunks - 1 - c) if reverse else c) * CHUNK, CHUNK)

    def local_body(step_i, carry):
        units = []
        for cc in range(CHUNKS_PER_STEP):
            c = step_i * CHUNKS_PER_STEP + cc
            rows = pl.ds(chunk_off(c), CHUNK)
            lw = lw_s[rows, :]
            cum = sum(_bdot(tri, piece) for piece in _split(lw, CUM_PIECES))
            tot = cum[last:last + 1, :]
            g_inv = jnp.exp(-cum)
            g_end = jnp.exp(tot - cum)
            g_tot = jnp.exp(tot)
            abar = al_s[rows, :] * jnp.exp(cum - lw)
            rbar = r_s[rows, :] * jnp.exp(cum)
            be = be_s[rows, :]
            kdc = kd_s[rows, :]
            btil, ktil, bhat, khat = be * g_inv, kdc * g_inv, be * g_end, kdc * g_end
            vc = v_s[rows, :]
            for p in range(N_PAIRS):
                lanes = slice(p * PAIR, (p + 1) * PAIR)
                cut = lambda x: x[:, lanes]
                units.append(dict(c=c, p=p, a=cut(abar).astype(BF16), r=cut(rbar), v=cut(vc).astype(BF16),
                                  bt=cut(btil).astype(BF16), kt=cut(ktil).astype(BF16),
                                  bh=cut(bhat), kh=cut(khat), g_tot=cut(g_tot)))
        for u in units:
            ar = jnp.concatenate([u["a"], u["r"].astype(BF16)], axis=0)
            u["gb"] = _bdot(ar, blockdiag(u["bt"]), NT)
            u["gk"] = _bdot(ar, blockdiag(u["kt"]), NT)
        for u in units:
            gb, gk = u.pop("gb"), u.pop("gk")
            u["l_ab"] = jnp.where(strict, gb[:CHUNK], 0.0)
            u["l_ak"] = jnp.where(strict, gk[:CHUNK], 0.0).astype(BF16)
            u["l_rbk"] = jnp.concatenate([jnp.where(incl, gb[CHUNK:], 0.0),
                                          jnp.where(incl, gk[CHUNK:], 0.0)], axis=1).astype(BF16)
            u["t"] = jnp.where(eye, 1.0, jnp.where(base_mask, u["l_ab"], 0.0))
        for n in levels:
            for u in units:
                e = jnp.where(pair_masks[n], u["l_ab"], 0.0).astype(BF16)
                u["et"] = _bdot(e, blockdiag(u["t"].astype(BF16)))
            for u in units:
                u["t"] = u["t"] + _bdot(u["t"].astype(BF16), blockdiag(u["et"].astype(BF16)))
        for u in units:
            u["x1"] = _bdot(u["l_ak"], blockdiag(u["v"]))
        for u in units:
            rhs = jnp.concatenate([blockdiag(u["a"]), blockdiag(u["x1"].astype(BF16))], axis=1)
            u["pu"] = _bdot(u["t"].astype(BF16), rhs)
        for u in units:
            pu = u["pu"].astype(BF16)
            v_bd = blockdiag(u["v"])
            rhs = jnp.concatenate(
                [jnp.concatenate([blockdiag(pu[:, :PAIR]), blockdiag(pu[:, PAIR:])], axis=1),
                 jnp.concatenate([jnp.zeros_like(v_bd), v_bd], axis=1)], axis=0)
            bkh_t = jnp.concatenate([head_transpose(u["bh"]), head_transpose(u["kh"])], axis=1)
            o = _bdot(jnp.concatenate([u["l_rbk"], bkh_t.astype(BF16)], axis=0), rhs)
            q = u["r"] + o[:CHUNK, :PAIR]
            m = jnp.where(eye, u["g_tot"], 0.0) + o[CHUNK:, :PAIR]
            qm_s[u["c"], u["p"]] = jnp.concatenate([q, m], axis=0).astype(BF16)
            yn_s[u["c"], u["p"]] = o[:, PAIR:]
        return carry

    lax.fori_loop(0, n_chunks // CHUNKS_PER_STEP, local_body, 0)

    def state_body(c, carry):
        rows = pl.ds(chunk_off(c), CHUNK)
        for p in range(N_PAIRS):
            oh = _bdot(qm_s[c, p], blockdiag(h_s[p].astype(BF16))) + yn_s[c, p]
            y_s[rows, p * PAIR:(p + 1) * PAIR] = oh[:CHUNK]
            h_s[p] = oh[CHUNK:]
        return carry

    lax.fori_loop(0, n_chunks, state_body, 0)

    @pl.when(step == n_tiles - 1)
    def _():
        hout_ref[...] = h_s[...]

    if combine:
        y_sum = y_s[...] + yo_ref[...]
        mean = _head_sums(y_sum, ones_ref) * (1.0 / HEAD_SIZE)
        cen = y_sum - mean
        var = _head_sums(cen * cen, ones_ref) * (1.0 / HEAD_SIZE)
        y_gn = cen * lax.rsqrt(var + GN_EPS) * lnw_ref[...] + lnb_ref[...]
        y_ref[...] = (y_gn + y_ref[...]) * g_s[...]
    else:
        y_ref[...] = y_s[...]


def _scan(z, h0, prm, d, grid, other=None):
    b, s, _ = z.shape
    tm = 512 if grid else s
    n_tiles = s // tm
    reverse = d == 1
    combine = other is not None
    tile_of = (lambda j: n_tiles - 1 - j) if reverse else (lambda j: j)
    tok = lambda w: pl.BlockSpec((None, tm, w), lambda i, j: (i, tile_of(j), 0))
    halo_per_tile = tm // GRID_W
    n_halo = s // GRID_W
    in_specs = [tok(D_Z)]
    args = [z]
    if grid:
        in_specs += [pl.BlockSpec((None, GRID_W, D_Z),
                                  lambda i, j: (i, jnp.maximum(tile_of(j) * halo_per_tile - 1, 0), 0)),
                     pl.BlockSpec((None, GRID_W, D_Z),
                                  lambda i, j: (i, jnp.minimum((tile_of(j) + 1) * halo_per_tile, n_halo - 1), 0))]
        args += [z, z]
    state_spec = pl.BlockSpec((None, N_PAIRS, HEAD_SIZE, PAIR), lambda i, j: (i, 0, 0, 0))
    in_specs.append(state_spec)
    args.append(h0)
    if combine:
        in_specs += [tok(D_RWKV), tok(D_RWKV)]
        args += list(other)
    row = lambda x: x.reshape(1, -1)
    small = [row(prm["mu"]), row(prm["k_k"]), row(prm["k_a"]), row(prm["r_k"]),
             row(prm["w0"][d]), row(prm["a0"][d]),
             jnp.zeros((2 * DECAY_LORA, D_RWKV), BF16).at[d * DECAY_LORA:(d + 1) * DECAY_LORA].set(
                 prm["w2"][d].astype(BF16)),
             jnp.zeros((2 * ICLR_LORA, D_RWKV), BF16).at[d * ICLR_LORA:(d + 1) * ICLR_LORA].set(
                 prm["a2"][d].astype(BF16)),
             jnp.asarray(np.kron(np.eye(HEADS_PER_SUM), np.ones((HEAD_SIZE, HEAD_SIZE))), BF16)]
    if combine:
        small += [row(prm["ln_w"]), row(prm["ln_b"]), prm["g2"].astype(BF16)]
    in_specs += [_const_spec(x.shape) for x in small]
    args += small
    out_specs = [tok(D_RWKV)]
    out_shape = [jax.ShapeDtypeStruct((b, s, D_RWKV), F32)]
    if not combine:
        out_specs.append(tok(D_RWKV))
        out_shape.append(jax.ShapeDtypeStruct((b, s, D_RWKV), F32))
    out_specs.append(state_spec)
    out_shape.append(jax.ShapeDtypeStruct((b, N_PAIRS, HEAD_SIZE, PAIR), F32))
    n_chunks = tm // CHUNK
    scratch = ([pltpu.VMEM((tm, D_RWKV), F32) for _ in range(8)]
               + [pltpu.VMEM((N_PAIRS, CHUNK, PAIR), F32),
                  pltpu.VMEM((n_chunks, N_PAIRS, 2 * CHUNK, PAIR), BF16),
                  pltpu.VMEM((n_chunks, N_PAIRS, 2 * CHUNK, PAIR), F32),
                  pltpu.VMEM((tm + 2 * (GRID_W if grid else F32_SUBLANES), D_Z), F32)])
    return pl.pallas_call(
        functools.partial(_scan_kernel, grid=grid, reverse=reverse, combine=combine, n_tiles=n_tiles, tm=tm),
        grid=(b, n_tiles),
        in_specs=in_specs,
        out_specs=out_specs,
        out_shape=out_shape,
        scratch_shapes=scratch,
        compiler_params=_params(("arbitrary", "arbitrary")),
        name="scan_bwd" if reverse else "scan_fwd",
    )(*args)


def _pack_state(s):
    b = s.shape[0]
    h = jnp.swapaxes(s, -1, -2).reshape(b, N_PAIRS, 2, HEAD_SIZE, HEAD_SIZE)
    return jnp.transpose(h, (0, 1, 3, 2, 4)).reshape(b, N_PAIRS, HEAD_SIZE, PAIR)


def _unpack_state(hp):
    b = hp.shape[0]
    h = jnp.transpose(hp.reshape(b, N_PAIRS, HEAD_SIZE, 2, HEAD_SIZE), (0, 1, 3, 2, 4))
    return jnp.swapaxes(h.reshape(b, N_RWKV_HEADS, HEAD_SIZE, HEAD_SIZE), -1, -2)


def _mixer_heads(z, s0_f, s0_b, prm, grid):
    y_b, bonus_b, h_b = _scan(z, _pack_state(s0_b), prm, 1, grid)
    y, h_f = _scan(z, _pack_state(s0_f), prm, 0, grid, other=(y_b, bonus_b))
    return y, _unpack_state(h_f), _unpack_state(h_b)


def _path(x, mod, mod_per_batch, s0_f, s0_b, grid, w, prm, tm):
    x1, u, z = _front(x, mod, mod_per_batch, w["norm_g"], w["g0"], w["u0"], w["d0"], w["w_in"], tm)
    y_four = _fourier_two_stage(u) if grid else _fourier_dense(u)
    y_rwkv, s_f, s_b = _mixer_heads(z, s0_f, s0_b, prm, grid)
    y = _back(x1, y_four, y_rwkv, mod, mod_per_batch, w["norm_g"], w["final"], w["wo_f"], w["wo_r"],
              w["g1"], w["u1"], w["d1"], tm)
    return y, s_f, s_b


def kernel(x_prompt, x_sample, state_fwd, state_bwd, c, c_ctx, w_mod, b_mod, norm_g, ffn_w_gate, ffn_w_up,
           ffn_w_down, w_in, shift_mu, decay_w0, decay_w2, iclr_a0, iclr_a2, gate_g2, k_k, k_a, r_k,
           ln_x_w, ln_x_b, w_out, final_norm):
    depth = w_mod.shape[0]
    assert depth == 1, "the back kernel applies the final norm, so exactly one layer is supported"
    bp = x_prompt.shape[0]
    bs = x_sample.shape[0]
    xp, xs = x_prompt, x_sample
    new_f, new_b = [], []
    for l in range(depth):
        cvec = jnp.zeros((8, D_MODEL), F32).at[:bs].set(c).at[bs].set(c_ctx)
        mod = _modulation(cvec, w_mod[l], b_mod[l]).reshape(8, N_MOD, D_MODEL)
        w = {"norm_g": norm_g[l], "final": final_norm,
             "g0": ffn_w_gate[l, 0].astype(BF16), "u0": ffn_w_up[l, 0].astype(BF16),
             "d0": ffn_w_down[l, 0].astype(BF16),
             "g1": ffn_w_gate[l, 1].astype(BF16), "u1": ffn_w_up[l, 1].astype(BF16),
             "d1": ffn_w_down[l, 1].astype(BF16),
             "w_in": jnp.concatenate([_fold_group_dft(w_in[l, :, :D_FOURIER]).astype(BF16),
                                      w_in[l, :, D_FOURIER:].astype(BF16)], axis=1),
             "wo_f": w_out[l, :D_FOURIER].astype(BF16), "wo_r": w_out[l, D_FOURIER:].astype(BF16)}
        prm = {"mu": shift_mu[l], "w0": decay_w0[l], "w2": decay_w2[l], "a0": iclr_a0[l], "a2": iclr_a2[l],
               "g2": gate_g2[l], "k_k": k_k[l], "k_a": k_a[l], "r_k": r_k[l], "ln_w": ln_x_w[l], "ln_b": ln_x_b[l]}
        zero_state = jnp.zeros((bp, N_RWKV_HEADS, HEAD_SIZE, HEAD_SIZE), F32)
        xp, s_f, s_b = _path(xp, mod[bs:bs + 1], False, zero_state, zero_state, False, w, prm, 256)
        new_f.append(s_f)
        new_b.append(s_b)
        xs, _, _ = _path(xs, mod[:bs], True, state_fwd[:, l], state_bwd[:, l], True, w, prm, 256)
    return xp, xs, jnp.stack(new_f, axis=1), jnp.stack(new_b, axis=1)
```

```python
import functools

import numpy as np
import jax
import jax.numpy as jnp
from jax import lax
from jax.experimental import pallas as pl
from jax.experimental.pallas import tpu as pltpu

D_MODEL = 1024
GRID_W = 64
D_FOURIER = 512
N_FOURIER_GROUPS = 8
FOURIER_GROUP = D_FOURIER // N_FOURIER_GROUPS
D_RWKV = D_MODEL - D_FOURIER
HEAD_SIZE = 64
N_RWKV_HEADS = D_RWKV // HEAD_SIZE
N_PAIRS = N_RWKV_HEADS // 2
PAIR = 2 * HEAD_SIZE
DECAY_LORA = 64
ICLR_LORA = 64
GATE_LORA = 128
D_Z = 3 * D_RWKV + 2 * DECAY_LORA + 2 * ICLR_LORA + GATE_LORA
D_FOLD = 2 * D_FOURIER
D_IN = D_FOLD + D_Z
D_FF = 2816
N_MOD = 9
RMS_EPS = 1e-6
GN_EPS = 64e-5
CHUNK = 64
OFF_WD = 3 * D_RWKV
OFF_AD = OFF_WD + 2 * DECAY_LORA
OFF_GD = OFF_AD + 2 * ICLR_LORA
DECAY_SCALE = float(np.exp(-0.5))
V7X_VMEM_LIMIT = 56 * 1024 * 1024

F32 = jnp.float32
BF16 = jnp.bfloat16


def _dot32(a, b):
    return jnp.dot(a, b, precision=lax.Precision.HIGHEST, preferred_element_type=F32)


def _sigmoid(x):
    return 1.0 / (1.0 + jnp.exp(-x))


def _norm_mod(x, g, shift, scale):
    ms = jnp.mean(x * x, axis=-1, keepdims=True)
    return x * lax.rsqrt(ms + RMS_EPS) * g * (1.0 + scale) + shift


def _swiglu(h, wg_ref, wu_ref, wd_ref):
    hb = h.astype(BF16)
    gate = jnp.dot(hb, wg_ref[...], preferred_element_type=F32)
    up = jnp.dot(hb, wu_ref[...], preferred_element_type=F32)
    act = gate * _sigmoid(gate) * up
    return jnp.dot(act.astype(BF16), wd_ref[...], preferred_element_type=F32)


def _const_spec(shape):
    nd = len(shape)
    return pl.BlockSpec(shape, lambda *_: (0,) * nd, pipeline_mode=pl.Buffered(1))


def _params(sem):
    return pltpu.CompilerParams(dimension_semantics=sem, vmem_limit_bytes=V7X_VMEM_LIMIT)


def _mod_kernel(c_ref, w_ref, b_ref, o_ref):
    c = c_ref[...]
    o_ref[...] = _dot32(c * _sigmoid(c), w_ref[...]) + b_ref[...]


def _modulation(cvec, w_mod, b_mod):
    n = w_mod.shape[1]
    tn = n // 8
    return pl.pallas_call(
        _mod_kernel,
        grid=(n // tn,),
        in_specs=[pl.BlockSpec((8, D_MODEL), lambda j: (0, 0)),
                  pl.BlockSpec((D_MODEL, tn), lambda j: (0, j)),
                  pl.BlockSpec((1, tn), lambda j: (0, j))],
        out_specs=pl.BlockSpec((8, tn), lambda j: (0, j)),
        out_shape=jax.ShapeDtypeStruct((8, n), F32),
        compiler_params=_params(("arbitrary",)),
        name="modulation",
    )(cvec, w_mod, b_mod.reshape(1, n))


def _shift_adjacent(z, mu, grid):
    tm = z.shape[0]
    row = lax.broadcasted_iota(jnp.int32, (tm, 1), 0)
    lane = lax.broadcasted_iota(jnp.int32, (1, z.shape[1]), 1) % 4
    back1 = pltpu.roll(z, 1, 0)
    fwd1 = pltpu.roll(z, tm - 1, 0)
    if grid:
        col = row % GRID_W
        side = jnp.where(lane == 0, jnp.where(col == 0, 0.0, back1), jnp.where(col == GRID_W - 1, 0.0, fwd1))
        return jnp.where(lane < 2, z + mu * (side - z), z)
    side = jnp.where(lane % 2 == 0, jnp.where(row == 0, 0.0, back1), jnp.where(row == tm - 1, 0.0, fwd1))
    return z + mu * (side - z)


def _front_kernel(x_ref, mod_ref, g_ref, wg_ref, wu_ref, wd_ref, win_ref, mu_ref, x1_ref, u_ref, z_ref, *, grid):
    x = x_ref[...]
    h = _norm_mod(x, g_ref[0:1, :], mod_ref[0:1, :], mod_ref[1:2, :])
    x1 = x + 0.5 * mod_ref[2:3, :] * _swiglu(h, wg_ref, wu_ref, wd_ref)
    x1_ref[...] = x1
    h2 = _norm_mod(x1, g_ref[1:2, :], mod_ref[3:4, :], mod_ref[4:5, :])
    proj = jnp.dot(h2.astype(BF16), win_ref[...], preferred_element_type=F32)
    u_ref[...] = proj[:, :D_FOLD]
    z_ref[...] = _shift_adjacent(proj[:, D_FOLD:], mu_ref[...], grid)


def _front(x, mod, mod_per_batch, norm_g, wg, wu, wd, w_in, mu, tm, grid):
    b, s, _ = x.shape
    nt = s // tm
    assert tm % GRID_W == 0 if grid else tm == s
    tok = lambda w: pl.BlockSpec((None, tm, w), lambda i, j: (i, j, 0))
    mod_map = (lambda i, j: (i, 0, 0)) if mod_per_batch else (lambda i, j: (0, 0, 0))
    return pl.pallas_call(
        functools.partial(_front_kernel, grid=grid),
        grid=(b, nt),
        in_specs=[tok(D_MODEL),
                  pl.BlockSpec((None, N_MOD, D_MODEL), mod_map),
                  _const_spec((3, D_MODEL)),
                  _const_spec((D_MODEL, D_FF)), _const_spec((D_MODEL, D_FF)),
                  _const_spec((D_FF, D_MODEL)), _const_spec((D_MODEL, D_IN)), _const_spec((1, D_Z))],
        out_specs=[tok(D_MODEL), tok(D_FOLD), tok(D_Z)],
        out_shape=[jax.ShapeDtypeStruct((b, s, D_MODEL), F32),
                   jax.ShapeDtypeStruct((b, s, D_FOLD), F32),
                   jax.ShapeDtypeStruct((b, s, D_Z), F32)],
        compiler_params=_params(("arbitrary", "arbitrary")),
        name="front",
    )(x, mod, norm_g, wg, wu, wd, w_in, mu.reshape(1, D_Z))


def _back_kernel(x1_ref, yf_ref, yr_ref, mod_ref, g_ref, fin_ref, wof_ref, wor_ref,
                 wg_ref, wu_ref, wd_ref, o_ref):
    mixed = (jnp.dot(yf_ref[...].astype(BF16), wof_ref[...], preferred_element_type=F32)
             + jnp.dot(yr_ref[...].astype(BF16), wor_ref[...], preferred_element_type=F32))
    x2 = x1_ref[...] + mod_ref[5:6, :] * mixed
    h = _norm_mod(x2, g_ref[2:3, :], mod_ref[6:7, :], mod_ref[7:8, :])
    x3 = x2 + 0.5 * mod_ref[8:9, :] * _swiglu(h, wg_ref, wu_ref, wd_ref)
    ms = jnp.mean(x3 * x3, axis=-1, keepdims=True)
    o_ref[...] = x3 * lax.rsqrt(ms + RMS_EPS) * fin_ref[...]


def _back(x1, yf, yr, mod, mod_per_batch, norm_g, final_norm, wo_f, wo_r, wg, wu, wd, tm):
    b, s, _ = x1.shape
    nt = s // tm
    tok = lambda w: pl.BlockSpec((None, tm, w), lambda i, j: (i, j, 0))
    mod_map = (lambda i, j: (i, 0, 0)) if mod_per_batch else (lambda i, j: (0, 0, 0))
    return pl.pallas_call(
        _back_kernel,
        grid=(b, nt),
        in_specs=[tok(D_MODEL), tok(D_FOURIER), tok(D_RWKV),
                  pl.BlockSpec((None, N_MOD, D_MODEL), mod_map),
                  _const_spec((3, D_MODEL)), _const_spec((1, D_MODEL)),
                  _const_spec((D_FOURIER, D_MODEL)), _const_spec((D_RWKV, D_MODEL)),
                  _const_spec((D_MODEL, D_FF)), _const_spec((D_MODEL, D_FF)),
                  _const_spec((D_FF, D_MODEL))],
        out_specs=tok(D_MODEL),
        out_shape=jax.ShapeDtypeStruct((b, s, D_MODEL), F32),
        compiler_params=_params(("arbitrary", "arbitrary")),
        name="back",
    )(x1, yf, yr, mod, norm_g, final_norm.reshape(1, D_MODEL), wo_f, wo_r, wg, wu, wd)


def _fold_kernel(w_ref, cs_ref, o_ref):
    o_ref[...] = _dot32(w_ref[...], cs_ref[...])


def _fold_group_dft(w_u):
    q = np.arange(FOURIER_GROUP)
    ang = 2.0 * np.pi * ((q[:, None] * q[None, :]) % FOURIER_GROUP) / FOURIER_GROUP
    eye = np.eye(N_FOURIER_GROUPS)
    cs = jnp.asarray(np.concatenate([np.kron(eye, np.cos(ang)), np.kron(eye, np.sin(ang))], axis=1), F32)
    return pl.pallas_call(
        _fold_kernel,
        out_shape=jax.ShapeDtypeStruct((D_MODEL, D_FOLD), F32),
        compiler_params=pltpu.CompilerParams(vmem_limit_bytes=V7X_VMEM_LIMIT),
        name="fold_group_dft",
    )(w_u, cs)


def _stack_cos_sin(x):
    return jnp.concatenate([x[:, :D_FOURIER], x[:, D_FOURIER:]], axis=0).astype(BF16)


def _fourier_dense_kernel(x_ref, p_ref, o_ref):
    o_ref[...] = _bdot(p_ref[...], _stack_cos_sin(x_ref[...]))


def _fourier_dense(xcs):
    b, s, _ = xcs.shape
    pos = np.arange(s)
    ang = 2.0 * np.pi * ((pos[:, None] * pos[None, :]) % s) / s
    table = np.concatenate([np.cos(ang), -np.sin(ang)], axis=1) / np.sqrt(s * FOURIER_GROUP)
    return pl.pallas_call(
        _fourier_dense_kernel,
        grid=(b,),
        in_specs=[pl.BlockSpec((None, s, D_FOLD), lambda i: (i, 0, 0)), _const_spec((s, 2 * s))],
        out_specs=pl.BlockSpec((None, s, D_FOURIER), lambda i: (i, 0, 0)),
        out_shape=jax.ShapeDtypeStruct((b, s, D_FOURIER), F32),
        compiler_params=_params(("arbitrary",)),
        name="fourier_dense",
    )(xcs, jnp.asarray(table, F32).astype(BF16))


FFT_ROWS = 16


def _fourier_stage1_kernel(x_ref, f_ref, z_ref):
    for j in range(FFT_ROWS):
        z = _bdot(f_ref[...], _stack_cos_sin(x_ref[:, j, :]))
        n = z.shape[0] // 2
        z_ref[:, j, :] = jnp.concatenate([z[:n], z[n:]], axis=1)


def _fourier_stage2_kernel(z_ref, g_ref, o_ref):
    for j in range(FFT_ROWS):
        o_ref[:, j, :] = _bdot(g_ref[j], _stack_cos_sin(z_ref[j]))


def _fourier_two_stage(xcs):
    b, s, _ = xcs.shape
    n = int(round(np.sqrt(s)))
    assert n * n == s and n % FFT_ROWS == 0
    idx = np.arange(n)
    ang = 2.0 * np.pi * ((idx[:, None] * idx[None, :]) % n) / n
    fc, fs = np.cos(ang), np.sin(ang)
    f2 = jnp.asarray(np.block([[fc, -fs], [fs, fc]]), F32).astype(BF16)
    z = pl.pallas_call(
        _fourier_stage1_kernel,
        grid=(b, n // FFT_ROWS),
        in_specs=[pl.BlockSpec((None, n, FFT_ROWS, D_FOLD), lambda i, j: (i, 0, j, 0)),
                  _const_spec((2 * n, 2 * n))],
        out_specs=pl.BlockSpec((None, n, FFT_ROWS, D_FOLD), lambda i, j: (i, 0, j, 0)),
        out_shape=jax.ShapeDtypeStruct((b, n, n, D_FOLD), F32),
        compiler_params=_params(("arbitrary", "arbitrary")),
        name="fourier_stage1",
    )(xcs.reshape(b, n, n, D_FOLD), f2)
    bb, aa, s1 = idx[:, None, None], idx[None, :, None], idx[None, None, :]
    ang2 = 2.0 * np.pi * ((s1 * (n * aa + bb)) % s) / s
    g2 = np.concatenate([np.cos(ang2), -np.sin(ang2)], axis=2) / np.sqrt(s * FOURIER_GROUP)
    out = pl.pallas_call(
        _fourier_stage2_kernel,
        grid=(b, n // FFT_ROWS),
        in_specs=[pl.BlockSpec((None, FFT_ROWS, n, D_FOLD), lambda i, j: (i, j, 0, 0)),
                  pl.BlockSpec((FFT_ROWS, n, 2 * n), lambda i, j: (j, 0, 0))],
        out_specs=pl.BlockSpec((None, n, FFT_ROWS, D_FOURIER), lambda i, j: (i, 0, j, 0)),
        out_shape=jax.ShapeDtypeStruct((b, n, n, D_FOURIER), F32),
        compiler_params=_params(("arbitrary", "arbitrary")),
        name="fourier_stage2",
    )(z, jnp.asarray(g2, F32).astype(BF16))
    return out.reshape(b, s, D_FOURIER)


def _shift_rows(z, zf_ref, mu):
    tm = z.shape[0]
    lane = lax.broadcasted_iota(jnp.int32, (1, z.shape[1]), 1) % 4
    up = zf_ref[0:tm, :]
    down = zf_ref[2 * GRID_W:2 * GRID_W + tm, :]
    return jnp.where(lane < 2, z, z + mu * (jnp.where(lane == 2, up, down) - z))


def _stack_masked(x, head0):
    return jnp.concatenate([jnp.where(head0, x, 0.0), jnp.where(head0, 0.0, x)], axis=0)


NT = (((1,), (1,)), ((), ()))
CUM_PIECES = 3
CHUNKS_PER_STEP = 4


def _split(x, pieces):
    out = []
    for i in range(pieces):
        hi = x.astype(BF16)
        out.append(hi)
        if i + 1 < pieces:
            x = x - hi.astype(F32)
    return out


def _bdot(a, b, dims=None):
    if dims is None:
        return jnp.dot(a, b, preferred_element_type=F32)
    return lax.dot_general(a, b, dims, preferred_element_type=F32)


HEADS_PER_SUM = 4


def _head_sums(x, ones_ref):
    w = HEADS_PER_SUM * HEAD_SIZE
    return jnp.concatenate([_bdot(x[:, i:i + w].astype(BF16), ones_ref[...]) for i in range(0, D_RWKV, w)], axis=1)


def _scan_kernel(*refs, grid, reverse, combine, n_tiles, tm):
    it = iter(refs)
    z_ref = next(it)
    zp_ref = next(it) if grid else None
    zn_ref = next(it) if grid else None
    h0_ref = next(it)
    if combine:
        yo_ref, bo_ref = next(it), next(it)
    mu_ref, kk_ref, ka_ref, rk_ref, w0_ref, a0_ref, w2_ref, a2_ref, ones_ref = (next(it) for _ in range(9))
    if combine:
        lnw_ref, lnb_ref, g2_ref = next(it), next(it), next(it)
    y_ref = next(it)
    b_ref = None if combine else next(it)
    hout_ref = next(it)
    r_s, v_s, al_s, be_s, kd_s, lw_s, y_s, g_s, h_s, qm_s, yn_s = (next(it) for _ in range(11))
    zf_s = next(it) if grid else None

    step = pl.program_id(1)
    tile = (n_tiles - 1 - step) if reverse else step

    @pl.when(step == 0)
    def _():
        h_s[...] = h0_ref[...]

    z = z_ref[...]
    if grid:
        zf_s[GRID_W:GRID_W + tm, :] = z
        zf_s[0:GRID_W, :] = jnp.where(tile == 0, 0.0, zp_ref[...])
        zf_s[GRID_W + tm:, :] = jnp.where(tile == n_tiles - 1, 0.0, zn_ref[...])
        z = _shift_rows(z, zf_s, mu_ref[...])
    r = z[:, 0:D_RWKV]
    k = z[:, D_RWKV:2 * D_RWKV]
    v = z[:, 2 * D_RWKV:3 * D_RWKV]
    kk = k * kk_ref[...]
    kk = kk * lax.rsqrt(jnp.maximum(_head_sums(kk * kk, ones_ref), 1e-24))
    w_raw = w0_ref[...] + _bdot(jnp.tanh(z[:, OFF_WD:OFF_AD]).astype(BF16), w2_ref[...])
    a = _sigmoid(a0_ref[...] + _bdot(z[:, OFF_AD:OFF_GD].astype(BF16), a2_ref[...]))
    kd = k * (1.0 + (a - 1.0) * ka_ref[...])
    bonus = _head_sums(r * kd * rk_ref[...], ones_ref) * v
    if combine:
        y_ref[...] = bonus + bo_ref[...]
        g_s[...] = _bdot(_sigmoid(z[:, OFF_GD:]).astype(BF16), g2_ref[...])
    else:
        b_ref[...] = bonus
    r_s[...] = r
    v_s[...] = v
    al_s[...] = -kk
    be_s[...] = a * kk
    kd_s[...] = kd
    lw_s[...] = -DECAY_SCALE * _sigmoid(w_raw)

    n_chunks = tm // CHUNK
    ri = lax.broadcasted_iota(jnp.int32, (CHUNK, CHUNK), 0)
    ci = lax.broadcasted_iota(jnp.int32, (CHUNK, CHUNK), 1)
    tri = ((ci >= ri) if reverse else (ci <= ri)).astype(BF16)
    rj = lax.broadcasted_iota(jnp.int32, (CHUNK, PAIR), 0)
    cj = lax.broadcasted_iota(jnp.int32, (CHUNK, PAIR), 1) % CHUNK
    eye = rj == cj
    strict = (cj > rj) if reverse else (cj < rj)
    incl = (cj >= rj) if reverse else (cj <= rj)
    head0 = lax.broadcasted_iota(jnp.int32, (1, PAIR), 1) < HEAD_SIZE
    last = 0 if reverse else CHUNK - 1
    blockdiag = lambda x: _stack_masked(x, head0)

    def head_transpose(x):
        t = blockdiag(x).T
        return t[:CHUNK] + t[CHUNK:]

    same_block = lambda n: (rj & -n) == (cj & -n)
    levels = [2 ** i for i in range(1, int(np.log2(CHUNK)))]
    base_mask = same_block(2)
    pair_masks = {n: same_block(2 * n) & ~same_block(n) for n in levels}

    def chunk_off(c):
        return pl.multiple_of(((n_chunks - 1 - c) if reverse else c) * CHUNK, CHUNK)

    def local_body(step_i, carry):
        units = []
        for cc in range(CHUNKS_PER_STEP):
            c = step_i * CHUNKS_PER_STEP + cc
            rows = pl.ds(chunk_off(c), CHUNK)
            lw = lw_s[rows, :]
            cum = sum(_bdot(tri, piece) for piece in _split(lw, CUM_PIECES))
            tot = cum[last:last + 1, :]
            g_inv = jnp.exp(-cum)
            g_end = jnp.exp(tot - cum)
            g_tot = jnp.exp(tot)
            abar = al_s[rows, :] * jnp.exp(cum - lw)
            rbar = r_s[rows, :] * jnp.exp(cum)
            be = be_s[rows, :]
            kdc = kd_s[rows, :]
            btil, ktil, bhat, khat = be * g_inv, kdc * g_inv, be * g_end, kdc * g_end
            vc = v_s[rows, :]
            for p in range(N_PAIRS):
                lanes = slice(p * PAIR, (p + 1) * PAIR)
                cut = lambda x: x[:, lanes]
                units.append(dict(c=c, p=p, a=cut(abar).astype(BF16), r=cut(rbar), v=cut(vc).astype(BF16),
                                  bt=cut(btil).astype(BF16), kt=cut(ktil).astype(BF16),
                                  bh=cut(bhat), kh=cut(khat), g_tot=cut(g_tot)))
        for u in units:
            ar = jnp.concatenate([u["a"], u["r"].astype(BF16)], axis=0)
            u["gb"] = _bdot(ar, blockdiag(u["bt"]), NT)
            u["gk"] = _bdot(ar, blockdiag(u["kt"]), NT)
        for u in units:
            gb, gk = u.pop("gb"), u.pop("gk")
            u["l_ab"] = jnp.where(strict, gb[:CHUNK], 0.0)
            u["l_ak"] = jnp.where(strict, gk[:CHUNK], 0.0).astype(BF16)
            u["l_rbk"] = jnp.concatenate([jnp.where(incl, gb[CHUNK:], 0.0),
                                          jnp.where(incl, gk[CHUNK:], 0.0)], axis=1).astype(BF16)
            u["t"] = jnp.where(eye, 1.0, jnp.where(base_mask, u["l_ab"], 0.0))
        for n in levels:
            for u in units:
                e = jnp.where(pair_masks[n], u["l_ab"], 0.0).astype(BF16)
                u["et"] = _bdot(e, blockdiag(u["t"].astype(BF16)))
            for u in units:
                u["t"] = u["t"] + _bdot(u["t"].astype(BF16), blockdiag(u["et"].astype(BF16)))
        for u in units:
            u["x1"] = _bdot(u["l_ak"], blockdiag(u["v"]))
        for u in units:
            rhs = jnp.concatenate([blockdiag(u["a"]), blockdiag(u["x1"].astype(BF16))], axis=1)
            u["pu"] = _bdot(u["t"].astype(BF16), rhs)
        for u in units:
            pu = u["pu"].astype(BF16)
            v_bd = blockdiag(u["v"])
            rhs = jnp.concatenate(
                [jnp.concatenate([blockdiag(pu[:, :PAIR]), blockdiag(pu[:, PAIR:])], axis=1),
                 jnp.concatenate([jnp.zeros_like(v_bd), v_bd], axis=1)], axis=0)
            bkh_t = jnp.concatenate([head_transpose(u["bh"]), head_transpose(u["kh"])], axis=1)
            o = _bdot(jnp.concatenate([u["l_rbk"], bkh_t.astype(BF16)], axis=0), rhs)
            q = u["r"] + o[:CHUNK, :PAIR]
            m = jnp.where(eye, u["g_tot"], 0.0) + o[CHUNK:, :PAIR]
            qm_s[u["c"], u["p"]] = jnp.concatenate([q, m], axis=0).astype(BF16)
            yn_s[u["c"], u["p"]] = o[:, PAIR:]
        return carry

    lax.fori_loop(0, n_chunks // CHUNKS_PER_STEP, local_body, 0)

    def state_body(c, carry):
        rows = pl.ds(chunk_off(c), CHUNK)
        for p in range(N_PAIRS):
            oh = _bdot(qm_s[c, p], blockdiag(h_s[p].astype(BF16))) + yn_s[c, p]
            y_s[rows, p * PAIR:(p + 1) * PAIR] = oh[:CHUNK]
            h_s[p] = oh[CHUNK:]
        return carry

    lax.fori_loop(0, n_chunks, state_body, 0)

    @pl.when(step == n_tiles - 1)
    def _():
        hout_ref[...] = h_s[...]

    if combine:
        y_sum = y_s[...] + yo_ref[...]
        mean = _head_sums(y_sum, ones_ref) * (1.0 / HEAD_SIZE)
        cen = y_sum - mean
        var = _head_sums(cen * cen, ones_ref) * (1.0 / HEAD_SIZE)
        y_gn = cen * lax.rsqrt(var + GN_EPS) * lnw_ref[...] + lnb_ref[...]
        y_ref[...] = (y_gn + y_ref[...]) * g_s[...]
    else:
        y_ref[...] = y_s[...]


def _scan(z, h0, prm, d, grid, other=None):
    b, s, _ = z.shape
    tm = 512 if grid else s
    n_tiles = s // tm
    reverse = d == 1
    combine = other is not None
    tile_of = (lambda j: n_tiles - 1 - j) if reverse else (lambda j: j)
    tok = lambda w: pl.BlockSpec((None, tm, w), lambda i, j: (i, tile_of(j), 0))
    halo_per_tile = tm // GRID_W
    n_halo = s // GRID_W
    in_specs = [tok(D_Z)]
    args = [z]
    if grid:
        in_specs += [pl.BlockSpec((None, GRID_W, D_Z),
                                  lambda i, j: (i, jnp.maximum(tile_of(j) * halo_per_tile - 1, 0), 0)),
                     pl.BlockSpec((None, GRID_W, D_Z),
                                  lambda i, j: (i, jnp.minimum((tile_of(j) + 1) * halo_per_tile, n_halo - 1), 0))]
        args += [z, z]
    state_spec = pl.BlockSpec((None, N_PAIRS, HEAD_SIZE, PAIR), lambda i, j: (i, 0, 0, 0))
    in_specs.append(state_spec)
    args.append(h0)
    if combine:
        in_specs += [tok(D_RWKV), tok(D_RWKV)]
        args += list(other)
    row = lambda x: x.reshape(1, -1)
    small = [row(prm["mu"]), row(prm["k_k"]), row(prm["k_a"]), row(prm["r_k"]),
             row(prm["w0"][d]), row(prm["a0"][d]),
             jnp.zeros((2 * DECAY_LORA, D_RWKV), BF16).at[d * DECAY_LORA:(d + 1) * DECAY_LORA].set(
                 prm["w2"][d].astype(BF16)),
             jnp.zeros((2 * ICLR_LORA, D_RWKV), BF16).at[d * ICLR_LORA:(d + 1) * ICLR_LORA].set(
                 prm["a2"][d].astype(BF16)),
             jnp.asarray(np.kron(np.eye(HEADS_PER_SUM), np.ones((HEAD_SIZE, HEAD_SIZE))), BF16)]
    if combine:
        small += [row(prm["ln_w"]), row(prm["ln_b"]), prm["g2"].astype(BF16)]
    in_specs += [_const_spec(x.shape) for x in small]
    args += small
    out_specs = [tok(D_RWKV)]
    out_shape = [jax.ShapeDtypeStruct((b, s, D_RWKV), F32)]
    if not combine:
        out_specs.append(tok(D_RWKV))
        out_shape.append(jax.ShapeDtypeStruct((b, s, D_RWKV), F32))
    out_specs.append(state_spec)
    out_shape.append(jax.ShapeDtypeStruct((b, N_PAIRS, HEAD_SIZE, PAIR), F32))
    n_chunks = tm // CHUNK
    scratch = ([pltpu.VMEM((tm, D_RWKV), F32) for _ in range(8)]
               + [pltpu.VMEM((N_PAIRS, CHUNK, PAIR), F32),
                  pltpu.VMEM((n_chunks, N_PAIRS, 2 * CHUNK, PAIR), BF16),
                  pltpu.VMEM((n_chunks, N_PAIRS, 2 * CHUNK, PAIR), F32)]
               + ([pltpu.VMEM((tm + 2 * GRID_W, D_Z), F32)] if grid else []))
    return pl.pallas_call(
        functools.partial(_scan_kernel, grid=grid, reverse=reverse, combine=combine, n_tiles=n_tiles, tm=tm),
        grid=(b, n_tiles),
        in_specs=in_specs,
        out_specs=out_specs,
        out_shape=out_shape,
        scratch_shapes=scratch,
        compiler_params=_params(("arbitrary", "arbitrary")),
        name="scan_bwd" if reverse else "scan_fwd",
    )(*args)


def _pack_state(s):
    b = s.shape[0]
    h = jnp.swapaxes(s, -1, -2).reshape(b, N_PAIRS, 2, HEAD_SIZE, HEAD_SIZE)
    return jnp.transpose(h, (0, 1, 3, 2, 4)).reshape(b, N_PAIRS, HEAD_SIZE, PAIR)


def _unpack_state(hp):
    b = hp.shape[0]
    h = jnp.transpose(hp.reshape(b, N_PAIRS, HEAD_SIZE, 2, HEAD_SIZE), (0, 1, 3, 2, 4))
    return jnp.swapaxes(h.reshape(b, N_RWKV_HEADS, HEAD_SIZE, HEAD_SIZE), -1, -2)


def _mixer_heads(z, s0_f, s0_b, prm, grid):
    y_b, bonus_b, h_b = _scan(z, _pack_state(s0_b), prm, 1, grid)
    y, h_f = _scan(z, _pack_state(s0_f), prm, 0, grid, other=(y_b, bonus_b))
    return y, _unpack_state(h_f), _unpack_state(h_b)


def _path(x, mod, mod_per_batch, s0_f, s0_b, grid, w, prm, tm):
    x1, u, z = _front(x, mod, mod_per_batch, w["norm_g"], w["g0"], w["u0"], w["d0"], w["w_in"], prm["mu"], tm, grid)
    y_four = _fourier_two_stage(u) if grid else _fourier_dense(u)
    y_rwkv, s_f, s_b = _mixer_heads(z, s0_f, s0_b, prm, grid)
    y = _back(x1, y_four, y_rwkv, mod, mod_per_batch, w["norm_g"], w["final"], w["wo_f"], w["wo_r"],
              w["g1"], w["u1"], w["d1"], tm)
    return y, s_f, s_b


def kernel(x_prompt, x_sample, state_fwd, state_bwd, c, c_ctx, w_mod, b_mod, norm_g, ffn_w_gate, ffn_w_up,
           ffn_w_down, w_in, shift_mu, decay_w0, decay_w2, iclr_a0, iclr_a2, gate_g2, k_k, k_a, r_k,
           ln_x_w, ln_x_b, w_out, final_norm):
    depth = w_mod.shape[0]
    assert depth == 1, "the back kernel applies the final norm, so exactly one layer is supported"
    bp = x_prompt.shape[0]
    bs = x_sample.shape[0]
    xp, xs = x_prompt, x_sample
    new_f, new_b = [], []
    for l in range(depth):
        cvec = jnp.zeros((8, D_MODEL), F32).at[:bs].set(c).at[bs].set(c_ctx)
        mod = _modulation(cvec, w_mod[l], b_mod[l]).reshape(8, N_MOD, D_MODEL)
        w = {"norm_g": norm_g[l], "final": final_norm,
             "g0": ffn_w_gate[l, 0].astype(BF16), "u0": ffn_w_up[l, 0].astype(BF16),
             "d0": ffn_w_down[l, 0].astype(BF16),
             "g1": ffn_w_gate[l, 1].astype(BF16), "u1": ffn_w_up[l, 1].astype(BF16),
             "d1": ffn_w_down[l, 1].astype(BF16),
             "w_in": jnp.concatenate([_fold_group_dft(w_in[l, :, :D_FOURIER]).astype(BF16),
                                      w_in[l, :, D_FOURIER:].astype(BF16)], axis=1),
             "wo_f": w_out[l, :D_FOURIER].astype(BF16), "wo_r": w_out[l, D_FOURIER:].astype(BF16)}
        prm = {"mu": shift_mu[l], "w0": decay_w0[l], "w2": decay_w2[l], "a0": iclr_a0[l], "a2": iclr_a2[l],
               "g2": gate_g2[l], "k_k": k_k[l], "k_a": k_a[l], "r_k": r_k[l], "ln_w": ln_x_w[l], "ln_b": ln_x_b[l]}
        zero_state = jnp.zeros((bp, N_RWKV_HEADS, HEAD_SIZE, HEAD_SIZE), F32)
        xp, s_f, s_b = _path(xp, mod[bs:bs + 1], False, zero_state, zero_state, False, w, prm, 256)
        new_f.append(s_f)
        new_b.append(s_b)
        xs, _, _ = _path(xs, mod[:bs], True, state_fwd[:, l], state_bwd[:, l], True, w, prm, 256)
    return xp, xs, jnp.stack(new_f, axis=1), jnp.stack(new_b, axis=1)
```

```python
import functools

import numpy as np
import jax
import jax.numpy as jnp
from jax import lax
from jax.experimental import pallas as pl
from jax.experimental.pallas import tpu as pltpu

D_MODEL = 1024
GRID_W = 64
D_FOURIER = 512
N_FOURIER_GROUPS = 8
FOURIER_GROUP = D_FOURIER // N_FOURIER_GROUPS
D_RWKV = D_MODEL - D_FOURIER
HEAD_SIZE = 64
N_RWKV_HEADS = D_RWKV // HEAD_SIZE
N_PAIRS = N_RWKV_HEADS // 2
PAIR = 2 * HEAD_SIZE
DECAY_LORA = 64
ICLR_LORA = 64
GATE_LORA = 128
D_Z = 3 * D_RWKV + 2 * DECAY_LORA + 2 * ICLR_LORA + GATE_LORA
D_FOLD = 2 * D_FOURIER
D_IN = D_FOLD + D_Z
D_FF = 2816
N_MOD = 9
RMS_EPS = 1e-6
GN_EPS = 64e-5
CHUNK = 64
OFF_WD = 3 * D_RWKV
OFF_AD = OFF_WD + 2 * DECAY_LORA
OFF_GD = OFF_AD + 2 * ICLR_LORA
DECAY_SCALE = float(np.exp(-0.5))
V7X_VMEM_LIMIT = 56 * 1024 * 1024

F32 = jnp.float32
BF16 = jnp.bfloat16


def _dot32(a, b):
    a_hi = a.astype(BF16)
    b_hi = b.astype(BF16)
    a_lo = (a - a_hi.astype(F32)).astype(BF16)
    b_lo = (b - b_hi.astype(F32)).astype(BF16)
    dot = lambda x, y: jnp.dot(x, y, preferred_element_type=F32)
    return dot(a_hi, b_hi) + (dot(a_hi, b_lo) + dot(a_lo, b_hi))


def _sigmoid(x):
    return 1.0 / (1.0 + jnp.exp(-x))


def _norm_mod(x, g, shift, scale):
    ms = jnp.mean(x * x, axis=-1, keepdims=True)
    return x * lax.rsqrt(ms + RMS_EPS) * g * (1.0 + scale) + shift


def _swiglu(h, wg_ref, wu_ref, wd_ref):
    hb = h.astype(BF16)
    gate = jnp.dot(hb, wg_ref[...], preferred_element_type=F32)
    up = jnp.dot(hb, wu_ref[...], preferred_element_type=F32)
    act = gate * _sigmoid(gate) * up
    return jnp.dot(act.astype(BF16), wd_ref[...], preferred_element_type=F32)


def _const_spec(shape):
    nd = len(shape)
    return pl.BlockSpec(shape, lambda *_: (0,) * nd, pipeline_mode=pl.Buffered(1))


def _params(sem):
    return pltpu.CompilerParams(dimension_semantics=sem, vmem_limit_bytes=V7X_VMEM_LIMIT)


def _mod_kernel(c_ref, w_ref, b_ref, o_ref):
    c = c_ref[...]
    o_ref[...] = _dot32(c * _sigmoid(c), w_ref[...]) + b_ref[...]


def _modulation(cvec, w_mod, b_mod):
    n = w_mod.shape[1]
    tn = n // 8
    return pl.pallas_call(
        _mod_kernel,
        grid=(n // tn,),
        in_specs=[pl.BlockSpec((8, D_MODEL), lambda j: (0, 0)),
                  pl.BlockSpec((D_MODEL, tn), lambda j: (0, j)),
                  pl.BlockSpec((1, tn), lambda j: (0, j))],
        out_specs=pl.BlockSpec((8, tn), lambda j: (0, j)),
        out_shape=jax.ShapeDtypeStruct((8, n), F32),
        compiler_params=_params(("arbitrary",)),
        name="modulation",
    )(cvec, w_mod, b_mod.reshape(1, n))


def _shift_adjacent(z, mu, grid):
    tm = z.shape[0]
    row = lax.broadcasted_iota(jnp.int32, (tm, 1), 0)
    lane = lax.broadcasted_iota(jnp.int32, (1, z.shape[1]), 1) % 4
    back1 = pltpu.roll(z, 1, 0)
    fwd1 = pltpu.roll(z, tm - 1, 0)
    if grid:
        col = row % GRID_W
        side = jnp.where(lane == 0, jnp.where(col == 0, 0.0, back1), jnp.where(col == GRID_W - 1, 0.0, fwd1))
        return jnp.where(lane < 2, z + mu * (side - z), z)
    side = jnp.where(lane % 2 == 0, jnp.where(row == 0, 0.0, back1), jnp.where(row == tm - 1, 0.0, fwd1))
    return z + mu * (side - z)


def _front_kernel(x_ref, mod_ref, g_ref, wg_ref, wu_ref, wd_ref, win_ref, mu_ref, x1_ref, u_ref, z_ref, *, grid):
    x = x_ref[...]
    h = _norm_mod(x, g_ref[0:1, :], mod_ref[0:1, :], mod_ref[1:2, :])
    x1 = x + 0.5 * mod_ref[2:3, :] * _swiglu(h, wg_ref, wu_ref, wd_ref)
    x1_ref[...] = x1
    h2 = _norm_mod(x1, g_ref[1:2, :], mod_ref[3:4, :], mod_ref[4:5, :])
    proj = jnp.dot(h2.astype(BF16), win_ref[...], preferred_element_type=F32)
    u_ref[...] = proj[:, :D_FOLD]
    z_ref[...] = _shift_adjacent(proj[:, D_FOLD:], mu_ref[...], grid)


def _front(x, mod, mod_per_batch, norm_g, wg, wu, wd, w_in, mu, tm, grid):
    b, s, _ = x.shape
    nt = s // tm
    assert tm % GRID_W == 0 if grid else tm == s
    tok = lambda w: pl.BlockSpec((None, tm, w), lambda i, j: (i, j, 0))
    mod_map = (lambda i, j: (i, 0, 0)) if mod_per_batch else (lambda i, j: (0, 0, 0))
    return pl.pallas_call(
        functools.partial(_front_kernel, grid=grid),
        grid=(b, nt),
        in_specs=[tok(D_MODEL),
                  pl.BlockSpec((None, N_MOD, D_MODEL), mod_map),
                  _const_spec((3, D_MODEL)),
                  _const_spec((D_MODEL, D_FF)), _const_spec((D_MODEL, D_FF)),
                  _const_spec((D_FF, D_MODEL)), _const_spec((D_MODEL, D_IN)), _const_spec((1, D_Z))],
        out_specs=[tok(D_MODEL), tok(D_FOLD), tok(D_Z)],
        out_shape=[jax.ShapeDtypeStruct((b, s, D_MODEL), F32),
                   jax.ShapeDtypeStruct((b, s, D_FOLD), F32),
                   jax.ShapeDtypeStruct((b, s, D_Z), F32)],
        compiler_params=_params(("arbitrary", "arbitrary")),
        name="front",
    )(x, mod, norm_g, wg, wu, wd, w_in, mu.reshape(1, D_Z))


def _back_kernel(x1_ref, yf_ref, yr_ref, mod_ref, g_ref, fin_ref, wof_ref, wor_ref,
                 wg_ref, wu_ref, wd_ref, o_ref):
    mixed = (jnp.dot(yf_ref[...].astype(BF16), wof_ref[...], preferred_element_type=F32)
             + jnp.dot(yr_ref[...].astype(BF16), wor_ref[...], preferred_element_type=F32))
    x2 = x1_ref[...] + mod_ref[5:6, :] * mixed
    h = _norm_mod(x2, g_ref[2:3, :], mod_ref[6:7, :], mod_ref[7:8, :])
    x3 = x2 + 0.5 * mod_ref[8:9, :] * _swiglu(h, wg_ref, wu_ref, wd_ref)
    ms = jnp.mean(x3 * x3, axis=-1, keepdims=True)
    o_ref[...] = x3 * lax.rsqrt(ms + RMS_EPS) * fin_ref[...]


def _back(x1, yf, yr, mod, mod_per_batch, norm_g, final_norm, wo_f, wo_r, wg, wu, wd, tm):
    b, s, _ = x1.shape
    nt = s // tm
    tok = lambda w: pl.BlockSpec((None, tm, w), lambda i, j: (i, j, 0))
    mod_map = (lambda i, j: (i, 0, 0)) if mod_per_batch else (lambda i, j: (0, 0, 0))
    return pl.pallas_call(
        _back_kernel,
        grid=(b, nt),
        in_specs=[tok(D_MODEL), tok(D_FOURIER), tok(D_RWKV),
                  pl.BlockSpec((None, N_MOD, D_MODEL), mod_map),
                  _const_spec((3, D_MODEL)), _const_spec((1, D_MODEL)),
                  _const_spec((D_FOURIER, D_MODEL)), _const_spec((D_RWKV, D_MODEL)),
                  _const_spec((D_MODEL, D_FF)), _const_spec((D_MODEL, D_FF)),
                  _const_spec((D_FF, D_MODEL))],
        out_specs=tok(D_MODEL),
        out_shape=jax.ShapeDtypeStruct((b, s, D_MODEL), F32),
        compiler_params=_params(("arbitrary", "arbitrary")),
        name="back",
    )(x1, yf, yr, mod, norm_g, final_norm.reshape(1, D_MODEL), wo_f, wo_r, wg, wu, wd)


def _fold_kernel(w_ref, cs_ref, o_ref):
    o_ref[...] = _dot32(w_ref[...], cs_ref[...])


def _fold_group_dft(w_u):
    q = np.arange(FOURIER_GROUP)
    ang = 2.0 * np.pi * ((q[:, None] * q[None, :]) % FOURIER_GROUP) / FOURIER_GROUP
    eye = np.eye(N_FOURIER_GROUPS)
    cs = jnp.asarray(np.concatenate([np.kron(eye, np.cos(ang)), np.kron(eye, np.sin(ang))], axis=1), F32)
    return pl.pallas_call(
        _fold_kernel,
        out_shape=jax.ShapeDtypeStruct((D_MODEL, D_FOLD), F32),
        compiler_params=pltpu.CompilerParams(vmem_limit_bytes=V7X_VMEM_LIMIT),
        name="fold_group_dft",
    )(w_u, cs)


def _stack_cos_sin(x):
    return jnp.concatenate([x[:, :D_FOURIER], x[:, D_FOURIER:]], axis=0).astype(BF16)


def _fourier_dense_kernel(x_ref, p_ref, o_ref):
    o_ref[...] = _bdot(p_ref[...], _stack_cos_sin(x_ref[...]))


def _fourier_dense(xcs):
    b, s, _ = xcs.shape
    pos = np.arange(s)
    ang = 2.0 * np.pi * ((pos[:, None] * pos[None, :]) % s) / s
    table = np.concatenate([np.cos(ang), -np.sin(ang)], axis=1) / np.sqrt(s * FOURIER_GROUP)
    return pl.pallas_call(
        _fourier_dense_kernel,
        grid=(b,),
        in_specs=[pl.BlockSpec((None, s, D_FOLD), lambda i: (i, 0, 0)), _const_spec((s, 2 * s))],
        out_specs=pl.BlockSpec((None, s, D_FOURIER), lambda i: (i, 0, 0)),
        out_shape=jax.ShapeDtypeStruct((b, s, D_FOURIER), F32),
        compiler_params=_params(("arbitrary",)),
        name="fourier_dense",
    )(xcs, jnp.asarray(table, F32).astype(BF16))


FFT_ROWS = 16


def _fourier_stage1_kernel(x_ref, f_ref, z_ref):
    for j in range(FFT_ROWS):
        z = _bdot(f_ref[...], _stack_cos_sin(x_ref[:, j, :]))
        n = z.shape[0] // 2
        z_ref[:, j, :] = jnp.concatenate([z[:n], z[n:]], axis=1)


def _fourier_stage2_kernel(z_ref, g_ref, o_ref):
    for j in range(FFT_ROWS):
        o_ref[:, j, :] = _bdot(g_ref[j], _stack_cos_sin(z_ref[j]))


def _fourier_two_stage(xcs):
    b, s, _ = xcs.shape
    n = int(round(np.sqrt(s)))
    assert n * n == s and n % FFT_ROWS == 0
    idx = np.arange(n)
    ang = 2.0 * np.pi * ((idx[:, None] * idx[None, :]) % n) / n
    fc, fs = np.cos(ang), np.sin(ang)
    f2 = jnp.asarray(np.block([[fc, -fs], [fs, fc]]), F32).astype(BF16)
    z = pl.pallas_call(
        _fourier_stage1_kernel,
        grid=(b, n // FFT_ROWS),
        in_specs=[pl.BlockSpec((None, n, FFT_ROWS, D_FOLD), lambda i, j: (i, 0, j, 0)),
                  _const_spec((2 * n, 2 * n))],
        out_specs=pl.BlockSpec((None, n, FFT_ROWS, D_FOLD), lambda i, j: (i, 0, j, 0)),
        out_shape=jax.ShapeDtypeStruct((b, n, n, D_FOLD), F32),
        compiler_params=_params(("arbitrary", "arbitrary")),
        name="fourier_stage1",
    )(xcs.reshape(b, n, n, D_FOLD), f2)
    bb, aa, s1 = idx[:, None, None], idx[None, :, None], idx[None, None, :]
    ang2 = 2.0 * np.pi * ((s1 * (n * aa + bb)) % s) / s
    g2 = np.concatenate([np.cos(ang2), -np.sin(ang2)], axis=2) / np.sqrt(s * FOURIER_GROUP)
    out = pl.pallas_call(
        _fourier_stage2_kernel,
        grid=(b, n // FFT_ROWS),
        in_specs=[pl.BlockSpec((None, FFT_ROWS, n, D_FOLD), lambda i, j: (i, j, 0, 0)),
                  pl.BlockSpec((FFT_ROWS, n, 2 * n), lambda i, j: (j, 0, 0))],
        out_specs=pl.BlockSpec((None, n, FFT_ROWS, D_FOURIER), lambda i, j: (i, 0, j, 0)),
        out_shape=jax.ShapeDtypeStruct((b, n, n, D_FOURIER), F32),
        compiler_params=_params(("arbitrary", "arbitrary")),
        name="fourier_stage2",
    )(z, jnp.asarray(g2, F32).astype(BF16))
    return out.reshape(b, s, D_FOURIER)


def _shift_rows(z, zf_ref, mu):
    tm = z.shape[0]
    lane = lax.broadcasted_iota(jnp.int32, (1, z.shape[1]), 1) % 4
    up = zf_ref[0:tm, :]
    down = zf_ref[2 * GRID_W:2 * GRID_W + tm, :]
    return z + jnp.where(lane < 2, 0.0, mu) * (jnp.where(lane == 2, up, down) - z)


def _stack_masked(x, head0):
    return jnp.concatenate([jnp.where(head0, x, 0.0), jnp.where(head0, 0.0, x)], axis=0)


NT = (((1,), (1,)), ((), ()))
CUM_PIECES = 3
CHUNKS_PER_STEP = 4


def _split(x, pieces):
    out = []
    for i in range(pieces):
        hi = x.astype(BF16)
        out.append(hi)
        if i + 1 < pieces:
            x = x - hi.astype(F32)
    return out


def _bdot(a, b, dims=None):
    if dims is None:
        return jnp.dot(a, b, preferred_element_type=F32)
    return lax.dot_general(a, b, dims, preferred_element_type=F32)


HEADS_PER_SUM = 4


def _head_sums(x, ones_ref):
    w = HEADS_PER_SUM * HEAD_SIZE
    return jnp.concatenate([_bdot(x[:, i:i + w].astype(BF16), ones_ref[...]) for i in range(0, D_RWKV, w)], axis=1)


def _scan_kernel(*refs, grid, reverse, combine, n_tiles, tm):
    it = iter(refs)
    z_ref = next(it)
    zp_ref = next(it) if grid else None
    zn_ref = next(it) if grid else None
    h0_ref = next(it)
    if combine:
        yo_ref, bo_ref = next(it), next(it)
    mu_ref, kk_ref, ka_ref, rk_ref, w0_ref, a0_ref, w2_ref, a2_ref, ones_ref = (next(it) for _ in range(9))
    if combine:
        lnw_ref, lnb_ref, g2_ref = next(it), next(it), next(it)
    y_ref = next(it)
    b_ref = None if combine else next(it)
    hout_ref = next(it)
    r_s, v_s, al_s, be_s, kd_s, lw_s, y_s, g_s, h_s, qm_s, yn_s = (next(it) for _ in range(11))
    zf_s = next(it) if grid else None

    step = pl.program_id(1)
    tile = (n_tiles - 1 - step) if reverse else step

    @pl.when(step == 0)
    def _():
        h_s[...] = h0_ref[...]

    z = z_ref[...]
    if grid:
        zf_s[GRID_W:GRID_W + tm, :] = z
        zf_s[0:GRID_W, :] = jnp.where(tile == 0, 0.0, zp_ref[...])
        zf_s[GRID_W + tm:, :] = jnp.where(tile == n_tiles - 1, 0.0, zn_ref[...])
        z = _shift_rows(z, zf_s, mu_ref[...])
    r = z[:, 0:D_RWKV]
    k = z[:, D_RWKV:2 * D_RWKV]
    v = z[:, 2 * D_RWKV:3 * D_RWKV]
    kk = k * kk_ref[...]
    kk = kk * lax.rsqrt(jnp.maximum(_head_sums(kk * kk, ones_ref), 1e-24))
    w_raw = w0_ref[...] + _bdot(jnp.tanh(z[:, OFF_WD:OFF_AD]).astype(BF16), w2_ref[...])
    a = _sigmoid(a0_ref[...] + _bdot(z[:, OFF_AD:OFF_GD].astype(BF16), a2_ref[...]))
    kd = k * (1.0 + (a - 1.0) * ka_ref[...])
    bonus = _head_sums(r * kd * rk_ref[...], ones_ref) * v
    if combine:
        y_ref[...] = bonus + bo_ref[...]
        g_s[...] = _bdot(_sigmoid(z[:, OFF_GD:]).astype(BF16), g2_ref[...])
    else:
        b_ref[...] = bonus
    r_s[...] = r
    v_s[...] = v
    al_s[...] = -kk
    be_s[...] = a * kk
    kd_s[...] = kd
    lw_s[...] = -DECAY_SCALE * _sigmoid(w_raw)

    n_chunks = tm // CHUNK
    ri = lax.broadcasted_iota(jnp.int32, (CHUNK, CHUNK), 0)
    ci = lax.broadcasted_iota(jnp.int32, (CHUNK, CHUNK), 1)
    tri = ((ci >= ri) if reverse else (ci <= ri)).astype(BF16)
    rj = lax.broadcasted_iota(jnp.int32, (CHUNK, PAIR), 0)
    cj = lax.broadcasted_iota(jnp.int32, (CHUNK, PAIR), 1) % CHUNK
    eye = rj == cj
    strict = (cj > rj) if reverse else (cj < rj)
    incl = (cj >= rj) if reverse else (cj <= rj)
    head0 = lax.broadcasted_iota(jnp.int32, (1, PAIR), 1) < HEAD_SIZE
    last = 0 if reverse else CHUNK - 1
    blockdiag = lambda x: _stack_masked(x, head0)

    def head_transpose(x):
        t = blockdiag(x).T
        return t[:CHUNK] + t[CHUNK:]

    same_block = lambda n: (rj & -n) == (cj & -n)
    levels = [2 ** i for i in range(1, int(np.log2(CHUNK)))]
    base_mask = same_block(2)
    pair_masks = {n: same_block(2 * n) & ~same_block(n) for n in levels}

    def chunk_rows(c):
        off = ((n_chunks - 1 - c) if reverse else c) * CHUNK
        return slice(off, off + CHUNK)

    def local_phase(chunks):
        units = []
        for c in chunks:
            rows = chunk_rows(c)
            lw = lw_s[rows, :]
            cum = sum(_bdot(tri, piece) for piece in _split(lw, CUM_PIECES))
            tot = cum[last:last + 1, :]
            g_inv = jnp.exp(-cum)
            g_end = jnp.exp(tot - cum)
            g_tot = jnp.exp(tot)
            abar = al_s[rows, :] * jnp.exp(cum - lw)
            rbar = r_s[rows, :] * jnp.exp(cum)
            be = be_s[rows, :]
            kdc = kd_s[rows, :]
            btil, ktil, bhat, khat = be * g_inv, kdc * g_inv, be * g_end, kdc * g_end
            vc = v_s[rows, :]
            for p in range(N_PAIRS):
                lanes = slice(p * PAIR, (p + 1) * PAIR)
                cut = lambda x: x[:, lanes]
                units.append(dict(c=c, p=p, a=cut(abar).astype(BF16), r=cut(rbar), v=cut(vc).astype(BF16),
                                  bt=cut(btil).astype(BF16), kt=cut(ktil).astype(BF16),
                                  bh=cut(bhat), kh=cut(khat), g_tot=cut(g_tot)))
        for u in units:
            ar = jnp.concatenate([u["a"], u["r"].astype(BF16)], axis=0)
            u["gb"] = _bdot(ar, blockdiag(u["bt"]), NT)
            u["gk"] = _bdot(ar, blockdiag(u["kt"]), NT)
        for u in units:
            gb, gk = u.pop("gb"), u.pop("gk")
            u["l_ab"] = jnp.where(strict, gb[:CHUNK], 0.0)
            u["l_ak"] = jnp.where(strict, gk[:CHUNK], 0.0).astype(BF16)
            u["l_rbk"] = jnp.concatenate([jnp.where(incl, gb[CHUNK:], 0.0),
                                          jnp.where(incl, gk[CHUNK:], 0.0)], axis=1).astype(BF16)
            u["t"] = jnp.where(eye, 1.0, jnp.where(base_mask, u["l_ab"], 0.0))
        for n in levels:
            for u in units:
                e = jnp.where(pair_masks[n], u["l_ab"], 0.0).astype(BF16)
                u["et"] = _bdot(e, blockdiag(u["t"].astype(BF16)))
            for u in units:
                u["t"] = u["t"] + _bdot(u["t"].astype(BF16), blockdiag(u["et"].astype(BF16)))
        for u in units:
            u["x1"] = _bdot(u["l_ak"], blockdiag(u["v"]))
        for u in units:
            rhs = jnp.concatenate([blockdiag(u["a"]), blockdiag(u["x1"].astype(BF16))], axis=1)
            u["pu"] = _bdot(u["t"].astype(BF16), rhs)
        for u in units:
            pu = u["pu"].astype(BF16)
            v_bd = blockdiag(u["v"])
            rhs = jnp.concatenate(
                [jnp.concatenate([blockdiag(pu[:, :PAIR]), blockdiag(pu[:, PAIR:])], axis=1),
                 jnp.concatenate([jnp.zeros_like(v_bd), v_bd], axis=1)], axis=0)
            bkh_t = jnp.concatenate([head_transpose(u["bh"]), head_transpose(u["kh"])], axis=1)
            o = _bdot(jnp.concatenate([u["l_rbk"], bkh_t.astype(BF16)], axis=0), rhs)
            q = u["r"] + o[:CHUNK, :PAIR]
            m = jnp.where(eye, u["g_tot"], 0.0) + o[CHUNK:, :PAIR]
            qm_s[u["c"], u["p"]] = jnp.concatenate([q, m], axis=0).astype(BF16)
            yn_s[u["c"], u["p"]] = o[:, PAIR:]

    def state_phase(chunks):
        for c in chunks:
            for p in range(N_PAIRS):
                oh = _bdot(qm_s[c, p], blockdiag(h_s[p].astype(BF16))) + yn_s[c, p]
                y_s[chunk_rows(c), p * PAIR:(p + 1) * PAIR] = oh[:CHUNK]
                h_s[p] = oh[CHUNK:]

    groups = [list(range(g, min(g + CHUNKS_PER_STEP, n_chunks))) for g in range(0, n_chunks, CHUNKS_PER_STEP)]
    local_phase(groups[0])
    for prev, cur in zip(groups[:-1], groups[1:]):
        local_phase(cur)
        state_phase(prev)
    state_phase(groups[-1])

    @pl.when(step == n_tiles - 1)
    def _():
        hout_ref[...] = h_s[...]

    if combine:
        y_sum = y_s[...] + yo_ref[...]
        mean = _head_sums(y_sum, ones_ref) * (1.0 / HEAD_SIZE)
        cen = y_sum - mean
        var = _head_sums(cen * cen, ones_ref) * (1.0 / HEAD_SIZE)
        y_gn = cen * lax.rsqrt(var + GN_EPS) * lnw_ref[...] + lnb_ref[...]
        y_ref[...] = (y_gn + y_ref[...]) * g_s[...]
    else:
        y_ref[...] = y_s[...]


def _scan(z, h0, prm, d, grid, other=None):
    b, s, _ = z.shape
    tm = 512 if grid else s
    n_tiles = s // tm
    reverse = d == 1
    combine = other is not None
    tile_of = (lambda j: n_tiles - 1 - j) if reverse else (lambda j: j)
    tok = lambda w: pl.BlockSpec((None, tm, w), lambda i, j: (i, tile_of(j), 0))
    halo_per_tile = tm // GRID_W
    n_halo = s // GRID_W
    in_specs = [tok(D_Z)]
    args = [z]
    if grid:
        in_specs += [pl.BlockSpec((None, GRID_W, D_Z),
                                  lambda i, j: (i, jnp.maximum(tile_of(j) * halo_per_tile - 1, 0), 0)),
                     pl.BlockSpec((None, GRID_W, D_Z),
                                  lambda i, j: (i, jnp.minimum((tile_of(j) + 1) * halo_per_tile, n_halo - 1), 0))]
        args += [z, z]
    state_spec = pl.BlockSpec((None, N_PAIRS, HEAD_SIZE, PAIR), lambda i, j: (i, 0, 0, 0))
    in_specs.append(state_spec)
    args.append(h0)
    if combine:
        in_specs += [tok(D_RWKV), tok(D_RWKV)]
        args += list(other)
    row = lambda x: x.reshape(1, -1)
    small = [row(prm["mu"]), row(prm["k_k"]), row(prm["k_a"]), row(prm["r_k"]),
             row(prm["w0"][d]), row(prm["a0"][d]),
             jnp.zeros((2 * DECAY_LORA, D_RWKV), BF16).at[d * DECAY_LORA:(d + 1) * DECAY_LORA].set(
                 prm["w2"][d].astype(BF16)),
             jnp.zeros((2 * ICLR_LORA, D_RWKV), BF16).at[d * ICLR_LORA:(d + 1) * ICLR_LORA].set(
                 prm["a2"][d].astype(BF16)),
             jnp.asarray(np.kron(np.eye(HEADS_PER_SUM), np.ones((HEAD_SIZE, HEAD_SIZE))), BF16)]
    if combine:
        small += [row(prm["ln_w"]), row(prm["ln_b"]), prm["g2"].astype(BF16)]
    in_specs += [_const_spec(x.shape) for x in small]
    args += small
    out_specs = [tok(D_RWKV)]
    out_shape = [jax.ShapeDtypeStruct((b, s, D_RWKV), F32)]
    if not combine:
        out_specs.append(tok(D_RWKV))
        out_shape.append(jax.ShapeDtypeStruct((b, s, D_RWKV), F32))
    out_specs.append(state_spec)
    out_shape.append(jax.ShapeDtypeStruct((b, N_PAIRS, HEAD_SIZE, PAIR), F32))
    n_chunks = tm // CHUNK
    scratch = ([pltpu.VMEM((tm, D_RWKV), F32) for _ in range(8)]
               + [pltpu.VMEM((N_PAIRS, CHUNK, PAIR), F32),
                  pltpu.VMEM((n_chunks, N_PAIRS, 2 * CHUNK, PAIR), BF16),
                  pltpu.VMEM((n_chunks, N_PAIRS, 2 * CHUNK, PAIR), F32)]
               + ([pltpu.VMEM((tm + 2 * GRID_W, D_Z), F32)] if grid else []))
    return pl.pallas_call(
        functools.partial(_scan_kernel, grid=grid, reverse=reverse, combine=combine, n_tiles=n_tiles, tm=tm),
        grid=(b, n_tiles),
        in_specs=in_specs,
        out_specs=out_specs,
        out_shape=out_shape,
        scratch_shapes=scratch,
        compiler_params=_params(("arbitrary", "arbitrary")),
        name="scan_bwd" if reverse else "scan_fwd",
    )(*args)


def _pack_state(s):
    b = s.shape[0]
    h = jnp.swapaxes(s, -1, -2).reshape(b, N_PAIRS, 2, HEAD_SIZE, HEAD_SIZE)
    return jnp.transpose(h, (0, 1, 3, 2, 4)).reshape(b, N_PAIRS, HEAD_SIZE, PAIR)


def _unpack_state(hp):
    b = hp.shape[0]
    h = jnp.transpose(hp.reshape(b, N_PAIRS, HEAD_SIZE, 2, HEAD_SIZE), (0, 1, 3, 2, 4))
    return jnp.swapaxes(h.reshape(b, N_RWKV_HEADS, HEAD_SIZE, HEAD_SIZE), -1, -2)


def _mixer_heads(z, s0_f, s0_b, prm, grid):
    y_b, bonus_b, h_b = _scan(z, _pack_state(s0_b), prm, 1, grid)
    y, h_f = _scan(z, _pack_state(s0_f), prm, 0, grid, other=(y_b, bonus_b))
    return y, _unpack_state(h_f), _unpack_state(h_b)


def _path(x, mod, mod_per_batch, s0_f, s0_b, grid, w, prm, tm):
    x1, u, z = _front(x, mod, mod_per_batch, w["norm_g"], w["g0"], w["u0"], w["d0"], w["w_in"], prm["mu"], tm, grid)
    y_four = _fourier_two_stage(u) if grid else _fourier_dense(u)
    y_rwkv, s_f, s_b = _mixer_heads(z, s0_f, s0_b, prm, grid)
    y = _back(x1, y_four, y_rwkv, mod, mod_per_batch, w["norm_g"], w["final"], w["wo_f"], w["wo_r"],
              w["g1"], w["u1"], w["d1"], tm)
    return y, s_f, s_b


def kernel(x_prompt, x_sample, state_fwd, state_bwd, c, c_ctx, w_mod, b_mod, norm_g, ffn_w_gate, ffn_w_up,
           ffn_w_down, w_in, shift_mu, decay_w0, decay_w2, iclr_a0, iclr_a2, gate_g2, k_k, k_a, r_k,
           ln_x_w, ln_x_b, w_out, final_norm):
    depth = w_mod.shape[0]
    assert depth == 1, "the back kernel applies the final norm, so exactly one layer is supported"
    bp = x_prompt.shape[0]
    bs = x_sample.shape[0]
    xp, xs = x_prompt, x_sample
    new_f, new_b = [], []
    for l in range(depth):
        cvec = jnp.zeros((8, D_MODEL), F32).at[:bs].set(c).at[bs].set(c_ctx)
        mod = _modulation(cvec, w_mod[l], b_mod[l]).reshape(8, N_MOD, D_MODEL)
        w = {"norm_g": norm_g[l], "final": final_norm,
             "g0": ffn_w_gate[l, 0].astype(BF16), "u0": ffn_w_up[l, 0].astype(BF16),
             "d0": ffn_w_down[l, 0].astype(BF16),
             "g1": ffn_w_gate[l, 1].astype(BF16), "u1": ffn_w_up[l, 1].astype(BF16),
             "d1": ffn_w_down[l, 1].astype(BF16),
             "w_in": jnp.concatenate([_fold_group_dft(w_in[l, :, :D_FOURIER]).astype(BF16),
                                      w_in[l, :, D_FOURIER:].astype(BF16)], axis=1),
             "wo_f": w_out[l, :D_FOURIER].astype(BF16), "wo_r": w_out[l, D_FOURIER:].astype(BF16)}
        prm = {"mu": shift_mu[l], "w0": decay_w0[l], "w2": decay_w2[l], "a0": iclr_a0[l], "a2": iclr_a2[l],
               "g2": gate_g2[l], "k_k": k_k[l], "k_a": k_a[l], "r_k": r_k[l], "ln_w": ln_x_w[l], "ln_b": ln_x_b[l]}
        zero_state = jnp.zeros((bp, N_RWKV_HEADS, HEAD_SIZE, HEAD_SIZE), F32)
        xp, s_f, s_b = _path(xp, mod[bs:bs + 1], False, zero_state, zero_state, False, w, prm, 256)
        new_f.append(s_f)
        new_b.append(s_b)
        xs, _, _ = _path(xs, mod[:bs], True, state_fwd[:, l], state_bwd[:, l], True, w, prm, 256)
    return xp, xs, jnp.stack(new_f, axis=1), jnp.stack(new_b, axis=1)
```

```python
import functools

import numpy as np
import jax
import jax.numpy as jnp
from jax import lax
from jax.experimental import pallas as pl
from jax.experimental.pallas import tpu as pltpu

D_MODEL = 1024
GRID_W = 64
D_FOURIER = 512
N_FOURIER_GROUPS = 8
FOURIER_GROUP = D_FOURIER // N_FOURIER_GROUPS
D_RWKV = D_MODEL - D_FOURIER
HEAD_SIZE = 64
N_RWKV_HEADS = D_RWKV // HEAD_SIZE
N_PAIRS = N_RWKV_HEADS // 2
PAIR = 2 * HEAD_SIZE
DECAY_LORA = 64
ICLR_LORA = 64
GATE_LORA = 128
D_Z = 3 * D_RWKV + 2 * DECAY_LORA + 2 * ICLR_LORA + GATE_LORA
D_FOLD = 2 * D_FOURIER
D_FF = 2816
N_MOD = 9
RMS_EPS = 1e-6
GN_EPS = 64e-5
CHUNK = 64
OFF_WD = 3 * D_RWKV
OFF_AD = OFF_WD + 2 * DECAY_LORA
OFF_GD = OFF_AD + 2 * ICLR_LORA
DECAY_SCALE = float(np.exp(-0.5))
V7X_VMEM_LIMIT = 56 * 1024 * 1024

F32 = jnp.float32
BF16 = jnp.bfloat16


def _dot32(a, b):
    a_hi = a.astype(BF16)
    b_hi = b.astype(BF16)
    a_lo = (a - a_hi.astype(F32)).astype(BF16)
    b_lo = (b - b_hi.astype(F32)).astype(BF16)
    dot = lambda x, y: jnp.dot(x, y, preferred_element_type=F32)
    return dot(a_hi, b_hi) + (dot(a_hi, b_lo) + dot(a_lo, b_hi))


def _sigmoid(x):
    return 1.0 / (1.0 + jnp.exp(-x))


def _norm_mod(x, g, shift, scale):
    ms = jnp.mean(x * x, axis=-1, keepdims=True)
    return x * lax.rsqrt(ms + RMS_EPS) * g * (1.0 + scale) + shift


def _swiglu(h, wg_ref, wu_ref, wd_ref):
    hb = h.astype(BF16)
    gate = jnp.dot(hb, wg_ref[...], preferred_element_type=F32)
    up = jnp.dot(hb, wu_ref[...], preferred_element_type=F32)
    act = gate * _sigmoid(gate) * up
    return jnp.dot(act.astype(BF16), wd_ref[...], preferred_element_type=F32)


def _const_spec(shape):
    nd = len(shape)
    return pl.BlockSpec(shape, lambda *_: (0,) * nd, pipeline_mode=pl.Buffered(1))


def _params(sem):
    return pltpu.CompilerParams(dimension_semantics=sem, vmem_limit_bytes=V7X_VMEM_LIMIT)


def _mod_kernel(c_ref, w_ref, b_ref, o_ref):
    c = c_ref[...]
    o_ref[...] = _dot32(c * _sigmoid(c), w_ref[...]) + b_ref[...]


def _modulation(cvec, w_mod, b_mod):
    n = w_mod.shape[1]
    tn = n // 8
    return pl.pallas_call(
        _mod_kernel,
        grid=(n // tn,),
        in_specs=[pl.BlockSpec((8, D_MODEL), lambda j: (0, 0)),
                  pl.BlockSpec((D_MODEL, tn), lambda j: (0, j)),
                  pl.BlockSpec((1, tn), lambda j: (0, j))],
        out_specs=pl.BlockSpec((8, tn), lambda j: (0, j)),
        out_shape=jax.ShapeDtypeStruct((8, n), F32),
        compiler_params=_params(("arbitrary",)),
        name="modulation",
    )(cvec, w_mod, b_mod.reshape(1, n))


def _shift_adjacent(z, mu, grid):
    tm = z.shape[0]
    row = lax.broadcasted_iota(jnp.int32, (tm, 1), 0)
    lane = lax.broadcasted_iota(jnp.int32, (1, z.shape[1]), 1) % 4
    back1 = pltpu.roll(z, 1, 0)
    fwd1 = pltpu.roll(z, tm - 1, 0)
    if grid:
        col = row % GRID_W
        side = jnp.where(lane == 0, jnp.where(col == 0, 0.0, back1), jnp.where(col == GRID_W - 1, 0.0, fwd1))
        return jnp.where(lane < 2, z + mu * (side - z), z)
    side = jnp.where(lane % 2 == 0, jnp.where(row == 0, 0.0, back1), jnp.where(row == tm - 1, 0.0, fwd1))
    return z + mu * (side - z)


def _front_kernel(x_ref, mod_ref, g_ref, wg_ref, wu_ref, wd_ref, wfold_ref, wz_ref, mu_ref,
                  x1_ref, u_ref, z_ref, *, grid):
    x = x_ref[...]
    h = _norm_mod(x, g_ref[0:1, :], mod_ref[0:1, :], mod_ref[1:2, :])
    x1 = x + 0.5 * mod_ref[2:3, :] * _swiglu(h, wg_ref, wu_ref, wd_ref)
    x1_ref[...] = x1
    h2 = _norm_mod(x1, g_ref[1:2, :], mod_ref[3:4, :], mod_ref[4:5, :]).astype(BF16)
    u_ref[...] = jnp.dot(h2, wfold_ref[...], preferred_element_type=F32)
    z_ref[...] = _shift_adjacent(jnp.dot(h2, wz_ref[...], preferred_element_type=F32), mu_ref[...], grid)


def _front(x, mod, mod_per_batch, norm_g, wg, wu, wd, w_fold, w_z, mu, tm, grid):
    b, s, _ = x.shape
    nt = s // tm
    assert tm % GRID_W == 0 if grid else tm == s
    tok = lambda w: pl.BlockSpec((None, tm, w), lambda i, j: (i, j, 0))
    mod_map = (lambda i, j: (i, 0, 0)) if mod_per_batch else (lambda i, j: (0, 0, 0))
    return pl.pallas_call(
        functools.partial(_front_kernel, grid=grid),
        grid=(b, nt),
        in_specs=[tok(D_MODEL),
                  pl.BlockSpec((None, N_MOD, D_MODEL), mod_map),
                  _const_spec((3, D_MODEL)),
                  _const_spec((D_MODEL, D_FF)), _const_spec((D_MODEL, D_FF)),
                  _const_spec((D_FF, D_MODEL)), _const_spec((D_MODEL, D_FOLD)), _const_spec((D_MODEL, D_Z)),
                  _const_spec((1, D_Z))],
        out_specs=[tok(D_MODEL), tok(D_FOLD), tok(D_Z)],
        out_shape=[jax.ShapeDtypeStruct((b, s, D_MODEL), F32),
                   jax.ShapeDtypeStruct((b, s, D_FOLD), F32),
                   jax.ShapeDtypeStruct((b, s, D_Z), F32)],
        compiler_params=_params(("arbitrary", "arbitrary")),
        name="front",
    )(x, mod, norm_g, wg, wu, wd, w_fold, w_z, mu.reshape(1, D_Z))


def _back_kernel(x1_ref, yf_ref, yr_ref, mod_ref, g_ref, fin_ref, wo_ref, wg_ref, wu_ref, wd_ref, o_ref):
    mixed = (jnp.dot(yf_ref[...].astype(BF16), wo_ref[0:D_FOURIER, :], preferred_element_type=F32)
             + jnp.dot(yr_ref[...].astype(BF16), wo_ref[D_FOURIER:, :], preferred_element_type=F32))
    x2 = x1_ref[...] + mod_ref[5:6, :] * mixed
    h = _norm_mod(x2, g_ref[2:3, :], mod_ref[6:7, :], mod_ref[7:8, :])
    x3 = x2 + 0.5 * mod_ref[8:9, :] * _swiglu(h, wg_ref, wu_ref, wd_ref)
    ms = jnp.mean(x3 * x3, axis=-1, keepdims=True)
    o_ref[...] = x3 * lax.rsqrt(ms + RMS_EPS) * fin_ref[...]


def _back(x1, yf, yr, mod, mod_per_batch, norm_g, final_norm, w_out, wg, wu, wd, tm):
    b, s, _ = x1.shape
    nt = s // tm
    tok = lambda w: pl.BlockSpec((None, tm, w), lambda i, j: (i, j, 0))
    mod_map = (lambda i, j: (i, 0, 0)) if mod_per_batch else (lambda i, j: (0, 0, 0))
    return pl.pallas_call(
        _back_kernel,
        grid=(b, nt),
        in_specs=[tok(D_MODEL), tok(D_FOURIER), tok(D_RWKV),
                  pl.BlockSpec((None, N_MOD, D_MODEL), mod_map),
                  _const_spec((3, D_MODEL)), _const_spec((1, D_MODEL)),
                  _const_spec((D_MODEL, D_MODEL)),
                  _const_spec((D_MODEL, D_FF)), _const_spec((D_MODEL, D_FF)),
                  _const_spec((D_FF, D_MODEL))],
        out_specs=tok(D_MODEL),
        out_shape=jax.ShapeDtypeStruct((b, s, D_MODEL), F32),
        compiler_params=_params(("arbitrary", "arbitrary")),
        name="back",
    )(x1, yf, yr, mod, norm_g, final_norm.reshape(1, D_MODEL), w_out, wg, wu, wd)


def _fold_kernel(w_ref, cs_ref, o_ref):
    o_ref[...] = _dot32(w_ref[...], cs_ref[...]).astype(o_ref.dtype)


def _fold_group_dft(w_in):
    q = np.arange(FOURIER_GROUP)
    ang = 2.0 * np.pi * ((q[:, None] * q[None, :]) % FOURIER_GROUP) / FOURIER_GROUP
    eye = np.eye(N_FOURIER_GROUPS)
    cs = jnp.asarray(np.concatenate([np.kron(eye, np.cos(ang)), np.kron(eye, np.sin(ang))], axis=1), F32)
    return pl.pallas_call(
        _fold_kernel,
        grid=(1,),
        in_specs=[pl.BlockSpec((D_MODEL, D_FOURIER), lambda i: (0, 0)),
                  pl.BlockSpec((D_FOURIER, D_FOLD), lambda i: (0, 0))],
        out_specs=pl.BlockSpec((D_MODEL, D_FOLD), lambda i: (0, 0)),
        out_shape=jax.ShapeDtypeStruct((D_MODEL, D_FOLD), BF16),
        compiler_params=_params(("arbitrary",)),
        name="fold_group_dft",
    )(w_in, cs)


def _stack_cos_sin(x):
    return jnp.concatenate([x[:, :D_FOURIER], x[:, D_FOURIER:]], axis=0).astype(BF16)


def _fourier_dense_kernel(x_ref, p_ref, o_ref):
    o_ref[...] = _bdot(p_ref[...], _stack_cos_sin(x_ref[...]))


def _fourier_dense(xcs):
    b, s, _ = xcs.shape
    pos = np.arange(s)
    ang = 2.0 * np.pi * ((pos[:, None] * pos[None, :]) % s) / s
    table = np.concatenate([np.cos(ang), -np.sin(ang)], axis=1) / np.sqrt(s * FOURIER_GROUP)
    return pl.pallas_call(
        _fourier_dense_kernel,
        grid=(b,),
        in_specs=[pl.BlockSpec((None, s, D_FOLD), lambda i: (i, 0, 0)), _const_spec((s, 2 * s))],
        out_specs=pl.BlockSpec((None, s, D_FOURIER), lambda i: (i, 0, 0)),
        out_shape=jax.ShapeDtypeStruct((b, s, D_FOURIER), F32),
        compiler_params=_params(("arbitrary",)),
        name="fourier_dense",
    )(xcs, jnp.asarray(table, F32).astype(BF16))


FFT_ROWS = 16


def _fourier_stage1_kernel(x_ref, f_ref, z_ref):
    for j in range(FFT_ROWS):
        z = _bdot(f_ref[...], _stack_cos_sin(x_ref[:, j, :]))
        n = z.shape[0] // 2
        z_ref[:, j, :] = jnp.concatenate([z[:n], z[n:]], axis=1)


def _fourier_stage2_kernel(z_ref, g_ref, o_ref):
    for j in range(FFT_ROWS):
        o_ref[:, j, :] = _bdot(g_ref[j], _stack_cos_sin(z_ref[j]))


def _fourier_two_stage(xcs):
    b, s, _ = xcs.shape
    n = int(round(np.sqrt(s)))
    assert n * n == s and n % FFT_ROWS == 0
    idx = np.arange(n)
    ang = 2.0 * np.pi * ((idx[:, None] * idx[None, :]) % n) / n
    fc, fs = np.cos(ang), np.sin(ang)
    f2 = jnp.asarray(np.block([[fc, -fs], [fs, fc]]), F32).astype(BF16)
    z = pl.pallas_call(
        _fourier_stage1_kernel,
        grid=(b, n // FFT_ROWS),
        in_specs=[pl.BlockSpec((None, n, FFT_ROWS, D_FOLD), lambda i, j: (i, 0, j, 0)),
                  _const_spec((2 * n, 2 * n))],
        out_specs=pl.BlockSpec((None, n, FFT_ROWS, D_FOLD), lambda i, j: (i, 0, j, 0)),
        out_shape=jax.ShapeDtypeStruct((b, n, n, D_FOLD), F32),
        compiler_params=_params(("arbitrary", "arbitrary")),
        name="fourier_stage1",
    )(xcs.reshape(b, n, n, D_FOLD), f2)
    bb, aa, s1 = idx[:, None, None], idx[None, :, None], idx[None, None, :]
    ang2 = 2.0 * np.pi * ((s1 * (n * aa + bb)) % s) / s
    g2 = np.concatenate([np.cos(ang2), -np.sin(ang2)], axis=2) / np.sqrt(s * FOURIER_GROUP)
    out = pl.pallas_call(
        _fourier_stage2_kernel,
        grid=(b, n // FFT_ROWS),
        in_specs=[pl.BlockSpec((None, FFT_ROWS, n, D_FOLD), lambda i, j: (i, j, 0, 0)),
                  pl.BlockSpec((FFT_ROWS, n, 2 * n), lambda i, j: (j, 0, 0))],
        out_specs=pl.BlockSpec((None, n, FFT_ROWS, D_FOURIER), lambda i, j: (i, 0, j, 0)),
        out_shape=jax.ShapeDtypeStruct((b, n, n, D_FOURIER), F32),
        compiler_params=_params(("arbitrary", "arbitrary")),
        name="fourier_stage2",
    )(z, jnp.asarray(g2, F32).astype(BF16))
    return out.reshape(b, s, D_FOURIER)


def _shift_rows(z, zf_ref, mu):
    tm = z.shape[0]
    lane = lax.broadcasted_iota(jnp.int32, (1, z.shape[1]), 1) % 4
    up = zf_ref[0:tm, :]
    down = zf_ref[2 * GRID_W:2 * GRID_W + tm, :]
    return z + jnp.where(lane < 2, 0.0, mu) * (jnp.where(lane == 2, up, down) - z)


def _stack_masked(x, head0):
    return jnp.concatenate([jnp.where(head0, x, 0.0), jnp.where(head0, 0.0, x)], axis=0)


NT = (((1,), (1,)), ((), ()))
CUM_PIECES = 3
CHUNKS_PER_STEP = 4


def _split(x, pieces):
    out = []
    for i in range(pieces):
        hi = x.astype(BF16)
        out.append(hi)
        if i + 1 < pieces:
            x = x - hi.astype(F32)
    return out


def _bdot(a, b, dims=None):
    if dims is None:
        return jnp.dot(a, b, preferred_element_type=F32)
    return lax.dot_general(a, b, dims, preferred_element_type=F32)


HEADS_PER_SUM = 4


def _head_sums(x, ones_ref):
    w = HEADS_PER_SUM * HEAD_SIZE
    return jnp.concatenate([_bdot(x[:, i:i + w].astype(BF16), ones_ref[...]) for i in range(0, D_RWKV, w)], axis=1)


def _scan_kernel(*refs, grid, reverse, combine, n_tiles, tm):
    it = iter(refs)
    z_ref = next(it)
    zp_ref = next(it) if grid else None
    zn_ref = next(it) if grid else None
    h0_ref = next(it)
    if combine:
        yo_ref, bo_ref = next(it), next(it)
    mu_ref, kk_ref, ka_ref, rk_ref, w0_ref, a0_ref, w2_ref, a2_ref, ones_ref = (next(it) for _ in range(9))
    if combine:
        lnw_ref, lnb_ref, g2_ref = next(it), next(it), next(it)
    y_ref = next(it)
    b_ref = None if combine else next(it)
    hout_ref = next(it)
    r_s, v_s, al_s, be_s, kd_s, lw_s, y_s, g_s, h_s, qm_s, yn_s = (next(it) for _ in range(11))
    zf_s = next(it) if grid else None

    step = pl.program_id(1)
    tile = (n_tiles - 1 - step) if reverse else step

    @pl.when(step == 0)
    def _():
        h_s[...] = h0_ref[...]

    if grid:
        zf_s[GRID_W:GRID_W + tm, :] = z_ref[...]
        zf_s[0:GRID_W, :] = jnp.where(tile == 0, 0.0, zp_ref[...])
        zf_s[GRID_W + tm:, :] = jnp.where(tile == n_tiles - 1, 0.0, zn_ref[...])

    def prepare(rows):
        z = z_ref[rows, :]
        if grid:
            z = _shift_rows(z, zf_s.at[rows.start:rows.stop + 2 * GRID_W, :], mu_ref[...])
        r = z[:, 0:D_RWKV]
        k = z[:, D_RWKV:2 * D_RWKV]
        v = z[:, 2 * D_RWKV:3 * D_RWKV]
        kk = k * kk_ref[...]
        kk = kk * lax.rsqrt(jnp.maximum(_head_sums(kk * kk, ones_ref), 1e-24))
        w_raw = w0_ref[...] + _bdot(jnp.tanh(z[:, OFF_WD:OFF_AD]).astype(BF16), w2_ref[...])
        a = _sigmoid(a0_ref[...] + _bdot(z[:, OFF_AD:OFF_GD].astype(BF16), a2_ref[...]))
        kd = k * (1.0 + (a - 1.0) * ka_ref[...])
        bonus = _head_sums(r * kd * rk_ref[...], ones_ref) * v
        if combine:
            y_ref[rows, :] = bonus + bo_ref[rows, :]
            g_s[rows, :] = _bdot(_sigmoid(z[:, OFF_GD:]).astype(BF16), g2_ref[...])
        else:
            b_ref[rows, :] = bonus
        r_s[rows, :] = r
        v_s[rows, :] = v
        al_s[rows, :] = -kk
        be_s[rows, :] = a * kk
        kd_s[rows, :] = kd
        lw_s[rows, :] = -DECAY_SCALE * _sigmoid(w_raw)

    def finish(rows):
        if combine:
            y_sum = y_s[rows, :] + yo_ref[rows, :]
            mean = _head_sums(y_sum, ones_ref) * (1.0 / HEAD_SIZE)
            cen = y_sum - mean
            var = _head_sums(cen * cen, ones_ref) * (1.0 / HEAD_SIZE)
            y_gn = cen * lax.rsqrt(var + GN_EPS) * lnw_ref[...] + lnb_ref[...]
            y_ref[rows, :] = (y_gn + y_ref[rows, :]) * g_s[rows, :]
        else:
            y_ref[rows, :] = y_s[rows, :]

    n_chunks = tm // CHUNK
    ri = lax.broadcasted_iota(jnp.int32, (CHUNK, CHUNK), 0)
    ci = lax.broadcasted_iota(jnp.int32, (CHUNK, CHUNK), 1)
    tri = ((ci >= ri) if reverse else (ci <= ri)).astype(BF16)
    rj = lax.broadcasted_iota(jnp.int32, (CHUNK, PAIR), 0)
    cj = lax.broadcasted_iota(jnp.int32, (CHUNK, PAIR), 1) % CHUNK
    eye = rj == cj
    strict = (cj > rj) if reverse else (cj < rj)
    incl = (cj >= rj) if reverse else (cj <= rj)
    head0 = lax.broadcasted_iota(jnp.int32, (1, PAIR), 1) < HEAD_SIZE
    last = 0 if reverse else CHUNK - 1
    blockdiag = lambda x: _stack_masked(x, head0)

    def head_transpose(x):
        t = blockdiag(x).T
        return t[:CHUNK] + t[CHUNK:]

    same_block = lambda n: (rj & -n) == (cj & -n)
    levels = [2 ** i for i in range(1, int(np.log2(CHUNK)))]
    base_mask = same_block(2)
    pair_masks = {n: same_block(2 * n) & ~same_block(n) for n in levels}

    def chunk_rows(c):
        off = ((n_chunks - 1 - c) if reverse else c) * CHUNK
        return slice(off, off + CHUNK)

    def local_phase(chunks):
        units = []
        for c in chunks:
            rows = chunk_rows(c)
            lw = lw_s[rows, :]
            cum = sum(_bdot(tri, piece) for piece in _split(lw, CUM_PIECES))
            tot = cum[last:last + 1, :]
            g_inv = jnp.exp(-cum)
            g_end = jnp.exp(tot - cum)
            g_tot = jnp.exp(tot)
            abar = al_s[rows, :] * jnp.exp(cum - lw)
            rbar = r_s[rows, :] * jnp.exp(cum)
            be = be_s[rows, :]
            kdc = kd_s[rows, :]
            btil, ktil, bhat, khat = be * g_inv, kdc * g_inv, be * g_end, kdc * g_end
            vc = v_s[rows, :]
            for p in range(N_PAIRS):
                lanes = slice(p * PAIR, (p + 1) * PAIR)
                cut = lambda x: x[:, lanes]
                units.append(dict(c=c, p=p, a=cut(abar).astype(BF16), r=cut(rbar), v=cut(vc).astype(BF16),
                                  bt=cut(btil).astype(BF16), kt=cut(ktil).astype(BF16),
                                  bh=cut(bhat), kh=cut(khat), g_tot=cut(g_tot)))
        for u in units:
            ar = jnp.concatenate([u["a"], u["r"].astype(BF16)], axis=0)
            u["gb"] = _bdot(ar, blockdiag(u["bt"]), NT)
            u["gk"] = _bdot(ar, blockdiag(u["kt"]), NT)
        for u in units:
            gb, gk = u.pop("gb"), u.pop("gk")
            u["l_ab"] = jnp.where(strict, gb[:CHUNK], 0.0)
            u["l_ak"] = jnp.where(strict, gk[:CHUNK], 0.0).astype(BF16)
            u["l_rbk"] = jnp.concatenate([jnp.where(incl, gb[CHUNK:], 0.0),
                                          jnp.where(incl, gk[CHUNK:], 0.0)], axis=1).astype(BF16)
            u["t"] = jnp.where(eye, 1.0, jnp.where(base_mask, u["l_ab"], 0.0))
        for n in levels:
            for u in units:
                e = jnp.where(pair_masks[n], u["l_ab"], 0.0).astype(BF16)
                u["et"] = _bdot(e, blockdiag(u["t"].astype(BF16)))
            for u in units:
                u["t"] = u["t"] + _bdot(u["t"].astype(BF16), blockdiag(u["et"].astype(BF16)))
        for u in units:
            u["x1"] = _bdot(u["l_ak"], blockdiag(u["v"]))
        for u in units:
            rhs = jnp.concatenate([blockdiag(u["a"]), blockdiag(u["x1"].astype(BF16))], axis=1)
            u["pu"] = _bdot(u["t"].astype(BF16), rhs)
        for u in units:
            pu = u["pu"].astype(BF16)
            v_bd = blockdiag(u["v"])
            rhs = jnp.concatenate(
                [jnp.concatenate([blockdiag(pu[:, :PAIR]), blockdiag(pu[:, PAIR:])], axis=1),
                 jnp.concatenate([jnp.zeros_like(v_bd), v_bd], axis=1)], axis=0)
            bkh_t = jnp.concatenate([head_transpose(u["bh"]), head_transpose(u["kh"])], axis=1)
            o = _bdot(jnp.concatenate([u["l_rbk"], bkh_t.astype(BF16)], axis=0), rhs)
            q = u["r"] + o[:CHUNK, :PAIR]
            m = jnp.where(eye, u["g_tot"], 0.0) + o[CHUNK:, :PAIR]
            qm_s[u["c"], u["p"]] = jnp.concatenate([q, m], axis=0).astype(BF16)
            yn_s[u["c"], u["p"]] = o[:, PAIR:]

    def state_step(c):
        for p in range(N_PAIRS):
            oh = _bdot(qm_s[c, p], blockdiag(h_s[p].astype(BF16))) + yn_s[c, p]
            y_s[chunk_rows(c), p * PAIR:(p + 1) * PAIR] = oh[:CHUNK]
            h_s[p] = oh[CHUNK:]

    groups = [list(range(g, min(g + CHUNKS_PER_STEP, n_chunks))) for g in range(0, n_chunks, CHUNKS_PER_STEP)]

    def group_rows(chunks):
        spans = [chunk_rows(c) for c in chunks]
        return slice(min(s.start for s in spans), max(s.stop for s in spans))

    def state_phase(chunks):
        for c in chunks:
            state_step(c)

    prepare(group_rows(groups[0]))
    local_phase(groups[0])
    for prev, cur in zip(groups[:-1], groups[1:]):
        prepare(group_rows(cur))
        local_phase(cur)
        state_phase(prev)
        finish(group_rows(prev))
    state_phase(groups[-1])
    finish(group_rows(groups[-1]))

    @pl.when(step == n_tiles - 1)
    def _():
        hout_ref[...] = h_s[...]


def _scan(z, h0, prm, d, grid, other=None):
    b, s, _ = z.shape
    tm = 512 if grid else s
    n_tiles = s // tm
    reverse = d == 1
    combine = other is not None
    tile_of = (lambda j: n_tiles - 1 - j) if reverse else (lambda j: j)
    tok = lambda w: pl.BlockSpec((None, tm, w), lambda i, j: (i, tile_of(j), 0))
    halo_per_tile = tm // GRID_W
    n_halo = s // GRID_W
    in_specs = [tok(D_Z)]
    args = [z]
    if grid:
        in_specs += [pl.BlockSpec((None, GRID_W, D_Z),
                                  lambda i, j: (i, jnp.maximum(tile_of(j) * halo_per_tile - 1, 0), 0)),
                     pl.BlockSpec((None, GRID_W, D_Z),
                                  lambda i, j: (i, jnp.minimum((tile_of(j) + 1) * halo_per_tile, n_halo - 1), 0))]
        args += [z, z]
    state_spec = pl.BlockSpec((None, N_PAIRS, HEAD_SIZE, PAIR), lambda i, j: (i, 0, 0, 0))
    in_specs.append(state_spec)
    args.append(h0)
    if combine:
        in_specs += [tok(D_RWKV), tok(D_RWKV)]
        args += list(other)
    row = lambda x: x.reshape(1, -1)
    small = [row(prm["mu"]), row(prm["k_k"]), row(prm["k_a"]), row(prm["r_k"]),
             row(prm["w0"][d]), row(prm["a0"][d]),
             jnp.zeros((2 * DECAY_LORA, D_RWKV), BF16).at[d * DECAY_LORA:(d + 1) * DECAY_LORA].set(
                 prm["w2"][d].astype(BF16)),
             jnp.zeros((2 * ICLR_LORA, D_RWKV), BF16).at[d * ICLR_LORA:(d + 1) * ICLR_LORA].set(
                 prm["a2"][d].astype(BF16)),
             jnp.asarray(np.kron(np.eye(HEADS_PER_SUM), np.ones((HEAD_SIZE, HEAD_SIZE))), BF16)]
    if combine:
        small += [row(prm["ln_w"]), row(prm["ln_b"]), prm["g2"].astype(BF16)]
    in_specs += [_const_spec(x.shape) for x in small]
    args += small
    out_specs = [tok(D_RWKV)]
    out_shape = [jax.ShapeDtypeStruct((b, s, D_RWKV), F32)]
    if not combine:
        out_specs.append(tok(D_RWKV))
        out_shape.append(jax.ShapeDtypeStruct((b, s, D_RWKV), F32))
    out_specs.append(state_spec)
    out_shape.append(jax.ShapeDtypeStruct((b, N_PAIRS, HEAD_SIZE, PAIR), F32))
    n_chunks = tm // CHUNK
    scratch = ([pltpu.VMEM((tm, D_RWKV), F32) for _ in range(8)]
               + [pltpu.VMEM((N_PAIRS, CHUNK, PAIR), F32),
                  pltpu.VMEM((n_chunks, N_PAIRS, 2 * CHUNK, PAIR), BF16),
                  pltpu.VMEM((n_chunks, N_PAIRS, 2 * CHUNK, PAIR), F32)]
               + ([pltpu.VMEM((tm + 2 * GRID_W, D_Z), F32)] if grid else []))
    return pl.pallas_call(
        functools.partial(_scan_kernel, grid=grid, reverse=reverse, combine=combine, n_tiles=n_tiles, tm=tm),
        grid=(b, n_tiles),
        in_specs=in_specs,
        out_specs=out_specs,
        out_shape=out_shape,
        scratch_shapes=scratch,
        compiler_params=_params(("arbitrary", "arbitrary")),
        name="scan_bwd" if reverse else "scan_fwd",
    )(*args)


def _pack_state(s):
    b = s.shape[0]
    h = jnp.swapaxes(s, -1, -2).reshape(b, N_PAIRS, 2, HEAD_SIZE, HEAD_SIZE)
    return jnp.transpose(h, (0, 1, 3, 2, 4)).reshape(b, N_PAIRS, HEAD_SIZE, PAIR)


def _unpack_state(hp):
    b = hp.shape[0]
    h = jnp.transpose(hp.reshape(b, N_PAIRS, HEAD_SIZE, 2, HEAD_SIZE), (0, 1, 3, 2, 4))
    return jnp.swapaxes(h.reshape(b, N_RWKV_HEADS, HEAD_SIZE, HEAD_SIZE), -1, -2)


def _mixer_heads(z, s0_f, s0_b, prm, grid):
    y_b, bonus_b, h_b = _scan(z, _pack_state(s0_b), prm, 1, grid)
    y, h_f = _scan(z, _pack_state(s0_f), prm, 0, grid, other=(y_b, bonus_b))
    return y, _unpack_state(h_f), _unpack_state(h_b)


def _path(x, mod, mod_per_batch, s0_f, s0_b, grid, w, prm, tm):
    x1, u, z = _front(x, mod, mod_per_batch, w["norm_g"], w["g0"], w["u0"], w["d0"], w["w_fold"], w["w_z"],
                      prm["mu"], tm, grid)
    y_four = _fourier_two_stage(u) if grid else _fourier_dense(u)
    y_rwkv, s_f, s_b = _mixer_heads(z, s0_f, s0_b, prm, grid)
    y = _back(x1, y_four, y_rwkv, mod, mod_per_batch, w["norm_g"], w["final"], w["w_out"],
              w["g1"], w["u1"], w["d1"], tm)
    return y, s_f, s_b


def kernel(x_prompt, x_sample, state_fwd, state_bwd, c, c_ctx, w_mod, b_mod, norm_g, ffn_w_gate, ffn_w_up,
           ffn_w_down, w_in, shift_mu, decay_w0, decay_w2, iclr_a0, iclr_a2, gate_g2, k_k, k_a, r_k,
           ln_x_w, ln_x_b, w_out, final_norm):
    depth = w_mod.shape[0]
    assert depth == 1, "the back kernel applies the final norm, so exactly one layer is supported"
    bp = x_prompt.shape[0]
    bs = x_sample.shape[0]
    xp, xs = x_prompt, x_sample
    new_f, new_b = [], []
    for l in range(depth):
        cvec = jnp.zeros((8, D_MODEL), F32).at[:bs].set(c).at[bs].set(c_ctx)
        mod = _modulation(cvec, w_mod[l], b_mod[l]).reshape(8, N_MOD, D_MODEL)
        w = {"norm_g": norm_g[l], "final": final_norm,
             "g0": ffn_w_gate[l, 0].astype(BF16), "u0": ffn_w_up[l, 0].astype(BF16),
             "d0": ffn_w_down[l, 0].astype(BF16),
             "g1": ffn_w_gate[l, 1].astype(BF16), "u1": ffn_w_up[l, 1].astype(BF16),
             "d1": ffn_w_down[l, 1].astype(BF16),
             "w_fold": _fold_group_dft(w_in[l]), "w_z": w_in[l, :, D_FOURIER:].astype(BF16),
             "w_out": w_out[l].astype(BF16)}
        prm = {"mu": shift_mu[l], "w0": decay_w0[l], "w2": decay_w2[l], "a0": iclr_a0[l], "a2": iclr_a2[l],
               "g2": gate_g2[l], "k_k": k_k[l], "k_a": k_a[l], "r_k": r_k[l], "ln_w": ln_x_w[l], "ln_b": ln_x_b[l]}
        zero_state = jnp.zeros((bp, N_RWKV_HEADS, HEAD_SIZE, HEAD_SIZE), F32)
        xp, s_f, s_b = _path(xp, mod[bs:bs + 1], False, zero_state, zero_state, False, w, prm, 256)
        new_f.append(s_f)
        new_b.append(s_b)
        xs, _, _ = _path(xs, mod[:bs], True, state_fwd[:, l], state_bwd[:, l], True, w, prm, 256)
    return xp, xs, jnp.stack(new_f, axis=1), jnp.stack(new_b, axis=1)
```

```python
import functools

import numpy as np
import jax
import jax.numpy as jnp
from jax import lax
from jax.experimental import pallas as pl
from jax.experimental.pallas import tpu as pltpu

D_MODEL = 1024
GRID_W = 64
D_FOURIER = 512
N_FOURIER_GROUPS = 8
FOURIER_GROUP = D_FOURIER // N_FOURIER_GROUPS
D_RWKV = D_MODEL - D_FOURIER
HEAD_SIZE = 64
N_RWKV_HEADS = D_RWKV // HEAD_SIZE
N_PAIRS = N_RWKV_HEADS // 2
PAIR = 2 * HEAD_SIZE
DECAY_LORA = 64
ICLR_LORA = 64
GATE_LORA = 128
D_Z = 3 * D_RWKV + 2 * DECAY_LORA + 2 * ICLR_LORA + GATE_LORA
D_FOLD = 2 * D_FOURIER
D_FF = 2816
N_MOD = 9
RMS_EPS = 1e-6
GN_EPS = 64e-5
CHUNK = 64
OFF_WD = 3 * D_RWKV
OFF_AD = OFF_WD + 2 * DECAY_LORA
OFF_GD = OFF_AD + 2 * ICLR_LORA
DECAY_SCALE = float(np.exp(-0.5))
V7X_VMEM_LIMIT = 56 * 1024 * 1024

F32 = jnp.float32
BF16 = jnp.bfloat16


def _dot32(a, b):
    a_hi = a.astype(BF16)
    b_hi = b.astype(BF16)
    a_lo = (a - a_hi.astype(F32)).astype(BF16)
    b_lo = (b - b_hi.astype(F32)).astype(BF16)
    dot = lambda x, y: jnp.dot(x, y, preferred_element_type=F32)
    return dot(a_hi, b_hi) + (dot(a_hi, b_lo) + dot(a_lo, b_hi))


def _sigmoid(x):
    return 1.0 / (1.0 + jnp.exp(-x))


def _norm_mod(x, g, shift, scale):
    ms = jnp.mean(x * x, axis=-1, keepdims=True)
    return x * lax.rsqrt(ms + RMS_EPS) * g * (1.0 + scale) + shift


FF_CHUNKS = 1


def _swiglu(h, wg_ref, wu_ref, wd_ref):
    hb = h.astype(BF16)
    width = D_FF // FF_CHUNKS
    out = None
    for j in range(FF_CHUNKS):
        cols = slice(j * width, (j + 1) * width)
        gate = jnp.dot(hb, wg_ref[:, cols], preferred_element_type=F32)
        up = jnp.dot(hb, wu_ref[:, cols], preferred_element_type=F32)
        act = gate * _sigmoid(gate) * up
        part = jnp.dot(act.astype(BF16), wd_ref[cols, :], preferred_element_type=F32)
        out = part if out is None else out + part
    return out


def _const_spec(shape):
    nd = len(shape)
    return pl.BlockSpec(shape, lambda *_: (0,) * nd, pipeline_mode=pl.Buffered(1))


def _ffn_spec(shape, which):
    return pl.BlockSpec((None,) + shape, lambda *_: (which, 0, 0), pipeline_mode=pl.Buffered(1))


def _params(sem):
    return pltpu.CompilerParams(dimension_semantics=sem, vmem_limit_bytes=V7X_VMEM_LIMIT)


def _mod_kernel(c_ref, w_ref, b_ref, o_ref):
    c = c_ref[...]
    o_ref[...] = _dot32(c * _sigmoid(c), w_ref[...]) + b_ref[...]


def _modulation(cvec, w_mod, b_mod):
    n = w_mod.shape[1]
    tn = n // 8
    return pl.pallas_call(
        _mod_kernel,
        grid=(n // tn,),
        in_specs=[pl.BlockSpec((8, D_MODEL), lambda j: (0, 0)),
                  pl.BlockSpec((D_MODEL, tn), lambda j: (0, j)),
                  pl.BlockSpec((1, tn), lambda j: (0, j))],
        out_specs=pl.BlockSpec((8, tn), lambda j: (0, j)),
        out_shape=jax.ShapeDtypeStruct((8, n), F32),
        compiler_params=_params(("arbitrary",)),
        name="modulation",
    )(cvec, w_mod, b_mod.reshape(1, n))


def _shift_adjacent(z, mu, grid):
    tm = z.shape[0]
    row = lax.broadcasted_iota(jnp.int32, (tm, 1), 0)
    lane = lax.broadcasted_iota(jnp.int32, (1, z.shape[1]), 1) % 4
    back1 = pltpu.roll(z, 1, 0)
    fwd1 = pltpu.roll(z, tm - 1, 0)
    if grid:
        col = row % GRID_W
        side = jnp.where(lane == 0, jnp.where(col == 0, 0.0, back1), jnp.where(col == GRID_W - 1, 0.0, fwd1))
        return jnp.where(lane < 2, z + mu * (side - z), z)
    side = jnp.where(lane % 2 == 0, jnp.where(row == 0, 0.0, back1), jnp.where(row == tm - 1, 0.0, fwd1))
    return z + mu * (side - z)


def _front_kernel(x_ref, mod_ref, g_ref, wg_ref, wu_ref, wd_ref, wfold_ref, wz_ref, mu_ref,
                  x1_ref, u_ref, z_ref, *, grid):
    x = x_ref[...]
    h = _norm_mod(x, g_ref[0:1, :], mod_ref[0:1, :], mod_ref[1:2, :])
    x1 = x + 0.5 * mod_ref[2:3, :] * _swiglu(h, wg_ref, wu_ref, wd_ref)
    x1_ref[...] = x1
    h2 = _norm_mod(x1, g_ref[1:2, :], mod_ref[3:4, :], mod_ref[4:5, :]).astype(BF16)
    u_ref[...] = jnp.dot(h2, wfold_ref[...], preferred_element_type=F32)
    z_ref[...] = _shift_adjacent(jnp.dot(h2, wz_ref[...], preferred_element_type=F32), mu_ref[...], grid)


def _front(x, mod, mod_per_batch, norm_g, wg, wu, wd, w_fold, w_z, mu, tm, grid):
    b, s, _ = x.shape
    nt = s // tm
    assert tm % GRID_W == 0 if grid else tm == s
    tok = lambda w: pl.BlockSpec((None, tm, w), lambda i, j: (i, j, 0))
    mod_map = (lambda i, j: (i, 0, 0)) if mod_per_batch else (lambda i, j: (0, 0, 0))
    return pl.pallas_call(
        functools.partial(_front_kernel, grid=grid),
        grid=(b, nt),
        in_specs=[tok(D_MODEL),
                  pl.BlockSpec((None, N_MOD, D_MODEL), mod_map),
                  _const_spec((3, D_MODEL)),
                  _ffn_spec((D_MODEL, D_FF), 0), _ffn_spec((D_MODEL, D_FF), 0),
                  _ffn_spec((D_FF, D_MODEL), 0), _const_spec((D_MODEL, D_FOLD)), _const_spec((D_MODEL, D_Z)),
                  _const_spec((1, D_Z))],
        out_specs=[tok(D_MODEL), tok(D_FOLD), tok(D_Z)],
        out_shape=[jax.ShapeDtypeStruct((b, s, D_MODEL), F32),
                   jax.ShapeDtypeStruct((b, s, D_FOLD), F32),
                   jax.ShapeDtypeStruct((b, s, D_Z), F32)],
        compiler_params=_params(("arbitrary", "arbitrary")),
        name="front",
    )(x, mod, norm_g, wg, wu, wd, w_fold, w_z, mu.reshape(1, D_Z))


def _back_kernel(x1_ref, yf_ref, yr_ref, mod_ref, g_ref, fin_ref, wo_ref, wg_ref, wu_ref, wd_ref, o_ref):
    mixed = (jnp.dot(yf_ref[...].astype(BF16), wo_ref[0:D_FOURIER, :], preferred_element_type=F32)
             + jnp.dot(yr_ref[...].astype(BF16), wo_ref[D_FOURIER:, :], preferred_element_type=F32))
    x2 = x1_ref[...] + mod_ref[5:6, :] * mixed
    h = _norm_mod(x2, g_ref[2:3, :], mod_ref[6:7, :], mod_ref[7:8, :])
    x3 = x2 + 0.5 * mod_ref[8:9, :] * _swiglu(h, wg_ref, wu_ref, wd_ref)
    ms = jnp.mean(x3 * x3, axis=-1, keepdims=True)
    o_ref[...] = x3 * lax.rsqrt(ms + RMS_EPS) * fin_ref[...]


def _back(x1, yf, yr, mod, mod_per_batch, norm_g, final_norm, w_out, wg, wu, wd, tm):
    b, s, _ = x1.shape
    nt = s // tm
    tok = lambda w: pl.BlockSpec((None, tm, w), lambda i, j: (i, j, 0))
    mod_map = (lambda i, j: (i, 0, 0)) if mod_per_batch else (lambda i, j: (0, 0, 0))
    return pl.pallas_call(
        _back_kernel,
        grid=(b, nt),
        in_specs=[tok(D_MODEL), tok(D_FOURIER), tok(D_RWKV),
                  pl.BlockSpec((None, N_MOD, D_MODEL), mod_map),
                  _const_spec((3, D_MODEL)), _const_spec((1, D_MODEL)),
                  _const_spec((D_MODEL, D_MODEL)),
                  _ffn_spec((D_MODEL, D_FF), 1), _ffn_spec((D_MODEL, D_FF), 1),
                  _ffn_spec((D_FF, D_MODEL), 1)],
        out_specs=tok(D_MODEL),
        out_shape=jax.ShapeDtypeStruct((b, s, D_MODEL), F32),
        compiler_params=_params(("arbitrary", "arbitrary")),
        name="back",
    )(x1, yf, yr, mod, norm_g, final_norm.reshape(1, D_MODEL), w_out, wg, wu, wd)


def _fold_kernel(w_ref, cs_ref, o_ref):
    o_ref[...] = _dot32(w_ref[...], cs_ref[...]).astype(o_ref.dtype)


def _fold_group_dft(w_in):
    q = np.arange(FOURIER_GROUP)
    ang = 2.0 * np.pi * ((q[:, None] * q[None, :]) % FOURIER_GROUP) / FOURIER_GROUP
    eye = np.eye(N_FOURIER_GROUPS)
    cs = jnp.asarray(np.concatenate([np.kron(eye, np.cos(ang)), np.kron(eye, np.sin(ang))], axis=1), F32)
    return pl.pallas_call(
        _fold_kernel,
        grid=(1,),
        in_specs=[pl.BlockSpec((D_MODEL, D_FOURIER), lambda i: (0, 0)),
                  pl.BlockSpec((D_FOURIER, D_FOLD), lambda i: (0, 0))],
        out_specs=pl.BlockSpec((D_MODEL, D_FOLD), lambda i: (0, 0)),
        out_shape=jax.ShapeDtypeStruct((D_MODEL, D_FOLD), BF16),
        compiler_params=_params(("arbitrary",)),
        name="fold_group_dft",
    )(w_in, cs)


def _stack_cos_sin(x):
    return jnp.concatenate([x[:, :D_FOURIER], x[:, D_FOURIER:]], axis=0).astype(BF16)


def _fourier_dense_kernel(x_ref, p_ref, o_ref):
    o_ref[...] = _bdot(p_ref[...], _stack_cos_sin(x_ref[...]))


def _fourier_dense(xcs):
    b, s, _ = xcs.shape
    pos = np.arange(s)
    ang = 2.0 * np.pi * ((pos[:, None] * pos[None, :]) % s) / s
    table = np.concatenate([np.cos(ang), -np.sin(ang)], axis=1) / np.sqrt(s * FOURIER_GROUP)
    return pl.pallas_call(
        _fourier_dense_kernel,
        grid=(b,),
        in_specs=[pl.BlockSpec((None, s, D_FOLD), lambda i: (i, 0, 0)), _const_spec((s, 2 * s))],
        out_specs=pl.BlockSpec((None, s, D_FOURIER), lambda i: (i, 0, 0)),
        out_shape=jax.ShapeDtypeStruct((b, s, D_FOURIER), F32),
        compiler_params=_params(("arbitrary",)),
        name="fourier_dense",
    )(xcs, jnp.asarray(table, F32).astype(BF16))


FFT_ROWS = 16


def _fourier_stage1_kernel(x_ref, f_ref, z_ref):
    for j in range(FFT_ROWS):
        z = _bdot(f_ref[...], _stack_cos_sin(x_ref[:, j, :]))
        n = z.shape[0] // 2
        z_ref[:, j, :] = jnp.concatenate([z[:n], z[n:]], axis=1)


def _fourier_stage2_kernel(z_ref, g_ref, o_ref):
    for j in range(FFT_ROWS):
        o_ref[:, j, :] = _bdot(g_ref[j], _stack_cos_sin(z_ref[j]))


def _fourier_two_stage(xcs):
    b, s, _ = xcs.shape
    n = int(round(np.sqrt(s)))
    assert n * n == s and n % FFT_ROWS == 0
    idx = np.arange(n)
    ang = 2.0 * np.pi * ((idx[:, None] * idx[None, :]) % n) / n
    fc, fs = np.cos(ang), np.sin(ang)
    f2 = jnp.asarray(np.block([[fc, -fs], [fs, fc]]), F32).astype(BF16)
    z = pl.pallas_call(
        _fourier_stage1_kernel,
        grid=(b, n // FFT_ROWS),
        in_specs=[pl.BlockSpec((None, n, FFT_ROWS, D_FOLD), lambda i, j: (i, 0, j, 0)),
                  _const_spec((2 * n, 2 * n))],
        out_specs=pl.BlockSpec((None, n, FFT_ROWS, D_FOLD), lambda i, j: (i, 0, j, 0)),
        out_shape=jax.ShapeDtypeStruct((b, n, n, D_FOLD), F32),
        compiler_params=_params(("arbitrary", "arbitrary")),
        name="fourier_stage1",
    )(xcs.reshape(b, n, n, D_FOLD), f2)
    bb, aa, s1 = idx[:, None, None], idx[None, :, None], idx[None, None, :]
    ang2 = 2.0 * np.pi * ((s1 * (n * aa + bb)) % s) / s
    g2 = np.concatenate([np.cos(ang2), -np.sin(ang2)], axis=2) / np.sqrt(s * FOURIER_GROUP)
    out = pl.pallas_call(
        _fourier_stage2_kernel,
        grid=(b, n // FFT_ROWS),
        in_specs=[pl.BlockSpec((None, FFT_ROWS, n, D_FOLD), lambda i, j: (i, j, 0, 0)),
                  pl.BlockSpec((FFT_ROWS, n, 2 * n), lambda i, j: (j, 0, 0))],
        out_specs=pl.BlockSpec((None, n, FFT_ROWS, D_FOURIER), lambda i, j: (i, 0, j, 0)),
        out_shape=jax.ShapeDtypeStruct((b, n, n, D_FOURIER), F32),
        compiler_params=_params(("arbitrary", "arbitrary")),
        name="fourier_stage2",
    )(z, jnp.asarray(g2, F32).astype(BF16))
    return out.reshape(b, s, D_FOURIER)


def _shift_rows(z, zf_ref, mu):
    tm = z.shape[0]
    lane = lax.broadcasted_iota(jnp.int32, (1, z.shape[1]), 1) % 4
    up = zf_ref[0:tm, :]
    down = zf_ref[2 * GRID_W:2 * GRID_W + tm, :]
    return z + jnp.where(lane < 2, 0.0, mu) * (jnp.where(lane == 2, up, down) - z)


def _stack_masked(x, head0):
    return jnp.concatenate([jnp.where(head0, x, 0.0), jnp.where(head0, 0.0, x)], axis=0)


NT = (((1,), (1,)), ((), ()))
CUM_PIECES = 3
CHUNKS_PER_STEP = 4


def _split(x, pieces):
    out = []
    for i in range(pieces):
        hi = x.astype(BF16)
        out.append(hi)
        if i + 1 < pieces:
            x = x - hi.astype(F32)
    return out


def _bdot(a, b, dims=None):
    if dims is None:
        return jnp.dot(a, b, preferred_element_type=F32)
    return lax.dot_general(a, b, dims, preferred_element_type=F32)


HEADS_PER_SUM = 4


def _head_sums(x, ones_ref):
    w = HEADS_PER_SUM * HEAD_SIZE
    return jnp.concatenate([_bdot(x[:, i:i + w].astype(BF16), ones_ref[...]) for i in range(0, D_RWKV, w)], axis=1)


def _scan_kernel(*refs, grid, reverse, combine, n_tiles, tm):
    it = iter(refs)
    z_ref = next(it)
    zp_ref = next(it) if grid else None
    zn_ref = next(it) if grid else None
    h0_ref = next(it)
    if combine:
        yo_ref, bo_ref = next(it), next(it)
    mu_ref, kk_ref, ka_ref, rk_ref, w0_ref, a0_ref, w2_ref, a2_ref, ones_ref = (next(it) for _ in range(9))
    if combine:
        lnw_ref, lnb_ref, g2_ref = next(it), next(it), next(it)
    y_ref = next(it)
    b_ref = None if combine else next(it)
    hout_ref = next(it)
    r_s, v_s, al_s, be_s, kd_s, lw_s, y_s, g_s, h_s, qm_s, yn_s = (next(it) for _ in range(11))
    zf_s = next(it) if grid else None

    step = pl.program_id(1)
    tile = (n_tiles - 1 - step) if reverse else step

    @pl.when(step == 0)
    def _():
        h_s[...] = h0_ref[...]

    if grid:
        zf_s[GRID_W:GRID_W + tm, :] = z_ref[...]
        zf_s[0:GRID_W, :] = jnp.where(tile == 0, 0.0, zp_ref[...])
        zf_s[GRID_W + tm:, :] = jnp.where(tile == n_tiles - 1, 0.0, zn_ref[...])

    def prepare(rows):
        z = z_ref[rows, :]
        if grid:
            z = _shift_rows(z, zf_s.at[rows.start:rows.stop + 2 * GRID_W, :], mu_ref[...])
        r = z[:, 0:D_RWKV]
        k = z[:, D_RWKV:2 * D_RWKV]
        v = z[:, 2 * D_RWKV:3 * D_RWKV]
        kk = k * kk_ref[...]
        kk = kk * lax.rsqrt(jnp.maximum(_head_sums(kk * kk, ones_ref), 1e-24))
        w_raw = w0_ref[...] + _bdot(jnp.tanh(z[:, OFF_WD:OFF_AD]).astype(BF16), w2_ref[...])
        a = _sigmoid(a0_ref[...] + _bdot(z[:, OFF_AD:OFF_GD].astype(BF16), a2_ref[...]))
        kd = k * (1.0 + (a - 1.0) * ka_ref[...])
        bonus = _head_sums(r * kd * rk_ref[...], ones_ref) * v
        if combine:
            y_ref[rows, :] = bonus + bo_ref[rows, :]
            g_s[rows, :] = _bdot(_sigmoid(z[:, OFF_GD:]).astype(BF16), g2_ref[...])
        else:
            b_ref[rows, :] = bonus
        r_s[rows, :] = r
        v_s[rows, :] = v
        al_s[rows, :] = -kk
        be_s[rows, :] = a * kk
        kd_s[rows, :] = kd
        lw_s[rows, :] = -DECAY_SCALE * _sigmoid(w_raw)

    def finish(rows):
        if combine:
            y_sum = y_s[rows, :] + yo_ref[rows, :]
            mean = _head_sums(y_sum, ones_ref) * (1.0 / HEAD_SIZE)
            cen = y_sum - mean
            var = _head_sums(cen * cen, ones_ref) * (1.0 / HEAD_SIZE)
            y_gn = cen * lax.rsqrt(var + GN_EPS) * lnw_ref[...] + lnb_ref[...]
            y_ref[rows, :] = (y_gn + y_ref[rows, :]) * g_s[rows, :]
        else:
            y_ref[rows, :] = y_s[rows, :]

    n_chunks = tm // CHUNK
    ri = lax.broadcasted_iota(jnp.int32, (CHUNK, CHUNK), 0)
    ci = lax.broadcasted_iota(jnp.int32, (CHUNK, CHUNK), 1)
    tri = ((ci >= ri) if reverse else (ci <= ri)).astype(BF16)
    rj = lax.broadcasted_iota(jnp.int32, (CHUNK, PAIR), 0)
    cj = lax.broadcasted_iota(jnp.int32, (CHUNK, PAIR), 1) % CHUNK
    eye = rj == cj
    strict = (cj > rj) if reverse else (cj < rj)
    incl = (cj >= rj) if reverse else (cj <= rj)
    head0 = lax.broadcasted_iota(jnp.int32, (1, PAIR), 1) < HEAD_SIZE
    last = 0 if reverse else CHUNK - 1
    blockdiag = lambda x: _stack_masked(x, head0)

    def head_transpose(x):
        t = blockdiag(x).T
        return t[:CHUNK] + t[CHUNK:]

    same_block = lambda n: (rj & -n) == (cj & -n)
    levels = [2 ** i for i in range(1, int(np.log2(CHUNK)))]
    base_mask = same_block(2)
    pair_masks = {n: same_block(2 * n) & ~same_block(n) for n in levels}

    def chunk_rows(c):
        off = ((n_chunks - 1 - c) if reverse else c) * CHUNK
        return slice(off, off + CHUNK)

    def local_phase(chunks):
        units = []
        for c in chunks:
            rows = chunk_rows(c)
            lw = lw_s[rows, :]
            cum = sum(_bdot(tri, piece) for piece in _split(lw, CUM_PIECES))
            tot = cum[last:last + 1, :]
            g_inv = jnp.exp(-cum)
            g_end = jnp.exp(tot - cum)
            g_tot = jnp.exp(tot)
            abar = al_s[rows, :] * jnp.exp(cum - lw)
            rbar = r_s[rows, :] * jnp.exp(cum)
            be = be_s[rows, :]
            kdc = kd_s[rows, :]
            btil, ktil, bhat, khat = be * g_inv, kdc * g_inv, be * g_end, kdc * g_end
            vc = v_s[rows, :]
            for p in range(N_PAIRS):
                lanes = slice(p * PAIR, (p + 1) * PAIR)
                cut = lambda x: x[:, lanes]
                units.append(dict(c=c, p=p, a=cut(abar).astype(BF16), r=cut(rbar), v=cut(vc).astype(BF16),
                                  bt=cut(btil).astype(BF16), kt=cut(ktil).astype(BF16),
                                  bh=cut(bhat), kh=cut(khat), g_tot=cut(g_tot)))
        for u in units:
            ar = jnp.concatenate([u["a"], u["r"].astype(BF16)], axis=0)
            u["gb"] = _bdot(ar, blockdiag(u["bt"]), NT)
            u["gk"] = _bdot(ar, blockdiag(u["kt"]), NT)
        for u in units:
            gb, gk = u.pop("gb"), u.pop("gk")
            u["l_ab"] = jnp.where(strict, gb[:CHUNK], 0.0)
            u["l_ak"] = jnp.where(strict, gk[:CHUNK], 0.0).astype(BF16)
            u["l_rbk"] = jnp.concatenate([jnp.where(incl, gb[CHUNK:], 0.0),
                                          jnp.where(incl, gk[CHUNK:], 0.0)], axis=1).astype(BF16)
            u["t"] = jnp.where(eye, 1.0, jnp.where(base_mask, u["l_ab"], 0.0))
        for n in levels:
            for u in units:
                e = jnp.where(pair_masks[n], u["l_ab"], 0.0).astype(BF16)
                u["et"] = _bdot(e, blockdiag(u["t"].astype(BF16)))
            for u in units:
                u["t"] = u["t"] + _bdot(u["t"].astype(BF16), blockdiag(u["et"].astype(BF16)))
        for u in units:
            u["x1"] = _bdot(u["l_ak"], blockdiag(u["v"]))
        for u in units:
            rhs = jnp.concatenate([blockdiag(u["a"]), blockdiag(u["x1"].astype(BF16))], axis=1)
            u["pu"] = _bdot(u["t"].astype(BF16), rhs)
        for u in units:
            pu = u["pu"].astype(BF16)
            v_bd = blockdiag(u["v"])
            rhs = jnp.concatenate(
                [jnp.concatenate([blockdiag(pu[:, :PAIR]), blockdiag(pu[:, PAIR:])], axis=1),
                 jnp.concatenate([jnp.zeros_like(v_bd), v_bd], axis=1)], axis=0)
            bkh_t = jnp.concatenate([head_transpose(u["bh"]), head_transpose(u["kh"])], axis=1)
            o = _bdot(jnp.concatenate([u["l_rbk"], bkh_t.astype(BF16)], axis=0), rhs)
            q = u["r"] + o[:CHUNK, :PAIR]
            m = jnp.where(eye, u["g_tot"], 0.0) + o[CHUNK:, :PAIR]
            qm_s[u["c"], u["p"]] = jnp.concatenate([q, m], axis=0).astype(BF16)
            yn_s[u["c"], u["p"]] = o[:, PAIR:]

    def state_step(c):
        for p in range(N_PAIRS):
            oh = _bdot(qm_s[c, p], blockdiag(h_s[p].astype(BF16))) + yn_s[c, p]
            y_s[chunk_rows(c), p * PAIR:(p + 1) * PAIR] = oh[:CHUNK]
            h_s[p] = oh[CHUNK:]

    groups = [list(range(g, min(g + CHUNKS_PER_STEP, n_chunks))) for g in range(0, n_chunks, CHUNKS_PER_STEP)]

    def group_rows(chunks):
        spans = [chunk_rows(c) for c in chunks]
        return slice(min(s.start for s in spans), max(s.stop for s in spans))

    def state_phase(chunks):
        for c in chunks:
            state_step(c)

    prepare(group_rows(groups[0]))
    local_phase(groups[0])
    for prev, cur in zip(groups[:-1], groups[1:]):
        prepare(group_rows(cur))
        local_phase(cur)
        state_phase(prev)
        finish(group_rows(prev))
    state_phase(groups[-1])
    finish(group_rows(groups[-1]))

    @pl.when(step == n_tiles - 1)
    def _():
        hout_ref[...] = h_s[...]


def _scan(z, h0, prm, d, grid, other=None):
    b, s, _ = z.shape
    tm = 512 if grid else s
    n_tiles = s // tm
    reverse = d == 1
    combine = other is not None
    tile_of = (lambda j: n_tiles - 1 - j) if reverse else (lambda j: j)
    tok = lambda w: pl.BlockSpec((None, tm, w), lambda i, j: (i, tile_of(j), 0))
    halo_per_tile = tm // GRID_W
    n_halo = s // GRID_W
    in_specs = [tok(D_Z)]
    args = [z]
    if grid:
        in_specs += [pl.BlockSpec((None, GRID_W, D_Z),
                                  lambda i, j: (i, jnp.maximum(tile_of(j) * halo_per_tile - 1, 0), 0)),
                     pl.BlockSpec((None, GRID_W, D_Z),
                                  lambda i, j: (i, jnp.minimum((tile_of(j) + 1) * halo_per_tile, n_halo - 1), 0))]
        args += [z, z]
    state_spec = pl.BlockSpec((None, N_PAIRS, HEAD_SIZE, PAIR), lambda i, j: (i, 0, 0, 0))
    in_specs.append(state_spec)
    args.append(h0)
    if combine:
        in_specs += [tok(D_RWKV), tok(D_RWKV)]
        args += list(other)
    row = lambda x: x.reshape(1, -1)
    small = [row(prm["mu"]), row(prm["k_k"]), row(prm["k_a"]), row(prm["r_k"]),
             row(prm["w0"][d]), row(prm["a0"][d]),
             jnp.zeros((2 * DECAY_LORA, D_RWKV), BF16).at[d * DECAY_LORA:(d + 1) * DECAY_LORA].set(
                 prm["w2"][d].astype(BF16)),
             jnp.zeros((2 * ICLR_LORA, D_RWKV), BF16).at[d * ICLR_LORA:(d + 1) * ICLR_LORA].set(
                 prm["a2"][d].astype(BF16)),
             jnp.asarray(np.kron(np.eye(HEADS_PER_SUM), np.ones((HEAD_SIZE, HEAD_SIZE))), BF16)]
    if combine:
        small += [row(prm["ln_w"]), row(prm["ln_b"]), prm["g2"].astype(BF16)]
    in_specs += [_const_spec(x.shape) for x in small]
    args += small
    out_specs = [tok(D_RWKV)]
    out_shape = [jax.ShapeDtypeStruct((b, s, D_RWKV), F32)]
    if not combine:
        out_specs.append(tok(D_RWKV))
        out_shape.append(jax.ShapeDtypeStruct((b, s, D_RWKV), F32))
    out_specs.append(state_spec)
    out_shape.append(jax.ShapeDtypeStruct((b, N_PAIRS, HEAD_SIZE, PAIR), F32))
    n_chunks = tm // CHUNK
    scratch = ([pltpu.VMEM((tm, D_RWKV), F32) for _ in range(8)]
               + [pltpu.VMEM((N_PAIRS, CHUNK, PAIR), F32),
                  pltpu.VMEM((n_chunks, N_PAIRS, 2 * CHUNK, PAIR), BF16),
                  pltpu.VMEM((n_chunks, N_PAIRS, 2 * CHUNK, PAIR), F32)]
               + ([pltpu.VMEM((tm + 2 * GRID_W, D_Z), F32)] if grid else []))
    return pl.pallas_call(
        functools.partial(_scan_kernel, grid=grid, reverse=reverse, combine=combine, n_tiles=n_tiles, tm=tm),
        grid=(b, n_tiles),
        in_specs=in_specs,
        out_specs=out_specs,
        out_shape=out_shape,
        scratch_shapes=scratch,
        compiler_params=_params(("arbitrary", "arbitrary")),
        name="scan_bwd" if reverse else "scan_fwd",
    )(*args)


def _pack_state(s):
    b = s.shape[0]
    h = jnp.swapaxes(s, -1, -2).reshape(b, N_PAIRS, 2, HEAD_SIZE, HEAD_SIZE)
    return jnp.transpose(h, (0, 1, 3, 2, 4)).reshape(b, N_PAIRS, HEAD_SIZE, PAIR)


def _unpack_state(hp):
    b = hp.shape[0]
    h = jnp.transpose(hp.reshape(b, N_PAIRS, HEAD_SIZE, 2, HEAD_SIZE), (0, 1, 3, 2, 4))
    return jnp.swapaxes(h.reshape(b, N_RWKV_HEADS, HEAD_SIZE, HEAD_SIZE), -1, -2)


def _mixer_heads(z, s0_f, s0_b, prm, grid):
    y_b, bonus_b, h_b = _scan(z, _pack_state(s0_b), prm, 1, grid)
    y, h_f = _scan(z, _pack_state(s0_f), prm, 0, grid, other=(y_b, bonus_b))
    return y, _unpack_state(h_f), _unpack_state(h_b)


def _path(x, mod, mod_per_batch, s0_f, s0_b, grid, w, prm, tm, tm_back):
    x1, u, z = _front(x, mod, mod_per_batch, w["norm_g"], w["gate"], w["up"], w["down"], w["w_fold"], w["w_z"],
                      prm["mu"], tm, grid)
    y_four = _fourier_two_stage(u) if grid else _fourier_dense(u)
    y_rwkv, s_f, s_b = _mixer_heads(z, s0_f, s0_b, prm, grid)
    y = _back(x1, y_four, y_rwkv, mod, mod_per_batch, w["norm_g"], w["final"], w["w_out"],
              w["gate"], w["up"], w["down"], tm_back)
    return y, s_f, s_b


def kernel(x_prompt, x_sample, state_fwd, state_bwd, c, c_ctx, w_mod, b_mod, norm_g, ffn_w_gate, ffn_w_up,
           ffn_w_down, w_in, shift_mu, decay_w0, decay_w2, iclr_a0, iclr_a2, gate_g2, k_k, k_a, r_k,
           ln_x_w, ln_x_b, w_out, final_norm):
    depth = w_mod.shape[0]
    assert depth == 1, "the back kernel applies the final norm, so exactly one layer is supported"
    bp = x_prompt.shape[0]
    bs = x_sample.shape[0]
    xp, xs = x_prompt, x_sample
    new_f, new_b = [], []
    for l in range(depth):
        cvec = jnp.zeros((8, D_MODEL), F32).at[:bs].set(c).at[bs].set(c_ctx)
        mod = _modulation(cvec, w_mod[l], b_mod[l]).reshape(8, N_MOD, D_MODEL)
        w = {"norm_g": norm_g[l], "final": final_norm,
             "gate": ffn_w_gate[l].astype(BF16), "up": ffn_w_up[l].astype(BF16), "down": ffn_w_down[l].astype(BF16),
             "w_fold": _fold_group_dft(w_in[l]), "w_z": w_in[l, :, D_FOURIER:].astype(BF16),
             "w_out": w_out[l].astype(BF16)}
        prm = {"mu": shift_mu[l], "w0": decay_w0[l], "w2": decay_w2[l], "a0": iclr_a0[l], "a2": iclr_a2[l],
               "g2": gate_g2[l], "k_k": k_k[l], "k_a": k_a[l], "r_k": r_k[l], "ln_w": ln_x_w[l], "ln_b": ln_x_b[l]}
        zero_state = jnp.zeros((bp, N_RWKV_HEADS, HEAD_SIZE, HEAD_SIZE), F32)
        xp, s_f, s_b = _path(xp, mod[bs:bs + 1], False, zero_state, zero_state, False, w, prm, 256, 256)
        new_f.append(s_f)
        new_b.append(s_b)
        xs, _, _ = _path(xs, mod[:bs], True, state_fwd[:, l], state_bwd[:, l], True, w, prm, 512, 512)
    return xp, xs, jnp.stack(new_f, axis=1), jnp.stack(new_b, axis=1)
```

```python
import functools

import numpy as np
import jax
import jax.numpy as jnp
from jax import lax
from jax.experimental import pallas as pl
from jax.experimental.pallas import tpu as pltpu

D_MODEL = 1024
GRID_W = 64
D_FOURIER = 512
N_FOURIER_GROUPS = 8
FOURIER_GROUP = D_FOURIER // N_FOURIER_GROUPS
D_RWKV = D_MODEL - D_FOURIER
HEAD_SIZE = 64
N_RWKV_HEADS = D_RWKV // HEAD_SIZE
N_PAIRS = N_RWKV_HEADS // 2
PAIR = 2 * HEAD_SIZE
DECAY_LORA = 64
ICLR_LORA = 64
GATE_LORA = 128
D_Z = 3 * D_RWKV + 2 * DECAY_LORA + 2 * ICLR_LORA + GATE_LORA
D_FOLD = 2 * D_FOURIER
D_FF = 2816
N_MOD = 9
RMS_EPS = 1e-6
GN_EPS = 64e-5
CHUNK = 64
TOKEN_TILE = 512
SCAN_TILE = 512
OFF_WD = 3 * D_RWKV
OFF_AD = OFF_WD + 2 * DECAY_LORA
OFF_GD = OFF_AD + 2 * ICLR_LORA
DECAY_SCALE = float(np.exp(-0.5))
V7X_VMEM_LIMIT = 56 * 1024 * 1024

F32 = jnp.float32
BF16 = jnp.bfloat16


def _dot32(a, b):
    a_hi = a.astype(BF16)
    b_hi = b.astype(BF16)
    a_lo = (a - a_hi.astype(F32)).astype(BF16)
    b_lo = (b - b_hi.astype(F32)).astype(BF16)
    dot = lambda x, y: jnp.dot(x, y, preferred_element_type=F32)
    return dot(a_hi, b_hi) + (dot(a_hi, b_lo) + dot(a_lo, b_hi))


def _sigmoid(x):
    return 1.0 / (1.0 + jnp.exp(-x))


def _norm_mod(x, g, shift, scale):
    ms = jnp.mean(x * x, axis=-1, keepdims=True)
    return x * lax.rsqrt(ms + RMS_EPS) * g * (1.0 + scale) + shift


def _swiglu(h, wg_ref, wu_ref, wd_ref):
    hb = h.astype(BF16)
    gate = jnp.dot(hb, wg_ref[...], preferred_element_type=F32)
    up = jnp.dot(hb, wu_ref[...], preferred_element_type=F32)
    act = gate * _sigmoid(gate) * up
    return jnp.dot(act.astype(BF16), wd_ref[...], preferred_element_type=F32)


def _const_spec(shape):
    nd = len(shape)
    return pl.BlockSpec(shape, lambda *_: (0,) * nd, pipeline_mode=pl.Buffered(1))


def _ffn_spec(shape, which):
    return pl.BlockSpec((None,) + shape, lambda *_: (which, 0, 0), pipeline_mode=pl.Buffered(1))


def _params(sem):
    return pltpu.CompilerParams(dimension_semantics=sem, vmem_limit_bytes=V7X_VMEM_LIMIT)


def _mod_kernel(c_ref, w_ref, b_ref, o_ref):
    c = c_ref[...]
    o_ref[...] = _dot32(c * _sigmoid(c), w_ref[...]) + b_ref[...]


def _modulation(cvec, w_mod, b_mod):
    n = w_mod.shape[1]
    tn = n // 8
    return pl.pallas_call(
        _mod_kernel,
        grid=(n // tn,),
        in_specs=[pl.BlockSpec((8, D_MODEL), lambda j: (0, 0)),
                  pl.BlockSpec((D_MODEL, tn), lambda j: (0, j)),
                  pl.BlockSpec((1, tn), lambda j: (0, j))],
        out_specs=pl.BlockSpec((8, tn), lambda j: (0, j)),
        out_shape=jax.ShapeDtypeStruct((8, n), F32),
        compiler_params=_params(("arbitrary",)),
        name="modulation",
    )(cvec, w_mod, b_mod.reshape(1, n))


def _shift_adjacent(z, mu, period, grid):
    tm = z.shape[0]
    col = lax.broadcasted_iota(jnp.int32, (tm, 1), 0) % period
    lane = lax.broadcasted_iota(jnp.int32, (1, z.shape[1]), 1) % 4
    back1 = jnp.where(col == 0, 0.0, pltpu.roll(z, 1, 0))
    fwd1 = jnp.where(col == period - 1, 0.0, pltpu.roll(z, tm - 1, 0))
    if grid:
        return jnp.where(lane < 2, z + mu * (jnp.where(lane == 0, back1, fwd1) - z), z)
    return z + mu * (jnp.where(lane % 2 == 0, back1, fwd1) - z)


def _front_kernel(x_ref, mod_ref, g_ref, wg_ref, wu_ref, wd_ref, wfold_ref, wz_ref, mu_ref,
                  x1_ref, u_ref, z_ref, *, period, grid):
    x = x_ref[...]
    h = _norm_mod(x, g_ref[0:1, :], mod_ref[0:1, :], mod_ref[1:2, :])
    x1 = x + 0.5 * mod_ref[2:3, :] * _swiglu(h, wg_ref, wu_ref, wd_ref)
    x1_ref[...] = x1
    h2 = _norm_mod(x1, g_ref[1:2, :], mod_ref[3:4, :], mod_ref[4:5, :]).astype(BF16)
    u_ref[...] = jnp.dot(h2, wfold_ref[...], preferred_element_type=F32)
    z_ref[...] = _shift_adjacent(jnp.dot(h2, wz_ref[...], preferred_element_type=F32), mu_ref[...], period, grid)


def _front(x, mod, mod_per_batch, norm_g, wg, wu, wd, w_fold, w_z, mu, tm, period, grid):
    b, s, _ = x.shape
    nt = s // tm
    assert tm % period == 0 and s % tm == 0
    tok = lambda w: pl.BlockSpec((None, tm, w), lambda i, j: (i, j, 0))
    mod_map = (lambda i, j: (i, 0, 0)) if mod_per_batch else (lambda i, j: (0, 0, 0))
    return pl.pallas_call(
        functools.partial(_front_kernel, period=period, grid=grid),
        grid=(b, nt),
        in_specs=[tok(D_MODEL),
                  pl.BlockSpec((None, N_MOD, D_MODEL), mod_map),
                  _const_spec((3, D_MODEL)),
                  _ffn_spec((D_MODEL, D_FF), 0), _ffn_spec((D_MODEL, D_FF), 0),
                  _ffn_spec((D_FF, D_MODEL), 0), _const_spec((D_MODEL, D_FOLD)), _const_spec((D_MODEL, D_Z)),
                  _const_spec((1, D_Z))],
        out_specs=[tok(D_MODEL), tok(D_FOLD), tok(D_Z)],
        out_shape=[jax.ShapeDtypeStruct((b, s, D_MODEL), F32),
                   jax.ShapeDtypeStruct((b, s, D_FOLD), F32),
                   jax.ShapeDtypeStruct((b, s, D_Z), F32)],
        compiler_params=_params(("arbitrary", "arbitrary")),
        name="front",
    )(x, mod, norm_g, wg, wu, wd, w_fold, w_z, mu.reshape(1, D_Z))


def _back_kernel(x1_ref, yf_ref, yr_ref, mod_ref, g_ref, fin_ref, wo_ref, wg_ref, wu_ref, wd_ref, o_ref):
    mixed = (jnp.dot(yf_ref[...].astype(BF16), wo_ref[0:D_FOURIER, :], preferred_element_type=F32)
             + jnp.dot(yr_ref[...].astype(BF16), wo_ref[D_FOURIER:, :], preferred_element_type=F32))
    x2 = x1_ref[...] + mod_ref[5:6, :] * mixed
    h = _norm_mod(x2, g_ref[2:3, :], mod_ref[6:7, :], mod_ref[7:8, :])
    x3 = x2 + 0.5 * mod_ref[8:9, :] * _swiglu(h, wg_ref, wu_ref, wd_ref)
    ms = jnp.mean(x3 * x3, axis=-1, keepdims=True)
    o_ref[...] = x3 * lax.rsqrt(ms + RMS_EPS) * fin_ref[...]


def _back(x1, yf, yr, mod, mod_per_batch, norm_g, final_norm, w_out, wg, wu, wd, tm):
    b, s, _ = x1.shape
    nt = s // tm
    tok = lambda w: pl.BlockSpec((None, tm, w), lambda i, j: (i, j, 0))
    mod_map = (lambda i, j: (i, 0, 0)) if mod_per_batch else (lambda i, j: (0, 0, 0))
    return pl.pallas_call(
        _back_kernel,
        grid=(b, nt),
        in_specs=[tok(D_MODEL), tok(D_FOURIER), tok(D_RWKV),
                  pl.BlockSpec((None, N_MOD, D_MODEL), mod_map),
                  _const_spec((3, D_MODEL)), _const_spec((1, D_MODEL)),
                  _const_spec((D_MODEL, D_MODEL)),
                  _ffn_spec((D_MODEL, D_FF), 1), _ffn_spec((D_MODEL, D_FF), 1),
                  _ffn_spec((D_FF, D_MODEL), 1)],
        out_specs=tok(D_MODEL),
        out_shape=jax.ShapeDtypeStruct((b, s, D_MODEL), F32),
        compiler_params=_params(("arbitrary", "arbitrary")),
        name="back",
    )(x1, yf, yr, mod, norm_g, final_norm.reshape(1, D_MODEL), w_out, wg, wu, wd)


def _fold_kernel(w_ref, cs_ref, o_ref):
    o_ref[...] = _dot32(w_ref[...], cs_ref[...]).astype(o_ref.dtype)


def _fold_group_dft(w_in):
    q = np.arange(FOURIER_GROUP)
    ang = 2.0 * np.pi * ((q[:, None] * q[None, :]) % FOURIER_GROUP) / FOURIER_GROUP
    eye = np.eye(N_FOURIER_GROUPS)
    cs = jnp.asarray(np.concatenate([np.kron(eye, np.cos(ang)), np.kron(eye, np.sin(ang))], axis=1), F32)
    return pl.pallas_call(
        _fold_kernel,
        grid=(1,),
        in_specs=[pl.BlockSpec((D_MODEL, D_FOURIER), lambda i: (0, 0)),
                  pl.BlockSpec((D_FOURIER, D_FOLD), lambda i: (0, 0))],
        out_specs=pl.BlockSpec((D_MODEL, D_FOLD), lambda i: (0, 0)),
        out_shape=jax.ShapeDtypeStruct((D_MODEL, D_FOLD), BF16),
        compiler_params=_params(("arbitrary",)),
        name="fold_group_dft",
    )(w_in, cs)


def _stack_cos_sin(x):
    return jnp.concatenate([x[:, :D_FOURIER], x[:, D_FOURIER:]], axis=0).astype(BF16)


def _fourier_dense_kernel(x_ref, p_ref, o_ref):
    o_ref[...] = _bdot(p_ref[...], _stack_cos_sin(x_ref[...]))


def _fourier_dense(xcs):
    b, s, _ = xcs.shape
    pos = np.arange(s)
    ang = 2.0 * np.pi * ((pos[:, None] * pos[None, :]) % s) / s
    table = np.concatenate([np.cos(ang), -np.sin(ang)], axis=1) / np.sqrt(s * FOURIER_GROUP)
    return pl.pallas_call(
        _fourier_dense_kernel,
        grid=(b,),
        in_specs=[pl.BlockSpec((None, s, D_FOLD), lambda i: (i, 0, 0)), _const_spec((s, 2 * s))],
        out_specs=pl.BlockSpec((None, s, D_FOURIER), lambda i: (i, 0, 0)),
        out_shape=jax.ShapeDtypeStruct((b, s, D_FOURIER), F32),
        compiler_params=_params(("arbitrary",)),
        name="fourier_dense",
    )(xcs, jnp.asarray(table, F32).astype(BF16))


FFT_ROWS = 16


def _fourier_stage1_kernel(x_ref, f_ref, z_ref):
    for j in range(FFT_ROWS):
        z = _bdot(f_ref[...], _stack_cos_sin(x_ref[:, j, :]))
        n = z.shape[0] // 2
        z_ref[:, j, :] = jnp.concatenate([z[:n], z[n:]], axis=1)


def _fourier_stage2_kernel(z_ref, g_ref, o_ref):
    for j in range(FFT_ROWS):
        o_ref[:, j, :] = _bdot(g_ref[j], _stack_cos_sin(z_ref[j]))


def _fourier_two_stage(xcs):
    b, s, _ = xcs.shape
    n = int(round(np.sqrt(s)))
    assert n * n == s and n % FFT_ROWS == 0
    idx = np.arange(n)
    ang = 2.0 * np.pi * ((idx[:, None] * idx[None, :]) % n) / n
    fc, fs = np.cos(ang), np.sin(ang)
    f2 = jnp.asarray(np.block([[fc, -fs], [fs, fc]]), F32).astype(BF16)
    z = pl.pallas_call(
        _fourier_stage1_kernel,
        grid=(b, n // FFT_ROWS),
        in_specs=[pl.BlockSpec((None, n, FFT_ROWS, D_FOLD), lambda i, j: (i, 0, j, 0)),
                  _const_spec((2 * n, 2 * n))],
        out_specs=pl.BlockSpec((None, n, FFT_ROWS, D_FOLD), lambda i, j: (i, 0, j, 0)),
        out_shape=jax.ShapeDtypeStruct((b, n, n, D_FOLD), F32),
        compiler_params=_params(("arbitrary", "arbitrary")),
        name="fourier_stage1",
    )(xcs.reshape(b, n, n, D_FOLD), f2)
    bb, aa, s1 = idx[:, None, None], idx[None, :, None], idx[None, None, :]
    ang2 = 2.0 * np.pi * ((s1 * (n * aa + bb)) % s) / s
    g2 = np.concatenate([np.cos(ang2), -np.sin(ang2)], axis=2) / np.sqrt(s * FOURIER_GROUP)
    out = pl.pallas_call(
        _fourier_stage2_kernel,
        grid=(b, n // FFT_ROWS),
        in_specs=[pl.BlockSpec((None, FFT_ROWS, n, D_FOLD), lambda i, j: (i, j, 0, 0)),
                  pl.BlockSpec((FFT_ROWS, n, 2 * n), lambda i, j: (j, 0, 0))],
        out_specs=pl.BlockSpec((None, n, FFT_ROWS, D_FOURIER), lambda i, j: (i, 0, j, 0)),
        out_shape=jax.ShapeDtypeStruct((b, n, n, D_FOURIER), F32),
        compiler_params=_params(("arbitrary", "arbitrary")),
        name="fourier_stage2",
    )(z, jnp.asarray(g2, F32).astype(BF16))
    return out.reshape(b, s, D_FOURIER)


def _shift_rows(z, zf_ref, mu):
    tm = z.shape[0]
    lane = lax.broadcasted_iota(jnp.int32, (1, z.shape[1]), 1) % 4
    up = zf_ref[0:tm, :]
    down = zf_ref[2 * GRID_W:2 * GRID_W + tm, :]
    return z + jnp.where(lane < 2, 0.0, mu) * (jnp.where(lane == 2, up, down) - z)


def _stack_masked(x, head0):
    return jnp.concatenate([jnp.where(head0, x, 0.0), jnp.where(head0, 0.0, x)], axis=0)


NT = (((1,), (1,)), ((), ()))
CUM_PIECES = 3
CHUNKS_PER_STEP = 4


def _split(x, pieces):
    out = []
    for i in range(pieces):
        hi = x.astype(BF16)
        out.append(hi)
        if i + 1 < pieces:
            x = x - hi.astype(F32)
    return out


def _bdot(a, b, dims=None):
    if dims is None:
        return jnp.dot(a, b, preferred_element_type=F32)
    return lax.dot_general(a, b, dims, preferred_element_type=F32)


HEADS_PER_SUM = 4


def _head_sums(x, ones_ref):
    w = HEADS_PER_SUM * HEAD_SIZE
    return jnp.concatenate([_bdot(x[:, i:i + w].astype(BF16), ones_ref[...]) for i in range(0, D_RWKV, w)], axis=1)


def _scan_kernel(*refs, grid, reverse, combine, n_tiles, tm, seqs):
    it = iter(refs)
    z_ref = next(it)
    zp_ref = next(it) if grid else None
    zn_ref = next(it) if grid else None
    h0_ref = next(it)
    if combine:
        yo_ref, bo_ref = next(it), next(it)
    mu_ref, kk_ref, ka_ref, rk_ref, w0_ref, a0_ref, w2_ref, a2_ref, ones_ref = (next(it) for _ in range(9))
    if combine:
        lnw_ref, lnb_ref, g2_ref = next(it), next(it), next(it)
    y_ref = next(it)
    b_ref = None if combine else next(it)
    hout_ref = next(it)
    r_s, v_s, al_s, be_s, kd_s, lw_s, y_s, g_s, h_s, qm_s, yn_s = (next(it) for _ in range(11))
    zf_s = next(it) if grid else None

    step = pl.program_id(1)
    tile = (n_tiles - 1 - step) if reverse else step

    @pl.when(step == 0)
    def _():
        h_s[...] = h0_ref[...]

    if grid:
        zf_s[GRID_W:GRID_W + tm, :] = z_ref[...]
        zf_s[0:GRID_W, :] = jnp.where(tile == 0, 0.0, zp_ref[...])
        zf_s[GRID_W + tm:, :] = jnp.where(tile == n_tiles - 1, 0.0, zn_ref[...])

    def prepare(rows):
        z = z_ref[rows, :]
        if grid:
            z = _shift_rows(z, zf_s.at[rows.start:rows.stop + 2 * GRID_W, :], mu_ref[...])
        r = z[:, 0:D_RWKV]
        k = z[:, D_RWKV:2 * D_RWKV]
        v = z[:, 2 * D_RWKV:3 * D_RWKV]
        kk = k * kk_ref[...]
        kk = kk * lax.rsqrt(jnp.maximum(_head_sums(kk * kk, ones_ref), 1e-24))
        w_raw = w0_ref[...] + _bdot(jnp.tanh(z[:, OFF_WD:OFF_AD]).astype(BF16), w2_ref[...])
        a = _sigmoid(a0_ref[...] + _bdot(z[:, OFF_AD:OFF_GD].astype(BF16), a2_ref[...]))
        kd = k * (1.0 + (a - 1.0) * ka_ref[...])
        bonus = _head_sums(r * kd * rk_ref[...], ones_ref) * v
        if combine:
            y_ref[rows, :] = bonus + bo_ref[rows, :]
            g_s[rows, :] = _bdot(_sigmoid(z[:, OFF_GD:]).astype(BF16), g2_ref[...])
        else:
            b_ref[rows, :] = bonus
        r_s[rows, :] = r
        v_s[rows, :] = v
        al_s[rows, :] = -kk
        be_s[rows, :] = a * kk
        kd_s[rows, :] = kd
        lw_s[rows, :] = -DECAY_SCALE * _sigmoid(w_raw)

    def finish(rows):
        if combine:
            y_sum = y_s[rows, :] + yo_ref[rows, :]
            mean = _head_sums(y_sum, ones_ref) * (1.0 / HEAD_SIZE)
            cen = y_sum - mean
            var = _head_sums(cen * cen, ones_ref) * (1.0 / HEAD_SIZE)
            y_gn = cen * lax.rsqrt(var + GN_EPS) * lnw_ref[...] + lnb_ref[...]
            y_ref[rows, :] = (y_gn + y_ref[rows, :]) * g_s[rows, :]
        else:
            y_ref[rows, :] = y_s[rows, :]

    n_chunks = tm // CHUNK
    ri = lax.broadcasted_iota(jnp.int32, (CHUNK, CHUNK), 0)
    ci = lax.broadcasted_iota(jnp.int32, (CHUNK, CHUNK), 1)
    tri = ((ci >= ri) if reverse else (ci <= ri)).astype(BF16)
    rj = lax.broadcasted_iota(jnp.int32, (CHUNK, PAIR), 0)
    cj = lax.broadcasted_iota(jnp.int32, (CHUNK, PAIR), 1) % CHUNK
    eye = rj == cj
    strict = (cj > rj) if reverse else (cj < rj)
    incl = (cj >= rj) if reverse else (cj <= rj)
    head0 = lax.broadcasted_iota(jnp.int32, (1, PAIR), 1) < HEAD_SIZE
    last = 0 if reverse else CHUNK - 1
    blockdiag = lambda x: _stack_masked(x, head0)

    def head_transpose(x):
        t = blockdiag(x).T
        return t[:CHUNK] + t[CHUNK:]

    same_block = lambda n: (rj & -n) == (cj & -n)
    levels = [2 ** i for i in range(1, int(np.log2(CHUNK)))]
    base_mask = same_block(2)
    pair_masks = {n: same_block(2 * n) & ~same_block(n) for n in levels}

    per_seq = n_chunks // seqs
    assert per_seq % CHUNKS_PER_STEP == 0, "a group of chunks must not straddle two sequences"

    def chunk_rows(c):
        seq, k = divmod(c, per_seq)
        off = (seq * per_seq + ((per_seq - 1 - k) if reverse else k)) * CHUNK
        return slice(off, off + CHUNK)

    def local_phase(chunks):
        units = []
        for c in chunks:
            rows = chunk_rows(c)
            lw = lw_s[rows, :]
            cum = sum(_bdot(tri, piece) for piece in _split(lw, CUM_PIECES))
            tot = cum[last:last + 1, :]
            g_inv = jnp.exp(-cum)
            g_end = jnp.exp(tot - cum)
            g_tot = jnp.exp(tot)
            abar = al_s[rows, :] * jnp.exp(cum - lw)
            rbar = r_s[rows, :] * jnp.exp(cum)
            be = be_s[rows, :]
            kdc = kd_s[rows, :]
            btil, ktil, bhat, khat = be * g_inv, kdc * g_inv, be * g_end, kdc * g_end
            vc = v_s[rows, :]
            for p in range(N_PAIRS):
                lanes = slice(p * PAIR, (p + 1) * PAIR)
                cut = lambda x: x[:, lanes]
                units.append(dict(c=c, p=p, a=cut(abar).astype(BF16), r=cut(rbar), v=cut(vc).astype(BF16),
                                  bt=cut(btil).astype(BF16), kt=cut(ktil).astype(BF16),
                                  bh=cut(bhat), kh=cut(khat), g_tot=cut(g_tot)))
        for u in units:
            ar = jnp.concatenate([u["a"], u["r"].astype(BF16)], axis=0)
            u["gb"] = _bdot(ar, blockdiag(u["bt"]), NT)
            u["gk"] = _bdot(ar, blockdiag(u["kt"]), NT)
        for u in units:
            gb, gk = u.pop("gb"), u.pop("gk")
            u["l_ab"] = jnp.where(strict, gb[:CHUNK], 0.0)
            u["l_ak"] = jnp.where(strict, gk[:CHUNK], 0.0).astype(BF16)
            u["l_rbk"] = jnp.concatenate([jnp.where(incl, gb[CHUNK:], 0.0),
                                          jnp.where(incl, gk[CHUNK:], 0.0)], axis=1).astype(BF16)
            u["t"] = jnp.where(eye, 1.0, jnp.where(base_mask, u["l_ab"], 0.0))
        for n in levels:
            for u in units:
                e = jnp.where(pair_masks[n], u["l_ab"], 0.0).astype(BF16)
                u["et"] = _bdot(e, blockdiag(u["t"].astype(BF16)))
            for u in units:
                u["t"] = u["t"] + _bdot(u["t"].astype(BF16), blockdiag(u["et"].astype(BF16)))
        for u in units:
            u["x1"] = _bdot(u["l_ak"], blockdiag(u["v"]))
        for u in units:
            rhs = jnp.concatenate([blockdiag(u["a"]), blockdiag(u["x1"].astype(BF16))], axis=1)
            u["pu"] = _bdot(u["t"].astype(BF16), rhs)
        for u in units:
            pu = u["pu"].astype(BF16)
            v_bd = blockdiag(u["v"])
            rhs = jnp.concatenate(
                [jnp.concatenate([blockdiag(pu[:, :PAIR]), blockdiag(pu[:, PAIR:])], axis=1),
                 jnp.concatenate([jnp.zeros_like(v_bd), v_bd], axis=1)], axis=0)
            bkh_t = jnp.concatenate([head_transpose(u["bh"]), head_transpose(u["kh"])], axis=1)
            o = _bdot(jnp.concatenate([u["l_rbk"], bkh_t.astype(BF16)], axis=0), rhs)
            q = u["r"] + o[:CHUNK, :PAIR]
            m = jnp.where(eye, u["g_tot"], 0.0) + o[CHUNK:, :PAIR]
            qm_s[u["c"], u["p"]] = jnp.concatenate([q, m], axis=0).astype(BF16)
            yn_s[u["c"], u["p"]] = o[:, PAIR:]

    def state_step(c):
        seq = c // per_seq
        for p in range(N_PAIRS):
            oh = _bdot(qm_s[c, p], blockdiag(h_s[seq, p].astype(BF16))) + yn_s[c, p]
            y_s[chunk_rows(c), p * PAIR:(p + 1) * PAIR] = oh[:CHUNK]
            h_s[seq, p] = oh[CHUNK:]

    groups = [list(range(g, min(g + CHUNKS_PER_STEP, n_chunks))) for g in range(0, n_chunks, CHUNKS_PER_STEP)]

    def group_rows(chunks):
        spans = [chunk_rows(c) for c in chunks]
        return slice(min(s.start for s in spans), max(s.stop for s in spans))

    def state_phase(chunks):
        for c in chunks:
            state_step(c)

    prepare(group_rows(groups[0]))
    local_phase(groups[0])
    for prev, cur in zip(groups[:-1], groups[1:]):
        prepare(group_rows(cur))
        local_phase(cur)
        state_phase(prev)
        finish(group_rows(prev))
    state_phase(groups[-1])
    finish(group_rows(groups[-1]))

    @pl.when(step == n_tiles - 1)
    def _():
        hout_ref[...] = h_s[...]


def _scan(z, h0, prm, d, grid, other=None):
    n_seq, seq_len, _ = z.shape
    tm = SCAN_TILE
    seqs = max(tm // seq_len, 1)
    assert (seq_len % tm == 0) if seqs == 1 else (tm == seqs * seq_len and n_seq % seqs == 0 and not grid)
    b, s = n_seq // seqs, seq_len * seqs
    fold = lambda a: a.reshape((b, seqs * a.shape[1]) + a.shape[2:])
    z = fold(z)
    h0 = h0.reshape((b, seqs) + h0.shape[1:])
    if other is not None:
        other = tuple(fold(a) for a in other)
    n_tiles = s // tm
    reverse = d == 1
    combine = other is not None
    tile_of = (lambda j: n_tiles - 1 - j) if reverse else (lambda j: j)
    tok = lambda w: pl.BlockSpec((None, tm, w), lambda i, j: (i, tile_of(j), 0))
    halo_per_tile = tm // GRID_W
    n_halo = s // GRID_W
    in_specs = [tok(D_Z)]
    args = [z]
    if grid:
        in_specs += [pl.BlockSpec((None, GRID_W, D_Z),
                                  lambda i, j: (i, jnp.maximum(tile_of(j) * halo_per_tile - 1, 0), 0)),
                     pl.BlockSpec((None, GRID_W, D_Z),
                                  lambda i, j: (i, jnp.minimum((tile_of(j) + 1) * halo_per_tile, n_halo - 1), 0))]
        args += [z, z]
    state_spec = pl.BlockSpec((None, seqs, N_PAIRS, HEAD_SIZE, PAIR), lambda i, j: (i, 0, 0, 0, 0))
    in_specs.append(state_spec)
    args.append(h0)
    if combine:
        in_specs += [tok(D_RWKV), tok(D_RWKV)]
        args += list(other)
    row = lambda x: x.reshape(1, -1)
    small = [row(prm["mu"]), row(prm["k_k"]), row(prm["k_a"]), row(prm["r_k"]),
             row(prm["w0"][d]), row(prm["a0"][d]),
             jnp.zeros((2 * DECAY_LORA, D_RWKV), BF16).at[d * DECAY_LORA:(d + 1) * DECAY_LORA].set(
                 prm["w2"][d].astype(BF16)),
             jnp.zeros((2 * ICLR_LORA, D_RWKV), BF16).at[d * ICLR_LORA:(d + 1) * ICLR_LORA].set(
                 prm["a2"][d].astype(BF16)),
             jnp.asarray(np.kron(np.eye(HEADS_PER_SUM), np.ones((HEAD_SIZE, HEAD_SIZE))), BF16)]
    if combine:
        small += [row(prm["ln_w"]), row(prm["ln_b"]), prm["g2"].astype(BF16)]
    in_specs += [_const_spec(x.shape) for x in small]
    args += small
    out_specs = [tok(D_RWKV)]
    out_shape = [jax.ShapeDtypeStruct((b, s, D_RWKV), F32)]
    if not combine:
        out_specs.append(tok(D_RWKV))
        out_shape.append(jax.ShapeDtypeStruct((b, s, D_RWKV), F32))
    out_specs.append(state_spec)
    out_shape.append(jax.ShapeDtypeStruct((b, seqs, N_PAIRS, HEAD_SIZE, PAIR), F32))
    n_chunks = tm // CHUNK
    scratch = ([pltpu.VMEM((tm, D_RWKV), F32) for _ in range(8)]
               + [pltpu.VMEM((seqs, N_PAIRS, CHUNK, PAIR), F32),
                  pltpu.VMEM((n_chunks, N_PAIRS, 2 * CHUNK, PAIR), BF16),
                  pltpu.VMEM((n_chunks, N_PAIRS, 2 * CHUNK, PAIR), F32)]
               + ([pltpu.VMEM((tm + 2 * GRID_W, D_Z), F32)] if grid else []))
    outs = pl.pallas_call(
        functools.partial(_scan_kernel, grid=grid, reverse=reverse, combine=combine, n_tiles=n_tiles, tm=tm,
                          seqs=seqs),
        grid=(b, n_tiles),
        in_specs=in_specs,
        out_specs=out_specs,
        out_shape=out_shape,
        scratch_shapes=scratch,
        compiler_params=_params(("arbitrary", "arbitrary")),
        name="scan_bwd" if reverse else "scan_fwd",
    )(*args)
    unfold = lambda a: a.reshape((n_seq, seq_len) + a.shape[2:])
    return tuple(unfold(a) for a in outs[:-1]) + (outs[-1].reshape((n_seq,) + outs[-1].shape[2:]),)


def _pack_state(s):
    b = s.shape[0]
    h = jnp.swapaxes(s, -1, -2).reshape(b, N_PAIRS, 2, HEAD_SIZE, HEAD_SIZE)
    return jnp.transpose(h, (0, 1, 3, 2, 4)).reshape(b, N_PAIRS, HEAD_SIZE, PAIR)


def _unpack_state(hp):
    b = hp.shape[0]
    h = jnp.transpose(hp.reshape(b, N_PAIRS, HEAD_SIZE, 2, HEAD_SIZE), (0, 1, 3, 2, 4))
    return jnp.swapaxes(h.reshape(b, N_RWKV_HEADS, HEAD_SIZE, HEAD_SIZE), -1, -2)


def _mixer_heads(z, s0_f, s0_b, prm, grid):
    y_b, bonus_b, h_b = _scan(z, _pack_state(s0_b), prm, 1, grid)
    y, h_f = _scan(z, _pack_state(s0_f), prm, 0, grid, other=(y_b, bonus_b))
    return y, _unpack_state(h_f), _unpack_state(h_b)


def _path(x, mod, mod_per_batch, s0_f, s0_b, grid, w, prm):
    b, s, _ = x.shape
    rows = (b, s) if mod_per_batch else (1, b * s)
    as_rows = lambda a: a.reshape(rows + a.shape[2:])
    as_seqs = lambda a: a.reshape((b, s) + a.shape[2:])
    x1, u, z = _front(as_rows(x), mod, mod_per_batch, w["norm_g"], w["gate"], w["up"], w["down"], w["w_fold"],
                      w["w_z"], prm["mu"], TOKEN_TILE, GRID_W if grid else s, grid)
    y_four = _fourier_two_stage(as_seqs(u)) if grid else _fourier_dense(as_seqs(u))
    y_rwkv, s_f, s_b = _mixer_heads(as_seqs(z), s0_f, s0_b, prm, grid)
    y = _back(x1, as_rows(y_four), as_rows(y_rwkv), mod, mod_per_batch, w["norm_g"], w["final"], w["w_out"],
              w["gate"], w["up"], w["down"], TOKEN_TILE)
    return as_seqs(y), s_f, s_b


def kernel(x_prompt, x_sample, state_fwd, state_bwd, c, c_ctx, w_mod, b_mod, norm_g, ffn_w_gate, ffn_w_up,
           ffn_w_down, w_in, shift_mu, decay_w0, decay_w2, iclr_a0, iclr_a2, gate_g2, k_k, k_a, r_k,
           ln_x_w, ln_x_b, w_out, final_norm):
    depth = w_mod.shape[0]
    assert depth == 1, "the back kernel applies the final norm, so exactly one layer is supported"
    bp = x_prompt.shape[0]
    bs = x_sample.shape[0]
    xp, xs = x_prompt, x_sample
    new_f, new_b = [], []
    for l in range(depth):
        cvec = jnp.zeros((8, D_MODEL), F32).at[:bs].set(c).at[bs].set(c_ctx)
        mod = _modulation(cvec, w_mod[l], b_mod[l]).reshape(8, N_MOD, D_MODEL)
        w = {"norm_g": norm_g[l], "final": final_norm,
             "gate": ffn_w_gate[l].astype(BF16), "up": ffn_w_up[l].astype(BF16), "down": ffn_w_down[l].astype(BF16),
             "w_fold": _fold_group_dft(w_in[l]), "w_z": w_in[l, :, D_FOURIER:].astype(BF16),
             "w_out": w_out[l].astype(BF16)}
        prm = {"mu": shift_mu[l], "w0": decay_w0[l], "w2": decay_w2[l], "a0": iclr_a0[l], "a2": iclr_a2[l],
               "g2": gate_g2[l], "k_k": k_k[l], "k_a": k_a[l], "r_k": r_k[l], "ln_w": ln_x_w[l], "ln_b": ln_x_b[l]}
        zero_state = jnp.zeros((bp, N_RWKV_HEADS, HEAD_SIZE, HEAD_SIZE), F32)
        xp, s_f, s_b = _path(xp, mod[bs:bs + 1], False, zero_state, zero_state, False, w, prm)
        new_f.append(s_f)
        new_b.append(s_b)
        xs, _, _ = _path(xs, mod[:bs], True, state_fwd[:, l], state_bwd[:, l], True, w, prm)
    return xp, xs, jnp.stack(new_f, axis=1), jnp.stack(new_b, axis=1)
```

```python
import functools

import numpy as np
import jax
import jax.numpy as jnp
from jax import lax
from jax.experimental import pallas as pl
from jax.experimental.pallas import tpu as pltpu

D_MODEL = 1024
GRID_W = 64
D_FOURIER = 512
N_FOURIER_GROUPS = 8
FOURIER_GROUP = D_FOURIER // N_FOURIER_GROUPS
D_RWKV = D_MODEL - D_FOURIER
HEAD_SIZE = 64
N_RWKV_HEADS = D_RWKV // HEAD_SIZE
N_PAIRS = N_RWKV_HEADS // 2
PAIR = 2 * HEAD_SIZE
DECAY_LORA = 64
ICLR_LORA = 64
GATE_LORA = 128
D_Z = 3 * D_RWKV + 2 * DECAY_LORA + 2 * ICLR_LORA + GATE_LORA
D_FOLD = 2 * D_FOURIER
D_FF = 2816
N_MOD = 9
RMS_EPS = 1e-6
GN_EPS = 64e-5
CHUNK = 64
TOKEN_TILE = 512
SCAN_TILE = 512
OFF_WD = 3 * D_RWKV
OFF_AD = OFF_WD + 2 * DECAY_LORA
OFF_GD = OFF_AD + 2 * ICLR_LORA
DECAY_SCALE = float(np.exp(-0.5))
V7X_VMEM_LIMIT = 56 * 1024 * 1024

F32 = jnp.float32
BF16 = jnp.bfloat16


def _dot32(a, b):
    a_hi = a.astype(BF16)
    b_hi = b.astype(BF16)
    a_lo = (a - a_hi.astype(F32)).astype(BF16)
    b_lo = (b - b_hi.astype(F32)).astype(BF16)
    dot = lambda x, y: jnp.dot(x, y, preferred_element_type=F32)
    return dot(a_hi, b_hi) + (dot(a_hi, b_lo) + dot(a_lo, b_hi))


def _sigmoid(x):
    return 1.0 / (1.0 + jnp.exp(-x))


def _norm_mod(x, g, shift, scale):
    ms = jnp.mean(x * x, axis=-1, keepdims=True)
    return x * lax.rsqrt(ms + RMS_EPS) * g * (1.0 + scale) + shift


def _swiglu(h, wg_ref, wu_ref, wd_ref):
    hb = h.astype(BF16)
    gate = jnp.dot(hb, wg_ref[...], preferred_element_type=F32)
    up = jnp.dot(hb, wu_ref[...], preferred_element_type=F32)
    act = gate * _sigmoid(gate) * up
    return jnp.dot(act.astype(BF16), wd_ref[...], preferred_element_type=F32)


def _const_spec(shape):
    nd = len(shape)
    return pl.BlockSpec(shape, lambda *_: (0,) * nd, pipeline_mode=pl.Buffered(1))


def _ffn_spec(shape, which):
    return pl.BlockSpec((None,) + shape, lambda *_: (which, 0, 0), pipeline_mode=pl.Buffered(1))


def _params(sem):
    return pltpu.CompilerParams(dimension_semantics=sem, vmem_limit_bytes=V7X_VMEM_LIMIT)


def _mod_kernel(c_ref, w_ref, b_ref, o_ref):
    c = c_ref[...]
    o_ref[...] = _dot32(c * _sigmoid(c), w_ref[...]) + b_ref[...]


def _modulation(cvec, w_mod, b_mod):
    n = w_mod.shape[1]
    tn = n // 8
    return pl.pallas_call(
        _mod_kernel,
        grid=(n // tn,),
        in_specs=[pl.BlockSpec((8, D_MODEL), lambda j: (0, 0)),
                  pl.BlockSpec((D_MODEL, tn), lambda j: (0, j)),
                  pl.BlockSpec((1, tn), lambda j: (0, j))],
        out_specs=pl.BlockSpec((8, tn), lambda j: (0, j)),
        out_shape=jax.ShapeDtypeStruct((8, n), F32),
        compiler_params=_params(("arbitrary",)),
        name="modulation",
    )(cvec, w_mod, b_mod.reshape(1, n))


def _shift_adjacent(z, mu, period, grid):
    tm = z.shape[0]
    col = lax.broadcasted_iota(jnp.int32, (tm, 1), 0) % period
    lane = lax.broadcasted_iota(jnp.int32, (1, z.shape[1]), 1) % 4
    back1 = jnp.where(col == 0, 0.0, pltpu.roll(z, 1, 0))
    fwd1 = jnp.where(col == period - 1, 0.0, pltpu.roll(z, tm - 1, 0))
    if grid:
        return jnp.where(lane < 2, z + mu * (jnp.where(lane == 0, back1, fwd1) - z), z)
    return z + mu * (jnp.where(lane % 2 == 0, back1, fwd1) - z)


def _front_kernel(x_ref, mod_ref, g_ref, wg_ref, wu_ref, wd_ref, wfold_ref, wz_ref, mu_ref,
                  x1_ref, u_ref, z_ref, *, period, grid):
    x = x_ref[...]
    h = _norm_mod(x, g_ref[0:1, :], mod_ref[0:1, :], mod_ref[1:2, :])
    x1 = x + 0.5 * mod_ref[2:3, :] * _swiglu(h, wg_ref, wu_ref, wd_ref)
    x1_ref[...] = x1
    h2 = _norm_mod(x1, g_ref[1:2, :], mod_ref[3:4, :], mod_ref[4:5, :]).astype(BF16)
    u_ref[...] = jnp.dot(h2, wfold_ref[...], preferred_element_type=F32)
    z_ref[...] = _shift_adjacent(jnp.dot(h2, wz_ref[...], preferred_element_type=F32), mu_ref[...], period, grid)


def _front(x, mod, mod_per_batch, norm_g, wg, wu, wd, w_fold, w_z, mu, tm, period, grid):
    b, s, _ = x.shape
    nt = s // tm
    assert tm % period == 0 and s % tm == 0
    tok = lambda w: pl.BlockSpec((None, tm, w), lambda i, j: (i, j, 0))
    mod_map = (lambda i, j: (i, 0, 0)) if mod_per_batch else (lambda i, j: (0, 0, 0))
    return pl.pallas_call(
        functools.partial(_front_kernel, period=period, grid=grid),
        grid=(b, nt),
        in_specs=[tok(D_MODEL),
                  pl.BlockSpec((None, N_MOD, D_MODEL), mod_map),
                  _const_spec((3, D_MODEL)),
                  _ffn_spec((D_MODEL, D_FF), 0), _ffn_spec((D_MODEL, D_FF), 0),
                  _ffn_spec((D_FF, D_MODEL), 0), _const_spec((D_MODEL, D_FOLD)), _const_spec((D_MODEL, D_Z)),
                  _const_spec((1, D_Z))],
        out_specs=[tok(D_MODEL), tok(D_FOLD), tok(D_Z)],
        out_shape=[jax.ShapeDtypeStruct((b, s, D_MODEL), F32),
                   jax.ShapeDtypeStruct((b, s, D_FOLD), F32),
                   jax.ShapeDtypeStruct((b, s, D_Z), F32)],
        compiler_params=_params(("arbitrary", "arbitrary")),
        name="front",
    )(x, mod, norm_g, wg, wu, wd, w_fold, w_z, mu.reshape(1, D_Z))


def _back_kernel(x1_ref, yf_ref, yr_ref, mod_ref, g_ref, fin_ref, wo_ref, wg_ref, wu_ref, wd_ref, o_ref):
    mixed = (jnp.dot(yf_ref[...].astype(BF16), wo_ref[0:D_FOURIER, :], preferred_element_type=F32)
             + jnp.dot(yr_ref[...].astype(BF16), wo_ref[D_FOURIER:, :], preferred_element_type=F32))
    x2 = x1_ref[...] + mod_ref[5:6, :] * mixed
    h = _norm_mod(x2, g_ref[2:3, :], mod_ref[6:7, :], mod_ref[7:8, :])
    x3 = x2 + 0.5 * mod_ref[8:9, :] * _swiglu(h, wg_ref, wu_ref, wd_ref)
    ms = jnp.mean(x3 * x3, axis=-1, keepdims=True)
    o_ref[...] = x3 * lax.rsqrt(ms + RMS_EPS) * fin_ref[...]


def _back(x1, yf, yr, mod, mod_per_batch, norm_g, final_norm, w_out, wg, wu, wd, tm):
    b, s, _ = x1.shape
    nt = s // tm
    tok = lambda w: pl.BlockSpec((None, tm, w), lambda i, j: (i, j, 0))
    mod_map = (lambda i, j: (i, 0, 0)) if mod_per_batch else (lambda i, j: (0, 0, 0))
    return pl.pallas_call(
        _back_kernel,
        grid=(b, nt),
        in_specs=[tok(D_MODEL), tok(D_FOURIER), tok(D_RWKV),
                  pl.BlockSpec((None, N_MOD, D_MODEL), mod_map),
                  _const_spec((3, D_MODEL)), _const_spec((1, D_MODEL)),
                  _const_spec((D_MODEL, D_MODEL)),
                  _ffn_spec((D_MODEL, D_FF), 1), _ffn_spec((D_MODEL, D_FF), 1),
                  _ffn_spec((D_FF, D_MODEL), 1)],
        out_specs=tok(D_MODEL),
        out_shape=jax.ShapeDtypeStruct((b, s, D_MODEL), F32),
        compiler_params=_params(("arbitrary", "arbitrary")),
        name="back",
    )(x1, yf, yr, mod, norm_g, final_norm.reshape(1, D_MODEL), w_out, wg, wu, wd)


def _fold_kernel(w_ref, cs_ref, o_ref):
    o_ref[...] = _dot32(w_ref[...], cs_ref[...]).astype(o_ref.dtype)


def _fold_group_dft(w_in):
    q = np.arange(FOURIER_GROUP)
    ang = 2.0 * np.pi * ((q[:, None] * q[None, :]) % FOURIER_GROUP) / FOURIER_GROUP
    eye = np.eye(N_FOURIER_GROUPS)
    cs = jnp.asarray(np.concatenate([np.kron(eye, np.cos(ang)), np.kron(eye, np.sin(ang))], axis=1), F32)
    return pl.pallas_call(
        _fold_kernel,
        grid=(1,),
        in_specs=[pl.BlockSpec((D_MODEL, D_FOURIER), lambda i: (0, 0)),
                  pl.BlockSpec((D_FOURIER, D_FOLD), lambda i: (0, 0))],
        out_specs=pl.BlockSpec((D_MODEL, D_FOLD), lambda i: (0, 0)),
        out_shape=jax.ShapeDtypeStruct((D_MODEL, D_FOLD), BF16),
        compiler_params=_params(("arbitrary",)),
        name="fold_group_dft",
    )(w_in, cs)


def _stack_cos_sin(x):
    return jnp.concatenate([x[:, :D_FOURIER], x[:, D_FOURIER:]], axis=0).astype(BF16)


def _fourier_dense_kernel(x_ref, p_ref, o_ref):
    o_ref[...] = _bdot(p_ref[...], _stack_cos_sin(x_ref[...]))


def _fourier_dense(xcs):
    b, s, _ = xcs.shape
    pos = np.arange(s)
    ang = 2.0 * np.pi * ((pos[:, None] * pos[None, :]) % s) / s
    table = np.concatenate([np.cos(ang), -np.sin(ang)], axis=1) / np.sqrt(s * FOURIER_GROUP)
    return pl.pallas_call(
        _fourier_dense_kernel,
        grid=(b,),
        in_specs=[pl.BlockSpec((None, s, D_FOLD), lambda i: (i, 0, 0)), _const_spec((s, 2 * s))],
        out_specs=pl.BlockSpec((None, s, D_FOURIER), lambda i: (i, 0, 0)),
        out_shape=jax.ShapeDtypeStruct((b, s, D_FOURIER), F32),
        compiler_params=_params(("arbitrary",)),
        name="fourier_dense",
    )(xcs, jnp.asarray(table, F32).astype(BF16))


FFT_ROWS = 16


def _fourier_stage1_kernel(x_ref, f_ref, z_ref):
    for j in range(FFT_ROWS):
        z = _bdot(f_ref[...], _stack_cos_sin(x_ref[:, j, :]))
        n = z.shape[0] // 2
        z_ref[:, j, :] = jnp.concatenate([z[:n], z[n:]], axis=1)


def _fourier_stage2_kernel(z_ref, g_ref, o_ref):
    for j in range(FFT_ROWS):
        o_ref[:, j, :] = _bdot(g_ref[j], _stack_cos_sin(z_ref[j]))


def _fourier_two_stage(xcs):
    b, s, _ = xcs.shape
    n = int(round(np.sqrt(s)))
    assert n * n == s and n % FFT_ROWS == 0
    idx = np.arange(n)
    ang = 2.0 * np.pi * ((idx[:, None] * idx[None, :]) % n) / n
    fc, fs = np.cos(ang), np.sin(ang)
    f2 = jnp.asarray(np.block([[fc, -fs], [fs, fc]]), F32).astype(BF16)
    z = pl.pallas_call(
        _fourier_stage1_kernel,
        grid=(b, n // FFT_ROWS),
        in_specs=[pl.BlockSpec((None, n, FFT_ROWS, D_FOLD), lambda i, j: (i, 0, j, 0)),
                  _const_spec((2 * n, 2 * n))],
        out_specs=pl.BlockSpec((None, n, FFT_ROWS, D_FOLD), lambda i, j: (i, 0, j, 0)),
        out_shape=jax.ShapeDtypeStruct((b, n, n, D_FOLD), F32),
        compiler_params=_params(("arbitrary", "arbitrary")),
        name="fourier_stage1",
    )(xcs.reshape(b, n, n, D_FOLD), f2)
    bb, aa, s1 = idx[:, None, None], idx[None, :, None], idx[None, None, :]
    ang2 = 2.0 * np.pi * ((s1 * (n * aa + bb)) % s) / s
    g2 = np.concatenate([np.cos(ang2), -np.sin(ang2)], axis=2) / np.sqrt(s * FOURIER_GROUP)
    out = pl.pallas_call(
        _fourier_stage2_kernel,
        grid=(b, n // FFT_ROWS),
        in_specs=[pl.BlockSpec((None, FFT_ROWS, n, D_FOLD), lambda i, j: (i, j, 0, 0)),
                  pl.BlockSpec((FFT_ROWS, n, 2 * n), lambda i, j: (j, 0, 0))],
        out_specs=pl.BlockSpec((None, n, FFT_ROWS, D_FOURIER), lambda i, j: (i, 0, j, 0)),
        out_shape=jax.ShapeDtypeStruct((b, n, n, D_FOURIER), F32),
        compiler_params=_params(("arbitrary", "arbitrary")),
        name="fourier_stage2",
    )(z, jnp.asarray(g2, F32).astype(BF16))
    return out.reshape(b, s, D_FOURIER)


def _shift_rows(z, zf_ref, mu):
    tm = z.shape[0]
    lane = lax.broadcasted_iota(jnp.int32, (1, z.shape[1]), 1) % 4
    up = zf_ref[0:tm, :]
    down = zf_ref[2 * GRID_W:2 * GRID_W + tm, :]
    return z + jnp.where(lane < 2, 0.0, mu) * (jnp.where(lane == 2, up, down) - z)


def _stack_masked(x, head0):
    return jnp.concatenate([jnp.where(head0, x, 0.0), jnp.where(head0, 0.0, x)], axis=0)


NT = (((1,), (1,)), ((), ()))
CUM_PIECES = 2
CHUNKS_PER_STEP = 4


def _split(x, pieces):
    out = []
    for i in range(pieces):
        hi = x.astype(BF16)
        out.append(hi)
        if i + 1 < pieces:
            x = x - hi.astype(F32)
    return out


def _bdot(a, b, dims=None):
    if dims is None:
        return jnp.dot(a, b, preferred_element_type=F32)
    return lax.dot_general(a, b, dims, preferred_element_type=F32)


HEADS_PER_SUM = 4


def _head_sums(x, ones_ref):
    w = HEADS_PER_SUM * HEAD_SIZE
    return jnp.concatenate([_bdot(x[:, i:i + w].astype(BF16), ones_ref[...]) for i in range(0, D_RWKV, w)], axis=1)


def _scan_kernel(*refs, grid, reverse, combine, n_tiles, tm, seqs):
    it = iter(refs)
    z_ref = next(it)
    zp_ref = next(it) if grid else None
    zn_ref = next(it) if grid else None
    h0_ref = next(it)
    if combine:
        yo_ref, bo_ref = next(it), next(it)
    mu_ref, kk_ref, ka_ref, rk_ref, w0_ref, a0_ref, w2_ref, a2_ref, ones_ref = (next(it) for _ in range(9))
    if combine:
        lnw_ref, lnb_ref, g2_ref = next(it), next(it), next(it)
    y_ref = next(it)
    b_ref = None if combine else next(it)
    hout_ref = next(it)
    r_s, v_s, al_s, be_s, kd_s, lw_s, y_s, g_s, h_s, qm_s, yn_s = (next(it) for _ in range(11))
    zf_s = next(it) if grid else None

    step = pl.program_id(1)
    tile = (n_tiles - 1 - step) if reverse else step

    @pl.when(step == 0)
    def _():
        h_s[...] = h0_ref[...]

    if grid:
        zf_s[GRID_W:GRID_W + tm, :] = z_ref[...]
        zf_s[0:GRID_W, :] = jnp.where(tile == 0, 0.0, zp_ref[...])
        zf_s[GRID_W + tm:, :] = jnp.where(tile == n_tiles - 1, 0.0, zn_ref[...])

    def prepare(rows):
        z = z_ref[rows, :]
        if grid:
            z = _shift_rows(z, zf_s.at[rows.start:rows.stop + 2 * GRID_W, :], mu_ref[...])
        r = z[:, 0:D_RWKV]
        k = z[:, D_RWKV:2 * D_RWKV]
        v = z[:, 2 * D_RWKV:3 * D_RWKV]
        kk = k * kk_ref[...]
        kk = kk * lax.rsqrt(jnp.maximum(_head_sums(kk * kk, ones_ref), 1e-24))
        w_raw = w0_ref[...] + _bdot(jnp.tanh(z[:, OFF_WD:OFF_AD]).astype(BF16), w2_ref[...])
        a = _sigmoid(a0_ref[...] + _bdot(z[:, OFF_AD:OFF_GD].astype(BF16), a2_ref[...]))
        kd = k * (1.0 + (a - 1.0) * ka_ref[...])
        bonus = _head_sums(r * kd * rk_ref[...], ones_ref) * v
        if combine:
            y_ref[rows, :] = bonus + bo_ref[rows, :]
            g_s[rows, :] = _bdot(_sigmoid(z[:, OFF_GD:]).astype(BF16), g2_ref[...])
        else:
            b_ref[rows, :] = bonus
        r_s[rows, :] = r
        v_s[rows, :] = v
        al_s[rows, :] = -kk
        be_s[rows, :] = a * kk
        kd_s[rows, :] = kd
        lw_s[rows, :] = -DECAY_SCALE * _sigmoid(w_raw)

    def finish(rows):
        if combine:
            y_sum = y_s[rows, :] + yo_ref[rows, :]
            mean = _head_sums(y_sum, ones_ref) * (1.0 / HEAD_SIZE)
            cen = y_sum - mean
            var = _head_sums(cen * cen, ones_ref) * (1.0 / HEAD_SIZE)
            y_gn = cen * lax.rsqrt(var + GN_EPS) * lnw_ref[...] + lnb_ref[...]
            y_ref[rows, :] = (y_gn + y_ref[rows, :]) * g_s[rows, :]
        else:
            y_ref[rows, :] = y_s[rows, :]

    n_chunks = tm // CHUNK
    ri = lax.broadcasted_iota(jnp.int32, (CHUNK, CHUNK), 0)
    ci = lax.broadcasted_iota(jnp.int32, (CHUNK, CHUNK), 1)
    tri = ((ci >= ri) if reverse else (ci <= ri)).astype(BF16)
    rj = lax.broadcasted_iota(jnp.int32, (CHUNK, PAIR), 0)
    cj = lax.broadcasted_iota(jnp.int32, (CHUNK, PAIR), 1) % CHUNK
    eye = rj == cj
    strict = (cj > rj) if reverse else (cj < rj)
    incl = (cj >= rj) if reverse else (cj <= rj)
    head0 = lax.broadcasted_iota(jnp.int32, (1, PAIR), 1) < HEAD_SIZE
    last = 0 if reverse else CHUNK - 1
    blockdiag = lambda x: _stack_masked(x, head0)

    def head_transpose(x):
        t = blockdiag(x).T
        return t[:CHUNK] + t[CHUNK:]

    same_block = lambda n: (rj & -n) == (cj & -n)
    levels = [2 ** i for i in range(1, int(np.log2(CHUNK)))]
    base_mask = same_block(2)
    pair_masks = {n: same_block(2 * n) & ~same_block(n) for n in levels}

    per_seq = n_chunks // seqs
    assert per_seq % CHUNKS_PER_STEP == 0, "a group of chunks must not straddle two sequences"

    def chunk_rows(c):
        seq, k = divmod(c, per_seq)
        off = (seq * per_seq + ((per_seq - 1 - k) if reverse else k)) * CHUNK
        return slice(off, off + CHUNK)

    def local_phase(chunks):
        units = []
        for c in chunks:
            rows = chunk_rows(c)
            lw = lw_s[rows, :]
            cum = sum(_bdot(tri, piece) for piece in _split(lw, CUM_PIECES))
            tot = cum[last:last + 1, :]
            g_inv = jnp.exp(-cum)
            g_tot = jnp.exp(tot)
            g_end = g_tot * g_inv
            abar = al_s[rows, :] * jnp.exp(cum - lw)
            rbar = r_s[rows, :] * jnp.exp(cum)
            be = be_s[rows, :]
            kdc = kd_s[rows, :]
            btil, ktil, bhat, khat = be * g_inv, kdc * g_inv, be * g_end, kdc * g_end
            vc = v_s[rows, :]
            for p in range(N_PAIRS):
                lanes = slice(p * PAIR, (p + 1) * PAIR)
                cut = lambda x: x[:, lanes]
                units.append(dict(c=c, p=p, a=cut(abar).astype(BF16), r=cut(rbar), v=cut(vc).astype(BF16),
                                  bt=cut(btil).astype(BF16), kt=cut(ktil).astype(BF16),
                                  bh=cut(bhat), kh=cut(khat), g_tot=cut(g_tot)))
        for u in units:
            ar = jnp.concatenate([u["a"], u["r"].astype(BF16)], axis=0)
            u["gb"] = _bdot(ar, blockdiag(u["bt"]), NT)
            u["gk"] = _bdot(ar, blockdiag(u["kt"]), NT)
        for u in units:
            gb, gk = u.pop("gb"), u.pop("gk")
            u["l_ab"] = jnp.where(strict, gb[:CHUNK], 0.0)
            u["l_ak"] = jnp.where(strict, gk[:CHUNK], 0.0).astype(BF16)
            u["l_rbk"] = jnp.concatenate([jnp.where(incl, gb[CHUNK:], 0.0),
                                          jnp.where(incl, gk[CHUNK:], 0.0)], axis=1).astype(BF16)
            u["t"] = jnp.where(eye, 1.0, jnp.where(base_mask, u["l_ab"], 0.0))
        for n in levels:
            for u in units:
                e = jnp.where(pair_masks[n], u["l_ab"], 0.0).astype(BF16)
                u["et"] = _bdot(e, blockdiag(u["t"].astype(BF16)))
            for u in units:
                u["t"] = u["t"] + _bdot(u["t"].astype(BF16), blockdiag(u["et"].astype(BF16)))
        for u in units:
            u["x1"] = _bdot(u["l_ak"], blockdiag(u["v"]))
        for u in units:
            rhs = jnp.concatenate([blockdiag(u["a"]), blockdiag(u["x1"].astype(BF16))], axis=1)
            u["pu"] = _bdot(u["t"].astype(BF16), rhs)
        for u in units:
            pu = u["pu"].astype(BF16)
            v_bd = blockdiag(u["v"])
            rhs = jnp.concatenate(
                [jnp.concatenate([blockdiag(pu[:, :PAIR]), blockdiag(pu[:, PAIR:])], axis=1),
                 jnp.concatenate([jnp.zeros_like(v_bd), v_bd], axis=1)], axis=0)
            bkh_t = jnp.concatenate([head_transpose(u["bh"]), head_transpose(u["kh"])], axis=1)
            o = _bdot(jnp.concatenate([u["l_rbk"], bkh_t.astype(BF16)], axis=0), rhs)
            q = u["r"] + o[:CHUNK, :PAIR]
            m = jnp.where(eye, u["g_tot"], 0.0) + o[CHUNK:, :PAIR]
            qm_s[u["c"], u["p"]] = jnp.concatenate([q, m], axis=0).astype(BF16)
            yn_s[u["c"], u["p"]] = o[:, PAIR:]

    def state_step(c):
        seq = c // per_seq
        for p in range(N_PAIRS):
            oh = _bdot(qm_s[c, p], blockdiag(h_s[seq, p].astype(BF16))) + yn_s[c, p]
            y_s[chunk_rows(c), p * PAIR:(p + 1) * PAIR] = oh[:CHUNK]
            h_s[seq, p] = oh[CHUNK:]

    groups = [list(range(g, min(g + CHUNKS_PER_STEP, n_chunks))) for g in range(0, n_chunks, CHUNKS_PER_STEP)]

    def group_rows(chunks):
        spans = [chunk_rows(c) for c in chunks]
        return slice(min(s.start for s in spans), max(s.stop for s in spans))

    def state_phase(chunks):
        for c in chunks:
            state_step(c)

    prepare(group_rows(groups[0]))
    local_phase(groups[0])
    for prev, cur in zip(groups[:-1], groups[1:]):
        prepare(group_rows(cur))
        local_phase(cur)
        state_phase(prev)
        finish(group_rows(prev))
    state_phase(groups[-1])
    finish(group_rows(groups[-1]))

    @pl.when(step == n_tiles - 1)
    def _():
        hout_ref[...] = h_s[...]


def _scan(z, h0, prm, d, grid, other=None):
    n_seq, seq_len, _ = z.shape
    tm = SCAN_TILE
    seqs = max(tm // seq_len, 1)
    assert (seq_len % tm == 0) if seqs == 1 else (tm == seqs * seq_len and n_seq % seqs == 0 and not grid)
    b, s = n_seq // seqs, seq_len * seqs
    fold = lambda a: a.reshape((b, seqs * a.shape[1]) + a.shape[2:])
    z = fold(z)
    h0 = h0.reshape((b, seqs) + h0.shape[1:])
    if other is not None:
        other = tuple(fold(a) for a in other)
    n_tiles = s // tm
    reverse = d == 1
    combine = other is not None
    tile_of = (lambda j: n_tiles - 1 - j) if reverse else (lambda j: j)
    tok = lambda w: pl.BlockSpec((None, tm, w), lambda i, j: (i, tile_of(j), 0))
    halo_per_tile = tm // GRID_W
    n_halo = s // GRID_W
    in_specs = [tok(D_Z)]
    args = [z]
    if grid:
        in_specs += [pl.BlockSpec((None, GRID_W, D_Z),
                                  lambda i, j: (i, jnp.maximum(tile_of(j) * halo_per_tile - 1, 0), 0)),
                     pl.BlockSpec((None, GRID_W, D_Z),
                                  lambda i, j: (i, jnp.minimum((tile_of(j) + 1) * halo_per_tile, n_halo - 1), 0))]
        args += [z, z]
    state_spec = pl.BlockSpec((None, seqs, N_PAIRS, HEAD_SIZE, PAIR), lambda i, j: (i, 0, 0, 0, 0))
    in_specs.append(state_spec)
    args.append(h0)
    if combine:
        in_specs += [tok(D_RWKV), tok(D_RWKV)]
        args += list(other)
    row = lambda x: x.reshape(1, -1)
    small = [row(prm["mu"]), row(prm["k_k"]), row(prm["k_a"]), row(prm["r_k"]),
             row(prm["w0"][d]), row(prm["a0"][d]),
             jnp.zeros((2 * DECAY_LORA, D_RWKV), BF16).at[d * DECAY_LORA:(d + 1) * DECAY_LORA].set(
                 prm["w2"][d].astype(BF16)),
             jnp.zeros((2 * ICLR_LORA, D_RWKV), BF16).at[d * ICLR_LORA:(d + 1) * ICLR_LORA].set(
                 prm["a2"][d].astype(BF16)),
             jnp.asarray(np.kron(np.eye(HEADS_PER_SUM), np.ones((HEAD_SIZE, HEAD_SIZE))), BF16)]
    if combine:
        small += [row(prm["ln_w"]), row(prm["ln_b"]), prm["g2"].astype(BF16)]
    in_specs += [_const_spec(x.shape) for x in small]
    args += small
    out_specs = [tok(D_RWKV)]
    out_shape = [jax.ShapeDtypeStruct((b, s, D_RWKV), F32)]
    if not combine:
        out_specs.append(tok(D_RWKV))
        out_shape.append(jax.ShapeDtypeStruct((b, s, D_RWKV), F32))
    out_specs.append(state_spec)
    out_shape.append(jax.ShapeDtypeStruct((b, seqs, N_PAIRS, HEAD_SIZE, PAIR), F32))
    n_chunks = tm // CHUNK
    scratch = ([pltpu.VMEM((tm, D_RWKV), F32) for _ in range(8)]
               + [pltpu.VMEM((seqs, N_PAIRS, CHUNK, PAIR), F32),
                  pltpu.VMEM((n_chunks, N_PAIRS, 2 * CHUNK, PAIR), BF16),
                  pltpu.VMEM((n_chunks, N_PAIRS, 2 * CHUNK, PAIR), F32)]
               + ([pltpu.VMEM((tm + 2 * GRID_W, D_Z), F32)] if grid else []))
    outs = pl.pallas_call(
        functools.partial(_scan_kernel, grid=grid, reverse=reverse, combine=combine, n_tiles=n_tiles, tm=tm,
                          seqs=seqs),
        grid=(b, n_tiles),
        in_specs=in_specs,
        out_specs=out_specs,
        out_shape=out_shape,
        scratch_shapes=scratch,
        compiler_params=_params(("arbitrary", "arbitrary")),
        name="scan_bwd" if reverse else "scan_fwd",
    )(*args)
    unfold = lambda a: a.reshape((n_seq, seq_len) + a.shape[2:])
    return tuple(unfold(a) for a in outs[:-1]) + (outs[-1].reshape((n_seq,) + outs[-1].shape[2:]),)


def _pack_state(s):
    b = s.shape[0]
    h = jnp.swapaxes(s, -1, -2).reshape(b, N_PAIRS, 2, HEAD_SIZE, HEAD_SIZE)
    return jnp.transpose(h, (0, 1, 3, 2, 4)).reshape(b, N_PAIRS, HEAD_SIZE, PAIR)


def _unpack_state(hp):
    b = hp.shape[0]
    h = jnp.transpose(hp.reshape(b, N_PAIRS, HEAD_SIZE, 2, HEAD_SIZE), (0, 1, 3, 2, 4))
    return jnp.swapaxes(h.reshape(b, N_RWKV_HEADS, HEAD_SIZE, HEAD_SIZE), -1, -2)


def _mixer_heads(z, s0_f, s0_b, prm, grid):
    y_b, bonus_b, h_b = _scan(z, _pack_state(s0_b), prm, 1, grid)
    y, h_f = _scan(z, _pack_state(s0_f), prm, 0, grid, other=(y_b, bonus_b))
    return y, _unpack_state(h_f), _unpack_state(h_b)


def _path(x, mod, mod_per_batch, s0_f, s0_b, grid, w, prm):
    b, s, _ = x.shape
    rows = (b, s) if mod_per_batch else (1, b * s)
    as_rows = lambda a: a.reshape(rows + a.shape[2:])
    as_seqs = lambda a: a.reshape((b, s) + a.shape[2:])
    x1, u, z = _front(as_rows(x), mod, mod_per_batch, w["norm_g"], w["gate"], w["up"], w["down"], w["w_fold"],
                      w["w_z"], prm["mu"], TOKEN_TILE, GRID_W if grid else s, grid)
    y_four = _fourier_two_stage(as_seqs(u)) if grid else _fourier_dense(as_seqs(u))
    y_rwkv, s_f, s_b = _mixer_heads(as_seqs(z), s0_f, s0_b, prm, grid)
    y = _back(x1, as_rows(y_four), as_rows(y_rwkv), mod, mod_per_batch, w["norm_g"], w["final"], w["w_out"],
              w["gate"], w["up"], w["down"], TOKEN_TILE)
    return as_seqs(y), s_f, s_b


def kernel(x_prompt, x_sample, state_fwd, state_bwd, c, c_ctx, w_mod, b_mod, norm_g, ffn_w_gate, ffn_w_up,
           ffn_w_down, w_in, shift_mu, decay_w0, decay_w2, iclr_a0, iclr_a2, gate_g2, k_k, k_a, r_k,
           ln_x_w, ln_x_b, w_out, final_norm):
    depth = w_mod.shape[0]
    assert depth == 1, "the back kernel applies the final norm, so exactly one layer is supported"
    bp = x_prompt.shape[0]
    bs = x_sample.shape[0]
    xp, xs = x_prompt, x_sample
    new_f, new_b = [], []
    for l in range(depth):
        cvec = jnp.zeros((8, D_MODEL), F32).at[:bs].set(c).at[bs].set(c_ctx)
        mod = _modulation(cvec, w_mod[l], b_mod[l]).reshape(8, N_MOD, D_MODEL)
        w = {"norm_g": norm_g[l], "final": final_norm,
             "gate": ffn_w_gate[l].astype(BF16), "up": ffn_w_up[l].astype(BF16), "down": ffn_w_down[l].astype(BF16),
             "w_fold": _fold_group_dft(w_in[l]), "w_z": w_in[l, :, D_FOURIER:].astype(BF16),
             "w_out": w_out[l].astype(BF16)}
        prm = {"mu": shift_mu[l], "w0": decay_w0[l], "w2": decay_w2[l], "a0": iclr_a0[l], "a2": iclr_a2[l],
               "g2": gate_g2[l], "k_k": k_k[l], "k_a": k_a[l], "r_k": r_k[l], "ln_w": ln_x_w[l], "ln_b": ln_x_b[l]}
        zero_state = jnp.zeros((bp, N_RWKV_HEADS, HEAD_SIZE, HEAD_SIZE), F32)
        xp, s_f, s_b = _path(xp, mod[bs:bs + 1], False, zero_state, zero_state, False, w, prm)
        new_f.append(s_f)
        new_b.append(s_b)
        xs, _, _ = _path(xs, mod[:bs], True, state_fwd[:, l], state_bwd[:, l], True, w, prm)
    return xp, xs, jnp.stack(new_f, axis=1), jnp.stack(new_b, axis=1)
```

```python
import functools

import numpy as np
import jax
import jax.numpy as jnp
from jax import lax
from jax.experimental import pallas as pl
from jax.experimental.pallas import tpu as pltpu

D_MODEL = 1024
GRID_W = 64
D_FOURIER = 512
N_FOURIER_GROUPS = 8
FOURIER_GROUP = D_FOURIER // N_FOURIER_GROUPS
D_RWKV = D_MODEL - D_FOURIER
HEAD_SIZE = 64
N_RWKV_HEADS = D_RWKV // HEAD_SIZE
N_PAIRS = N_RWKV_HEADS // 2
PAIR = 2 * HEAD_SIZE
DECAY_LORA = 64
ICLR_LORA = 64
GATE_LORA = 128
D_Z = 3 * D_RWKV + 2 * DECAY_LORA + 2 * ICLR_LORA + GATE_LORA
D_FOLD = 2 * D_FOURIER
D_FF = 2816
N_MOD = 9
RMS_EPS = 1e-6
GN_EPS = 64e-5
CHUNK = 64
TOKEN_TILE = 512
SCAN_TILE = 512
OFF_WD = 3 * D_RWKV
OFF_AD = OFF_WD + 2 * DECAY_LORA
OFF_GD = OFF_AD + 2 * ICLR_LORA
DECAY_SCALE = float(np.exp(-0.5))
V7X_VMEM_LIMIT = 56 * 1024 * 1024

F32 = jnp.float32
BF16 = jnp.bfloat16


def _dot32(a, b):
    a_hi = a.astype(BF16)
    b_hi = b.astype(BF16)
    a_lo = (a - a_hi.astype(F32)).astype(BF16)
    b_lo = (b - b_hi.astype(F32)).astype(BF16)
    dot = lambda x, y: jnp.dot(x, y, preferred_element_type=F32)
    return dot(a_hi, b_hi) + (dot(a_hi, b_lo) + dot(a_lo, b_hi))


def _sigmoid(x):
    return 1.0 / (1.0 + jnp.exp(-x))


def _norm_mod(x, g, shift, scale):
    ms = jnp.mean(x * x, axis=-1, keepdims=True)
    return x * lax.rsqrt(ms + RMS_EPS) * g * (1.0 + scale) + shift


def _swiglu(h, wg_ref, wu_ref, wd_ref):
    hb = h.astype(BF16)
    gate = jnp.dot(hb, wg_ref[...], preferred_element_type=F32)
    up = jnp.dot(hb, wu_ref[...], preferred_element_type=F32)
    act = gate * _sigmoid(gate) * up
    return jnp.dot(act.astype(BF16), wd_ref[...], preferred_element_type=F32)


def _const_spec(shape):
    nd = len(shape)
    return pl.BlockSpec(shape, lambda *_: (0,) * nd, pipeline_mode=pl.Buffered(1))


def _ffn_spec(shape, which):
    return pl.BlockSpec((None,) + shape, lambda *_: (which, 0, 0), pipeline_mode=pl.Buffered(1))


def _params(sem):
    return pltpu.CompilerParams(dimension_semantics=sem, vmem_limit_bytes=V7X_VMEM_LIMIT)


def _mod_kernel(c_ref, w_ref, b_ref, o_ref):
    c = c_ref[...]
    o_ref[...] = _dot32(c * _sigmoid(c), w_ref[...]) + b_ref[...]


def _modulation(cvec, w_mod, b_mod):
    n = w_mod.shape[1]
    tn = n // 8
    return pl.pallas_call(
        _mod_kernel,
        grid=(n // tn,),
        in_specs=[pl.BlockSpec((8, D_MODEL), lambda j: (0, 0)),
                  pl.BlockSpec((D_MODEL, tn), lambda j: (0, j)),
                  pl.BlockSpec((1, tn), lambda j: (0, j))],
        out_specs=pl.BlockSpec((8, tn), lambda j: (0, j)),
        out_shape=jax.ShapeDtypeStruct((8, n), F32),
        compiler_params=_params(("arbitrary",)),
        name="modulation",
    )(cvec, w_mod, b_mod.reshape(1, n))


def _shift_adjacent(z, mu, period, grid):
    tm = z.shape[0]
    col = lax.broadcasted_iota(jnp.int32, (tm, 1), 0) % period
    lane = lax.broadcasted_iota(jnp.int32, (1, z.shape[1]), 1) % 4
    back1 = jnp.where(col == 0, 0.0, pltpu.roll(z, 1, 0))
    fwd1 = jnp.where(col == period - 1, 0.0, pltpu.roll(z, tm - 1, 0))
    if grid:
        return jnp.where(lane < 2, z + mu * (jnp.where(lane == 0, back1, fwd1) - z), z)
    return z + mu * (jnp.where(lane % 2 == 0, back1, fwd1) - z)


def _front_kernel(x_ref, mod_ref, g_ref, wg_ref, wu_ref, wd_ref, wfold_ref, wz_ref, mu_ref,
                  x1_ref, u_ref, z_ref, *, period, grid):
    x = x_ref[...]
    h = _norm_mod(x, g_ref[0:1, :], mod_ref[0:1, :], mod_ref[1:2, :])
    x1 = x + 0.5 * mod_ref[2:3, :] * _swiglu(h, wg_ref, wu_ref, wd_ref)
    x1_ref[...] = x1
    h2 = _norm_mod(x1, g_ref[1:2, :], mod_ref[3:4, :], mod_ref[4:5, :]).astype(BF16)
    u_ref[...] = jnp.dot(h2, wfold_ref[...], preferred_element_type=F32)
    z_ref[...] = _shift_adjacent(jnp.dot(h2, wz_ref[...], preferred_element_type=F32), mu_ref[...], period, grid)


def _front(x, mod, mod_per_batch, norm_g, wg, wu, wd, w_fold, w_z, mu, tm, period, grid):
    b, s, _ = x.shape
    nt = s // tm
    assert tm % period == 0 and s % tm == 0
    tok = lambda w: pl.BlockSpec((None, tm, w), lambda i, j: (i, j, 0))
    mod_map = (lambda i, j: (i, 0, 0)) if mod_per_batch else (lambda i, j: (0, 0, 0))
    return pl.pallas_call(
        functools.partial(_front_kernel, period=period, grid=grid),
        grid=(b, nt),
        in_specs=[tok(D_MODEL),
                  pl.BlockSpec((None, N_MOD, D_MODEL), mod_map),
                  _const_spec((3, D_MODEL)),
                  _ffn_spec((D_MODEL, D_FF), 0), _ffn_spec((D_MODEL, D_FF), 0),
                  _ffn_spec((D_FF, D_MODEL), 0), _const_spec((D_MODEL, D_FOLD)), _const_spec((D_MODEL, D_Z)),
                  _const_spec((1, D_Z))],
        out_specs=[tok(D_MODEL), tok(D_FOLD), tok(D_Z)],
        out_shape=[jax.ShapeDtypeStruct((b, s, D_MODEL), F32),
                   jax.ShapeDtypeStruct((b, s, D_FOLD), F32),
                   jax.ShapeDtypeStruct((b, s, D_Z), F32)],
        compiler_params=_params(("arbitrary", "arbitrary")),
        name="front",
    )(x, mod, norm_g, wg, wu, wd, w_fold, w_z, mu.reshape(1, D_Z))


def _back_kernel(x1_ref, yf_ref, yr_ref, mod_ref, g_ref, fin_ref, wo_ref, wg_ref, wu_ref, wd_ref, o_ref):
    mixed = (jnp.dot(yf_ref[...].astype(BF16), wo_ref[0:D_FOURIER, :], preferred_element_type=F32)
             + jnp.dot(yr_ref[...].astype(BF16), wo_ref[D_FOURIER:, :], preferred_element_type=F32))
    x2 = x1_ref[...] + mod_ref[5:6, :] * mixed
    h = _norm_mod(x2, g_ref[2:3, :], mod_ref[6:7, :], mod_ref[7:8, :])
    x3 = x2 + 0.5 * mod_ref[8:9, :] * _swiglu(h, wg_ref, wu_ref, wd_ref)
    ms = jnp.mean(x3 * x3, axis=-1, keepdims=True)
    o_ref[...] = x3 * lax.rsqrt(ms + RMS_EPS) * fin_ref[...]


def _back(x1, yf, yr, mod, mod_per_batch, norm_g, final_norm, w_out, wg, wu, wd, tm):
    b, s, _ = x1.shape
    nt = s // tm
    tok = lambda w: pl.BlockSpec((None, tm, w), lambda i, j: (i, j, 0))
    mod_map = (lambda i, j: (i, 0, 0)) if mod_per_batch else (lambda i, j: (0, 0, 0))
    return pl.pallas_call(
        _back_kernel,
        grid=(b, nt),
        in_specs=[tok(D_MODEL), tok(D_FOURIER), tok(D_RWKV),
                  pl.BlockSpec((None, N_MOD, D_MODEL), mod_map),
                  _const_spec((3, D_MODEL)), _const_spec((1, D_MODEL)),
                  _const_spec((D_MODEL, D_MODEL)),
                  _ffn_spec((D_MODEL, D_FF), 1), _ffn_spec((D_MODEL, D_FF), 1),
                  _ffn_spec((D_FF, D_MODEL), 1)],
        out_specs=tok(D_MODEL),
        out_shape=jax.ShapeDtypeStruct((b, s, D_MODEL), F32),
        compiler_params=_params(("arbitrary", "arbitrary")),
        name="back",
    )(x1, yf, yr, mod, norm_g, final_norm.reshape(1, D_MODEL), w_out, wg, wu, wd)


def _fold_kernel(w_ref, cs_ref, o_ref):
    o_ref[...] = _dot32(w_ref[...], cs_ref[...]).astype(o_ref.dtype)


def _fold_group_dft(w_in):
    q = np.arange(FOURIER_GROUP)
    ang = 2.0 * np.pi * ((q[:, None] * q[None, :]) % FOURIER_GROUP) / FOURIER_GROUP
    eye = np.eye(N_FOURIER_GROUPS)
    cs = jnp.asarray(np.concatenate([np.kron(eye, np.cos(ang)), np.kron(eye, np.sin(ang))], axis=1), F32)
    return pl.pallas_call(
        _fold_kernel,
        grid=(1,),
        in_specs=[pl.BlockSpec((D_MODEL, D_FOURIER), lambda i: (0, 0)),
                  pl.BlockSpec((D_FOURIER, D_FOLD), lambda i: (0, 0))],
        out_specs=pl.BlockSpec((D_MODEL, D_FOLD), lambda i: (0, 0)),
        out_shape=jax.ShapeDtypeStruct((D_MODEL, D_FOLD), BF16),
        compiler_params=_params(("arbitrary",)),
        name="fold_group_dft",
    )(w_in, cs)


def _stack_cos_sin(x):
    return jnp.concatenate([x[:, :D_FOURIER], x[:, D_FOURIER:]], axis=0).astype(BF16)


def _fourier_dense_kernel(x_ref, p_ref, o_ref):
    o_ref[...] = _bdot(p_ref[...], _stack_cos_sin(x_ref[...]))


def _fourier_dense(xcs):
    b, s, _ = xcs.shape
    pos = np.arange(s)
    ang = 2.0 * np.pi * ((pos[:, None] * pos[None, :]) % s) / s
    table = np.concatenate([np.cos(ang), -np.sin(ang)], axis=1) / np.sqrt(s * FOURIER_GROUP)
    return pl.pallas_call(
        _fourier_dense_kernel,
        grid=(b,),
        in_specs=[pl.BlockSpec((None, s, D_FOLD), lambda i: (i, 0, 0)), _const_spec((s, 2 * s))],
        out_specs=pl.BlockSpec((None, s, D_FOURIER), lambda i: (i, 0, 0)),
        out_shape=jax.ShapeDtypeStruct((b, s, D_FOURIER), F32),
        compiler_params=_params(("arbitrary",)),
        name="fourier_dense",
    )(xcs, jnp.asarray(table, F32).astype(BF16))


FFT_ROWS = 16


def _fourier_stage1_kernel(x_ref, f_ref, z_ref):
    for j in range(FFT_ROWS):
        z = _bdot(f_ref[...], _stack_cos_sin(x_ref[:, j, :]))
        n = z.shape[0] // 2
        z_ref[:, j, :] = jnp.concatenate([z[:n], z[n:]], axis=1)


def _fourier_stage2_kernel(z_ref, g_ref, o_ref):
    for j in range(FFT_ROWS):
        o_ref[:, j, :] = _bdot(g_ref[j], _stack_cos_sin(z_ref[j]))


def _fourier_two_stage(xcs):
    b, s, _ = xcs.shape
    n = int(round(np.sqrt(s)))
    assert n * n == s and n % FFT_ROWS == 0
    idx = np.arange(n)
    ang = 2.0 * np.pi * ((idx[:, None] * idx[None, :]) % n) / n
    fc, fs = np.cos(ang), np.sin(ang)
    f2 = jnp.asarray(np.block([[fc, -fs], [fs, fc]]), F32).astype(BF16)
    z = pl.pallas_call(
        _fourier_stage1_kernel,
        grid=(b, n // FFT_ROWS),
        in_specs=[pl.BlockSpec((None, n, FFT_ROWS, D_FOLD), lambda i, j: (i, 0, j, 0)),
                  _const_spec((2 * n, 2 * n))],
        out_specs=pl.BlockSpec((None, n, FFT_ROWS, D_FOLD), lambda i, j: (i, 0, j, 0)),
        out_shape=jax.ShapeDtypeStruct((b, n, n, D_FOLD), F32),
        compiler_params=_params(("arbitrary", "arbitrary")),
        name="fourier_stage1",
    )(xcs.reshape(b, n, n, D_FOLD), f2)
    bb, aa, s1 = idx[:, None, None], idx[None, :, None], idx[None, None, :]
    ang2 = 2.0 * np.pi * ((s1 * (n * aa + bb)) % s) / s
    g2 = np.concatenate([np.cos(ang2), -np.sin(ang2)], axis=2) / np.sqrt(s * FOURIER_GROUP)
    out = pl.pallas_call(
        _fourier_stage2_kernel,
        grid=(b, n // FFT_ROWS),
        in_specs=[pl.BlockSpec((None, FFT_ROWS, n, D_FOLD), lambda i, j: (i, j, 0, 0)),
                  pl.BlockSpec((FFT_ROWS, n, 2 * n), lambda i, j: (j, 0, 0))],
        out_specs=pl.BlockSpec((None, n, FFT_ROWS, D_FOURIER), lambda i, j: (i, 0, j, 0)),
        out_shape=jax.ShapeDtypeStruct((b, n, n, D_FOURIER), F32),
        compiler_params=_params(("arbitrary", "arbitrary")),
        name="fourier_stage2",
    )(z, jnp.asarray(g2, F32).astype(BF16))
    return out.reshape(b, s, D_FOURIER)


def _shift_rows(z, zf_ref, mu):
    tm = z.shape[0]
    lane = lax.broadcasted_iota(jnp.int32, (1, z.shape[1]), 1) % 4
    up = zf_ref[0:tm, :]
    down = zf_ref[2 * GRID_W:2 * GRID_W + tm, :]
    return z + jnp.where(lane < 2, 0.0, mu) * (jnp.where(lane == 2, up, down) - z)


def _stack_masked(x, head0):
    return jnp.concatenate([jnp.where(head0, x, 0.0), jnp.where(head0, 0.0, x)], axis=0)


NT = (((1,), (1,)), ((), ()))
CUM_PIECES = 2
CHUNKS_PER_STEP = 4


def _split(x, pieces):
    out = []
    for i in range(pieces):
        hi = x.astype(BF16)
        out.append(hi)
        if i + 1 < pieces:
            x = x - hi.astype(F32)
    return out


def _bdot(a, b, dims=None):
    if dims is None:
        return jnp.dot(a, b, preferred_element_type=F32)
    return lax.dot_general(a, b, dims, preferred_element_type=F32)


HEADS_PER_SUM = 4


def _head_sums(x, ones_ref):
    w = HEADS_PER_SUM * HEAD_SIZE
    return jnp.concatenate([_bdot(x[:, i:i + w].astype(BF16), ones_ref[...]) for i in range(0, D_RWKV, w)], axis=1)


def _scan_kernel(*refs, grid, dirs, other, n_tiles, tm, seqs):
    final = other or len(dirs) == 2
    it = iter(refs)
    z_ref = next(it)
    zp_ref = next(it) if grid else None
    zn_ref = next(it) if grid else None
    h0_refs = {d: next(it) for d in dirs}
    if other:
        yo_ref, bo_ref = next(it), next(it)
    mu_ref, kk_ref, ka_ref, rk_ref, ones_ref = (next(it) for _ in range(5))
    lora = {d: tuple(next(it) for _ in range(4)) for d in dirs}
    if final:
        lnw_ref, lnb_ref, g2_ref = next(it), next(it), next(it)
    y_ref = next(it)
    b_ref = None if final else next(it)
    hout_refs = {d: next(it) for d in dirs}
    r_s, v_s, al_s, g_s = (next(it) for _ in range(4))
    per_dir = {d: tuple(next(it) for _ in range(7)) for d in dirs}
    zf_s = next(it) if grid else None

    step = pl.program_id(1)
    tile = (n_tiles - 1 - step) if dirs == (1,) else step

    @pl.when(step == 0)
    def _():
        for d in dirs:
            per_dir[d][4][...] = h0_refs[d][...]

    if grid:
        zf_s[GRID_W:GRID_W + tm, :] = z_ref[...]
        zf_s[0:GRID_W, :] = jnp.where(tile == 0, 0.0, zp_ref[...])
        zf_s[GRID_W + tm:, :] = jnp.where(tile == n_tiles - 1, 0.0, zn_ref[...])

    def prepare(rows):
        z = z_ref[rows, :]
        if grid:
            z = _shift_rows(z, zf_s.at[rows.start:rows.stop + 2 * GRID_W, :], mu_ref[...])
        r = z[:, 0:D_RWKV]
        k = z[:, D_RWKV:2 * D_RWKV]
        v = z[:, 2 * D_RWKV:3 * D_RWKV]
        kk = k * kk_ref[...]
        kk = kk * lax.rsqrt(jnp.maximum(_head_sums(kk * kk, ones_ref), 1e-24))
        r_s[rows, :] = r
        v_s[rows, :] = v
        al_s[rows, :] = -kk
        bonus = bo_ref[rows, :] if other else None
        for d in dirs:
            w0_ref, a0_ref, w2_ref, a2_ref = lora[d]
            be_s, kd_s, lw_s = per_dir[d][:3]
            w_raw = w0_ref[...] + _bdot(jnp.tanh(z[:, OFF_WD:OFF_AD]).astype(BF16), w2_ref[...])
            a = _sigmoid(a0_ref[...] + _bdot(z[:, OFF_AD:OFF_GD].astype(BF16), a2_ref[...]))
            kd = k * (1.0 + (a - 1.0) * ka_ref[...])
            bonus_d = _head_sums(r * kd * rk_ref[...], ones_ref) * v
            bonus = bonus_d if bonus is None else bonus + bonus_d
            be_s[rows, :] = a * kk
            kd_s[rows, :] = kd
            lw_s[rows, :] = -DECAY_SCALE * _sigmoid(w_raw)
        if final:
            y_ref[rows, :] = bonus
            g_s[rows, :] = _bdot(_sigmoid(z[:, OFF_GD:]).astype(BF16), g2_ref[...])
        else:
            b_ref[rows, :] = bonus

    def finish(rows):
        y_sum = yo_ref[rows, :] if other else None
        for d in dirs:
            y_d = per_dir[d][3][rows, :]
            y_sum = y_d if y_sum is None else y_sum + y_d
        if final:
            mean = _head_sums(y_sum, ones_ref) * (1.0 / HEAD_SIZE)
            cen = y_sum - mean
            var = _head_sums(cen * cen, ones_ref) * (1.0 / HEAD_SIZE)
            y_gn = cen * lax.rsqrt(var + GN_EPS) * lnw_ref[...] + lnb_ref[...]
            y_ref[rows, :] = (y_gn + y_ref[rows, :]) * g_s[rows, :]
        else:
            y_ref[rows, :] = y_sum

    n_chunks = tm // CHUNK
    per_seq = n_chunks // seqs
    assert per_seq % CHUNKS_PER_STEP == 0, "a group of chunks must not straddle two sequences"
    ri = lax.broadcasted_iota(jnp.int32, (CHUNK, CHUNK), 0)
    ci = lax.broadcasted_iota(jnp.int32, (CHUNK, CHUNK), 1)
    rj = lax.broadcasted_iota(jnp.int32, (CHUNK, PAIR), 0)
    cj = lax.broadcasted_iota(jnp.int32, (CHUNK, PAIR), 1) % CHUNK
    eye = rj == cj
    tri = {0: (ci <= ri).astype(BF16), 1: (ci >= ri).astype(BF16)}
    strict = {0: cj < rj, 1: cj > rj}
    incl = {0: cj <= rj, 1: cj >= rj}
    last = {0: CHUNK - 1, 1: 0}
    head0 = lax.broadcasted_iota(jnp.int32, (1, PAIR), 1) < HEAD_SIZE
    blockdiag = lambda x: _stack_masked(x, head0)

    def head_transpose(x):
        t = blockdiag(x).T
        return t[:CHUNK] + t[CHUNK:]

    same_block = lambda n: (rj & -n) == (cj & -n)
    levels = [2 ** i for i in range(1, int(np.log2(CHUNK)))]
    base_mask = same_block(2)
    pair_masks = {n: same_block(2 * n) & ~same_block(n) for n in levels}

    def chunk_rows(c, d):
        seq, k = divmod(c, per_seq)
        off = (seq * per_seq + ((per_seq - 1 - k) if d == 1 else k)) * CHUNK
        return slice(off, off + CHUNK)

    def local_phase(chunks, d):
        be_s, kd_s, lw_s, _, _, qm_s, yn_s = per_dir[d]
        units = []
        for c in chunks:
            rows = chunk_rows(c, d)
            lw = lw_s[rows, :]
            cum = sum(_bdot(tri[d], piece) for piece in _split(lw, CUM_PIECES))
            tot = cum[last[d]:last[d] + 1, :]
            g_inv = jnp.exp(-cum)
            g_tot = jnp.exp(tot)
            g_end = g_tot * g_inv
            abar = al_s[rows, :] * jnp.exp(cum - lw)
            rbar = r_s[rows, :] * jnp.exp(cum)
            be = be_s[rows, :]
            kdc = kd_s[rows, :]
            btil, ktil, bhat, khat = be * g_inv, kdc * g_inv, be * g_end, kdc * g_end
            vc = v_s[rows, :]
            for p in range(N_PAIRS):
                lanes = slice(p * PAIR, (p + 1) * PAIR)
                cut = lambda x: x[:, lanes]
                units.append(dict(c=c, p=p, a=cut(abar).astype(BF16), r=cut(rbar), v=cut(vc).astype(BF16),
                                  bt=cut(btil).astype(BF16), kt=cut(ktil).astype(BF16),
                                  bh=cut(bhat), kh=cut(khat), g_tot=cut(g_tot)))
        for u in units:
            ar = jnp.concatenate([u["a"], u["r"].astype(BF16)], axis=0)
            u["gb"] = _bdot(ar, blockdiag(u["bt"]), NT)
            u["gk"] = _bdot(ar, blockdiag(u["kt"]), NT)
        for u in units:
            gb, gk = u.pop("gb"), u.pop("gk")
            u["l_ab"] = jnp.where(strict[d], gb[:CHUNK], 0.0)
            u["l_ak"] = jnp.where(strict[d], gk[:CHUNK], 0.0).astype(BF16)
            u["l_rbk"] = jnp.concatenate([jnp.where(incl[d], gb[CHUNK:], 0.0),
                                          jnp.where(incl[d], gk[CHUNK:], 0.0)], axis=1).astype(BF16)
            u["t"] = jnp.where(eye, 1.0, jnp.where(base_mask, u["l_ab"], 0.0))
        for n in levels:
            for u in units:
                e = jnp.where(pair_masks[n], u["l_ab"], 0.0).astype(BF16)
                u["et"] = _bdot(e, blockdiag(u["t"].astype(BF16)))
            for u in units:
                u["t"] = u["t"] + _bdot(u["t"].astype(BF16), blockdiag(u["et"].astype(BF16)))
        for u in units:
            u["x1"] = _bdot(u["l_ak"], blockdiag(u["v"]))
        for u in units:
            rhs = jnp.concatenate([blockdiag(u["a"]), blockdiag(u["x1"].astype(BF16))], axis=1)
            u["pu"] = _bdot(u["t"].astype(BF16), rhs)
        for u in units:
            pu = u["pu"].astype(BF16)
            v_bd = blockdiag(u["v"])
            rhs = jnp.concatenate(
                [jnp.concatenate([blockdiag(pu[:, :PAIR]), blockdiag(pu[:, PAIR:])], axis=1),
                 jnp.concatenate([jnp.zeros_like(v_bd), v_bd], axis=1)], axis=0)
            bkh_t = jnp.concatenate([head_transpose(u["bh"]), head_transpose(u["kh"])], axis=1)
            o = _bdot(jnp.concatenate([u["l_rbk"], bkh_t.astype(BF16)], axis=0), rhs)
            q = u["r"] + o[:CHUNK, :PAIR]
            m = jnp.where(eye, u["g_tot"], 0.0) + o[CHUNK:, :PAIR]
            qm_s[u["c"], u["p"]] = jnp.concatenate([q, m], axis=0).astype(BF16)
            yn_s[u["c"], u["p"]] = o[:, PAIR:]

    def state_phase(chunks, d):
        _, _, _, y_s, h_s, qm_s, yn_s = per_dir[d]
        for c in chunks:
            seq = c // per_seq
            for p in range(N_PAIRS):
                oh = _bdot(qm_s[c, p], blockdiag(h_s[seq, p].astype(BF16))) + yn_s[c, p]
                y_s[chunk_rows(c, d), p * PAIR:(p + 1) * PAIR] = oh[:CHUNK]
                h_s[seq, p] = oh[CHUNK:]

    spans = [list(range(g, g + CHUNKS_PER_STEP)) for g in range(0, n_chunks, CHUNKS_PER_STEP)]
    groups = [(chunks, d) for chunks in spans for d in dirs]

    def group_rows(chunks, d):
        spans = [chunk_rows(c, d) for c in chunks]
        return slice(min(r.start for r in spans), max(r.stop for r in spans))

    for i, (chunks, d) in enumerate(groups):
        if d == dirs[0]:
            prepare(group_rows(chunks, d))
        local_phase(chunks, d)
        if i > 0:
            prev_chunks, prev_d = groups[i - 1]
            state_phase(prev_chunks, prev_d)
            if prev_d == dirs[-1]:
                finish(group_rows(prev_chunks, prev_d))
    state_phase(*groups[-1])
    finish(group_rows(*groups[-1]))

    @pl.when(step == n_tiles - 1)
    def _():
        for d in dirs:
            hout_refs[d][...] = per_dir[d][4][...]


def _scan(z, h0, prm, dirs, grid, other=None):
    n_seq, seq_len, _ = z.shape
    tm = SCAN_TILE
    seqs = max(tm // seq_len, 1)
    assert (seq_len % tm == 0) if seqs == 1 else (tm == seqs * seq_len and n_seq % seqs == 0 and not grid)
    b, s = n_seq // seqs, seq_len * seqs
    n_tiles = s // tm
    assert len(dirs) == 1 or n_tiles == 1, "both directions in one call need the whole sequence in a tile"
    final = other is not None or len(dirs) == 2
    fold = lambda a: a.reshape((b, seqs * a.shape[1]) + a.shape[2:])
    tile_of = (lambda j: n_tiles - 1 - j) if dirs == (1,) else (lambda j: j)
    tok = lambda w: pl.BlockSpec((None, tm, w), lambda i, j: (i, tile_of(j), 0))
    in_specs, args = [tok(D_Z)], [fold(z)]
    if grid:
        halo_per_tile, n_halo = tm // GRID_W, s // GRID_W
        in_specs += [pl.BlockSpec((None, GRID_W, D_Z),
                                  lambda i, j: (i, jnp.maximum(tile_of(j) * halo_per_tile - 1, 0), 0)),
                     pl.BlockSpec((None, GRID_W, D_Z),
                                  lambda i, j: (i, jnp.minimum((tile_of(j) + 1) * halo_per_tile, n_halo - 1), 0))]
        args += [args[0], args[0]]
    state_spec = pl.BlockSpec((None, seqs, N_PAIRS, HEAD_SIZE, PAIR), lambda i, j: (i, 0, 0, 0, 0))
    state_shape = jax.ShapeDtypeStruct((b, seqs, N_PAIRS, HEAD_SIZE, PAIR), F32)
    for d in dirs:
        in_specs.append(state_spec)
        args.append(h0[d].reshape(state_shape.shape))
    if other is not None:
        in_specs += [tok(D_RWKV), tok(D_RWKV)]
        args += [fold(a) for a in other]
    row = lambda x: x.reshape(1, -1)
    lora_pad = lambda w, d: jnp.zeros((2 * w.shape[1], D_RWKV), BF16).at[d * w.shape[1]:(d + 1) * w.shape[1]].set(
        w[d].astype(BF16))
    small = [row(prm["mu"]), row(prm["k_k"]), row(prm["k_a"]), row(prm["r_k"]),
             jnp.asarray(np.kron(np.eye(HEADS_PER_SUM), np.ones((HEAD_SIZE, HEAD_SIZE))), BF16)]
    for d in dirs:
        small += [row(prm["w0"][d]), row(prm["a0"][d]), lora_pad(prm["w2"], d), lora_pad(prm["a2"], d)]
    if final:
        small += [row(prm["ln_w"]), row(prm["ln_b"]), prm["g2"].astype(BF16)]
    in_specs += [_const_spec(x.shape) for x in small]
    args += small
    tokens = jax.ShapeDtypeStruct((b, s, D_RWKV), F32)
    out_specs, out_shape = [tok(D_RWKV)], [tokens]
    if not final:
        out_specs.append(tok(D_RWKV))
        out_shape.append(tokens)
    out_specs += [state_spec] * len(dirs)
    out_shape += [state_shape] * len(dirs)
    n_chunks = tm // CHUNK
    rows = lambda: pltpu.VMEM((tm, D_RWKV), F32)
    scratch = [rows() for _ in range(4)]
    for d in dirs:
        scratch += [rows() for _ in range(4)] + [pltpu.VMEM((seqs, N_PAIRS, CHUNK, PAIR), F32),
                                                  pltpu.VMEM((n_chunks, N_PAIRS, 2 * CHUNK, PAIR), BF16),
                                                  pltpu.VMEM((n_chunks, N_PAIRS, 2 * CHUNK, PAIR), F32)]
    if grid:
        scratch.append(pltpu.VMEM((tm + 2 * GRID_W, D_Z), F32))
    outs = pl.pallas_call(
        functools.partial(_scan_kernel, grid=grid, dirs=dirs, other=other is not None, n_tiles=n_tiles, tm=tm,
                          seqs=seqs),
        grid=(b, n_tiles),
        in_specs=in_specs,
        out_specs=out_specs,
        out_shape=out_shape,
        scratch_shapes=scratch,
        compiler_params=_params(("arbitrary", "arbitrary")),
        name="scan_" + "".join("fb"[d] for d in dirs),
    )(*args)
    unfold = lambda a: a.reshape((n_seq, seq_len) + a.shape[2:])
    states = {d: a.reshape((n_seq,) + a.shape[2:]) for d, a in zip(dirs, outs[-len(dirs):])}
    return tuple(unfold(a) for a in outs[:-len(dirs)]) + (states,)


def _pack_state(s):
    b = s.shape[0]
    h = jnp.swapaxes(s, -1, -2).reshape(b, N_PAIRS, 2, HEAD_SIZE, HEAD_SIZE)
    return jnp.transpose(h, (0, 1, 3, 2, 4)).reshape(b, N_PAIRS, HEAD_SIZE, PAIR)


def _unpack_state(hp):
    b = hp.shape[0]
    h = jnp.transpose(hp.reshape(b, N_PAIRS, HEAD_SIZE, 2, HEAD_SIZE), (0, 1, 3, 2, 4))
    return jnp.swapaxes(h.reshape(b, N_RWKV_HEADS, HEAD_SIZE, HEAD_SIZE), -1, -2)


def _mixer_heads(z, s0_f, s0_b, prm, grid):
    h0 = {0: _pack_state(s0_f), 1: _pack_state(s0_b)}
    if z.shape[1] <= SCAN_TILE:
        y, h = _scan(z, h0, prm, (0, 1), grid)
    else:
        y_b, bonus_b, h_b = _scan(z, h0, prm, (1,), grid)
        y, h_f = _scan(z, h0, prm, (0,), grid, other=(y_b, bonus_b))
        h = {**h_f, **h_b}
    return y, _unpack_state(h[0]), _unpack_state(h[1])


def _path(x, mod, mod_per_batch, s0_f, s0_b, grid, w, prm):
    b, s, _ = x.shape
    rows = (b, s) if mod_per_batch else (1, b * s)
    as_rows = lambda a: a.reshape(rows + a.shape[2:])
    as_seqs = lambda a: a.reshape((b, s) + a.shape[2:])
    x1, u, z = _front(as_rows(x), mod, mod_per_batch, w["norm_g"], w["gate"], w["up"], w["down"], w["w_fold"],
                      w["w_z"], prm["mu"], TOKEN_TILE, GRID_W if grid else s, grid)
    y_four = _fourier_two_stage(as_seqs(u)) if grid else _fourier_dense(as_seqs(u))
    y_rwkv, s_f, s_b = _mixer_heads(as_seqs(z), s0_f, s0_b, prm, grid)
    y = _back(x1, as_rows(y_four), as_rows(y_rwkv), mod, mod_per_batch, w["norm_g"], w["final"], w["w_out"],
              w["gate"], w["up"], w["down"], TOKEN_TILE)
    return as_seqs(y), s_f, s_b


def kernel(x_prompt, x_sample, state_fwd, state_bwd, c, c_ctx, w_mod, b_mod, norm_g, ffn_w_gate, ffn_w_up,
           ffn_w_down, w_in, shift_mu, decay_w0, decay_w2, iclr_a0, iclr_a2, gate_g2, k_k, k_a, r_k,
           ln_x_w, ln_x_b, w_out, final_norm):
    depth = w_mod.shape[0]
    assert depth == 1, "the back kernel applies the final norm, so exactly one layer is supported"
    bp = x_prompt.shape[0]
    bs = x_sample.shape[0]
    xp, xs = x_prompt, x_sample
    new_f, new_b = [], []
    for l in range(depth):
        cvec = jnp.zeros((8, D_MODEL), F32).at[:bs].set(c).at[bs].set(c_ctx)
        mod = _modulation(cvec, w_mod[l], b_mod[l]).reshape(8, N_MOD, D_MODEL)
        w = {"norm_g": norm_g[l], "final": final_norm,
             "gate": ffn_w_gate[l].astype(BF16), "up": ffn_w_up[l].astype(BF16), "down": ffn_w_down[l].astype(BF16),
             "w_fold": _fold_group_dft(w_in[l]), "w_z": w_in[l, :, D_FOURIER:].astype(BF16),
             "w_out": w_out[l].astype(BF16)}
        prm = {"mu": shift_mu[l], "w0": decay_w0[l], "w2": decay_w2[l], "a0": iclr_a0[l], "a2": iclr_a2[l],
               "g2": gate_g2[l], "k_k": k_k[l], "k_a": k_a[l], "r_k": r_k[l], "ln_w": ln_x_w[l], "ln_b": ln_x_b[l]}
        zero_state = jnp.zeros((bp, N_RWKV_HEADS, HEAD_SIZE, HEAD_SIZE), F32)
        xp, s_f, s_b = _path(xp, mod[bs:bs + 1], False, zero_state, zero_state, False, w, prm)
        new_f.append(s_f)
        new_b.append(s_b)
        xs, _, _ = _path(xs, mod[:bs], True, state_fwd[:, l], state_bwd[:, l], True, w, prm)
    return xp, xs, jnp.stack(new_f, axis=1), jnp.stack(new_b, axis=1)
```

```python
import functools

import numpy as np
import jax
import jax.numpy as jnp
from jax import lax
from jax.experimental import pallas as pl
from jax.experimental.pallas import tpu as pltpu

D_MODEL = 1024
GRID_W = 64
D_FOURIER = 512
N_FOURIER_GROUPS = 8
FOURIER_GROUP = D_FOURIER // N_FOURIER_GROUPS
D_RWKV = D_MODEL - D_FOURIER
HEAD_SIZE = 64
N_RWKV_HEADS = D_RWKV // HEAD_SIZE
N_PAIRS = N_RWKV_HEADS // 2
PAIR = 2 * HEAD_SIZE
DECAY_LORA = 64
ICLR_LORA = 64
GATE_LORA = 128
D_Z = 3 * D_RWKV + 2 * DECAY_LORA + 2 * ICLR_LORA + GATE_LORA
D_FOLD = 2 * D_FOURIER
D_FF = 2816
N_MOD = 9
RMS_EPS = 1e-6
GN_EPS = 64e-5
CHUNK = 64
TOKEN_TILE = 512
SCAN_TILE = 512
OFF_WD = 3 * D_RWKV
OFF_AD = OFF_WD + 2 * DECAY_LORA
OFF_GD = OFF_AD + 2 * ICLR_LORA
DECAY_SCALE = float(np.exp(-0.5))
V7X_VMEM_LIMIT = 56 * 1024 * 1024

F32 = jnp.float32
BF16 = jnp.bfloat16


def _dot32(a, b):
    a_hi = a.astype(BF16)
    b_hi = b.astype(BF16)
    a_lo = (a - a_hi.astype(F32)).astype(BF16)
    b_lo = (b - b_hi.astype(F32)).astype(BF16)
    dot = lambda x, y: jnp.dot(x, y, preferred_element_type=F32)
    return dot(a_hi, b_hi) + (dot(a_hi, b_lo) + dot(a_lo, b_hi))


def _sigmoid(x):
    return 1.0 / (1.0 + jnp.exp(-x))


def _norm_mod(x, g, shift, scale):
    ms = jnp.mean(x * x, axis=-1, keepdims=True)
    return x * lax.rsqrt(ms + RMS_EPS) * g * (1.0 + scale) + shift


def _swiglu(h, wg_ref, wu_ref, wd_ref):
    hb = h.astype(BF16)
    gate = jnp.dot(hb, wg_ref[...], preferred_element_type=F32)
    up = jnp.dot(hb, wu_ref[...], preferred_element_type=F32)
    act = gate * _sigmoid(gate) * up
    return jnp.dot(act.astype(BF16), wd_ref[...], preferred_element_type=F32)


def _const_spec(shape):
    nd = len(shape)
    return pl.BlockSpec(shape, lambda *_: (0,) * nd, pipeline_mode=pl.Buffered(1))


def _ffn_spec(shape, which):
    return pl.BlockSpec((None,) + shape, lambda *_: (which, 0, 0), pipeline_mode=pl.Buffered(1))


def _params(sem):
    return pltpu.CompilerParams(dimension_semantics=sem, vmem_limit_bytes=V7X_VMEM_LIMIT)


def _mod_kernel(c_ref, w_ref, b_ref, o_ref):
    c = c_ref[...]
    o_ref[...] = _dot32(c * _sigmoid(c), w_ref[...]) + b_ref[...]


def _modulation(cvec, w_mod, b_mod):
    n = w_mod.shape[1]
    tn = n // 8
    return pl.pallas_call(
        _mod_kernel,
        grid=(n // tn,),
        in_specs=[pl.BlockSpec((8, D_MODEL), lambda j: (0, 0)),
                  pl.BlockSpec((D_MODEL, tn), lambda j: (0, j)),
                  pl.BlockSpec((1, tn), lambda j: (0, j))],
        out_specs=pl.BlockSpec((8, tn), lambda j: (0, j)),
        out_shape=jax.ShapeDtypeStruct((8, n), F32),
        compiler_params=_params(("arbitrary",)),
        name="modulation",
    )(cvec, w_mod, b_mod.reshape(1, n))


def _shift_adjacent(z, mu, period, grid):
    tm = z.shape[0]
    col = lax.broadcasted_iota(jnp.int32, (tm, 1), 0) % period
    lane = lax.broadcasted_iota(jnp.int32, (1, z.shape[1]), 1) % 4
    back1 = jnp.where(col == 0, 0.0, pltpu.roll(z, 1, 0))
    fwd1 = jnp.where(col == period - 1, 0.0, pltpu.roll(z, tm - 1, 0))
    if grid:
        return jnp.where(lane < 2, z + mu * (jnp.where(lane == 0, back1, fwd1) - z), z)
    return z + mu * (jnp.where(lane % 2 == 0, back1, fwd1) - z)


def _front_kernel(x_ref, mod_ref, g_ref, wg_ref, wu_ref, wd_ref, wfold_ref, wz_ref, mu_ref,
                  x1_ref, u_ref, z_ref, *, period, grid):
    x = x_ref[...]
    h = _norm_mod(x, g_ref[0:1, :], mod_ref[0:1, :], mod_ref[1:2, :])
    x1 = x + 0.5 * mod_ref[2:3, :] * _swiglu(h, wg_ref, wu_ref, wd_ref)
    x1_ref[...] = x1
    h2 = _norm_mod(x1, g_ref[1:2, :], mod_ref[3:4, :], mod_ref[4:5, :]).astype(BF16)
    u_ref[...] = jnp.dot(h2, wfold_ref[...], preferred_element_type=F32)
    z_ref[...] = _shift_adjacent(jnp.dot(h2, wz_ref[...], preferred_element_type=F32), mu_ref[...], period, grid)


def _front(x, mod, mod_per_batch, norm_g, wg, wu, wd, w_fold, w_z, mu, tm, period, grid):
    b, s, _ = x.shape
    nt = s // tm
    assert tm % period == 0 and s % tm == 0
    tok = lambda w: pl.BlockSpec((None, tm, w), lambda i, j: (i, j, 0))
    mod_map = (lambda i, j: (i, 0, 0)) if mod_per_batch else (lambda i, j: (0, 0, 0))
    return pl.pallas_call(
        functools.partial(_front_kernel, period=period, grid=grid),
        grid=(b, nt),
        in_specs=[tok(D_MODEL),
                  pl.BlockSpec((None, N_MOD, D_MODEL), mod_map),
                  _const_spec((3, D_MODEL)),
                  _ffn_spec((D_MODEL, D_FF), 0), _ffn_spec((D_MODEL, D_FF), 0),
                  _ffn_spec((D_FF, D_MODEL), 0), _const_spec((D_MODEL, D_FOLD)), _const_spec((D_MODEL, D_Z)),
                  _const_spec((1, D_Z))],
        out_specs=[tok(D_MODEL), tok(D_FOLD), tok(D_Z)],
        out_shape=[jax.ShapeDtypeStruct((b, s, D_MODEL), F32),
                   jax.ShapeDtypeStruct((b, s, D_FOLD), F32),
                   jax.ShapeDtypeStruct((b, s, D_Z), F32)],
        compiler_params=_params(("arbitrary", "arbitrary")),
        name="front",
    )(x, mod, norm_g, wg, wu, wd, w_fold, w_z, mu.reshape(1, D_Z))


def _back_kernel(x1_ref, yf_ref, yr_ref, mod_ref, g_ref, fin_ref, wo_ref, wg_ref, wu_ref, wd_ref, o_ref):
    mixed = (jnp.dot(yf_ref[...].astype(BF16), wo_ref[0:D_FOURIER, :], preferred_element_type=F32)
             + jnp.dot(yr_ref[...].astype(BF16), wo_ref[D_FOURIER:, :], preferred_element_type=F32))
    x2 = x1_ref[...] + mod_ref[5:6, :] * mixed
    h = _norm_mod(x2, g_ref[2:3, :], mod_ref[6:7, :], mod_ref[7:8, :])
    x3 = x2 + 0.5 * mod_ref[8:9, :] * _swiglu(h, wg_ref, wu_ref, wd_ref)
    ms = jnp.mean(x3 * x3, axis=-1, keepdims=True)
    o_ref[...] = x3 * lax.rsqrt(ms + RMS_EPS) * fin_ref[...]


def _back(x1, yf, yr, mod, mod_per_batch, norm_g, final_norm, w_out, wg, wu, wd, tm):
    b, s, _ = x1.shape
    nt = s // tm
    tok = lambda w: pl.BlockSpec((None, tm, w), lambda i, j: (i, j, 0))
    mod_map = (lambda i, j: (i, 0, 0)) if mod_per_batch else (lambda i, j: (0, 0, 0))
    return pl.pallas_call(
        _back_kernel,
        grid=(b, nt),
        in_specs=[tok(D_MODEL), tok(D_FOURIER), tok(D_RWKV),
                  pl.BlockSpec((None, N_MOD, D_MODEL), mod_map),
                  _const_spec((3, D_MODEL)), _const_spec((1, D_MODEL)),
                  _const_spec((D_MODEL, D_MODEL)),
                  _ffn_spec((D_MODEL, D_FF), 1), _ffn_spec((D_MODEL, D_FF), 1),
                  _ffn_spec((D_FF, D_MODEL), 1)],
        out_specs=tok(D_MODEL),
        out_shape=jax.ShapeDtypeStruct((b, s, D_MODEL), F32),
        compiler_params=_params(("arbitrary", "arbitrary")),
        name="back",
    )(x1, yf, yr, mod, norm_g, final_norm.reshape(1, D_MODEL), w_out, wg, wu, wd)


def _fold_kernel(w_ref, cs_ref, o_ref):
    o_ref[...] = _dot32(w_ref[...], cs_ref[...]).astype(o_ref.dtype)


def _fold_group_dft(w_in):
    q = np.arange(FOURIER_GROUP)
    ang = 2.0 * np.pi * ((q[:, None] * q[None, :]) % FOURIER_GROUP) / FOURIER_GROUP
    eye = np.eye(N_FOURIER_GROUPS)
    cs = jnp.asarray(np.concatenate([np.kron(eye, np.cos(ang)), np.kron(eye, np.sin(ang))], axis=1), F32)
    return pl.pallas_call(
        _fold_kernel,
        grid=(1,),
        in_specs=[pl.BlockSpec((D_MODEL, D_FOURIER), lambda i: (0, 0)),
                  pl.BlockSpec((D_FOURIER, D_FOLD), lambda i: (0, 0))],
        out_specs=pl.BlockSpec((D_MODEL, D_FOLD), lambda i: (0, 0)),
        out_shape=jax.ShapeDtypeStruct((D_MODEL, D_FOLD), BF16),
        compiler_params=_params(("arbitrary",)),
        name="fold_group_dft",
    )(w_in, cs)


def _stack_cos_sin(x):
    return jnp.concatenate([x[:, :D_FOURIER], x[:, D_FOURIER:]], axis=0).astype(BF16)


def _fourier_dense_kernel(x_ref, p_ref, o_ref):
    o_ref[...] = _bdot(p_ref[...], _stack_cos_sin(x_ref[...]))


def _fourier_dense(xcs):
    b, s, _ = xcs.shape
    pos = np.arange(s)
    ang = 2.0 * np.pi * ((pos[:, None] * pos[None, :]) % s) / s
    table = np.concatenate([np.cos(ang), -np.sin(ang)], axis=1) / np.sqrt(s * FOURIER_GROUP)
    return pl.pallas_call(
        _fourier_dense_kernel,
        grid=(b,),
        in_specs=[pl.BlockSpec((None, s, D_FOLD), lambda i: (i, 0, 0)), _const_spec((s, 2 * s))],
        out_specs=pl.BlockSpec((None, s, D_FOURIER), lambda i: (i, 0, 0)),
        out_shape=jax.ShapeDtypeStruct((b, s, D_FOURIER), F32),
        compiler_params=_params(("arbitrary",)),
        name="fourier_dense",
    )(xcs, jnp.asarray(table, F32).astype(BF16))


FFT_ROWS = 16


def _fourier_stage1_kernel(x_ref, f_ref, z_ref):
    for j in range(FFT_ROWS):
        z = _bdot(f_ref[...], _stack_cos_sin(x_ref[:, j, :]))
        n = z.shape[0] // 2
        z_ref[j] = jnp.concatenate([z[:n], z[n:]], axis=1)


def _fourier_stage2_kernel(z_ref, g_ref, o_ref):
    for j in range(FFT_ROWS):
        o_ref[:, j, :] = _bdot(g_ref[j], _stack_cos_sin(z_ref[:, j, :]))


def _fourier_two_stage(xcs):
    b, s, _ = xcs.shape
    n = int(round(np.sqrt(s)))
    assert n * n == s and n % FFT_ROWS == 0
    idx = np.arange(n)
    ang = 2.0 * np.pi * ((idx[:, None] * idx[None, :]) % n) / n
    fc, fs = np.cos(ang), np.sin(ang)
    f2 = jnp.asarray(np.block([[fc, -fs], [fs, fc]]), F32).astype(BF16)
    z = pl.pallas_call(
        _fourier_stage1_kernel,
        grid=(b, n // FFT_ROWS),
        in_specs=[pl.BlockSpec((None, n, FFT_ROWS, D_FOLD), lambda i, j: (i, 0, j, 0)),
                  _const_spec((2 * n, 2 * n))],
        out_specs=pl.BlockSpec((None, FFT_ROWS, n, D_FOLD), lambda i, j: (i, j, 0, 0)),
        out_shape=jax.ShapeDtypeStruct((b, n, n, D_FOLD), F32),
        compiler_params=_params(("arbitrary", "arbitrary")),
        name="fourier_stage1",
    )(xcs.reshape(b, n, n, D_FOLD), f2)
    bb, aa, s1 = idx[:, None, None], idx[None, :, None], idx[None, None, :]
    ang2 = 2.0 * np.pi * ((s1 * (n * aa + bb)) % s) / s
    g2 = np.concatenate([np.cos(ang2), -np.sin(ang2)], axis=2) / np.sqrt(s * FOURIER_GROUP)
    out = pl.pallas_call(
        _fourier_stage2_kernel,
        grid=(b, n // FFT_ROWS),
        in_specs=[pl.BlockSpec((None, n, FFT_ROWS, D_FOLD), lambda i, j: (i, 0, j, 0)),
                  pl.BlockSpec((FFT_ROWS, n, 2 * n), lambda i, j: (j, 0, 0))],
        out_specs=pl.BlockSpec((None, n, FFT_ROWS, D_FOURIER), lambda i, j: (i, 0, j, 0)),
        out_shape=jax.ShapeDtypeStruct((b, n, n, D_FOURIER), F32),
        compiler_params=_params(("arbitrary", "arbitrary")),
        name="fourier_stage2",
    )(z, jnp.asarray(g2, F32).astype(BF16))
    return out.reshape(b, s, D_FOURIER)


def _shift_rows(z, zf_ref, mu):
    tm = z.shape[0]
    lane = lax.broadcasted_iota(jnp.int32, (1, z.shape[1]), 1) % 4
    up = zf_ref[0:tm, :]
    down = zf_ref[2 * GRID_W:2 * GRID_W + tm, :]
    return z + jnp.where(lane < 2, 0.0, mu) * (jnp.where(lane == 2, up, down) - z)


def _stack_masked(x, head0):
    return jnp.concatenate([jnp.where(head0, x, 0.0), jnp.where(head0, 0.0, x)], axis=0)


NT = (((1,), (1,)), ((), ()))
CUM_PIECES = 2
CHUNKS_PER_STEP = 4


def _split(x, pieces):
    out = []
    for i in range(pieces):
        hi = x.astype(BF16)
        out.append(hi)
        if i + 1 < pieces:
            x = x - hi.astype(F32)
    return out


def _bdot(a, b, dims=None):
    if dims is None:
        return jnp.dot(a, b, preferred_element_type=F32)
    return lax.dot_general(a, b, dims, preferred_element_type=F32)


HEADS_PER_SUM = 4


def _head_sums(x, ones_ref):
    w = HEADS_PER_SUM * HEAD_SIZE
    return jnp.concatenate([_bdot(x[:, i:i + w].astype(BF16), ones_ref[...]) for i in range(0, D_RWKV, w)], axis=1)


def _scan_kernel(*refs, grid, dirs, other, n_tiles, tm, seqs):
    final = other or len(dirs) == 2
    it = iter(refs)
    z_ref = next(it)
    zp_ref = next(it) if grid else None
    zn_ref = next(it) if grid else None
    h0_refs = {d: next(it) for d in dirs}
    if other:
        yo_ref, bo_ref = next(it), next(it)
    mu_ref, kk_ref, ka_ref, rk_ref, ones_ref = (next(it) for _ in range(5))
    lora = {d: tuple(next(it) for _ in range(4)) for d in dirs}
    if final:
        lnw_ref, lnb_ref, g2_ref = next(it), next(it), next(it)
    y_ref = next(it)
    b_ref = None if final else next(it)
    hout_refs = {d: next(it) for d in dirs}
    r_s, v_s, al_s, g_s = (next(it) for _ in range(4))
    per_dir = {d: tuple(next(it) for _ in range(7)) for d in dirs}
    zf_s = next(it) if grid else None

    step = pl.program_id(1)
    tile = (n_tiles - 1 - step) if dirs == (1,) else step

    @pl.when(step == 0)
    def _():
        for d in dirs:
            per_dir[d][4][...] = h0_refs[d][...]

    if grid:
        zf_s[GRID_W:GRID_W + tm, :] = z_ref[...]
        zf_s[0:GRID_W, :] = jnp.where(tile == 0, 0.0, zp_ref[...])
        zf_s[GRID_W + tm:, :] = jnp.where(tile == n_tiles - 1, 0.0, zn_ref[...])

    def prepare(rows):
        z = z_ref[rows, :]
        if grid:
            z = _shift_rows(z, zf_s.at[rows.start:rows.stop + 2 * GRID_W, :], mu_ref[...])
        r = z[:, 0:D_RWKV]
        k = z[:, D_RWKV:2 * D_RWKV]
        v = z[:, 2 * D_RWKV:3 * D_RWKV]
        kk = k * kk_ref[...]
        kk = kk * lax.rsqrt(jnp.maximum(_head_sums(kk * kk, ones_ref), 1e-24))
        r_s[rows, :] = r
        v_s[rows, :] = v
        al_s[rows, :] = -kk
        bonus = bo_ref[rows, :] if other else None
        for d in dirs:
            w0_ref, a0_ref, w2_ref, a2_ref = lora[d]
            be_s, kd_s, lw_s = per_dir[d][:3]
            w_raw = w0_ref[...] + _bdot(jnp.tanh(z[:, OFF_WD:OFF_AD]).astype(BF16), w2_ref[...])
            a = _sigmoid(a0_ref[...] + _bdot(z[:, OFF_AD:OFF_GD].astype(BF16), a2_ref[...]))
            kd = k * (1.0 + (a - 1.0) * ka_ref[...])
            bonus_d = _head_sums(r * kd * rk_ref[...], ones_ref) * v
            bonus = bonus_d if bonus is None else bonus + bonus_d
            be_s[rows, :] = a * kk
            kd_s[rows, :] = kd
            lw_s[rows, :] = -DECAY_SCALE * _sigmoid(w_raw)
        if final:
            y_ref[rows, :] = bonus
            g_s[rows, :] = _bdot(_sigmoid(z[:, OFF_GD:]).astype(BF16), g2_ref[...])
        else:
            b_ref[rows, :] = bonus

    def finish(rows):
        y_sum = yo_ref[rows, :] if other else None
        for d in dirs:
            y_d = per_dir[d][3][rows, :]
            y_sum = y_d if y_sum is None else y_sum + y_d
        if final:
            mean = _head_sums(y_sum, ones_ref) * (1.0 / HEAD_SIZE)
            cen = y_sum - mean
            var = _head_sums(cen * cen, ones_ref) * (1.0 / HEAD_SIZE)
            y_gn = cen * lax.rsqrt(var + GN_EPS) * lnw_ref[...] + lnb_ref[...]
            y_ref[rows, :] = (y_gn + y_ref[rows, :]) * g_s[rows, :]
        else:
            y_ref[rows, :] = y_sum

    n_chunks = tm // CHUNK
    per_seq = n_chunks // seqs
    assert per_seq % CHUNKS_PER_STEP == 0, "a group of chunks must not straddle two sequences"
    ri = lax.broadcasted_iota(jnp.int32, (CHUNK, CHUNK), 0)
    ci = lax.broadcasted_iota(jnp.int32, (CHUNK, CHUNK), 1)
    rj = lax.broadcasted_iota(jnp.int32, (CHUNK, PAIR), 0)
    cj = lax.broadcasted_iota(jnp.int32, (CHUNK, PAIR), 1) % CHUNK
    eye = rj == cj
    tri = {0: (ci <= ri).astype(BF16), 1: (ci >= ri).astype(BF16)}
    strict = {0: cj < rj, 1: cj > rj}
    incl = {0: cj <= rj, 1: cj >= rj}
    last = {0: CHUNK - 1, 1: 0}
    head0 = lax.broadcasted_iota(jnp.int32, (1, PAIR), 1) < HEAD_SIZE
    blockdiag = lambda x: _stack_masked(x, head0)

    def head_transpose(x):
        t = blockdiag(x).T
        return t[:CHUNK] + t[CHUNK:]

    same_block = lambda n: (rj & -n) == (cj & -n)
    levels = [2 ** i for i in range(1, int(np.log2(CHUNK)))]
    base_mask = same_block(2)
    pair_masks = {n: same_block(2 * n) & ~same_block(n) for n in levels}

    def chunk_rows(c, d):
        seq, k = divmod(c, per_seq)
        off = (seq * per_seq + ((per_seq - 1 - k) if d == 1 else k)) * CHUNK
        return slice(off, off + CHUNK)

    def local_phase(chunks, d):
        be_s, kd_s, lw_s, _, _, qm_s, yn_s = per_dir[d]
        units = []
        for c in chunks:
            rows = chunk_rows(c, d)
            lw = lw_s[rows, :]
            cum = sum(_bdot(tri[d], piece) for piece in _split(lw, CUM_PIECES))
            tot = cum[last[d]:last[d] + 1, :]
            g_inv = jnp.exp(-cum)
            g_tot = jnp.exp(tot)
            g_end = g_tot * g_inv
            abar = al_s[rows, :] * jnp.exp(cum - lw)
            rbar = r_s[rows, :] * jnp.exp(cum)
            be = be_s[rows, :]
            kdc = kd_s[rows, :]
            btil, ktil, bhat, khat = be * g_inv, kdc * g_inv, be * g_end, kdc * g_end
            vc = v_s[rows, :]
            for p in range(N_PAIRS):
                lanes = slice(p * PAIR, (p + 1) * PAIR)
                cut = lambda x: x[:, lanes]
                units.append(dict(c=c, p=p, a=cut(abar).astype(BF16), r=cut(rbar), v=cut(vc).astype(BF16),
                                  bt=cut(btil).astype(BF16), kt=cut(ktil).astype(BF16),
                                  bh=cut(bhat), kh=cut(khat), g_tot=cut(g_tot)))
        for u in units:
            ar = jnp.concatenate([u["a"], u["r"].astype(BF16)], axis=0)
            u["gb"] = _bdot(ar, blockdiag(u["bt"]), NT)
            u["gk"] = _bdot(ar, blockdiag(u["kt"]), NT)
        for u in units:
            gb, gk = u.pop("gb"), u.pop("gk")
            u["l_ab"] = jnp.where(strict[d], gb[:CHUNK], 0.0)
            u["l_ak"] = jnp.where(strict[d], gk[:CHUNK], 0.0).astype(BF16)
            u["l_rbk"] = jnp.concatenate([jnp.where(incl[d], gb[CHUNK:], 0.0),
                                          jnp.where(incl[d], gk[CHUNK:], 0.0)], axis=1).astype(BF16)
            u["t"] = jnp.where(eye, 1.0, jnp.where(base_mask, u["l_ab"], 0.0))
        for n in levels:
            for u in units:
                e = jnp.where(pair_masks[n], u["l_ab"], 0.0).astype(BF16)
                u["et"] = _bdot(e, blockdiag(u["t"].astype(BF16)))
            for u in units:
                u["t"] = u["t"] + _bdot(u["t"].astype(BF16), blockdiag(u["et"].astype(BF16)))
        for u in units:
            u["x1"] = _bdot(u["l_ak"], blockdiag(u["v"]))
        for u in units:
            rhs = jnp.concatenate([blockdiag(u["a"]), blockdiag(u["x1"].astype(BF16))], axis=1)
            u["pu"] = _bdot(u["t"].astype(BF16), rhs)
        for u in units:
            pu = u["pu"].astype(BF16)
            v_bd = blockdiag(u["v"])
            rhs = jnp.concatenate(
                [jnp.concatenate([blockdiag(pu[:, :PAIR]), blockdiag(pu[:, PAIR:])], axis=1),
                 jnp.concatenate([jnp.zeros_like(v_bd), v_bd], axis=1)], axis=0)
            bkh_t = jnp.concatenate([head_transpose(u["bh"]), head_transpose(u["kh"])], axis=1)
            o = _bdot(jnp.concatenate([u["l_rbk"], bkh_t.astype(BF16)], axis=0), rhs)
            q = u["r"] + o[:CHUNK, :PAIR]
            m = jnp.where(eye, u["g_tot"], 0.0) + o[CHUNK:, :PAIR]
            qm_s[u["c"], u["p"]] = jnp.concatenate([q, m], axis=0).astype(BF16)
            yn_s[u["c"], u["p"]] = o[:, PAIR:]

    def state_phase(chunks, d):
        _, _, _, y_s, h_s, qm_s, yn_s = per_dir[d]
        for c in chunks:
            seq = c // per_seq
            for p in range(N_PAIRS):
                oh = _bdot(qm_s[c, p], blockdiag(h_s[seq, p].astype(BF16))) + yn_s[c, p]
                y_s[chunk_rows(c, d), p * PAIR:(p + 1) * PAIR] = oh[:CHUNK]
                h_s[seq, p] = oh[CHUNK:]

    spans = [list(range(g, g + CHUNKS_PER_STEP)) for g in range(0, n_chunks, CHUNKS_PER_STEP)]
    groups = [(chunks, d) for chunks in spans for d in dirs]

    def group_rows(chunks, d):
        spans = [chunk_rows(c, d) for c in chunks]
        return slice(min(r.start for r in spans), max(r.stop for r in spans))

    for i, (chunks, d) in enumerate(groups):
        if d == dirs[0]:
            prepare(group_rows(chunks, d))
        local_phase(chunks, d)
        if i > 0:
            prev_chunks, prev_d = groups[i - 1]
            state_phase(prev_chunks, prev_d)
            if prev_d == dirs[-1]:
                finish(group_rows(prev_chunks, prev_d))
    state_phase(*groups[-1])
    finish(group_rows(*groups[-1]))

    @pl.when(step == n_tiles - 1)
    def _():
        for d in dirs:
            hout_refs[d][...] = per_dir[d][4][...]


def _scan(z, h0, prm, dirs, grid, other=None):
    n_seq, seq_len, _ = z.shape
    tm = SCAN_TILE
    seqs = max(tm // seq_len, 1)
    assert (seq_len % tm == 0) if seqs == 1 else (tm == seqs * seq_len and n_seq % seqs == 0 and not grid)
    b, s = n_seq // seqs, seq_len * seqs
    n_tiles = s // tm
    assert len(dirs) == 1 or n_tiles == 1, "both directions in one call need the whole sequence in a tile"
    final = other is not None or len(dirs) == 2
    fold = lambda a: a.reshape((b, seqs * a.shape[1]) + a.shape[2:])
    tile_of = (lambda j: n_tiles - 1 - j) if dirs == (1,) else (lambda j: j)
    tok = lambda w: pl.BlockSpec((None, tm, w), lambda i, j: (i, tile_of(j), 0))
    in_specs, args = [tok(D_Z)], [fold(z)]
    if grid:
        halo_per_tile, n_halo = tm // GRID_W, s // GRID_W
        in_specs += [pl.BlockSpec((None, GRID_W, D_Z),
                                  lambda i, j: (i, jnp.maximum(tile_of(j) * halo_per_tile - 1, 0), 0)),
                     pl.BlockSpec((None, GRID_W, D_Z),
                                  lambda i, j: (i, jnp.minimum((tile_of(j) + 1) * halo_per_tile, n_halo - 1), 0))]
        args += [args[0], args[0]]
    state_spec = pl.BlockSpec((None, seqs, N_PAIRS, HEAD_SIZE, PAIR), lambda i, j: (i, 0, 0, 0, 0))
    state_shape = jax.ShapeDtypeStruct((b, seqs, N_PAIRS, HEAD_SIZE, PAIR), F32)
    for d in dirs:
        in_specs.append(state_spec)
        args.append(h0[d].reshape(state_shape.shape))
    if other is not None:
        in_specs += [tok(D_RWKV), tok(D_RWKV)]
        args += [fold(a) for a in other]
    row = lambda x: x.reshape(1, -1)
    lora_pad = lambda w, d: jnp.zeros((2 * w.shape[1], D_RWKV), BF16).at[d * w.shape[1]:(d + 1) * w.shape[1]].set(
        w[d].astype(BF16))
    small = [row(prm["mu"]), row(prm["k_k"]), row(prm["k_a"]), row(prm["r_k"]),
             jnp.asarray(np.kron(np.eye(HEADS_PER_SUM), np.ones((HEAD_SIZE, HEAD_SIZE))), BF16)]
    for d in dirs:
        small += [row(prm["w0"][d]), row(prm["a0"][d]), lora_pad(prm["w2"], d), lora_pad(prm["a2"], d)]
    if final:
        small += [row(prm["ln_w"]), row(prm["ln_b"]), prm["g2"].astype(BF16)]
    in_specs += [_const_spec(x.shape) for x in small]
    args += small
    tokens = jax.ShapeDtypeStruct((b, s, D_RWKV), F32)
    out_specs, out_shape = [tok(D_RWKV)], [tokens]
    if not final:
        out_specs.append(tok(D_RWKV))
        out_shape.append(tokens)
    out_specs += [state_spec] * len(dirs)
    out_shape += [state_shape] * len(dirs)
    n_chunks = tm // CHUNK
    rows = lambda: pltpu.VMEM((tm, D_RWKV), F32)
    scratch = [rows() for _ in range(4)]
    for d in dirs:
        scratch += [rows() for _ in range(4)] + [pltpu.VMEM((seqs, N_PAIRS, CHUNK, PAIR), F32),
                                                  pltpu.VMEM((n_chunks, N_PAIRS, 2 * CHUNK, PAIR), BF16),
                                                  pltpu.VMEM((n_chunks, N_PAIRS, 2 * CHUNK, PAIR), F32)]
    if grid:
        scratch.append(pltpu.VMEM((tm + 2 * GRID_W, D_Z), F32))
    outs = pl.pallas_call(
        functools.partial(_scan_kernel, grid=grid, dirs=dirs, other=other is not None, n_tiles=n_tiles, tm=tm,
                          seqs=seqs),
        grid=(b, n_tiles),
        in_specs=in_specs,
        out_specs=out_specs,
        out_shape=out_shape,
        scratch_shapes=scratch,
        compiler_params=_params(("arbitrary", "arbitrary")),
        name="scan_" + "".join("fb"[d] for d in dirs),
    )(*args)
    unfold = lambda a: a.reshape((n_seq, seq_len) + a.shape[2:])
    states = {d: a.reshape((n_seq,) + a.shape[2:]) for d, a in zip(dirs, outs[-len(dirs):])}
    return tuple(unfold(a) for a in outs[:-len(dirs)]) + (states,)


def _pack_state(s):
    b = s.shape[0]
    h = jnp.swapaxes(s, -1, -2).reshape(b, N_PAIRS, 2, HEAD_SIZE, HEAD_SIZE)
    return jnp.transpose(h, (0, 1, 3, 2, 4)).reshape(b, N_PAIRS, HEAD_SIZE, PAIR)


def _unpack_state(hp):
    b = hp.shape[0]
    h = jnp.transpose(hp.reshape(b, N_PAIRS, HEAD_SIZE, 2, HEAD_SIZE), (0, 1, 3, 2, 4))
    return jnp.swapaxes(h.reshape(b, N_RWKV_HEADS, HEAD_SIZE, HEAD_SIZE), -1, -2)


def _mixer_heads(z, s0_f, s0_b, prm, grid):
    h0 = {0: _pack_state(s0_f), 1: _pack_state(s0_b)}
    if z.shape[1] <= SCAN_TILE:
        y, h = _scan(z, h0, prm, (0, 1), grid)
    else:
        y_b, bonus_b, h_b = _scan(z, h0, prm, (1,), grid)
        y, h_f = _scan(z, h0, prm, (0,), grid, other=(y_b, bonus_b))
        h = {**h_f, **h_b}
    return y, _unpack_state(h[0]), _unpack_state(h[1])


def _path(x, mod, mod_per_batch, s0_f, s0_b, grid, w, prm):
    b, s, _ = x.shape
    rows = (b, s) if mod_per_batch else (1, b * s)
    as_rows = lambda a: a.reshape(rows + a.shape[2:])
    as_seqs = lambda a: a.reshape((b, s) + a.shape[2:])
    x1, u, z = _front(as_rows(x), mod, mod_per_batch, w["norm_g"], w["gate"], w["up"], w["down"], w["w_fold"],
                      w["w_z"], prm["mu"], TOKEN_TILE, GRID_W if grid else s, grid)
    y_four = _fourier_two_stage(as_seqs(u)) if grid else _fourier_dense(as_seqs(u))
    y_rwkv, s_f, s_b = _mixer_heads(as_seqs(z), s0_f, s0_b, prm, grid)
    y = _back(x1, as_rows(y_four), as_rows(y_rwkv), mod, mod_per_batch, w["norm_g"], w["final"], w["w_out"],
              w["gate"], w["up"], w["down"], TOKEN_TILE)
    return as_seqs(y), s_f, s_b


def kernel(x_prompt, x_sample, state_fwd, state_bwd, c, c_ctx, w_mod, b_mod, norm_g, ffn_w_gate, ffn_w_up,
           ffn_w_down, w_in, shift_mu, decay_w0, decay_w2, iclr_a0, iclr_a2, gate_g2, k_k, k_a, r_k,
           ln_x_w, ln_x_b, w_out, final_norm):
    depth = w_mod.shape[0]
    assert depth == 1, "the back kernel applies the final norm, so exactly one layer is supported"
    bp = x_prompt.shape[0]
    bs = x_sample.shape[0]
    xp, xs = x_prompt, x_sample
    new_f, new_b = [], []
    for l in range(depth):
        cvec = jnp.zeros((8, D_MODEL), F32).at[:bs].set(c).at[bs].set(c_ctx)
        mod = _modulation(cvec, w_mod[l], b_mod[l]).reshape(8, N_MOD, D_MODEL)
        w = {"norm_g": norm_g[l], "final": final_norm,
             "gate": ffn_w_gate[l].astype(BF16), "up": ffn_w_up[l].astype(BF16), "down": ffn_w_down[l].astype(BF16),
             "w_fold": _fold_group_dft(w_in[l]), "w_z": w_in[l, :, D_FOURIER:].astype(BF16),
             "w_out": w_out[l].astype(BF16)}
        prm = {"mu": shift_mu[l], "w0": decay_w0[l], "w2": decay_w2[l], "a0": iclr_a0[l], "a2": iclr_a2[l],
               "g2": gate_g2[l], "k_k": k_k[l], "k_a": k_a[l], "r_k": r_k[l], "ln_w": ln_x_w[l], "ln_b": ln_x_b[l]}
        zero_state = jnp.zeros((bp, N_RWKV_HEADS, HEAD_SIZE, HEAD_SIZE), F32)
        xp, s_f, s_b = _path(xp, mod[bs:bs + 1], False, zero_state, zero_state, False, w, prm)
        new_f.append(s_f)
        new_b.append(s_b)
        xs, _, _ = _path(xs, mod[:bs], True, state_fwd[:, l], state_bwd[:, l], True, w, prm)
    return xp, xs, jnp.stack(new_f, axis=1), jnp.stack(new_b, axis=1)
```

```python
import functools

import numpy as np
import jax
import jax.numpy as jnp
from jax import lax
from jax.experimental import pallas as pl
from jax.experimental.pallas import tpu as pltpu

D_MODEL = 1024
GRID_W = 64
D_FOURIER = 512
N_FOURIER_GROUPS = 8
FOURIER_GROUP = D_FOURIER // N_FOURIER_GROUPS
D_RWKV = D_MODEL - D_FOURIER
HEAD_SIZE = 64
N_RWKV_HEADS = D_RWKV // HEAD_SIZE
N_PAIRS = N_RWKV_HEADS // 2
PAIR = 2 * HEAD_SIZE
DECAY_LORA = 64
ICLR_LORA = 64
GATE_LORA = 128
D_Z = 3 * D_RWKV + 2 * DECAY_LORA + 2 * ICLR_LORA + GATE_LORA
D_FOLD = 2 * D_FOURIER
D_FF = 2816
N_MOD = 9
RMS_EPS = 1e-6
GN_EPS = 64e-5
CHUNK = 64
TOKEN_TILE = 512
SCAN_TILE = 1024
OFF_WD = 3 * D_RWKV
OFF_AD = OFF_WD + 2 * DECAY_LORA
OFF_GD = OFF_AD + 2 * ICLR_LORA
DECAY_SCALE = float(np.exp(-0.5))
V7X_VMEM_LIMIT = 56 * 1024 * 1024

F32 = jnp.float32
BF16 = jnp.bfloat16


def _dot32(a, b):
    a_hi = a.astype(BF16)
    b_hi = b.astype(BF16)
    a_lo = (a - a_hi.astype(F32)).astype(BF16)
    b_lo = (b - b_hi.astype(F32)).astype(BF16)
    dot = lambda x, y: jnp.dot(x, y, preferred_element_type=F32)
    return dot(a_hi, b_hi) + (dot(a_hi, b_lo) + dot(a_lo, b_hi))


def _sigmoid(x):
    return 1.0 / (1.0 + jnp.exp(-x))


def _norm_mod(x, g, shift, scale):
    ms = jnp.mean(x * x, axis=-1, keepdims=True)
    return x * lax.rsqrt(ms + RMS_EPS) * g * (1.0 + scale) + shift


def _swiglu(h, wg_ref, wu_ref, wd_ref):
    hb = h.astype(BF16)
    gate = jnp.dot(hb, wg_ref[...], preferred_element_type=F32)
    up = jnp.dot(hb, wu_ref[...], preferred_element_type=F32)
    act = gate * _sigmoid(gate) * up
    return jnp.dot(act.astype(BF16), wd_ref[...], preferred_element_type=F32)


def _const_spec(shape):
    nd = len(shape)
    return pl.BlockSpec(shape, lambda *_: (0,) * nd, pipeline_mode=pl.Buffered(1))


def _ffn_spec(shape, which):
    return pl.BlockSpec((None,) + shape, lambda *_: (which, 0, 0), pipeline_mode=pl.Buffered(1))


def _params(sem):
    return pltpu.CompilerParams(dimension_semantics=sem, vmem_limit_bytes=V7X_VMEM_LIMIT)


def _mod_kernel(c_ref, w_ref, b_ref, o_ref):
    c = c_ref[...]
    o_ref[...] = _dot32(c * _sigmoid(c), w_ref[...]) + b_ref[...]


def _modulation(cvec, w_mod, b_mod):
    n = w_mod.shape[1]
    tn = n // 8
    return pl.pallas_call(
        _mod_kernel,
        grid=(n // tn,),
        in_specs=[pl.BlockSpec((8, D_MODEL), lambda j: (0, 0)),
                  pl.BlockSpec((D_MODEL, tn), lambda j: (0, j)),
                  pl.BlockSpec((1, tn), lambda j: (0, j))],
        out_specs=pl.BlockSpec((8, tn), lambda j: (0, j)),
        out_shape=jax.ShapeDtypeStruct((8, n), F32),
        compiler_params=_params(("arbitrary",)),
        name="modulation",
    )(cvec, w_mod, b_mod.reshape(1, n))


def _shift_adjacent(z, mu, period, grid):
    tm = z.shape[0]
    col = lax.broadcasted_iota(jnp.int32, (tm, 1), 0) % period
    lane = lax.broadcasted_iota(jnp.int32, (1, z.shape[1]), 1) % 4
    back1 = jnp.where(col == 0, 0.0, pltpu.roll(z, 1, 0))
    fwd1 = jnp.where(col == period - 1, 0.0, pltpu.roll(z, tm - 1, 0))
    if grid:
        return jnp.where(lane < 2, z + mu * (jnp.where(lane == 0, back1, fwd1) - z), z)
    return z + mu * (jnp.where(lane % 2 == 0, back1, fwd1) - z)


def _front_kernel(x_ref, mod_ref, g_ref, wg_ref, wu_ref, wd_ref, wfold_ref, wz_ref, mu_ref,
                  x1_ref, u_ref, z_ref, *, period, grid):
    x = x_ref[...]
    h = _norm_mod(x, g_ref[0:1, :], mod_ref[0:1, :], mod_ref[1:2, :])
    x1 = x + 0.5 * mod_ref[2:3, :] * _swiglu(h, wg_ref, wu_ref, wd_ref)
    x1_ref[...] = x1
    h2 = _norm_mod(x1, g_ref[1:2, :], mod_ref[3:4, :], mod_ref[4:5, :]).astype(BF16)
    u_ref[...] = jnp.dot(h2, wfold_ref[...], preferred_element_type=F32)
    z_ref[...] = _shift_adjacent(jnp.dot(h2, wz_ref[...], preferred_element_type=F32), mu_ref[...], period, grid)


def _front(x, mod, mod_per_batch, norm_g, wg, wu, wd, w_fold, w_z, mu, tm, period, grid):
    b, s, _ = x.shape
    nt = s // tm
    assert tm % period == 0 and s % tm == 0
    tok = lambda w: pl.BlockSpec((None, tm, w), lambda i, j: (i, j, 0))
    mod_map = (lambda i, j: (i, 0, 0)) if mod_per_batch else (lambda i, j: (0, 0, 0))
    return pl.pallas_call(
        functools.partial(_front_kernel, period=period, grid=grid),
        grid=(b, nt),
        in_specs=[tok(D_MODEL),
                  pl.BlockSpec((None, N_MOD, D_MODEL), mod_map),
                  _const_spec((3, D_MODEL)),
                  _ffn_spec((D_MODEL, D_FF), 0), _ffn_spec((D_MODEL, D_FF), 0),
                  _ffn_spec((D_FF, D_MODEL), 0), _const_spec((D_MODEL, D_FOLD)), _const_spec((D_MODEL, D_Z)),
                  _const_spec((1, D_Z))],
        out_specs=[tok(D_MODEL), tok(D_FOLD), tok(D_Z)],
        out_shape=[jax.ShapeDtypeStruct((b, s, D_MODEL), F32),
                   jax.ShapeDtypeStruct((b, s, D_FOLD), F32),
                   jax.ShapeDtypeStruct((b, s, D_Z), F32)],
        compiler_params=_params(("arbitrary", "arbitrary")),
        name="front",
    )(x, mod, norm_g, wg, wu, wd, w_fold, w_z, mu.reshape(1, D_Z))


def _back_kernel(x1_ref, yf_ref, yr_ref, mod_ref, g_ref, fin_ref, wo_ref, wg_ref, wu_ref, wd_ref, o_ref):
    mixed = (jnp.dot(yf_ref[...].astype(BF16), wo_ref[0:D_FOURIER, :], preferred_element_type=F32)
             + jnp.dot(yr_ref[...].astype(BF16), wo_ref[D_FOURIER:, :], preferred_element_type=F32))
    x2 = x1_ref[...] + mod_ref[5:6, :] * mixed
    h = _norm_mod(x2, g_ref[2:3, :], mod_ref[6:7, :], mod_ref[7:8, :])
    x3 = x2 + 0.5 * mod_ref[8:9, :] * _swiglu(h, wg_ref, wu_ref, wd_ref)
    ms = jnp.mean(x3 * x3, axis=-1, keepdims=True)
    o_ref[...] = x3 * lax.rsqrt(ms + RMS_EPS) * fin_ref[...]


def _back(x1, yf, yr, mod, mod_per_batch, norm_g, final_norm, w_out, wg, wu, wd, tm):
    b, s, _ = x1.shape
    nt = s // tm
    tok = lambda w: pl.BlockSpec((None, tm, w), lambda i, j: (i, j, 0))
    mod_map = (lambda i, j: (i, 0, 0)) if mod_per_batch else (lambda i, j: (0, 0, 0))
    return pl.pallas_call(
        _back_kernel,
        grid=(b, nt),
        in_specs=[tok(D_MODEL), tok(D_FOURIER), tok(D_RWKV),
                  pl.BlockSpec((None, N_MOD, D_MODEL), mod_map),
                  _const_spec((3, D_MODEL)), _const_spec((1, D_MODEL)),
                  _const_spec((D_MODEL, D_MODEL)),
                  _ffn_spec((D_MODEL, D_FF), 1), _ffn_spec((D_MODEL, D_FF), 1),
                  _ffn_spec((D_FF, D_MODEL), 1)],
        out_specs=tok(D_MODEL),
        out_shape=jax.ShapeDtypeStruct((b, s, D_MODEL), F32),
        compiler_params=_params(("arbitrary", "arbitrary")),
        name="back",
    )(x1, yf, yr, mod, norm_g, final_norm.reshape(1, D_MODEL), w_out, wg, wu, wd)


def _fold_kernel(w_ref, cs_ref, o_ref):
    o_ref[...] = _dot32(w_ref[...], cs_ref[...]).astype(o_ref.dtype)


def _fold_group_dft(w_in):
    q = np.arange(FOURIER_GROUP)
    ang = 2.0 * np.pi * ((q[:, None] * q[None, :]) % FOURIER_GROUP) / FOURIER_GROUP
    eye = np.eye(N_FOURIER_GROUPS)
    cs = jnp.asarray(np.concatenate([np.kron(eye, np.cos(ang)), np.kron(eye, np.sin(ang))], axis=1), F32)
    return pl.pallas_call(
        _fold_kernel,
        grid=(1,),
        in_specs=[pl.BlockSpec((D_MODEL, D_FOURIER), lambda i: (0, 0)),
                  pl.BlockSpec((D_FOURIER, D_FOLD), lambda i: (0, 0))],
        out_specs=pl.BlockSpec((D_MODEL, D_FOLD), lambda i: (0, 0)),
        out_shape=jax.ShapeDtypeStruct((D_MODEL, D_FOLD), BF16),
        compiler_params=_params(("arbitrary",)),
        name="fold_group_dft",
    )(w_in, cs)


def _stack_cos_sin(x):
    return jnp.concatenate([x[:, :D_FOURIER], x[:, D_FOURIER:]], axis=0).astype(BF16)


def _fourier_dense_kernel(x_ref, p_ref, o_ref):
    o_ref[...] = _bdot(p_ref[...], _stack_cos_sin(x_ref[...]))


def _fourier_dense(xcs):
    b, s, _ = xcs.shape
    pos = np.arange(s)
    ang = 2.0 * np.pi * ((pos[:, None] * pos[None, :]) % s) / s
    table = np.concatenate([np.cos(ang), -np.sin(ang)], axis=1) / np.sqrt(s * FOURIER_GROUP)
    return pl.pallas_call(
        _fourier_dense_kernel,
        grid=(b,),
        in_specs=[pl.BlockSpec((None, s, D_FOLD), lambda i: (i, 0, 0)), _const_spec((s, 2 * s))],
        out_specs=pl.BlockSpec((None, s, D_FOURIER), lambda i: (i, 0, 0)),
        out_shape=jax.ShapeDtypeStruct((b, s, D_FOURIER), F32),
        compiler_params=_params(("arbitrary",)),
        name="fourier_dense",
    )(xcs, jnp.asarray(table, F32).astype(BF16))


FFT_ROWS = 16


def _fourier_stage1_kernel(x_ref, f_ref, z_ref):
    for j in range(FFT_ROWS):
        z = _bdot(f_ref[...], _stack_cos_sin(x_ref[:, j, :]))
        n = z.shape[0] // 2
        z_ref[:, j, :] = jnp.concatenate([z[:n], z[n:]], axis=1)


def _fourier_stage2_kernel(z_ref, g_ref, o_ref):
    for j in range(FFT_ROWS):
        o_ref[:, j, :] = _bdot(g_ref[j], _stack_cos_sin(z_ref[j]))


def _fourier_two_stage(xcs):
    b, s, _ = xcs.shape
    n = int(round(np.sqrt(s)))
    assert n * n == s and n % FFT_ROWS == 0
    idx = np.arange(n)
    ang = 2.0 * np.pi * ((idx[:, None] * idx[None, :]) % n) / n
    fc, fs = np.cos(ang), np.sin(ang)
    f2 = jnp.asarray(np.block([[fc, -fs], [fs, fc]]), F32).astype(BF16)
    z = pl.pallas_call(
        _fourier_stage1_kernel,
        grid=(b, n // FFT_ROWS),
        in_specs=[pl.BlockSpec((None, n, FFT_ROWS, D_FOLD), lambda i, j: (i, 0, j, 0)),
                  _const_spec((2 * n, 2 * n))],
        out_specs=pl.BlockSpec((None, n, FFT_ROWS, D_FOLD), lambda i, j: (i, 0, j, 0)),
        out_shape=jax.ShapeDtypeStruct((b, n, n, D_FOLD), F32),
        compiler_params=_params(("arbitrary", "arbitrary")),
        name="fourier_stage1",
    )(xcs.reshape(b, n, n, D_FOLD), f2)
    bb, aa, s1 = idx[:, None, None], idx[None, :, None], idx[None, None, :]
    ang2 = 2.0 * np.pi * ((s1 * (n * aa + bb)) % s) / s
    g2 = np.concatenate([np.cos(ang2), -np.sin(ang2)], axis=2) / np.sqrt(s * FOURIER_GROUP)
    out = pl.pallas_call(
        _fourier_stage2_kernel,
        grid=(b, n // FFT_ROWS),
        in_specs=[pl.BlockSpec((None, FFT_ROWS, n, D_FOLD), lambda i, j: (i, j, 0, 0)),
                  pl.BlockSpec((FFT_ROWS, n, 2 * n), lambda i, j: (j, 0, 0))],
        out_specs=pl.BlockSpec((None, n, FFT_ROWS, D_FOURIER), lambda i, j: (i, 0, j, 0)),
        out_shape=jax.ShapeDtypeStruct((b, n, n, D_FOURIER), F32),
        compiler_params=_params(("arbitrary", "arbitrary")),
        name="fourier_stage2",
    )(z, jnp.asarray(g2, F32).astype(BF16))
    return out.reshape(b, s, D_FOURIER)


def _shift_rows(z, up, down, mu):
    lane = lax.broadcasted_iota(jnp.int32, (1, z.shape[1]), 1) % 4
    return z + jnp.where(lane < 2, 0.0, mu) * (jnp.where(lane == 2, up, down) - z)


def _stack_masked(x, head0):
    return jnp.concatenate([jnp.where(head0, x, 0.0), jnp.where(head0, 0.0, x)], axis=0)


NT = (((1,), (1,)), ((), ()))
CUM_PIECES = 2
CHUNKS_PER_STEP = 4


def _split(x, pieces):
    out = []
    for i in range(pieces):
        hi = x.astype(BF16)
        out.append(hi)
        if i + 1 < pieces:
            x = x - hi.astype(F32)
    return out


def _bdot(a, b, dims=None):
    if dims is None:
        return jnp.dot(a, b, preferred_element_type=F32)
    return lax.dot_general(a, b, dims, preferred_element_type=F32)


HEADS_PER_SUM = 4


def _head_sums(x, ones_ref):
    w = HEADS_PER_SUM * HEAD_SIZE
    return jnp.concatenate([_bdot(x[:, i:i + w].astype(BF16), ones_ref[...]) for i in range(0, D_RWKV, w)], axis=1)


def _scan_kernel(*refs, grid, dirs, other, n_tiles, tm, seqs):
    final = other or len(dirs) == 2
    it = iter(refs)
    z_ref = next(it)
    zp_ref = next(it) if grid else None
    zn_ref = next(it) if grid else None
    h0_refs = {d: next(it) for d in dirs}
    if other:
        yo_ref, bo_ref = next(it), next(it)
    mu_ref, kk_ref, ka_ref, rk_ref, ones_ref = (next(it) for _ in range(5))
    lora = {d: tuple(next(it) for _ in range(4)) for d in dirs}
    if final:
        lnw_ref, lnb_ref, g2_ref = next(it), next(it), next(it)
    y_ref = next(it)
    b_ref = None if final else next(it)
    hout_refs = {d: next(it) for d in dirs}
    r_s, v_s, al_s, g_s = (next(it) for _ in range(4))
    per_dir = {d: tuple(next(it) for _ in range(7)) for d in dirs}

    step = pl.program_id(1)
    tile = (n_tiles - 1 - step) if dirs == (1,) else step

    @pl.when(step == 0)
    def _():
        for d in dirs:
            per_dir[d][4][...] = h0_refs[d][...]

    def neighbour_rows(lo, hi):
        parts = []
        if lo < 0:
            parts.append(jnp.where(tile == 0, 0.0, zp_ref[GRID_W + lo:, :]))
        parts.append(z_ref[max(lo, 0):min(hi, tm), :])
        if hi > tm:
            parts.append(jnp.where(tile == n_tiles - 1, 0.0, zn_ref[:hi - tm, :]))
        return parts[0] if len(parts) == 1 else jnp.concatenate(parts, axis=0)

    def prepare(rows):
        z = z_ref[rows, :]
        if grid:
            z = _shift_rows(z, neighbour_rows(rows.start - GRID_W, rows.stop - GRID_W),
                            neighbour_rows(rows.start + GRID_W, rows.stop + GRID_W), mu_ref[...])
        r = z[:, 0:D_RWKV]
        k = z[:, D_RWKV:2 * D_RWKV]
        v = z[:, 2 * D_RWKV:3 * D_RWKV]
        kk = k * kk_ref[...]
        kk = kk * lax.rsqrt(jnp.maximum(_head_sums(kk * kk, ones_ref), 1e-24))
        r_s[rows, :] = r
        v_s[rows, :] = v
        al_s[rows, :] = -kk
        bonus = bo_ref[rows, :] if other else None
        for d in dirs:
            w0_ref, a0_ref, w2_ref, a2_ref = lora[d]
            be_s, kd_s, lw_s = per_dir[d][:3]
            w_raw = w0_ref[...] + _bdot(jnp.tanh(z[:, OFF_WD:OFF_AD]).astype(BF16), w2_ref[...])
            a = _sigmoid(a0_ref[...] + _bdot(z[:, OFF_AD:OFF_GD].astype(BF16), a2_ref[...]))
            kd = k * (1.0 + (a - 1.0) * ka_ref[...])
            bonus_d = _head_sums(r * kd * rk_ref[...], ones_ref) * v
            bonus = bonus_d if bonus is None else bonus + bonus_d
            be_s[rows, :] = a * kk
            kd_s[rows, :] = kd
            lw_s[rows, :] = -DECAY_SCALE * _sigmoid(w_raw)
        if final:
            y_ref[rows, :] = bonus
            g_s[rows, :] = _bdot(_sigmoid(z[:, OFF_GD:]).astype(BF16), g2_ref[...])
        else:
            b_ref[rows, :] = bonus

    def finish(rows):
        y_sum = yo_ref[rows, :] if other else None
        for d in dirs:
            y_d = per_dir[d][3][rows, :]
            y_sum = y_d if y_sum is None else y_sum + y_d
        if final:
            mean = _head_sums(y_sum, ones_ref) * (1.0 / HEAD_SIZE)
            cen = y_sum - mean
            var = _head_sums(cen * cen, ones_ref) * (1.0 / HEAD_SIZE)
            y_gn = cen * lax.rsqrt(var + GN_EPS) * lnw_ref[...] + lnb_ref[...]
            y_ref[rows, :] = (y_gn + y_ref[rows, :]) * g_s[rows, :]
        else:
            y_ref[rows, :] = y_sum

    n_chunks = tm // CHUNK
    per_seq = n_chunks // seqs
    assert per_seq % CHUNKS_PER_STEP == 0, "a group of chunks must not straddle two sequences"
    ri = lax.broadcasted_iota(jnp.int32, (CHUNK, CHUNK), 0)
    ci = lax.broadcasted_iota(jnp.int32, (CHUNK, CHUNK), 1)
    rj = lax.broadcasted_iota(jnp.int32, (CHUNK, PAIR), 0)
    cj = lax.broadcasted_iota(jnp.int32, (CHUNK, PAIR), 1) % CHUNK
    eye = rj == cj
    tri = {0: (ci <= ri).astype(BF16), 1: (ci >= ri).astype(BF16)}
    strict = {0: cj < rj, 1: cj > rj}
    incl = {0: cj <= rj, 1: cj >= rj}
    last = {0: CHUNK - 1, 1: 0}
    head0 = lax.broadcasted_iota(jnp.int32, (1, PAIR), 1) < HEAD_SIZE
    blockdiag = lambda x: _stack_masked(x, head0)

    def head_transpose(x):
        t = blockdiag(x).T
        return t[:CHUNK] + t[CHUNK:]

    same_block = lambda n: (rj & -n) == (cj & -n)
    levels = [2 ** i for i in range(1, int(np.log2(CHUNK)))]
    base_mask = same_block(2)
    pair_masks = {n: same_block(2 * n) & ~same_block(n) for n in levels}

    def chunk_rows(c, d):
        seq, k = divmod(c, per_seq)
        off = (seq * per_seq + ((per_seq - 1 - k) if d == 1 else k)) * CHUNK
        return slice(off, off + CHUNK)

    def local_phase(chunks, d):
        be_s, kd_s, lw_s, _, _, qm_s, yn_s = per_dir[d]
        units = []
        for c in chunks:
            rows = chunk_rows(c, d)
            lw = lw_s[rows, :]
            cum = sum(_bdot(tri[d], piece) for piece in _split(lw, CUM_PIECES))
            tot = cum[last[d]:last[d] + 1, :]
            g_inv = jnp.exp(-cum)
            g_tot = jnp.exp(tot)
            g_end = g_tot * g_inv
            abar = al_s[rows, :] * jnp.exp(cum - lw)
            rbar = r_s[rows, :] * jnp.exp(cum)
            be = be_s[rows, :]
            kdc = kd_s[rows, :]
            btil, ktil, bhat, khat = be * g_inv, kdc * g_inv, be * g_end, kdc * g_end
            vc = v_s[rows, :]
            for p in range(N_PAIRS):
                lanes = slice(p * PAIR, (p + 1) * PAIR)
                cut = lambda x: x[:, lanes]
                units.append(dict(c=c, p=p, a=cut(abar).astype(BF16), r=cut(rbar), v=cut(vc).astype(BF16),
                                  bt=cut(btil).astype(BF16), kt=cut(ktil).astype(BF16),
                                  bh=cut(bhat), kh=cut(khat), g_tot=cut(g_tot)))
        for u in units:
            ar = jnp.concatenate([u["a"], u["r"].astype(BF16)], axis=0)
            u["gb"] = _bdot(ar, blockdiag(u["bt"]), NT)
            u["gk"] = _bdot(ar, blockdiag(u["kt"]), NT)
        for u in units:
            gb, gk = u.pop("gb"), u.pop("gk")
            u["l_ab"] = jnp.where(strict[d], gb[:CHUNK], 0.0)
            u["l_ak"] = jnp.where(strict[d], gk[:CHUNK], 0.0).astype(BF16)
            u["l_rbk"] = jnp.concatenate([jnp.where(incl[d], gb[CHUNK:], 0.0),
                                          jnp.where(incl[d], gk[CHUNK:], 0.0)], axis=1).astype(BF16)
            u["t"] = jnp.where(eye, 1.0, jnp.where(base_mask, u["l_ab"], 0.0))
        for n in levels:
            for u in units:
                e = jnp.where(pair_masks[n], u["l_ab"], 0.0).astype(BF16)
                u["et"] = _bdot(e, blockdiag(u["t"].astype(BF16)))
            for u in units:
                u["t"] = u["t"] + _bdot(u["t"].astype(BF16), blockdiag(u["et"].astype(BF16)))
        for u in units:
            u["x1"] = _bdot(u["l_ak"], blockdiag(u["v"]))
        for u in units:
            rhs = jnp.concatenate([blockdiag(u["a"]), blockdiag(u["x1"].astype(BF16))], axis=1)
            u["pu"] = _bdot(u["t"].astype(BF16), rhs)
        for u in units:
            pu = u["pu"].astype(BF16)
            v_bd = blockdiag(u["v"])
            rhs = jnp.concatenate(
                [jnp.concatenate([blockdiag(pu[:, :PAIR]), blockdiag(pu[:, PAIR:])], axis=1),
                 jnp.concatenate([jnp.zeros_like(v_bd), v_bd], axis=1)], axis=0)
            bkh_t = jnp.concatenate([head_transpose(u["bh"]), head_transpose(u["kh"])], axis=1)
            o = _bdot(jnp.concatenate([u["l_rbk"], bkh_t.astype(BF16)], axis=0), rhs)
            q = u["r"] + o[:CHUNK, :PAIR]
            m = jnp.where(eye, u["g_tot"], 0.0) + o[CHUNK:, :PAIR]
            qm_s[u["c"], u["p"]] = jnp.concatenate([q, m], axis=0).astype(BF16)
            yn_s[u["c"], u["p"]] = o[:, PAIR:]

    def state_phase(chunks, d):
        _, _, _, y_s, h_s, qm_s, yn_s = per_dir[d]
        for c in chunks:
            seq = c // per_seq
            for p in range(N_PAIRS):
                oh = _bdot(qm_s[c, p], blockdiag(h_s[seq, p].astype(BF16))) + yn_s[c, p]
                y_s[chunk_rows(c, d), p * PAIR:(p + 1) * PAIR] = oh[:CHUNK]
                h_s[seq, p] = oh[CHUNK:]

    spans = [list(range(g, g + CHUNKS_PER_STEP)) for g in range(0, n_chunks, CHUNKS_PER_STEP)]
    groups = [(chunks, d) for chunks in spans for d in dirs]

    def group_rows(chunks, d):
        spans = [chunk_rows(c, d) for c in chunks]
        return slice(min(r.start for r in spans), max(r.stop for r in spans))

    for i, (chunks, d) in enumerate(groups):
        if d == dirs[0]:
            prepare(group_rows(chunks, d))
        local_phase(chunks, d)
        if i > 0:
            prev_chunks, prev_d = groups[i - 1]
            state_phase(prev_chunks, prev_d)
            if prev_d == dirs[-1]:
                finish(group_rows(prev_chunks, prev_d))
    state_phase(*groups[-1])
    finish(group_rows(*groups[-1]))

    @pl.when(step == n_tiles - 1)
    def _():
        for d in dirs:
            hout_refs[d][...] = per_dir[d][4][...]


def _scan(z, h0, prm, dirs, grid, other=None):
    n_seq, seq_len, _ = z.shape
    tm = SCAN_TILE
    seqs = max(tm // seq_len, 1)
    assert (seq_len % tm == 0) if seqs == 1 else (tm == seqs * seq_len and n_seq % seqs == 0 and not grid)
    b, s = n_seq // seqs, seq_len * seqs
    n_tiles = s // tm
    assert len(dirs) == 1 or n_tiles == 1, "both directions in one call need the whole sequence in a tile"
    final = other is not None or len(dirs) == 2
    fold = lambda a: a.reshape((b, seqs * a.shape[1]) + a.shape[2:])
    tile_of = (lambda j: n_tiles - 1 - j) if dirs == (1,) else (lambda j: j)
    tok = lambda w: pl.BlockSpec((None, tm, w), lambda i, j: (i, tile_of(j), 0))
    in_specs, args = [tok(D_Z)], [fold(z)]
    if grid:
        halo_per_tile, n_halo = tm // GRID_W, s // GRID_W
        in_specs += [pl.BlockSpec((None, GRID_W, D_Z),
                                  lambda i, j: (i, jnp.maximum(tile_of(j) * halo_per_tile - 1, 0), 0)),
                     pl.BlockSpec((None, GRID_W, D_Z),
                                  lambda i, j: (i, jnp.minimum((tile_of(j) + 1) * halo_per_tile, n_halo - 1), 0))]
        args += [args[0], args[0]]
    state_spec = pl.BlockSpec((None, seqs, N_PAIRS, HEAD_SIZE, PAIR), lambda i, j: (i, 0, 0, 0, 0))
    state_shape = jax.ShapeDtypeStruct((b, seqs, N_PAIRS, HEAD_SIZE, PAIR), F32)
    for d in dirs:
        in_specs.append(state_spec)
        args.append(h0[d].reshape(state_shape.shape))
    if other is not None:
        in_specs += [tok(D_RWKV), tok(D_RWKV)]
        args += [fold(a) for a in other]
    row = lambda x: x.reshape(1, -1)
    lora_pad = lambda w, d: jnp.zeros((2 * w.shape[1], D_RWKV), BF16).at[d * w.shape[1]:(d + 1) * w.shape[1]].set(
        w[d].astype(BF16))
    small = [row(prm["mu"]), row(prm["k_k"]), row(prm["k_a"]), row(prm["r_k"]),
             jnp.asarray(np.kron(np.eye(HEADS_PER_SUM), np.ones((HEAD_SIZE, HEAD_SIZE))), BF16)]
    for d in dirs:
        small += [row(prm["w0"][d]), row(prm["a0"][d]), lora_pad(prm["w2"], d), lora_pad(prm["a2"], d)]
    if final:
        small += [row(prm["ln_w"]), row(prm["ln_b"]), prm["g2"].astype(BF16)]
    in_specs += [_const_spec(x.shape) for x in small]
    args += small
    tokens = jax.ShapeDtypeStruct((b, s, D_RWKV), F32)
    out_specs, out_shape = [tok(D_RWKV)], [tokens]
    if not final:
        out_specs.append(tok(D_RWKV))
        out_shape.append(tokens)
    out_specs += [state_spec] * len(dirs)
    out_shape += [state_shape] * len(dirs)
    n_chunks = tm // CHUNK
    rows = lambda: pltpu.VMEM((tm, D_RWKV), F32)
    scratch = [rows() for _ in range(4)]
    for d in dirs:
        scratch += [rows() for _ in range(4)] + [pltpu.VMEM((seqs, N_PAIRS, CHUNK, PAIR), F32),
                                                  pltpu.VMEM((n_chunks, N_PAIRS, 2 * CHUNK, PAIR), BF16),
                                                  pltpu.VMEM((n_chunks, N_PAIRS, 2 * CHUNK, PAIR), F32)]
    outs = pl.pallas_call(
        functools.partial(_scan_kernel, grid=grid, dirs=dirs, other=other is not None, n_tiles=n_tiles, tm=tm,
                          seqs=seqs),
        grid=(b, n_tiles),
        in_specs=in_specs,
        out_specs=out_specs,
        out_shape=out_shape,
        scratch_shapes=scratch,
        compiler_params=_params(("arbitrary", "arbitrary")),
        name="scan_" + "".join("fb"[d] for d in dirs),
    )(*args)
    unfold = lambda a: a.reshape((n_seq, seq_len) + a.shape[2:])
    states = {d: a.reshape((n_seq,) + a.shape[2:]) for d, a in zip(dirs, outs[-len(dirs):])}
    return tuple(unfold(a) for a in outs[:-len(dirs)]) + (states,)


def _pack_state(s):
    b = s.shape[0]
    h = jnp.swapaxes(s, -1, -2).reshape(b, N_PAIRS, 2, HEAD_SIZE, HEAD_SIZE)
    return jnp.transpose(h, (0, 1, 3, 2, 4)).reshape(b, N_PAIRS, HEAD_SIZE, PAIR)


def _unpack_state(hp):
    b = hp.shape[0]
    h = jnp.transpose(hp.reshape(b, N_PAIRS, HEAD_SIZE, 2, HEAD_SIZE), (0, 1, 3, 2, 4))
    return jnp.swapaxes(h.reshape(b, N_RWKV_HEADS, HEAD_SIZE, HEAD_SIZE), -1, -2)


def _mixer_heads(z, s0_f, s0_b, prm, grid):
    h0 = {0: _pack_state(s0_f), 1: _pack_state(s0_b)}
    if z.shape[1] <= SCAN_TILE:
        y, h = _scan(z, h0, prm, (0, 1), grid)
    else:
        y_b, bonus_b, h_b = _scan(z, h0, prm, (1,), grid)
        y, h_f = _scan(z, h0, prm, (0,), grid, other=(y_b, bonus_b))
        h = {**h_f, **h_b}
    return y, _unpack_state(h[0]), _unpack_state(h[1])


def _path(x, mod, mod_per_batch, s0_f, s0_b, grid, w, prm):
    b, s, _ = x.shape
    rows = (b, s) if mod_per_batch else (1, b * s)
    as_rows = lambda a: a.reshape(rows + a.shape[2:])
    as_seqs = lambda a: a.reshape((b, s) + a.shape[2:])
    x1, u, z = _front(as_rows(x), mod, mod_per_batch, w["norm_g"], w["gate"], w["up"], w["down"], w["w_fold"],
                      w["w_z"], prm["mu"], TOKEN_TILE, GRID_W if grid else s, grid)
    y_four = _fourier_two_stage(as_seqs(u)) if grid else _fourier_dense(as_seqs(u))
    y_rwkv, s_f, s_b = _mixer_heads(as_seqs(z), s0_f, s0_b, prm, grid)
    y = _back(x1, as_rows(y_four), as_rows(y_rwkv), mod, mod_per_batch, w["norm_g"], w["final"], w["w_out"],
              w["gate"], w["up"], w["down"], TOKEN_TILE)
    return as_seqs(y), s_f, s_b


def kernel(x_prompt, x_sample, state_fwd, state_bwd, c, c_ctx, w_mod, b_mod, norm_g, ffn_w_gate, ffn_w_up,
           ffn_w_down, w_in, shift_mu, decay_w0, decay_w2, iclr_a0, iclr_a2, gate_g2, k_k, k_a, r_k,
           ln_x_w, ln_x_b, w_out, final_norm):
    depth = w_mod.shape[0]
    assert depth == 1, "the back kernel applies the final norm, so exactly one layer is supported"
    bp = x_prompt.shape[0]
    bs = x_sample.shape[0]
    xp, xs = x_prompt, x_sample
    new_f, new_b = [], []
    for l in range(depth):
        cvec = jnp.zeros((8, D_MODEL), F32).at[:bs].set(c).at[bs].set(c_ctx)
        mod = _modulation(cvec, w_mod[l], b_mod[l]).reshape(8, N_MOD, D_MODEL)
        w = {"norm_g": norm_g[l], "final": final_norm,
             "gate": ffn_w_gate[l].astype(BF16), "up": ffn_w_up[l].astype(BF16), "down": ffn_w_down[l].astype(BF16),
             "w_fold": _fold_group_dft(w_in[l]), "w_z": w_in[l, :, D_FOURIER:].astype(BF16),
             "w_out": w_out[l].astype(BF16)}
        prm = {"mu": shift_mu[l], "w0": decay_w0[l], "w2": decay_w2[l], "a0": iclr_a0[l], "a2": iclr_a2[l],
               "g2": gate_g2[l], "k_k": k_k[l], "k_a": k_a[l], "r_k": r_k[l], "ln_w": ln_x_w[l], "ln_b": ln_x_b[l]}
        zero_state = jnp.zeros((bp, N_RWKV_HEADS, HEAD_SIZE, HEAD_SIZE), F32)
        xp, s_f, s_b = _path(xp, mod[bs:bs + 1], False, zero_state, zero_state, False, w, prm)
        new_f.append(s_f)
        new_b.append(s_b)
        xs, _, _ = _path(xs, mod[:bs], True, state_fwd[:, l], state_bwd[:, l], True, w, prm)
    return xp, xs, jnp.stack(new_f, axis=1), jnp.stack(new_b, axis=1)
```

```python
import functools

import numpy as np
import jax
import jax.numpy as jnp
from jax import lax
from jax.experimental import pallas as pl
from jax.experimental.pallas import tpu as pltpu

D_MODEL = 1024
GRID_W = 64
D_FOURIER = 512
N_FOURIER_GROUPS = 8
FOURIER_GROUP = D_FOURIER // N_FOURIER_GROUPS
D_RWKV = D_MODEL - D_FOURIER
HEAD_SIZE = 64
N_RWKV_HEADS = D_RWKV // HEAD_SIZE
N_PAIRS = N_RWKV_HEADS // 2
PAIR = 2 * HEAD_SIZE
DECAY_LORA = 64
ICLR_LORA = 64
GATE_LORA = 128
D_Z = 3 * D_RWKV + 2 * DECAY_LORA + 2 * ICLR_LORA + GATE_LORA
D_FOLD = 2 * D_FOURIER
D_FF = 2816
N_MOD = 9
RMS_EPS = 1e-6
GN_EPS = 64e-5
CHUNK = 64
TOKEN_TILE = 512
SCAN_TILE = 1024
SHORT_SEQS_PER_TILE = 2
OFF_WD = 3 * D_RWKV
OFF_AD = OFF_WD + 2 * DECAY_LORA
OFF_GD = OFF_AD + 2 * ICLR_LORA
DECAY_SCALE = float(np.exp(-0.5))
V7X_VMEM_LIMIT = 56 * 1024 * 1024

F32 = jnp.float32
BF16 = jnp.bfloat16


def _dot32(a, b):
    a_hi = a.astype(BF16)
    b_hi = b.astype(BF16)
    a_lo = (a - a_hi.astype(F32)).astype(BF16)
    b_lo = (b - b_hi.astype(F32)).astype(BF16)
    dot = lambda x, y: jnp.dot(x, y, preferred_element_type=F32)
    return dot(a_hi, b_hi) + (dot(a_hi, b_lo) + dot(a_lo, b_hi))


def _sigmoid(x):
    return 1.0 / (1.0 + jnp.exp(-x))


def _norm_mod(x, g, shift, scale):
    ms = jnp.mean(x * x, axis=-1, keepdims=True)
    return x * lax.rsqrt(ms + RMS_EPS) * g * (1.0 + scale) + shift


def _swiglu(h, wg_ref, wu_ref, wd_ref):
    hb = h.astype(BF16)
    gate = jnp.dot(hb, wg_ref[...], preferred_element_type=F32)
    up = jnp.dot(hb, wu_ref[...], preferred_element_type=F32)
    act = gate * _sigmoid(gate) * up
    return jnp.dot(act.astype(BF16), wd_ref[...], preferred_element_type=F32)


def _const_spec(shape):
    nd = len(shape)
    return pl.BlockSpec(shape, lambda *_: (0,) * nd, pipeline_mode=pl.Buffered(1))


def _ffn_spec(shape, which):
    return pl.BlockSpec((None,) + shape, lambda *_: (which, 0, 0), pipeline_mode=pl.Buffered(1))


def _params(sem):
    return pltpu.CompilerParams(dimension_semantics=sem, vmem_limit_bytes=V7X_VMEM_LIMIT)


def _mod_kernel(c_ref, w_ref, b_ref, o_ref):
    c = c_ref[...]
    o_ref[...] = _dot32(c * _sigmoid(c), w_ref[...]) + b_ref[...]


def _modulation(cvec, w_mod, b_mod):
    n = w_mod.shape[1]
    tn = n // 8
    return pl.pallas_call(
        _mod_kernel,
        grid=(n // tn,),
        in_specs=[pl.BlockSpec((8, D_MODEL), lambda j: (0, 0)),
                  pl.BlockSpec((D_MODEL, tn), lambda j: (0, j)),
                  pl.BlockSpec((1, tn), lambda j: (0, j))],
        out_specs=pl.BlockSpec((8, tn), lambda j: (0, j)),
        out_shape=jax.ShapeDtypeStruct((8, n), F32),
        compiler_params=_params(("arbitrary",)),
        name="modulation",
    )(cvec, w_mod, b_mod.reshape(1, n))


def _shift_adjacent(z, mu, period, grid):
    tm = z.shape[0]
    col = lax.broadcasted_iota(jnp.int32, (tm, 1), 0) % period
    lane = lax.broadcasted_iota(jnp.int32, (1, z.shape[1]), 1) % 4
    back1 = jnp.where(col == 0, 0.0, pltpu.roll(z, 1, 0))
    fwd1 = jnp.where(col == period - 1, 0.0, pltpu.roll(z, tm - 1, 0))
    if grid:
        return jnp.where(lane < 2, z + mu * (jnp.where(lane == 0, back1, fwd1) - z), z)
    return z + mu * (jnp.where(lane % 2 == 0, back1, fwd1) - z)


def _front_kernel(x_ref, mod_ref, g_ref, wg_ref, wu_ref, wd_ref, wfold_ref, wz_ref, mu_ref,
                  x1_ref, u_ref, z_ref, *, period, grid):
    x = x_ref[...]
    h = _norm_mod(x, g_ref[0:1, :], mod_ref[0:1, :], mod_ref[1:2, :])
    x1 = x + 0.5 * mod_ref[2:3, :] * _swiglu(h, wg_ref, wu_ref, wd_ref)
    x1_ref[...] = x1
    h2 = _norm_mod(x1, g_ref[1:2, :], mod_ref[3:4, :], mod_ref[4:5, :]).astype(BF16)
    u_ref[...] = jnp.dot(h2, wfold_ref[...], preferred_element_type=F32)
    z_ref[...] = _shift_adjacent(jnp.dot(h2, wz_ref[...], preferred_element_type=F32), mu_ref[...], period, grid)


def _front(x, mod, mod_per_batch, norm_g, wg, wu, wd, w_fold, w_z, mu, tm, period, grid):
    b, s, _ = x.shape
    nt = s // tm
    assert tm % period == 0 and s % tm == 0
    tok = lambda w: pl.BlockSpec((None, tm, w), lambda i, j: (i, j, 0))
    mod_map = (lambda i, j: (i, 0, 0)) if mod_per_batch else (lambda i, j: (0, 0, 0))
    return pl.pallas_call(
        functools.partial(_front_kernel, period=period, grid=grid),
        grid=(b, nt),
        in_specs=[tok(D_MODEL),
                  pl.BlockSpec((None, N_MOD, D_MODEL), mod_map),
                  _const_spec((3, D_MODEL)),
                  _ffn_spec((D_MODEL, D_FF), 0), _ffn_spec((D_MODEL, D_FF), 0),
                  _ffn_spec((D_FF, D_MODEL), 0), _const_spec((D_MODEL, D_FOLD)), _const_spec((D_MODEL, D_Z)),
                  _const_spec((1, D_Z))],
        out_specs=[tok(D_MODEL), tok(D_FOLD), tok(D_Z)],
        out_shape=[jax.ShapeDtypeStruct((b, s, D_MODEL), F32),
                   jax.ShapeDtypeStruct((b, s, D_FOLD), F32),
                   jax.ShapeDtypeStruct((b, s, D_Z), F32)],
        compiler_params=_params(("arbitrary", "arbitrary")),
        name="front",
    )(x, mod, norm_g, wg, wu, wd, w_fold, w_z, mu.reshape(1, D_Z))


def _back_kernel(x1_ref, yf_ref, yr_ref, mod_ref, g_ref, fin_ref, wo_ref, wg_ref, wu_ref, wd_ref, o_ref):
    mixed = (jnp.dot(yf_ref[...].astype(BF16), wo_ref[0:D_FOURIER, :], preferred_element_type=F32)
             + jnp.dot(yr_ref[...].astype(BF16), wo_ref[D_FOURIER:, :], preferred_element_type=F32))
    x2 = x1_ref[...] + mod_ref[5:6, :] * mixed
    h = _norm_mod(x2, g_ref[2:3, :], mod_ref[6:7, :], mod_ref[7:8, :])
    x3 = x2 + 0.5 * mod_ref[8:9, :] * _swiglu(h, wg_ref, wu_ref, wd_ref)
    ms = jnp.mean(x3 * x3, axis=-1, keepdims=True)
    o_ref[...] = x3 * lax.rsqrt(ms + RMS_EPS) * fin_ref[...]


def _back(x1, yf, yr, mod, mod_per_batch, norm_g, final_norm, w_out, wg, wu, wd, tm):
    b, s, _ = x1.shape
    nt = s // tm
    tok = lambda w: pl.BlockSpec((None, tm, w), lambda i, j: (i, j, 0))
    mod_map = (lambda i, j: (i, 0, 0)) if mod_per_batch else (lambda i, j: (0, 0, 0))
    return pl.pallas_call(
        _back_kernel,
        grid=(b, nt),
        in_specs=[tok(D_MODEL), tok(D_FOURIER), tok(D_RWKV),
                  pl.BlockSpec((None, N_MOD, D_MODEL), mod_map),
                  _const_spec((3, D_MODEL)), _const_spec((1, D_MODEL)),
                  _const_spec((D_MODEL, D_MODEL)),
                  _ffn_spec((D_MODEL, D_FF), 1), _ffn_spec((D_MODEL, D_FF), 1),
                  _ffn_spec((D_FF, D_MODEL), 1)],
        out_specs=tok(D_MODEL),
        out_shape=jax.ShapeDtypeStruct((b, s, D_MODEL), F32),
        compiler_params=_params(("arbitrary", "arbitrary")),
        name="back",
    )(x1, yf, yr, mod, norm_g, final_norm.reshape(1, D_MODEL), w_out, wg, wu, wd)


def _fold_kernel(w_ref, cs_ref, o_ref):
    o_ref[...] = _dot32(w_ref[...], cs_ref[...]).astype(o_ref.dtype)


def _fold_group_dft(w_in):
    q = np.arange(FOURIER_GROUP)
    ang = 2.0 * np.pi * ((q[:, None] * q[None, :]) % FOURIER_GROUP) / FOURIER_GROUP
    eye = np.eye(N_FOURIER_GROUPS)
    cs = jnp.asarray(np.concatenate([np.kron(eye, np.cos(ang)), np.kron(eye, np.sin(ang))], axis=1), F32)
    return pl.pallas_call(
        _fold_kernel,
        grid=(1,),
        in_specs=[pl.BlockSpec((D_MODEL, D_FOURIER), lambda i: (0, 0)),
                  pl.BlockSpec((D_FOURIER, D_FOLD), lambda i: (0, 0))],
        out_specs=pl.BlockSpec((D_MODEL, D_FOLD), lambda i: (0, 0)),
        out_shape=jax.ShapeDtypeStruct((D_MODEL, D_FOLD), BF16),
        compiler_params=_params(("arbitrary",)),
        name="fold_group_dft",
    )(w_in, cs)


def _stack_cos_sin(x):
    return jnp.concatenate([x[:, :D_FOURIER], x[:, D_FOURIER:]], axis=0).astype(BF16)


def _fourier_dense_kernel(x_ref, p_ref, o_ref):
    o_ref[...] = _bdot(p_ref[...], _stack_cos_sin(x_ref[...]))


def _fourier_dense(xcs):
    b, s, _ = xcs.shape
    pos = np.arange(s)
    ang = 2.0 * np.pi * ((pos[:, None] * pos[None, :]) % s) / s
    table = np.concatenate([np.cos(ang), -np.sin(ang)], axis=1) / np.sqrt(s * FOURIER_GROUP)
    return pl.pallas_call(
        _fourier_dense_kernel,
        grid=(b,),
        in_specs=[pl.BlockSpec((None, s, D_FOLD), lambda i: (i, 0, 0)), _const_spec((s, 2 * s))],
        out_specs=pl.BlockSpec((None, s, D_FOURIER), lambda i: (i, 0, 0)),
        out_shape=jax.ShapeDtypeStruct((b, s, D_FOURIER), F32),
        compiler_params=_params(("arbitrary",)),
        name="fourier_dense",
    )(xcs, jnp.asarray(table, F32).astype(BF16))


FFT_ROWS = 16


def _fourier_stage1_kernel(x_ref, f_ref, z_ref):
    for j in range(FFT_ROWS):
        z = _bdot(f_ref[...], _stack_cos_sin(x_ref[:, j, :]))
        n = z.shape[0] // 2
        z_ref[:, j, :] = jnp.concatenate([z[:n], z[n:]], axis=1)


def _fourier_stage2_kernel(z_ref, g_ref, o_ref):
    for j in range(FFT_ROWS):
        o_ref[:, j, :] = _bdot(g_ref[j], _stack_cos_sin(z_ref[j]))


def _fourier_two_stage(xcs):
    b, s, _ = xcs.shape
    n = int(round(np.sqrt(s)))
    assert n * n == s and n % FFT_ROWS == 0
    idx = np.arange(n)
    ang = 2.0 * np.pi * ((idx[:, None] * idx[None, :]) % n) / n
    fc, fs = np.cos(ang), np.sin(ang)
    f2 = jnp.asarray(np.block([[fc, -fs], [fs, fc]]), F32).astype(BF16)
    z = pl.pallas_call(
        _fourier_stage1_kernel,
        grid=(b, n // FFT_ROWS),
        in_specs=[pl.BlockSpec((None, n, FFT_ROWS, D_FOLD), lambda i, j: (i, 0, j, 0)),
                  _const_spec((2 * n, 2 * n))],
        out_specs=pl.BlockSpec((None, n, FFT_ROWS, D_FOLD), lambda i, j: (i, 0, j, 0)),
        out_shape=jax.ShapeDtypeStruct((b, n, n, D_FOLD), F32),
        compiler_params=_params(("arbitrary", "arbitrary")),
        name="fourier_stage1",
    )(xcs.reshape(b, n, n, D_FOLD), f2)
    bb, aa, s1 = idx[:, None, None], idx[None, :, None], idx[None, None, :]
    ang2 = 2.0 * np.pi * ((s1 * (n * aa + bb)) % s) / s
    g2 = np.concatenate([np.cos(ang2), -np.sin(ang2)], axis=2) / np.sqrt(s * FOURIER_GROUP)
    out = pl.pallas_call(
        _fourier_stage2_kernel,
        grid=(b, n // FFT_ROWS),
        in_specs=[pl.BlockSpec((None, FFT_ROWS, n, D_FOLD), lambda i, j: (i, j, 0, 0)),
                  pl.BlockSpec((FFT_ROWS, n, 2 * n), lambda i, j: (j, 0, 0))],
        out_specs=pl.BlockSpec((None, n, FFT_ROWS, D_FOURIER), lambda i, j: (i, 0, j, 0)),
        out_shape=jax.ShapeDtypeStruct((b, n, n, D_FOURIER), F32),
        compiler_params=_params(("arbitrary", "arbitrary")),
        name="fourier_stage2",
    )(z, jnp.asarray(g2, F32).astype(BF16))
    return out.reshape(b, s, D_FOURIER)


def _shift_rows(z, up, down, mu):
    lane = lax.broadcasted_iota(jnp.int32, (1, z.shape[1]), 1) % 4
    return z + jnp.where(lane < 2, 0.0, mu) * (jnp.where(lane == 2, up, down) - z)


def _stack_masked(x, head0):
    return jnp.concatenate([jnp.where(head0, x, 0.0), jnp.where(head0, 0.0, x)], axis=0)


NT = (((1,), (1,)), ((), ()))
CUM_PIECES = 2
CHUNKS_PER_STEP = 4


def _split(x, pieces):
    out = []
    for i in range(pieces):
        hi = x.astype(BF16)
        out.append(hi)
        if i + 1 < pieces:
            x = x - hi.astype(F32)
    return out


def _bdot(a, b, dims=None):
    if dims is None:
        return jnp.dot(a, b, preferred_element_type=F32)
    return lax.dot_general(a, b, dims, preferred_element_type=F32)


HEADS_PER_SUM = 4


def _head_sums(x, ones_ref):
    w = HEADS_PER_SUM * HEAD_SIZE
    return jnp.concatenate([_bdot(x[:, i:i + w].astype(BF16), ones_ref[...]) for i in range(0, D_RWKV, w)], axis=1)


def _scan_kernel(*refs, grid, dirs, other, n_tiles, tm, seqs):
    final = other or len(dirs) == 2
    it = iter(refs)
    z_ref = next(it)
    zp_ref = next(it) if grid else None
    zn_ref = next(it) if grid else None
    h0_refs = {d: next(it) for d in dirs}
    if other:
        yo_ref, bo_ref = next(it), next(it)
    mu_ref, kk_ref, ka_ref, rk_ref, ones_ref = (next(it) for _ in range(5))
    lora = {d: tuple(next(it) for _ in range(4)) for d in dirs}
    if final:
        lnw_ref, lnb_ref, g2_ref = next(it), next(it), next(it)
    y_ref = next(it)
    b_ref = None if final else next(it)
    hout_refs = {d: next(it) for d in dirs}
    r_s, v_s, al_s, g_s = (next(it) for _ in range(4))
    per_dir = {d: tuple(next(it) for _ in range(7)) for d in dirs}

    step = pl.program_id(1)
    tile = (n_tiles - 1 - step) if dirs == (1,) else step

    @pl.when(step == 0)
    def _():
        for d in dirs:
            per_dir[d][4][...] = h0_refs[d][...]

    def neighbour_rows(lo, hi):
        parts = []
        if lo < 0:
            parts.append(jnp.where(tile == 0, 0.0, zp_ref[GRID_W + lo:, :]))
        parts.append(z_ref[max(lo, 0):min(hi, tm), :])
        if hi > tm:
            parts.append(jnp.where(tile == n_tiles - 1, 0.0, zn_ref[:hi - tm, :]))
        return parts[0] if len(parts) == 1 else jnp.concatenate(parts, axis=0)

    def prepare(rows):
        z = z_ref[rows, :]
        if grid:
            z = _shift_rows(z, neighbour_rows(rows.start - GRID_W, rows.stop - GRID_W),
                            neighbour_rows(rows.start + GRID_W, rows.stop + GRID_W), mu_ref[...])
        r = z[:, 0:D_RWKV]
        k = z[:, D_RWKV:2 * D_RWKV]
        v = z[:, 2 * D_RWKV:3 * D_RWKV]
        kk = k * kk_ref[...]
        kk = kk * lax.rsqrt(jnp.maximum(_head_sums(kk * kk, ones_ref), 1e-24))
        r_s[rows, :] = r
        v_s[rows, :] = v
        al_s[rows, :] = -kk
        bonus = bo_ref[rows, :] if other else None
        for d in dirs:
            w0_ref, a0_ref, w2_ref, a2_ref = lora[d]
            be_s, kd_s, lw_s = per_dir[d][:3]
            w_raw = w0_ref[...] + _bdot(jnp.tanh(z[:, OFF_WD:OFF_AD]).astype(BF16), w2_ref[...])
            a = _sigmoid(a0_ref[...] + _bdot(z[:, OFF_AD:OFF_GD].astype(BF16), a2_ref[...]))
            kd = k * (1.0 + (a - 1.0) * ka_ref[...])
            bonus_d = _head_sums(r * kd * rk_ref[...], ones_ref) * v
            bonus = bonus_d if bonus is None else bonus + bonus_d
            be_s[rows, :] = a * kk
            kd_s[rows, :] = kd
            lw_s[rows, :] = -DECAY_SCALE * _sigmoid(w_raw)
        if final:
            y_ref[rows, :] = bonus
            g_s[rows, :] = _bdot(_sigmoid(z[:, OFF_GD:]).astype(BF16), g2_ref[...])
        else:
            b_ref[rows, :] = bonus

    def finish(rows):
        y_sum = yo_ref[rows, :] if other else None
        for d in dirs:
            y_d = per_dir[d][3][rows, :]
            y_sum = y_d if y_sum is None else y_sum + y_d
        if final:
            mean = _head_sums(y_sum, ones_ref) * (1.0 / HEAD_SIZE)
            cen = y_sum - mean
            var = _head_sums(cen * cen, ones_ref) * (1.0 / HEAD_SIZE)
            y_gn = cen * lax.rsqrt(var + GN_EPS) * lnw_ref[...] + lnb_ref[...]
            y_ref[rows, :] = (y_gn + y_ref[rows, :]) * g_s[rows, :]
        else:
            y_ref[rows, :] = y_sum

    n_chunks = tm // CHUNK
    per_seq = n_chunks // seqs
    assert per_seq % CHUNKS_PER_STEP == 0, "a group of chunks must not straddle two sequences"
    ri = lax.broadcasted_iota(jnp.int32, (CHUNK, CHUNK), 0)
    ci = lax.broadcasted_iota(jnp.int32, (CHUNK, CHUNK), 1)
    rj = lax.broadcasted_iota(jnp.int32, (CHUNK, PAIR), 0)
    cj = lax.broadcasted_iota(jnp.int32, (CHUNK, PAIR), 1) % CHUNK
    eye = rj == cj
    tri = {0: (ci <= ri).astype(BF16), 1: (ci >= ri).astype(BF16)}
    strict = {0: cj < rj, 1: cj > rj}
    incl = {0: cj <= rj, 1: cj >= rj}
    last = {0: CHUNK - 1, 1: 0}
    head0 = lax.broadcasted_iota(jnp.int32, (1, PAIR), 1) < HEAD_SIZE
    blockdiag = lambda x: _stack_masked(x, head0)

    def head_transpose(x):
        t = blockdiag(x).T
        return t[:CHUNK] + t[CHUNK:]

    same_block = lambda n: (rj & -n) == (cj & -n)
    levels = [2 ** i for i in range(1, int(np.log2(CHUNK)))]
    base_mask = same_block(2)
    pair_masks = {n: same_block(2 * n) & ~same_block(n) for n in levels}

    def chunk_rows(c, d):
        seq, k = divmod(c, per_seq)
        off = (seq * per_seq + ((per_seq - 1 - k) if d == 1 else k)) * CHUNK
        return slice(off, off + CHUNK)

    def local_phase(chunks, d):
        be_s, kd_s, lw_s, _, _, qm_s, yn_s = per_dir[d]
        units = []
        for c in chunks:
            rows = chunk_rows(c, d)
            lw = lw_s[rows, :]
            cum = sum(_bdot(tri[d], piece) for piece in _split(lw, CUM_PIECES))
            tot = cum[last[d]:last[d] + 1, :]
            g_inv = jnp.exp(-cum)
            g_tot = jnp.exp(tot)
            g_end = g_tot * g_inv
            abar = al_s[rows, :] * jnp.exp(cum - lw)
            rbar = r_s[rows, :] * jnp.exp(cum)
            be = be_s[rows, :]
            kdc = kd_s[rows, :]
            btil, ktil, bhat, khat = be * g_inv, kdc * g_inv, be * g_end, kdc * g_end
            vc = v_s[rows, :]
            for p in range(N_PAIRS):
                lanes = slice(p * PAIR, (p + 1) * PAIR)
                cut = lambda x: x[:, lanes]
                units.append(dict(c=c, p=p, a=cut(abar).astype(BF16), r=cut(rbar), v=cut(vc).astype(BF16),
                                  bt=cut(btil).astype(BF16), kt=cut(ktil).astype(BF16),
                                  bh=cut(bhat), kh=cut(khat), g_tot=cut(g_tot)))
        for u in units:
            ar = jnp.concatenate([u["a"], u["r"].astype(BF16)], axis=0)
            u["gb"] = _bdot(ar, blockdiag(u["bt"]), NT)
            u["gk"] = _bdot(ar, blockdiag(u["kt"]), NT)
        for u in units:
            gb, gk = u.pop("gb"), u.pop("gk")
            u["l_ab"] = jnp.where(strict[d], gb[:CHUNK], 0.0)
            u["l_ak"] = jnp.where(strict[d], gk[:CHUNK], 0.0).astype(BF16)
            u["l_rbk"] = jnp.concatenate([jnp.where(incl[d], gb[CHUNK:], 0.0),
                                          jnp.where(incl[d], gk[CHUNK:], 0.0)], axis=1).astype(BF16)
            u["t"] = jnp.where(eye, 1.0, jnp.where(base_mask, u["l_ab"], 0.0))
        for n in levels:
            for u in units:
                e = jnp.where(pair_masks[n], u["l_ab"], 0.0).astype(BF16)
                u["et"] = _bdot(e, blockdiag(u["t"].astype(BF16)))
            for u in units:
                u["t"] = u["t"] + _bdot(u["t"].astype(BF16), blockdiag(u["et"].astype(BF16)))
        for u in units:
            u["x1"] = _bdot(u["l_ak"], blockdiag(u["v"]))
        for u in units:
            rhs = jnp.concatenate([blockdiag(u["a"]), blockdiag(u["x1"].astype(BF16))], axis=1)
            u["pu"] = _bdot(u["t"].astype(BF16), rhs)
        for u in units:
            pu = u["pu"].astype(BF16)
            v_bd = blockdiag(u["v"])
            rhs = jnp.concatenate(
                [jnp.concatenate([blockdiag(pu[:, :PAIR]), blockdiag(pu[:, PAIR:])], axis=1),
                 jnp.concatenate([jnp.zeros_like(v_bd), v_bd], axis=1)], axis=0)
            bkh_t = jnp.concatenate([head_transpose(u["bh"]), head_transpose(u["kh"])], axis=1)
            o = _bdot(jnp.concatenate([u["l_rbk"], bkh_t.astype(BF16)], axis=0), rhs)
            q = u["r"] + o[:CHUNK, :PAIR]
            m = jnp.where(eye, u["g_tot"], 0.0) + o[CHUNK:, :PAIR]
            qm_s[u["c"], u["p"]] = jnp.concatenate([q, m], axis=0).astype(BF16)
            yn_s[u["c"], u["p"]] = o[:, PAIR:]

    def state_phase(chunks, d):
        _, _, _, y_s, h_s, qm_s, yn_s = per_dir[d]
        for c in chunks:
            seq = c // per_seq
            for p in range(N_PAIRS):
                oh = _bdot(qm_s[c, p], blockdiag(h_s[seq, p].astype(BF16))) + yn_s[c, p]
                y_s[chunk_rows(c, d), p * PAIR:(p + 1) * PAIR] = oh[:CHUNK]
                h_s[seq, p] = oh[CHUNK:]

    spans = [list(range(g, g + CHUNKS_PER_STEP)) for g in range(0, n_chunks, CHUNKS_PER_STEP)]
    groups = [(chunks, d) for chunks in spans for d in dirs]

    def group_rows(chunks, d):
        spans = [chunk_rows(c, d) for c in chunks]
        return slice(min(r.start for r in spans), max(r.stop for r in spans))

    for i, (chunks, d) in enumerate(groups):
        if d == dirs[0]:
            prepare(group_rows(chunks, d))
        local_phase(chunks, d)
        if i > 0:
            prev_chunks, prev_d = groups[i - 1]
            state_phase(prev_chunks, prev_d)
            if prev_d == dirs[-1]:
                finish(group_rows(prev_chunks, prev_d))
    state_phase(*groups[-1])
    finish(group_rows(*groups[-1]))

    @pl.when(step == n_tiles - 1)
    def _():
        for d in dirs:
            hout_refs[d][...] = per_dir[d][4][...]


def _scan(z, h0, prm, dirs, grid, other=None):
    n_seq, seq_len, _ = z.shape
    tm = SCAN_TILE if seq_len >= SCAN_TILE else SHORT_SEQS_PER_TILE * seq_len
    seqs = max(tm // seq_len, 1)
    assert (seq_len % tm == 0) if seqs == 1 else (tm == seqs * seq_len and n_seq % seqs == 0 and not grid)
    b, s = n_seq // seqs, seq_len * seqs
    n_tiles = s // tm
    assert len(dirs) == 1 or n_tiles == 1, "both directions in one call need the whole sequence in a tile"
    final = other is not None or len(dirs) == 2
    fold = lambda a: a.reshape((b, seqs * a.shape[1]) + a.shape[2:])
    tile_of = (lambda j: n_tiles - 1 - j) if dirs == (1,) else (lambda j: j)
    tok = lambda w: pl.BlockSpec((None, tm, w), lambda i, j: (i, tile_of(j), 0))
    in_specs, args = [tok(D_Z)], [fold(z)]
    if grid:
        halo_per_tile, n_halo = tm // GRID_W, s // GRID_W
        in_specs += [pl.BlockSpec((None, GRID_W, D_Z),
                                  lambda i, j: (i, jnp.maximum(tile_of(j) * halo_per_tile - 1, 0), 0)),
                     pl.BlockSpec((None, GRID_W, D_Z),
                                  lambda i, j: (i, jnp.minimum((tile_of(j) + 1) * halo_per_tile, n_halo - 1), 0))]
        args += [args[0], args[0]]
    state_spec = pl.BlockSpec((None, seqs, N_PAIRS, HEAD_SIZE, PAIR), lambda i, j: (i, 0, 0, 0, 0))
    state_shape = jax.ShapeDtypeStruct((b, seqs, N_PAIRS, HEAD_SIZE, PAIR), F32)
    for d in dirs:
        in_specs.append(state_spec)
        args.append(h0[d].reshape(state_shape.shape))
    if other is not None:
        in_specs += [tok(D_RWKV), tok(D_RWKV)]
        args += [fold(a) for a in other]
    row = lambda x: x.reshape(1, -1)
    lora_pad = lambda w, d: jnp.zeros((2 * w.shape[1], D_RWKV), BF16).at[d * w.shape[1]:(d + 1) * w.shape[1]].set(
        w[d].astype(BF16))
    small = [row(prm["mu"]), row(prm["k_k"]), row(prm["k_a"]), row(prm["r_k"]),
             jnp.asarray(np.kron(np.eye(HEADS_PER_SUM), np.ones((HEAD_SIZE, HEAD_SIZE))), BF16)]
    for d in dirs:
        small += [row(prm["w0"][d]), row(prm["a0"][d]), lora_pad(prm["w2"], d), lora_pad(prm["a2"], d)]
    if final:
        small += [row(prm["ln_w"]), row(prm["ln_b"]), prm["g2"].astype(BF16)]
    in_specs += [_const_spec(x.shape) for x in small]
    args += small
    tokens = jax.ShapeDtypeStruct((b, s, D_RWKV), F32)
    out_specs, out_shape = [tok(D_RWKV)], [tokens]
    if not final:
        out_specs.append(tok(D_RWKV))
        out_shape.append(tokens)
    out_specs += [state_spec] * len(dirs)
    out_shape += [state_shape] * len(dirs)
    n_chunks = tm // CHUNK
    rows = lambda: pltpu.VMEM((tm, D_RWKV), F32)
    scratch = [rows() for _ in range(4)]
    for d in dirs:
        scratch += [rows() for _ in range(4)] + [pltpu.VMEM((seqs, N_PAIRS, CHUNK, PAIR), F32),
                                                  pltpu.VMEM((n_chunks, N_PAIRS, 2 * CHUNK, PAIR), BF16),
                                                  pltpu.VMEM((n_chunks, N_PAIRS, 2 * CHUNK, PAIR), F32)]
    outs = pl.pallas_call(
        functools.partial(_scan_kernel, grid=grid, dirs=dirs, other=other is not None, n_tiles=n_tiles, tm=tm,
                          seqs=seqs),
        grid=(b, n_tiles),
        in_specs=in_specs,
        out_specs=out_specs,
        out_shape=out_shape,
        scratch_shapes=scratch,
        compiler_params=_params(("arbitrary", "arbitrary")),
        name="scan_" + "".join("fb"[d] for d in dirs),
    )(*args)
    unfold = lambda a: a.reshape((n_seq, seq_len) + a.shape[2:])
    states = {d: a.reshape((n_seq,) + a.shape[2:]) for d, a in zip(dirs, outs[-len(dirs):])}
    return tuple(unfold(a) for a in outs[:-len(dirs)]) + (states,)


def _pack_state(s):
    b = s.shape[0]
    h = jnp.swapaxes(s, -1, -2).reshape(b, N_PAIRS, 2, HEAD_SIZE, HEAD_SIZE)
    return jnp.transpose(h, (0, 1, 3, 2, 4)).reshape(b, N_PAIRS, HEAD_SIZE, PAIR)


def _unpack_state(hp):
    b = hp.shape[0]
    h = jnp.transpose(hp.reshape(b, N_PAIRS, HEAD_SIZE, 2, HEAD_SIZE), (0, 1, 3, 2, 4))
    return jnp.swapaxes(h.reshape(b, N_RWKV_HEADS, HEAD_SIZE, HEAD_SIZE), -1, -2)


def _mixer_heads(z, s0_f, s0_b, prm, grid):
    h0 = {0: _pack_state(s0_f), 1: _pack_state(s0_b)}
    if z.shape[1] < SCAN_TILE:
        y, h = _scan(z, h0, prm, (0, 1), grid)
    else:
        y_b, bonus_b, h_b = _scan(z, h0, prm, (1,), grid)
        y, h_f = _scan(z, h0, prm, (0,), grid, other=(y_b, bonus_b))
        h = {**h_f, **h_b}
    return y, _unpack_state(h[0]), _unpack_state(h[1])


def _path(x, mod, mod_per_batch, s0_f, s0_b, grid, w, prm):
    b, s, _ = x.shape
    rows = (b, s) if mod_per_batch else (1, b * s)
    as_rows = lambda a: a.reshape(rows + a.shape[2:])
    as_seqs = lambda a: a.reshape((b, s) + a.shape[2:])
    x1, u, z = _front(as_rows(x), mod, mod_per_batch, w["norm_g"], w["gate"], w["up"], w["down"], w["w_fold"],
                      w["w_z"], prm["mu"], TOKEN_TILE, GRID_W if grid else s, grid)
    y_four = _fourier_two_stage(as_seqs(u)) if grid else _fourier_dense(as_seqs(u))
    y_rwkv, s_f, s_b = _mixer_heads(as_seqs(z), s0_f, s0_b, prm, grid)
    y = _back(x1, as_rows(y_four), as_rows(y_rwkv), mod, mod_per_batch, w["norm_g"], w["final"], w["w_out"],
              w["gate"], w["up"], w["down"], TOKEN_TILE)
    return as_seqs(y), s_f, s_b


def kernel(x_prompt, x_sample, state_fwd, state_bwd, c, c_ctx, w_mod, b_mod, norm_g, ffn_w_gate, ffn_w_up,
           ffn_w_down, w_in, shift_mu, decay_w0, decay_w2, iclr_a0, iclr_a2, gate_g2, k_k, k_a, r_k,
           ln_x_w, ln_x_b, w_out, final_norm):
    depth = w_mod.shape[0]
    assert depth == 1, "the back kernel applies the final norm, so exactly one layer is supported"
    bp = x_prompt.shape[0]
    bs = x_sample.shape[0]
    xp, xs = x_prompt, x_sample
    new_f, new_b = [], []
    for l in range(depth):
        cvec = jnp.zeros((8, D_MODEL), F32).at[:bs].set(c).at[bs].set(c_ctx)
        mod = _modulation(cvec, w_mod[l], b_mod[l]).reshape(8, N_MOD, D_MODEL)
        w = {"norm_g": norm_g[l], "final": final_norm,
             "gate": ffn_w_gate[l].astype(BF16), "up": ffn_w_up[l].astype(BF16), "down": ffn_w_down[l].astype(BF16),
             "w_fold": _fold_group_dft(w_in[l]), "w_z": w_in[l, :, D_FOURIER:].astype(BF16),
             "w_out": w_out[l].astype(BF16)}
        prm = {"mu": shift_mu[l], "w0": decay_w0[l], "w2": decay_w2[l], "a0": iclr_a0[l], "a2": iclr_a2[l],
               "g2": gate_g2[l], "k_k": k_k[l], "k_a": k_a[l], "r_k": r_k[l], "ln_w": ln_x_w[l], "ln_b": ln_x_b[l]}
        zero_state = jnp.zeros((bp, N_RWKV_HEADS, HEAD_SIZE, HEAD_SIZE), F32)
        xp, s_f, s_b = _path(xp, mod[bs:bs + 1], False, zero_state, zero_state, False, w, prm)
        new_f.append(s_f)
        new_b.append(s_b)
        xs, _, _ = _path(xs, mod[:bs], True, state_fwd[:, l], state_bwd[:, l], True, w, prm)
    return xp, xs, jnp.stack(new_f, axis=1), jnp.stack(new_b, axis=1)
```

```python
import functools

import numpy as np
import jax
import jax.numpy as jnp
from jax import lax
from jax.experimental import pallas as pl
from jax.experimental.pallas import tpu as pltpu

D_MODEL = 1024
GRID_W = 64
D_FOURIER = 512
N_FOURIER_GROUPS = 8
FOURIER_GROUP = D_FOURIER // N_FOURIER_GROUPS
D_RWKV = D_MODEL - D_FOURIER
HEAD_SIZE = 64
N_RWKV_HEADS = D_RWKV // HEAD_SIZE
N_PAIRS = N_RWKV_HEADS // 2
PAIR = 2 * HEAD_SIZE
DECAY_LORA = 64
ICLR_LORA = 64
GATE_LORA = 128
D_Z = 3 * D_RWKV + 2 * DECAY_LORA + 2 * ICLR_LORA + GATE_LORA
D_FOLD = 2 * D_FOURIER
D_FF = 2816
N_MOD = 9
RMS_EPS = 1e-6
GN_EPS = 64e-5
CHUNK = 64
TOKEN_TILE = 512
SCAN_TILE = 1024
SHORT_SEQS_PER_TILE = 2
OFF_WD = 3 * D_RWKV
OFF_AD = OFF_WD + 2 * DECAY_LORA
OFF_GD = OFF_AD + 2 * ICLR_LORA
DECAY_SCALE = float(np.exp(-0.5))
V7X_VMEM_LIMIT = 56 * 1024 * 1024

F32 = jnp.float32
BF16 = jnp.bfloat16


def _dot32(a, b):
    a_hi = a.astype(BF16)
    b_hi = b.astype(BF16)
    a_lo = (a - a_hi.astype(F32)).astype(BF16)
    b_lo = (b - b_hi.astype(F32)).astype(BF16)
    dot = lambda x, y: jnp.dot(x, y, preferred_element_type=F32)
    return dot(a_hi, b_hi) + (dot(a_hi, b_lo) + dot(a_lo, b_hi))


def _sigmoid(x):
    return 1.0 / (1.0 + jnp.exp(-x))


def _norm_mod(x, g, shift, scale):
    ms = jnp.mean(x * x, axis=-1, keepdims=True)
    return x * lax.rsqrt(ms + RMS_EPS) * g * (1.0 + scale) + shift


def _swiglu(h, wg_ref, wu_ref, wd_ref):
    hb = h.astype(BF16)
    gate = jnp.dot(hb, wg_ref[...], preferred_element_type=F32)
    up = jnp.dot(hb, wu_ref[...], preferred_element_type=F32)
    act = gate * _sigmoid(gate) * up
    return jnp.dot(act.astype(BF16), wd_ref[...], preferred_element_type=F32)


def _const_spec(shape):
    nd = len(shape)
    return pl.BlockSpec(shape, lambda *_: (0,) * nd, pipeline_mode=pl.Buffered(1))


def _ffn_spec(shape, which):
    return pl.BlockSpec((None,) + shape, lambda *_: (which, 0, 0), pipeline_mode=pl.Buffered(1))


def _params(sem):
    return pltpu.CompilerParams(dimension_semantics=sem, vmem_limit_bytes=V7X_VMEM_LIMIT)


def _mod_kernel(c_ref, w_ref, b_ref, o_ref):
    c = c_ref[...]
    o_ref[...] = _dot32(c * _sigmoid(c), w_ref[...]) + b_ref[...]


def _modulation(cvec, w_mod, b_mod):
    n = w_mod.shape[1]
    tn = n // 8
    return pl.pallas_call(
        _mod_kernel,
        grid=(n // tn,),
        in_specs=[pl.BlockSpec((8, D_MODEL), lambda j: (0, 0)),
                  pl.BlockSpec((D_MODEL, tn), lambda j: (0, j)),
                  pl.BlockSpec((1, tn), lambda j: (0, j))],
        out_specs=pl.BlockSpec((8, tn), lambda j: (0, j)),
        out_shape=jax.ShapeDtypeStruct((8, n), F32),
        compiler_params=_params(("arbitrary",)),
        name="modulation",
    )(cvec, w_mod, b_mod.reshape(1, n))


def _shift_adjacent(z, mu, period, grid):
    tm = z.shape[0]
    col = lax.broadcasted_iota(jnp.int32, (tm, 1), 0) % period
    lane = lax.broadcasted_iota(jnp.int32, (1, z.shape[1]), 1) % 4
    back1 = jnp.where(col == 0, 0.0, pltpu.roll(z, 1, 0))
    fwd1 = jnp.where(col == period - 1, 0.0, pltpu.roll(z, tm - 1, 0))
    if grid:
        return jnp.where(lane < 2, z + mu * (jnp.where(lane == 0, back1, fwd1) - z), z)
    return z + mu * (jnp.where(lane % 2 == 0, back1, fwd1) - z)


def _front_kernel(x_ref, mod_ref, g_ref, wg_ref, wu_ref, wd_ref, wfold_ref, wz_ref, mu_ref,
                  x1_ref, u_ref, z_ref, *, period, grid):
    x = x_ref[...]
    h = _norm_mod(x, g_ref[0:1, :], mod_ref[0:1, :], mod_ref[1:2, :])
    x1 = x + 0.5 * mod_ref[2:3, :] * _swiglu(h, wg_ref, wu_ref, wd_ref)
    x1_ref[...] = x1
    h2 = _norm_mod(x1, g_ref[1:2, :], mod_ref[3:4, :], mod_ref[4:5, :]).astype(BF16)
    u_ref[...] = jnp.dot(h2, wfold_ref[...], preferred_element_type=F32)
    z_ref[...] = _shift_adjacent(jnp.dot(h2, wz_ref[...], preferred_element_type=F32), mu_ref[...], period, grid)


def _front(x, mod, mod_per_batch, norm_g, wg, wu, wd, w_fold, w_z, mu, tm, period, grid):
    b, s, _ = x.shape
    nt = s // tm
    assert tm % period == 0 and s % tm == 0
    tok = lambda w: pl.BlockSpec((None, tm, w), lambda i, j: (i, j, 0))
    mod_map = (lambda i, j: (i, 0, 0)) if mod_per_batch else (lambda i, j: (0, 0, 0))
    return pl.pallas_call(
        functools.partial(_front_kernel, period=period, grid=grid),
        grid=(b, nt),
        in_specs=[tok(D_MODEL),
                  pl.BlockSpec((None, N_MOD, D_MODEL), mod_map),
                  _const_spec((3, D_MODEL)),
                  _ffn_spec((D_MODEL, D_FF), 0), _ffn_spec((D_MODEL, D_FF), 0),
                  _ffn_spec((D_FF, D_MODEL), 0), _const_spec((D_MODEL, D_FOLD)), _const_spec((D_MODEL, D_Z)),
                  _const_spec((1, D_Z))],
        out_specs=[tok(D_MODEL), tok(D_FOLD), tok(D_Z)],
        out_shape=[jax.ShapeDtypeStruct((b, s, D_MODEL), F32),
                   jax.ShapeDtypeStruct((b, s, D_FOLD), F32),
                   jax.ShapeDtypeStruct((b, s, D_Z), F32)],
        compiler_params=_params(("arbitrary", "arbitrary")),
        name="front",
    )(x, mod, norm_g, wg, wu, wd, w_fold, w_z, mu.reshape(1, D_Z))


def _back_kernel(x1_ref, yf_ref, yr_ref, mod_ref, g_ref, fin_ref, wo_ref, wg_ref, wu_ref, wd_ref, o_ref):
    mixed = (jnp.dot(yf_ref[...].astype(BF16), wo_ref[0:D_FOURIER, :], preferred_element_type=F32)
             + jnp.dot(yr_ref[...].astype(BF16), wo_ref[D_FOURIER:, :], preferred_element_type=F32))
    x2 = x1_ref[...] + mod_ref[5:6, :] * mixed
    h = _norm_mod(x2, g_ref[2:3, :], mod_ref[6:7, :], mod_ref[7:8, :])
    x3 = x2 + 0.5 * mod_ref[8:9, :] * _swiglu(h, wg_ref, wu_ref, wd_ref)
    ms = jnp.mean(x3 * x3, axis=-1, keepdims=True)
    o_ref[...] = x3 * lax.rsqrt(ms + RMS_EPS) * fin_ref[...]


def _back(x1, yf, yr, mod, mod_per_batch, norm_g, final_norm, w_out, wg, wu, wd, tm):
    b, s, _ = x1.shape
    nt = s // tm
    tok = lambda w: pl.BlockSpec((None, tm, w), lambda i, j: (i, j, 0))
    mod_map = (lambda i, j: (i, 0, 0)) if mod_per_batch else (lambda i, j: (0, 0, 0))
    return pl.pallas_call(
        _back_kernel,
        grid=(b, nt),
        in_specs=[tok(D_MODEL), tok(D_FOURIER), tok(D_RWKV),
                  pl.BlockSpec((None, N_MOD, D_MODEL), mod_map),
                  _const_spec((3, D_MODEL)), _const_spec((1, D_MODEL)),
                  _const_spec((D_MODEL, D_MODEL)),
                  _ffn_spec((D_MODEL, D_FF), 1), _ffn_spec((D_MODEL, D_FF), 1),
                  _ffn_spec((D_FF, D_MODEL), 1)],
        out_specs=tok(D_MODEL),
        out_shape=jax.ShapeDtypeStruct((b, s, D_MODEL), F32),
        compiler_params=_params(("arbitrary", "arbitrary")),
        name="back",
    )(x1, yf, yr, mod, norm_g, final_norm.reshape(1, D_MODEL), w_out, wg, wu, wd)


def _fold_kernel(w_ref, cs_ref, o_ref):
    o_ref[...] = _dot32(w_ref[...], cs_ref[...]).astype(o_ref.dtype)


def _fold_group_dft(w_in):
    q = np.arange(FOURIER_GROUP)
    ang = 2.0 * np.pi * ((q[:, None] * q[None, :]) % FOURIER_GROUP) / FOURIER_GROUP
    eye = np.eye(N_FOURIER_GROUPS)
    cs = jnp.asarray(np.concatenate([np.kron(eye, np.cos(ang)), np.kron(eye, np.sin(ang))], axis=1), F32)
    return pl.pallas_call(
        _fold_kernel,
        grid=(1,),
        in_specs=[pl.BlockSpec((D_MODEL, D_FOURIER), lambda i: (0, 0)),
                  pl.BlockSpec((D_FOURIER, D_FOLD), lambda i: (0, 0))],
        out_specs=pl.BlockSpec((D_MODEL, D_FOLD), lambda i: (0, 0)),
        out_shape=jax.ShapeDtypeStruct((D_MODEL, D_FOLD), BF16),
        compiler_params=_params(("arbitrary",)),
        name="fold_group_dft",
    )(w_in, cs)


def _stack_cos_sin(x):
    return jnp.concatenate([x[:, :D_FOURIER], x[:, D_FOURIER:]], axis=0).astype(BF16)


DENSE_DFT_SEQS = 4


def _fourier_dense_kernel(x_ref, p_ref, o_ref):
    for i in range(x_ref.shape[0]):
        o_ref[i] = _bdot(p_ref[...], _stack_cos_sin(x_ref[i]))


def _fourier_dense(xcs):
    b, s, _ = xcs.shape
    pos = np.arange(s)
    ang = 2.0 * np.pi * ((pos[:, None] * pos[None, :]) % s) / s
    table = np.concatenate([np.cos(ang), -np.sin(ang)], axis=1) / np.sqrt(s * FOURIER_GROUP)
    per_step = DENSE_DFT_SEQS if b % DENSE_DFT_SEQS == 0 else 1
    return pl.pallas_call(
        _fourier_dense_kernel,
        grid=(b // per_step,),
        in_specs=[pl.BlockSpec((per_step, s, D_FOLD), lambda i: (i, 0, 0)), _const_spec((s, 2 * s))],
        out_specs=pl.BlockSpec((per_step, s, D_FOURIER), lambda i: (i, 0, 0)),
        out_shape=jax.ShapeDtypeStruct((b, s, D_FOURIER), F32),
        compiler_params=_params(("arbitrary",)),
        name="fourier_dense",
    )(xcs, jnp.asarray(table, F32).astype(BF16))


FFT_ROWS = 16


def _fourier_stage1_kernel(x_ref, f_ref, z_ref):
    for j in range(FFT_ROWS):
        z = _bdot(f_ref[...], _stack_cos_sin(x_ref[:, j, :]))
        n = z.shape[0] // 2
        z_ref[:, j, :] = jnp.concatenate([z[:n], z[n:]], axis=1)


def _fourier_stage2_kernel(z_ref, g_ref, o_ref):
    for j in range(FFT_ROWS):
        o_ref[:, j, :] = _bdot(g_ref[j], _stack_cos_sin(z_ref[j]))


def _fourier_two_stage(xcs):
    b, s, _ = xcs.shape
    n = int(round(np.sqrt(s)))
    assert n * n == s and n % FFT_ROWS == 0
    idx = np.arange(n)
    ang = 2.0 * np.pi * ((idx[:, None] * idx[None, :]) % n) / n
    fc, fs = np.cos(ang), np.sin(ang)
    f2 = jnp.asarray(np.block([[fc, -fs], [fs, fc]]), F32).astype(BF16)
    z = pl.pallas_call(
        _fourier_stage1_kernel,
        grid=(b, n // FFT_ROWS),
        in_specs=[pl.BlockSpec((None, n, FFT_ROWS, D_FOLD), lambda i, j: (i, 0, j, 0)),
                  _const_spec((2 * n, 2 * n))],
        out_specs=pl.BlockSpec((None, n, FFT_ROWS, D_FOLD), lambda i, j: (i, 0, j, 0)),
        out_shape=jax.ShapeDtypeStruct((b, n, n, D_FOLD), F32),
        compiler_params=_params(("arbitrary", "arbitrary")),
        name="fourier_stage1",
    )(xcs.reshape(b, n, n, D_FOLD), f2)
    bb, aa, s1 = idx[:, None, None], idx[None, :, None], idx[None, None, :]
    ang2 = 2.0 * np.pi * ((s1 * (n * aa + bb)) % s) / s
    g2 = np.concatenate([np.cos(ang2), -np.sin(ang2)], axis=2) / np.sqrt(s * FOURIER_GROUP)
    out = pl.pallas_call(
        _fourier_stage2_kernel,
        grid=(b, n // FFT_ROWS),
        in_specs=[pl.BlockSpec((None, FFT_ROWS, n, D_FOLD), lambda i, j: (i, j, 0, 0)),
                  pl.BlockSpec((FFT_ROWS, n, 2 * n), lambda i, j: (j, 0, 0))],
        out_specs=pl.BlockSpec((None, n, FFT_ROWS, D_FOURIER), lambda i, j: (i, 0, j, 0)),
        out_shape=jax.ShapeDtypeStruct((b, n, n, D_FOURIER), F32),
        compiler_params=_params(("arbitrary", "arbitrary")),
        name="fourier_stage2",
    )(z, jnp.asarray(g2, F32).astype(BF16))
    return out.reshape(b, s, D_FOURIER)


def _shift_rows(z, up, down, mu):
    lane = lax.broadcasted_iota(jnp.int32, (1, z.shape[1]), 1) % 4
    return z + jnp.where(lane < 2, 0.0, mu) * (jnp.where(lane == 2, up, down) - z)


def _stack_masked(x, head0):
    return jnp.concatenate([jnp.where(head0, x, 0.0), jnp.where(head0, 0.0, x)], axis=0)


NT = (((1,), (1,)), ((), ()))
CUM_PIECES = 2
CHUNKS_PER_STEP = 4


def _split(x, pieces):
    out = []
    for i in range(pieces):
        hi = x.astype(BF16)
        out.append(hi)
        if i + 1 < pieces:
            x = x - hi.astype(F32)
    return out


def _bdot(a, b, dims=None):
    if dims is None:
        return jnp.dot(a, b, preferred_element_type=F32)
    return lax.dot_general(a, b, dims, preferred_element_type=F32)


HEADS_PER_SUM = 4


def _head_sums(x, ones_ref):
    w = HEADS_PER_SUM * HEAD_SIZE
    return jnp.concatenate([_bdot(x[:, i:i + w].astype(BF16), ones_ref[...]) for i in range(0, D_RWKV, w)], axis=1)


def _scan_kernel(*refs, grid, dirs, other, n_tiles, tm, seqs):
    final = other or len(dirs) == 2
    it = iter(refs)
    z_ref = next(it)
    zp_ref = next(it) if grid else None
    zn_ref = next(it) if grid else None
    h0_refs = {d: next(it) for d in dirs}
    if other:
        yo_ref, bo_ref = next(it), next(it)
    mu_ref, kk_ref, ka_ref, rk_ref, ones_ref = (next(it) for _ in range(5))
    lora = {d: tuple(next(it) for _ in range(4)) for d in dirs}
    if final:
        lnw_ref, lnb_ref, g2_ref = next(it), next(it), next(it)
    y_ref = next(it)
    b_ref = None if final else next(it)
    hout_refs = {d: next(it) for d in dirs}
    r_s, v_s, al_s, g_s = (next(it) for _ in range(4))
    per_dir = {d: tuple(next(it) for _ in range(7)) for d in dirs}

    step = pl.program_id(1)
    tile = (n_tiles - 1 - step) if dirs == (1,) else step

    @pl.when(step == 0)
    def _():
        for d in dirs:
            per_dir[d][4][...] = h0_refs[d][...]

    def neighbour_rows(lo, hi):
        parts = []
        if lo < 0:
            parts.append(jnp.where(tile == 0, 0.0, zp_ref[GRID_W + lo:, :]))
        parts.append(z_ref[max(lo, 0):min(hi, tm), :])
        if hi > tm:
            parts.append(jnp.where(tile == n_tiles - 1, 0.0, zn_ref[:hi - tm, :]))
        return parts[0] if len(parts) == 1 else jnp.concatenate(parts, axis=0)

    def prepare(rows):
        z = z_ref[rows, :]
        if grid:
            z = _shift_rows(z, neighbour_rows(rows.start - GRID_W, rows.stop - GRID_W),
                            neighbour_rows(rows.start + GRID_W, rows.stop + GRID_W), mu_ref[...])
        r = z[:, 0:D_RWKV]
        k = z[:, D_RWKV:2 * D_RWKV]
        v = z[:, 2 * D_RWKV:3 * D_RWKV]
        kk = k * kk_ref[...]
        kk = kk * lax.rsqrt(jnp.maximum(_head_sums(kk * kk, ones_ref), 1e-24))
        r_s[rows, :] = r
        v_s[rows, :] = v
        al_s[rows, :] = -kk
        bonus = bo_ref[rows, :] if other else None
        for d in dirs:
            w0_ref, a0_ref, w2_ref, a2_ref = lora[d]
            be_s, kd_s, lw_s = per_dir[d][:3]
            w_raw = w0_ref[...] + _bdot(jnp.tanh(z[:, OFF_WD:OFF_AD]).astype(BF16), w2_ref[...])
            a = _sigmoid(a0_ref[...] + _bdot(z[:, OFF_AD:OFF_GD].astype(BF16), a2_ref[...]))
            kd = k * (1.0 + (a - 1.0) * ka_ref[...])
            bonus_d = _head_sums(r * kd * rk_ref[...], ones_ref) * v
            bonus = bonus_d if bonus is None else bonus + bonus_d
            be_s[rows, :] = a * kk
            kd_s[rows, :] = kd
            lw_s[rows, :] = -DECAY_SCALE * _sigmoid(w_raw)
        if final:
            y_ref[rows, :] = bonus
            g_s[rows, :] = _bdot(_sigmoid(z[:, OFF_GD:]).astype(BF16), g2_ref[...])
        else:
            b_ref[rows, :] = bonus

    def finish(rows):
        y_sum = yo_ref[rows, :] if other else None
        for d in dirs:
            y_d = per_dir[d][3][rows, :]
            y_sum = y_d if y_sum is None else y_sum + y_d
        if final:
            mean = _head_sums(y_sum, ones_ref) * (1.0 / HEAD_SIZE)
            cen = y_sum - mean
            var = _head_sums(cen * cen, ones_ref) * (1.0 / HEAD_SIZE)
            y_gn = cen * lax.rsqrt(var + GN_EPS) * lnw_ref[...] + lnb_ref[...]
            y_ref[rows, :] = (y_gn + y_ref[rows, :]) * g_s[rows, :]
        else:
            y_ref[rows, :] = y_sum

    n_chunks = tm // CHUNK
    per_seq = n_chunks // seqs
    assert per_seq % CHUNKS_PER_STEP == 0, "a group of chunks must not straddle two sequences"
    ri = lax.broadcasted_iota(jnp.int32, (CHUNK, CHUNK), 0)
    ci = lax.broadcasted_iota(jnp.int32, (CHUNK, CHUNK), 1)
    rj = lax.broadcasted_iota(jnp.int32, (CHUNK, PAIR), 0)
    cj = lax.broadcasted_iota(jnp.int32, (CHUNK, PAIR), 1) % CHUNK
    eye = rj == cj
    tri = {0: (ci <= ri).astype(BF16), 1: (ci >= ri).astype(BF16)}
    strict = {0: cj < rj, 1: cj > rj}
    incl = {0: cj <= rj, 1: cj >= rj}
    last = {0: CHUNK - 1, 1: 0}
    head0 = lax.broadcasted_iota(jnp.int32, (1, PAIR), 1) < HEAD_SIZE
    blockdiag = lambda x: _stack_masked(x, head0)

    def head_transpose(x):
        t = blockdiag(x).T
        return t[:CHUNK] + t[CHUNK:]

    same_block = lambda n: (rj & -n) == (cj & -n)
    levels = [2 ** i for i in range(1, int(np.log2(CHUNK)))]
    base_mask = same_block(2)
    pair_masks = {n: same_block(2 * n) & ~same_block(n) for n in levels}

    def chunk_rows(c, d):
        seq, k = divmod(c, per_seq)
        off = (seq * per_seq + ((per_seq - 1 - k) if d == 1 else k)) * CHUNK
        return slice(off, off + CHUNK)

    def local_phase(chunks, d):
        be_s, kd_s, lw_s, _, _, qm_s, yn_s = per_dir[d]
        units = []
        for c in chunks:
            rows = chunk_rows(c, d)
            lw = lw_s[rows, :]
            cum = sum(_bdot(tri[d], piece) for piece in _split(lw, CUM_PIECES))
            tot = cum[last[d]:last[d] + 1, :]
            g_inv = jnp.exp(-cum)
            g_tot = jnp.exp(tot)
            g_end = g_tot * g_inv
            abar = al_s[rows, :] * jnp.exp(cum - lw)
            rbar = r_s[rows, :] * jnp.exp(cum)
            be = be_s[rows, :]
            kdc = kd_s[rows, :]
            btil, ktil, bhat, khat = be * g_inv, kdc * g_inv, be * g_end, kdc * g_end
            vc = v_s[rows, :]
            for p in range(N_PAIRS):
                lanes = slice(p * PAIR, (p + 1) * PAIR)
                cut = lambda x: x[:, lanes]
                units.append(dict(c=c, p=p, a=cut(abar).astype(BF16), r=cut(rbar), v=cut(vc).astype(BF16),
                                  bt=cut(btil).astype(BF16), kt=cut(ktil).astype(BF16),
                                  bh=cut(bhat), kh=cut(khat), g_tot=cut(g_tot)))
        for u in units:
            ar = jnp.concatenate([u["a"], u["r"].astype(BF16)], axis=0)
            u["gb"] = _bdot(ar, blockdiag(u["bt"]), NT)
            u["gk"] = _bdot(ar, blockdiag(u["kt"]), NT)
        for u in units:
            gb, gk = u.pop("gb"), u.pop("gk")
            u["l_ab"] = jnp.where(strict[d], gb[:CHUNK], 0.0)
            u["l_ak"] = jnp.where(strict[d], gk[:CHUNK], 0.0).astype(BF16)
            u["l_rbk"] = jnp.concatenate([jnp.where(incl[d], gb[CHUNK:], 0.0),
                                          jnp.where(incl[d], gk[CHUNK:], 0.0)], axis=1).astype(BF16)
            u["t"] = jnp.where(eye, 1.0, jnp.where(base_mask, u["l_ab"], 0.0))
        for n in levels:
            for u in units:
                e = jnp.where(pair_masks[n], u["l_ab"], 0.0).astype(BF16)
                u["et"] = _bdot(e, blockdiag(u["t"].astype(BF16)))
            for u in units:
                u["t"] = u["t"] + _bdot(u["t"].astype(BF16), blockdiag(u["et"].astype(BF16)))
        for u in units:
            u["x1"] = _bdot(u["l_ak"], blockdiag(u["v"]))
        for u in units:
            rhs = jnp.concatenate([blockdiag(u["a"]), blockdiag(u["x1"].astype(BF16))], axis=1)
            u["pu"] = _bdot(u["t"].astype(BF16), rhs)
        for u in units:
            pu = u["pu"].astype(BF16)
            v_bd = blockdiag(u["v"])
            rhs = jnp.concatenate(
                [jnp.concatenate([blockdiag(pu[:, :PAIR]), blockdiag(pu[:, PAIR:])], axis=1),
                 jnp.concatenate([jnp.zeros_like(v_bd), v_bd], axis=1)], axis=0)
            bkh_t = jnp.concatenate([head_transpose(u["bh"]), head_transpose(u["kh"])], axis=1)
            o = _bdot(jnp.concatenate([u["l_rbk"], bkh_t.astype(BF16)], axis=0), rhs)
            q = u["r"] + o[:CHUNK, :PAIR]
            m = jnp.where(eye, u["g_tot"], 0.0) + o[CHUNK:, :PAIR]
            qm_s[u["c"], u["p"]] = jnp.concatenate([q, m], axis=0).astype(BF16)
            yn_s[u["c"], u["p"]] = o[:, PAIR:]

    def state_phase(chunks, d):
        _, _, _, y_s, h_s, qm_s, yn_s = per_dir[d]
        for c in chunks:
            seq = c // per_seq
            for p in range(N_PAIRS):
                oh = _bdot(qm_s[c, p], blockdiag(h_s[seq, p].astype(BF16))) + yn_s[c, p]
                y_s[chunk_rows(c, d), p * PAIR:(p + 1) * PAIR] = oh[:CHUNK]
                h_s[seq, p] = oh[CHUNK:]

    spans = [list(range(g, g + CHUNKS_PER_STEP)) for g in range(0, n_chunks, CHUNKS_PER_STEP)]
    groups = [(chunks, d) for chunks in spans for d in dirs]

    def group_rows(chunks, d):
        spans = [chunk_rows(c, d) for c in chunks]
        return slice(min(r.start for r in spans), max(r.stop for r in spans))

    for i, (chunks, d) in enumerate(groups):
        if d == dirs[0]:
            prepare(group_rows(chunks, d))
        local_phase(chunks, d)
        if i > 0:
            prev_chunks, prev_d = groups[i - 1]
            state_phase(prev_chunks, prev_d)
            if prev_d == dirs[-1]:
                finish(group_rows(prev_chunks, prev_d))
    state_phase(*groups[-1])
    finish(group_rows(*groups[-1]))

    @pl.when(step == n_tiles - 1)
    def _():
        for d in dirs:
            hout_refs[d][...] = per_dir[d][4][...]


def _scan(z, h0, prm, dirs, grid, other=None):
    n_seq, seq_len, _ = z.shape
    tm = SCAN_TILE if seq_len >= SCAN_TILE else SHORT_SEQS_PER_TILE * seq_len
    seqs = max(tm // seq_len, 1)
    assert (seq_len % tm == 0) if seqs == 1 else (tm == seqs * seq_len and n_seq % seqs == 0 and not grid)
    b, s = n_seq // seqs, seq_len * seqs
    n_tiles = s // tm
    assert len(dirs) == 1 or n_tiles == 1, "both directions in one call need the whole sequence in a tile"
    final = other is not None or len(dirs) == 2
    fold = lambda a: a.reshape((b, seqs * a.shape[1]) + a.shape[2:])
    tile_of = (lambda j: n_tiles - 1 - j) if dirs == (1,) else (lambda j: j)
    tok = lambda w: pl.BlockSpec((None, tm, w), lambda i, j: (i, tile_of(j), 0))
    in_specs, args = [tok(D_Z)], [fold(z)]
    if grid:
        halo_per_tile, n_halo = tm // GRID_W, s // GRID_W
        in_specs += [pl.BlockSpec((None, GRID_W, D_Z),
                                  lambda i, j: (i, jnp.maximum(tile_of(j) * halo_per_tile - 1, 0), 0)),
                     pl.BlockSpec((None, GRID_W, D_Z),
                                  lambda i, j: (i, jnp.minimum((tile_of(j) + 1) * halo_per_tile, n_halo - 1), 0))]
        args += [args[0], args[0]]
    state_spec = pl.BlockSpec((None, seqs, N_PAIRS, HEAD_SIZE, PAIR), lambda i, j: (i, 0, 0, 0, 0))
    state_shape = jax.ShapeDtypeStruct((b, seqs, N_PAIRS, HEAD_SIZE, PAIR), F32)
    for d in dirs:
        in_specs.append(state_spec)
        args.append(h0[d].reshape(state_shape.shape))
    if other is not None:
        in_specs += [tok(D_RWKV), tok(D_RWKV)]
        args += [fold(a) for a in other]
    row = lambda x: x.reshape(1, -1)
    lora_pad = lambda w, d: jnp.zeros((2 * w.shape[1], D_RWKV), BF16).at[d * w.shape[1]:(d + 1) * w.shape[1]].set(
        w[d].astype(BF16))
    small = [row(prm["mu"]), row(prm["k_k"]), row(prm["k_a"]), row(prm["r_k"]),
             jnp.asarray(np.kron(np.eye(HEADS_PER_SUM), np.ones((HEAD_SIZE, HEAD_SIZE))), BF16)]
    for d in dirs:
        small += [row(prm["w0"][d]), row(prm["a0"][d]), lora_pad(prm["w2"], d), lora_pad(prm["a2"], d)]
    if final:
        small += [row(prm["ln_w"]), row(prm["ln_b"]), prm["g2"].astype(BF16)]
    in_specs += [_const_spec(x.shape) for x in small]
    args += small
    tokens = jax.ShapeDtypeStruct((b, s, D_RWKV), F32)
    out_specs, out_shape = [tok(D_RWKV)], [tokens]
    if not final:
        out_specs.append(tok(D_RWKV))
        out_shape.append(tokens)
    out_specs += [state_spec] * len(dirs)
    out_shape += [state_shape] * len(dirs)
    n_chunks = tm // CHUNK
    rows = lambda: pltpu.VMEM((tm, D_RWKV), F32)
    scratch = [rows() for _ in range(4)]
    for d in dirs:
        scratch += [rows() for _ in range(4)] + [pltpu.VMEM((seqs, N_PAIRS, CHUNK, PAIR), F32),
                                                  pltpu.VMEM((n_chunks, N_PAIRS, 2 * CHUNK, PAIR), BF16),
                                                  pltpu.VMEM((n_chunks, N_PAIRS, 2 * CHUNK, PAIR), F32)]
    outs = pl.pallas_call(
        functools.partial(_scan_kernel, grid=grid, dirs=dirs, other=other is not None, n_tiles=n_tiles, tm=tm,
                          seqs=seqs),
        grid=(b, n_tiles),
        in_specs=in_specs,
        out_specs=out_specs,
        out_shape=out_shape,
        scratch_shapes=scratch,
        compiler_params=_params(("arbitrary", "arbitrary")),
        name="scan_" + "".join("fb"[d] for d in dirs),
    )(*args)
    unfold = lambda a: a.reshape((n_seq, seq_len) + a.shape[2:])
    states = {d: a.reshape((n_seq,) + a.shape[2:]) for d, a in zip(dirs, outs[-len(dirs):])}
    return tuple(unfold(a) for a in outs[:-len(dirs)]) + (states,)


def _pack_state(s):
    b = s.shape[0]
    h = jnp.swapaxes(s, -1, -2).reshape(b, N_PAIRS, 2, HEAD_SIZE, HEAD_SIZE)
    return jnp.transpose(h, (0, 1, 3, 2, 4)).reshape(b, N_PAIRS, HEAD_SIZE, PAIR)


def _unpack_state(hp):
    b = hp.shape[0]
    h = jnp.transpose(hp.reshape(b, N_PAIRS, HEAD_SIZE, 2, HEAD_SIZE), (0, 1, 3, 2, 4))
    return jnp.swapaxes(h.reshape(b, N_RWKV_HEADS, HEAD_SIZE, HEAD_SIZE), -1, -2)


def _mixer_heads(z, s0_f, s0_b, prm, grid):
    h0 = {0: _pack_state(s0_f), 1: _pack_state(s0_b)}
    if z.shape[1] < SCAN_TILE:
        y, h = _scan(z, h0, prm, (0, 1), grid)
    else:
        y_b, bonus_b, h_b = _scan(z, h0, prm, (1,), grid)
        y, h_f = _scan(z, h0, prm, (0,), grid, other=(y_b, bonus_b))
        h = {**h_f, **h_b}
    return y, _unpack_state(h[0]), _unpack_state(h[1])


def _path(x, mod, mod_per_batch, s0_f, s0_b, grid, w, prm):
    b, s, _ = x.shape
    rows = (b, s) if mod_per_batch else (1, b * s)
    as_rows = lambda a: a.reshape(rows + a.shape[2:])
    as_seqs = lambda a: a.reshape((b, s) + a.shape[2:])
    x1, u, z = _front(as_rows(x), mod, mod_per_batch, w["norm_g"], w["gate"], w["up"], w["down"], w["w_fold"],
                      w["w_z"], prm["mu"], TOKEN_TILE, GRID_W if grid else s, grid)
    y_four = _fourier_two_stage(as_seqs(u)) if grid else _fourier_dense(as_seqs(u))
    y_rwkv, s_f, s_b = _mixer_heads(as_seqs(z), s0_f, s0_b, prm, grid)
    y = _back(x1, as_rows(y_four), as_rows(y_rwkv), mod, mod_per_batch, w["norm_g"], w["final"], w["w_out"],
              w["gate"], w["up"], w["down"], TOKEN_TILE)
    return as_seqs(y), s_f, s_b


def kernel(x_prompt, x_sample, state_fwd, state_bwd, c, c_ctx, w_mod, b_mod, norm_g, ffn_w_gate, ffn_w_up,
           ffn_w_down, w_in, shift_mu, decay_w0, decay_w2, iclr_a0, iclr_a2, gate_g2, k_k, k_a, r_k,
           ln_x_w, ln_x_b, w_out, final_norm):
    depth = w_mod.shape[0]
    assert depth == 1, "the back kernel applies the final norm, so exactly one layer is supported"
    bp = x_prompt.shape[0]
    bs = x_sample.shape[0]
    xp, xs = x_prompt, x_sample
    new_f, new_b = [], []
    for l in range(depth):
        cvec = jnp.zeros((8, D_MODEL), F32).at[:bs].set(c).at[bs].set(c_ctx)
        mod = _modulation(cvec, w_mod[l], b_mod[l]).reshape(8, N_MOD, D_MODEL)
        w = {"norm_g": norm_g[l], "final": final_norm,
             "gate": ffn_w_gate[l].astype(BF16), "up": ffn_w_up[l].astype(BF16), "down": ffn_w_down[l].astype(BF16),
             "w_fold": _fold_group_dft(w_in[l]), "w_z": w_in[l, :, D_FOURIER:].astype(BF16),
             "w_out": w_out[l].astype(BF16)}
        prm = {"mu": shift_mu[l], "w0": decay_w0[l], "w2": decay_w2[l], "a0": iclr_a0[l], "a2": iclr_a2[l],
               "g2": gate_g2[l], "k_k": k_k[l], "k_a": k_a[l], "r_k": r_k[l], "ln_w": ln_x_w[l], "ln_b": ln_x_b[l]}
        zero_state = jnp.zeros((bp, N_RWKV_HEADS, HEAD_SIZE, HEAD_SIZE), F32)
        xp, s_f, s_b = _path(xp, mod[bs:bs + 1], False, zero_state, zero_state, False, w, prm)
        new_f.append(s_f)
        new_b.append(s_b)
        xs, _, _ = _path(xs, mod[:bs], True, state_fwd[:, l], state_bwd[:, l], True, w, prm)
    return xp, xs, jnp.stack(new_f, axis=1), jnp.stack(new_b, axis=1)
```

```python
import functools

import numpy as np
import jax
import jax.numpy as jnp
from jax import lax
from jax.experimental import pallas as pl
from jax.experimental.pallas import tpu as pltpu

D_MODEL = 1024
GRID_W = 64
D_FOURIER = 512
N_FOURIER_GROUPS = 8
FOURIER_GROUP = D_FOURIER // N_FOURIER_GROUPS
D_RWKV = D_MODEL - D_FOURIER
HEAD_SIZE = 64
N_RWKV_HEADS = D_RWKV // HEAD_SIZE
N_PAIRS = N_RWKV_HEADS // 2
PAIR = 2 * HEAD_SIZE
DECAY_LORA = 64
ICLR_LORA = 64
GATE_LORA = 128
D_Z = 3 * D_RWKV + 2 * DECAY_LORA + 2 * ICLR_LORA + GATE_LORA
D_FOLD = 2 * D_FOURIER
D_FF = 2816
N_MOD = 9
RMS_EPS = 1e-6
GN_EPS = 64e-5
CHUNK = 64
TOKEN_TILE = 512
SCAN_TILE = 1024
SHORT_SEQS_PER_TILE = 2
OFF_WD = 3 * D_RWKV
OFF_AD = OFF_WD + 2 * DECAY_LORA
OFF_GD = OFF_AD + 2 * ICLR_LORA
DECAY_SCALE = float(np.exp(-0.5))
V7X_VMEM_LIMIT = 56 * 1024 * 1024

F32 = jnp.float32
BF16 = jnp.bfloat16


def _dot32(a, b):
    a_hi = a.astype(BF16)
    b_hi = b.astype(BF16)
    a_lo = (a - a_hi.astype(F32)).astype(BF16)
    b_lo = (b - b_hi.astype(F32)).astype(BF16)
    dot = lambda x, y: jnp.dot(x, y, preferred_element_type=F32)
    return dot(a_hi, b_hi) + (dot(a_hi, b_lo) + dot(a_lo, b_hi))


def _sigmoid(x):
    return 1.0 / (1.0 + jnp.exp(-x))


def _norm_mod(x, g, shift, scale):
    ms = jnp.mean(x * x, axis=-1, keepdims=True)
    return x * lax.rsqrt(ms + RMS_EPS) * g * (1.0 + scale) + shift


def _swiglu(h, wg_ref, wu_ref, wd_ref):
    hb = h.astype(BF16)
    gate = jnp.dot(hb, wg_ref[...], preferred_element_type=F32)
    up = jnp.dot(hb, wu_ref[...], preferred_element_type=F32)
    act = gate * _sigmoid(gate) * up
    return jnp.dot(act.astype(BF16), wd_ref[...], preferred_element_type=F32)


def _const_spec(shape):
    nd = len(shape)
    return pl.BlockSpec(shape, lambda *_: (0,) * nd, pipeline_mode=pl.Buffered(1))


def _ffn_spec(shape, which):
    return pl.BlockSpec((None,) + shape, lambda *_: (which, 0, 0), pipeline_mode=pl.Buffered(1))


def _params(sem):
    return pltpu.CompilerParams(dimension_semantics=sem, vmem_limit_bytes=V7X_VMEM_LIMIT)


def _mod_kernel(c_ref, w_ref, b_ref, o_ref):
    c = c_ref[...]
    o_ref[...] = _dot32(c * _sigmoid(c), w_ref[...]) + b_ref[...]


def _modulation(cvec, w_mod, b_mod):
    n = w_mod.shape[1]
    tn = n // 8
    return pl.pallas_call(
        _mod_kernel,
        grid=(n // tn,),
        in_specs=[pl.BlockSpec((8, D_MODEL), lambda j: (0, 0)),
                  pl.BlockSpec((D_MODEL, tn), lambda j: (0, j)),
                  pl.BlockSpec((1, tn), lambda j: (0, j))],
        out_specs=pl.BlockSpec((8, tn), lambda j: (0, j)),
        out_shape=jax.ShapeDtypeStruct((8, n), F32),
        compiler_params=_params(("arbitrary",)),
        name="modulation",
    )(cvec, w_mod, b_mod.reshape(1, n))


def _token_tiling(b, s, tm):
    if tm <= s:
        assert s % tm == 0
        return (b, s // tm), lambda w: pl.BlockSpec((None, tm, w), lambda i, j: (i, j, 0))
    assert tm % s == 0 and b % (tm // s) == 0
    return (1, b * s // tm), lambda w: pl.BlockSpec((tm // s, s, w), lambda i, j: (j, 0, 0))


def _rows(ref):
    return ref[...].reshape(-1, ref.shape[-1])


def _store_rows(ref, rows):
    ref[...] = rows.reshape(ref.shape)


def _shift_adjacent(z, mu, period, grid):
    tm = z.shape[0]
    col = lax.broadcasted_iota(jnp.int32, (tm, 1), 0) % period
    lane = lax.broadcasted_iota(jnp.int32, (1, z.shape[1]), 1) % 4
    back1 = jnp.where(col == 0, 0.0, pltpu.roll(z, 1, 0))
    fwd1 = jnp.where(col == period - 1, 0.0, pltpu.roll(z, tm - 1, 0))
    if grid:
        return jnp.where(lane < 2, z + mu * (jnp.where(lane == 0, back1, fwd1) - z), z)
    return z + mu * (jnp.where(lane % 2 == 0, back1, fwd1) - z)


def _front_kernel(x_ref, mod_ref, g_ref, wg_ref, wu_ref, wd_ref, wfold_ref, wz_ref, mu_ref,
                  x1_ref, u_ref, z_ref, *, period, grid):
    x = _rows(x_ref)
    h = _norm_mod(x, g_ref[0:1, :], mod_ref[0:1, :], mod_ref[1:2, :])
    x1 = x + 0.5 * mod_ref[2:3, :] * _swiglu(h, wg_ref, wu_ref, wd_ref)
    _store_rows(x1_ref, x1)
    h2 = _norm_mod(x1, g_ref[1:2, :], mod_ref[3:4, :], mod_ref[4:5, :]).astype(BF16)
    _store_rows(u_ref, jnp.dot(h2, wfold_ref[...], preferred_element_type=F32))
    _store_rows(z_ref, _shift_adjacent(jnp.dot(h2, wz_ref[...], preferred_element_type=F32), mu_ref[...],
                                       period, grid))


def _front(x, mod, mod_per_batch, norm_g, wg, wu, wd, w_fold, w_z, mu, tm, period, grid):
    b, s, _ = x.shape
    assert tm % period == 0
    steps, tok = _token_tiling(b, s, tm)
    mod_map = (lambda i, j: (i, 0, 0)) if mod_per_batch else (lambda i, j: (0, 0, 0))
    return pl.pallas_call(
        functools.partial(_front_kernel, period=period, grid=grid),
        grid=steps,
        in_specs=[tok(D_MODEL),
                  pl.BlockSpec((None, N_MOD, D_MODEL), mod_map),
                  _const_spec((3, D_MODEL)),
                  _ffn_spec((D_MODEL, D_FF), 0), _ffn_spec((D_MODEL, D_FF), 0),
                  _ffn_spec((D_FF, D_MODEL), 0), _const_spec((D_MODEL, D_FOLD)), _const_spec((D_MODEL, D_Z)),
                  _const_spec((1, D_Z))],
        out_specs=[tok(D_MODEL), tok(D_FOLD), tok(D_Z)],
        out_shape=[jax.ShapeDtypeStruct((b, s, D_MODEL), F32),
                   jax.ShapeDtypeStruct((b, s, D_FOLD), F32),
                   jax.ShapeDtypeStruct((b, s, D_Z), F32)],
        compiler_params=_params(("arbitrary", "arbitrary")),
        name="front",
    )(x, mod, norm_g, wg, wu, wd, w_fold, w_z, mu.reshape(1, D_Z))


def _back_kernel(x1_ref, yf_ref, yr_ref, mod_ref, g_ref, fin_ref, wo_ref, wg_ref, wu_ref, wd_ref, o_ref):
    mixed = (jnp.dot(_rows(yf_ref).astype(BF16), wo_ref[0:D_FOURIER, :], preferred_element_type=F32)
             + jnp.dot(_rows(yr_ref).astype(BF16), wo_ref[D_FOURIER:, :], preferred_element_type=F32))
    x2 = _rows(x1_ref) + mod_ref[5:6, :] * mixed
    h = _norm_mod(x2, g_ref[2:3, :], mod_ref[6:7, :], mod_ref[7:8, :])
    x3 = x2 + 0.5 * mod_ref[8:9, :] * _swiglu(h, wg_ref, wu_ref, wd_ref)
    ms = jnp.mean(x3 * x3, axis=-1, keepdims=True)
    _store_rows(o_ref, x3 * lax.rsqrt(ms + RMS_EPS) * fin_ref[...])


def _back(x1, yf, yr, mod, mod_per_batch, norm_g, final_norm, w_out, wg, wu, wd, tm):
    b, s, _ = x1.shape
    steps, tok = _token_tiling(b, s, tm)
    mod_map = (lambda i, j: (i, 0, 0)) if mod_per_batch else (lambda i, j: (0, 0, 0))
    return pl.pallas_call(
        _back_kernel,
        grid=steps,
        in_specs=[tok(D_MODEL), tok(D_FOURIER), tok(D_RWKV),
                  pl.BlockSpec((None, N_MOD, D_MODEL), mod_map),
                  _const_spec((3, D_MODEL)), _const_spec((1, D_MODEL)),
                  _const_spec((D_MODEL, D_MODEL)),
                  _ffn_spec((D_MODEL, D_FF), 1), _ffn_spec((D_MODEL, D_FF), 1),
                  _ffn_spec((D_FF, D_MODEL), 1)],
        out_specs=tok(D_MODEL),
        out_shape=jax.ShapeDtypeStruct((b, s, D_MODEL), F32),
        compiler_params=_params(("arbitrary", "arbitrary")),
        name="back",
    )(x1, yf, yr, mod, norm_g, final_norm.reshape(1, D_MODEL), w_out, wg, wu, wd)


def _fold_kernel(w_ref, cs_ref, o_ref):
    o_ref[...] = _dot32(w_ref[...], cs_ref[...]).astype(o_ref.dtype)


def _fold_group_dft(w_in):
    q = np.arange(FOURIER_GROUP)
    ang = 2.0 * np.pi * ((q[:, None] * q[None, :]) % FOURIER_GROUP) / FOURIER_GROUP
    eye = np.eye(N_FOURIER_GROUPS)
    cs = jnp.asarray(np.concatenate([np.kron(eye, np.cos(ang)), np.kron(eye, np.sin(ang))], axis=1), F32)
    return pl.pallas_call(
        _fold_kernel,
        grid=(1,),
        in_specs=[pl.BlockSpec((D_MODEL, D_FOURIER), lambda i: (0, 0)),
                  pl.BlockSpec((D_FOURIER, D_FOLD), lambda i: (0, 0))],
        out_specs=pl.BlockSpec((D_MODEL, D_FOLD), lambda i: (0, 0)),
        out_shape=jax.ShapeDtypeStruct((D_MODEL, D_FOLD), BF16),
        compiler_params=_params(("arbitrary",)),
        name="fold_group_dft",
    )(w_in, cs)


def _stack_cos_sin(x):
    return jnp.concatenate([x[:, :D_FOURIER], x[:, D_FOURIER:]], axis=0).astype(BF16)


DENSE_DFT_SEQS = 4


def _fourier_dense_kernel(x_ref, p_ref, o_ref):
    for i in range(x_ref.shape[0]):
        o_ref[i] = _bdot(p_ref[...], _stack_cos_sin(x_ref[i]))


def _fourier_dense(xcs):
    b, s, _ = xcs.shape
    pos = np.arange(s)
    ang = 2.0 * np.pi * ((pos[:, None] * pos[None, :]) % s) / s
    table = np.concatenate([np.cos(ang), -np.sin(ang)], axis=1) / np.sqrt(s * FOURIER_GROUP)
    per_step = DENSE_DFT_SEQS if b % DENSE_DFT_SEQS == 0 else 1
    return pl.pallas_call(
        _fourier_dense_kernel,
        grid=(b // per_step,),
        in_specs=[pl.BlockSpec((per_step, s, D_FOLD), lambda i: (i, 0, 0)), _const_spec((s, 2 * s))],
        out_specs=pl.BlockSpec((per_step, s, D_FOURIER), lambda i: (i, 0, 0)),
        out_shape=jax.ShapeDtypeStruct((b, s, D_FOURIER), F32),
        compiler_params=_params(("arbitrary",)),
        name="fourier_dense",
    )(xcs, jnp.asarray(table, F32).astype(BF16))


FFT_ROWS = 16


def _fourier_stage1_kernel(x_ref, f_ref, z_ref):
    for j in range(FFT_ROWS):
        z = _bdot(f_ref[...], _stack_cos_sin(x_ref[:, j, :]))
        n = z.shape[0] // 2
        z_ref[:, j, :] = jnp.concatenate([z[:n], z[n:]], axis=1)


def _fourier_stage2_kernel(z_ref, g_ref, o_ref):
    for j in range(FFT_ROWS):
        o_ref[:, j, :] = _bdot(g_ref[j], _stack_cos_sin(z_ref[j]))


def _fourier_two_stage(xcs):
    b, s, _ = xcs.shape
    n = int(round(np.sqrt(s)))
    assert n * n == s and n % FFT_ROWS == 0
    idx = np.arange(n)
    ang = 2.0 * np.pi * ((idx[:, None] * idx[None, :]) % n) / n
    fc, fs = np.cos(ang), np.sin(ang)
    f2 = jnp.asarray(np.block([[fc, -fs], [fs, fc]]), F32).astype(BF16)
    z = pl.pallas_call(
        _fourier_stage1_kernel,
        grid=(b, n // FFT_ROWS),
        in_specs=[pl.BlockSpec((None, n, FFT_ROWS, D_FOLD), lambda i, j: (i, 0, j, 0)),
                  _const_spec((2 * n, 2 * n))],
        out_specs=pl.BlockSpec((None, n, FFT_ROWS, D_FOLD), lambda i, j: (i, 0, j, 0)),
        out_shape=jax.ShapeDtypeStruct((b, n, n, D_FOLD), F32),
        compiler_params=_params(("arbitrary", "arbitrary")),
        name="fourier_stage1",
    )(xcs.reshape(b, n, n, D_FOLD), f2)
    bb, aa, s1 = idx[:, None, None], idx[None, :, None], idx[None, None, :]
    ang2 = 2.0 * np.pi * ((s1 * (n * aa + bb)) % s) / s
    g2 = np.concatenate([np.cos(ang2), -np.sin(ang2)], axis=2) / np.sqrt(s * FOURIER_GROUP)
    out = pl.pallas_call(
        _fourier_stage2_kernel,
        grid=(b, n // FFT_ROWS),
        in_specs=[pl.BlockSpec((None, FFT_ROWS, n, D_FOLD), lambda i, j: (i, j, 0, 0)),
                  pl.BlockSpec((FFT_ROWS, n, 2 * n), lambda i, j: (j, 0, 0))],
        out_specs=pl.BlockSpec((None, n, FFT_ROWS, D_FOURIER), lambda i, j: (i, 0, j, 0)),
        out_shape=jax.ShapeDtypeStruct((b, n, n, D_FOURIER), F32),
        compiler_params=_params(("arbitrary", "arbitrary")),
        name="fourier_stage2",
    )(z, jnp.asarray(g2, F32).astype(BF16))
    return out.reshape(b, s, D_FOURIER)


def _shift_rows(z, up, down, mu):
    lane = lax.broadcasted_iota(jnp.int32, (1, z.shape[1]), 1) % 4
    return z + jnp.where(lane < 2, 0.0, mu) * (jnp.where(lane == 2, up, down) - z)


def _stack_masked(x, head0):
    return jnp.concatenate([jnp.where(head0, x, 0.0), jnp.where(head0, 0.0, x)], axis=0)


NT = (((1,), (1,)), ((), ()))
CUM_PIECES = 2
CHUNKS_PER_STEP = 4


def _split(x, pieces):
    out = []
    for i in range(pieces):
        hi = x.astype(BF16)
        out.append(hi)
        if i + 1 < pieces:
            x = x - hi.astype(F32)
    return out


def _bdot(a, b, dims=None):
    if dims is None:
        return jnp.dot(a, b, preferred_element_type=F32)
    return lax.dot_general(a, b, dims, preferred_element_type=F32)


HEADS_PER_SUM = 4


def _head_sums(x, ones_ref):
    w = HEADS_PER_SUM * HEAD_SIZE
    return jnp.concatenate([_bdot(x[:, i:i + w].astype(BF16), ones_ref[...]) for i in range(0, D_RWKV, w)], axis=1)


def _scan_kernel(*refs, grid, dirs, other, n_tiles, tm, seqs):
    final = other or len(dirs) == 2
    it = iter(refs)
    z_ref = next(it)
    zp_ref = next(it) if grid else None
    zn_ref = next(it) if grid else None
    h0_refs = {d: next(it) for d in dirs}
    if other:
        yo_ref, bo_ref = next(it), next(it)
    mu_ref, kk_ref, ka_ref, rk_ref, ones_ref = (next(it) for _ in range(5))
    lora = {d: tuple(next(it) for _ in range(4)) for d in dirs}
    if final:
        lnw_ref, lnb_ref, g2_ref = next(it), next(it), next(it)
    y_ref = next(it)
    b_ref = None if final else next(it)
    hout_refs = {d: next(it) for d in dirs}
    r_s, v_s, al_s, g_s = (next(it) for _ in range(4))
    per_dir = {d: tuple(next(it) for _ in range(7)) for d in dirs}

    step = pl.program_id(1)
    tile = (n_tiles - 1 - step) if dirs == (1,) else step

    @pl.when(step == 0)
    def _():
        for d in dirs:
            per_dir[d][4][...] = h0_refs[d][...]

    def neighbour_rows(lo, hi):
        parts = []
        if lo < 0:
            parts.append(jnp.where(tile == 0, 0.0, zp_ref[GRID_W + lo:, :]))
        parts.append(z_ref[max(lo, 0):min(hi, tm), :])
        if hi > tm:
            parts.append(jnp.where(tile == n_tiles - 1, 0.0, zn_ref[:hi - tm, :]))
        return parts[0] if len(parts) == 1 else jnp.concatenate(parts, axis=0)

    def prepare(rows):
        z = z_ref[rows, :]
        if grid:
            z = _shift_rows(z, neighbour_rows(rows.start - GRID_W, rows.stop - GRID_W),
                            neighbour_rows(rows.start + GRID_W, rows.stop + GRID_W), mu_ref[...])
        r = z[:, 0:D_RWKV]
        k = z[:, D_RWKV:2 * D_RWKV]
        v = z[:, 2 * D_RWKV:3 * D_RWKV]
        kk = k * kk_ref[...]
        kk = kk * lax.rsqrt(jnp.maximum(_head_sums(kk * kk, ones_ref), 1e-24))
        r_s[rows, :] = r
        v_s[rows, :] = v
        al_s[rows, :] = -kk
        bonus = bo_ref[rows, :] if other else None
        for d in dirs:
            w0_ref, a0_ref, w2_ref, a2_ref = lora[d]
            be_s, kd_s, lw_s = per_dir[d][:3]
            w_raw = w0_ref[...] + _bdot(jnp.tanh(z[:, OFF_WD:OFF_AD]).astype(BF16), w2_ref[...])
            a = _sigmoid(a0_ref[...] + _bdot(z[:, OFF_AD:OFF_GD].astype(BF16), a2_ref[...]))
            kd = k * (1.0 + (a - 1.0) * ka_ref[...])
            bonus_d = _head_sums(r * kd * rk_ref[...], ones_ref) * v
            bonus = bonus_d if bonus is None else bonus + bonus_d
            be_s[rows, :] = a * kk
            kd_s[rows, :] = kd
            lw_s[rows, :] = -DECAY_SCALE * _sigmoid(w_raw)
        if final:
            y_ref[rows, :] = bonus
            g_s[rows, :] = _bdot(_sigmoid(z[:, OFF_GD:]).astype(BF16), g2_ref[...])
        else:
            b_ref[rows, :] = bonus

    def finish(rows):
        y_sum = yo_ref[rows, :] if other else None
        for d in dirs:
            y_d = per_dir[d][3][rows, :]
            y_sum = y_d if y_sum is None else y_sum + y_d
        if final:
            mean = _head_sums(y_sum, ones_ref) * (1.0 / HEAD_SIZE)
            cen = y_sum - mean
            var = _head_sums(cen * cen, ones_ref) * (1.0 / HEAD_SIZE)
            y_gn = cen * lax.rsqrt(var + GN_EPS) * lnw_ref[...] + lnb_ref[...]
            y_ref[rows, :] = (y_gn + y_ref[rows, :]) * g_s[rows, :]
        else:
            y_ref[rows, :] = y_sum

    n_chunks = tm // CHUNK
    per_seq = n_chunks // seqs
    assert per_seq % CHUNKS_PER_STEP == 0, "a group of chunks must not straddle two sequences"
    ri = lax.broadcasted_iota(jnp.int32, (CHUNK, CHUNK), 0)
    ci = lax.broadcasted_iota(jnp.int32, (CHUNK, CHUNK), 1)
    rj = lax.broadcasted_iota(jnp.int32, (CHUNK, PAIR), 0)
    cj = lax.broadcasted_iota(jnp.int32, (CHUNK, PAIR), 1) % CHUNK
    eye = rj == cj
    tri = {0: (ci <= ri).astype(BF16), 1: (ci >= ri).astype(BF16)}
    strict = {0: cj < rj, 1: cj > rj}
    incl = {0: cj <= rj, 1: cj >= rj}
    last = {0: CHUNK - 1, 1: 0}
    head0 = lax.broadcasted_iota(jnp.int32, (1, PAIR), 1) < HEAD_SIZE
    blockdiag = lambda x: _stack_masked(x, head0)

    def head_transpose(x):
        t = blockdiag(x).T
        return t[:CHUNK] + t[CHUNK:]

    same_block = lambda n: (rj & -n) == (cj & -n)
    levels = [2 ** i for i in range(1, int(np.log2(CHUNK)))]
    base_mask = same_block(2)
    pair_masks = {n: same_block(2 * n) & ~same_block(n) for n in levels}

    def chunk_rows(c, d):
        seq, k = divmod(c, per_seq)
        off = (seq * per_seq + ((per_seq - 1 - k) if d == 1 else k)) * CHUNK
        return slice(off, off + CHUNK)

    def local_phase(chunks, d):
        be_s, kd_s, lw_s, _, _, qm_s, yn_s = per_dir[d]
        units = []
        for c in chunks:
            rows = chunk_rows(c, d)
            lw = lw_s[rows, :]
            cum = sum(_bdot(tri[d], piece) for piece in _split(lw, CUM_PIECES))
            tot = cum[last[d]:last[d] + 1, :]
            g_inv = jnp.exp(-cum)
            g_tot = jnp.exp(tot)
            g_end = g_tot * g_inv
            abar = al_s[rows, :] * jnp.exp(cum - lw)
            rbar = r_s[rows, :] * jnp.exp(cum)
            be = be_s[rows, :]
            kdc = kd_s[rows, :]
            btil, ktil, bhat, khat = be * g_inv, kdc * g_inv, be * g_end, kdc * g_end
            vc = v_s[rows, :]
            for p in range(N_PAIRS):
                lanes = slice(p * PAIR, (p + 1) * PAIR)
                cut = lambda x: x[:, lanes]
                units.append(dict(c=c, p=p, a=cut(abar).astype(BF16), r=cut(rbar), v=cut(vc).astype(BF16),
                                  bt=cut(btil).astype(BF16), kt=cut(ktil).astype(BF16),
                                  bh=cut(bhat), kh=cut(khat), g_tot=cut(g_tot)))
        for u in units:
            ar = jnp.concatenate([u["a"], u["r"].astype(BF16)], axis=0)
            u["gb"] = _bdot(ar, blockdiag(u["bt"]), NT)
            u["gk"] = _bdot(ar, blockdiag(u["kt"]), NT)
        for u in units:
            gb, gk = u.pop("gb"), u.pop("gk")
            u["l_ab"] = jnp.where(strict[d], gb[:CHUNK], 0.0)
            u["l_ak"] = jnp.where(strict[d], gk[:CHUNK], 0.0).astype(BF16)
            u["l_rbk"] = jnp.concatenate([jnp.where(incl[d], gb[CHUNK:], 0.0),
                                          jnp.where(incl[d], gk[CHUNK:], 0.0)], axis=1).astype(BF16)
            u["t"] = jnp.where(eye, 1.0, jnp.where(base_mask, u["l_ab"], 0.0))
        for n in levels:
            for u in units:
                e = jnp.where(pair_masks[n], u["l_ab"], 0.0).astype(BF16)
                u["et"] = _bdot(e, blockdiag(u["t"].astype(BF16)))
            for u in units:
                u["t"] = u["t"] + _bdot(u["t"].astype(BF16), blockdiag(u["et"].astype(BF16)))
        for u in units:
            u["x1"] = _bdot(u["l_ak"], blockdiag(u["v"]))
        for u in units:
            rhs = jnp.concatenate([blockdiag(u["a"]), blockdiag(u["x1"].astype(BF16))], axis=1)
            u["pu"] = _bdot(u["t"].astype(BF16), rhs)
        for u in units:
            pu = u["pu"].astype(BF16)
            v_bd = blockdiag(u["v"])
            rhs = jnp.concatenate(
                [jnp.concatenate([blockdiag(pu[:, :PAIR]), blockdiag(pu[:, PAIR:])], axis=1),
                 jnp.concatenate([jnp.zeros_like(v_bd), v_bd], axis=1)], axis=0)
            bkh_t = jnp.concatenate([head_transpose(u["bh"]), head_transpose(u["kh"])], axis=1)
            o = _bdot(jnp.concatenate([u["l_rbk"], bkh_t.astype(BF16)], axis=0), rhs)
            q = u["r"] + o[:CHUNK, :PAIR]
            m = jnp.where(eye, u["g_tot"], 0.0) + o[CHUNK:, :PAIR]
            qm_s[u["c"], u["p"]] = jnp.concatenate([q, m], axis=0).astype(BF16)
            yn_s[u["c"], u["p"]] = o[:, PAIR:]

    def state_phase(chunks, d):
        _, _, _, y_s, h_s, qm_s, yn_s = per_dir[d]
        for c in chunks:
            seq = c // per_seq
            for p in range(N_PAIRS):
                oh = _bdot(qm_s[c, p], blockdiag(h_s[seq, p].astype(BF16))) + yn_s[c, p]
                y_s[chunk_rows(c, d), p * PAIR:(p + 1) * PAIR] = oh[:CHUNK]
                h_s[seq, p] = oh[CHUNK:]

    spans = [list(range(g, g + CHUNKS_PER_STEP)) for g in range(0, n_chunks, CHUNKS_PER_STEP)]
    groups = [(chunks, d) for chunks in spans for d in dirs]

    def group_rows(chunks, d):
        spans = [chunk_rows(c, d) for c in chunks]
        return slice(min(r.start for r in spans), max(r.stop for r in spans))

    for i, (chunks, d) in enumerate(groups):
        if d == dirs[0]:
            prepare(group_rows(chunks, d))
        local_phase(chunks, d)
        if i > 0:
            prev_chunks, prev_d = groups[i - 1]
            state_phase(prev_chunks, prev_d)
            if prev_d == dirs[-1]:
                finish(group_rows(prev_chunks, prev_d))
    state_phase(*groups[-1])
    finish(group_rows(*groups[-1]))

    @pl.when(step == n_tiles - 1)
    def _():
        for d in dirs:
            hout_refs[d][...] = per_dir[d][4][...]


def _scan(z, h0, prm, dirs, grid, other=None):
    n_seq, seq_len, _ = z.shape
    tm = SCAN_TILE if seq_len >= SCAN_TILE else SHORT_SEQS_PER_TILE * seq_len
    seqs = max(tm // seq_len, 1)
    assert (seq_len % tm == 0) if seqs == 1 else (tm == seqs * seq_len and n_seq % seqs == 0 and not grid)
    b, s = n_seq // seqs, seq_len * seqs
    n_tiles = s // tm
    assert len(dirs) == 1 or n_tiles == 1, "both directions in one call need the whole sequence in a tile"
    final = other is not None or len(dirs) == 2
    fold = lambda a: a.reshape((b, seqs * a.shape[1]) + a.shape[2:])
    tile_of = (lambda j: n_tiles - 1 - j) if dirs == (1,) else (lambda j: j)
    tok = lambda w: pl.BlockSpec((None, tm, w), lambda i, j: (i, tile_of(j), 0))
    in_specs, args = [tok(D_Z)], [fold(z)]
    if grid:
        halo_per_tile, n_halo = tm // GRID_W, s // GRID_W
        in_specs += [pl.BlockSpec((None, GRID_W, D_Z),
                                  lambda i, j: (i, jnp.maximum(tile_of(j) * halo_per_tile - 1, 0), 0)),
                     pl.BlockSpec((None, GRID_W, D_Z),
                                  lambda i, j: (i, jnp.minimum((tile_of(j) + 1) * halo_per_tile, n_halo - 1), 0))]
        args += [args[0], args[0]]
    state_spec = pl.BlockSpec((None, seqs, N_PAIRS, HEAD_SIZE, PAIR), lambda i, j: (i, 0, 0, 0, 0))
    state_shape = jax.ShapeDtypeStruct((b, seqs, N_PAIRS, HEAD_SIZE, PAIR), F32)
    for d in dirs:
        in_specs.append(state_spec)
        args.append(h0[d].reshape(state_shape.shape))
    if other is not None:
        in_specs += [tok(D_RWKV), tok(D_RWKV)]
        args += [fold(a) for a in other]
    row = lambda x: x.reshape(1, -1)
    lora_pad = lambda w, d: jnp.zeros((2 * w.shape[1], D_RWKV), BF16).at[d * w.shape[1]:(d + 1) * w.shape[1]].set(
        w[d].astype(BF16))
    small = [row(prm["mu"]), row(prm["k_k"]), row(prm["k_a"]), row(prm["r_k"]),
             jnp.asarray(np.kron(np.eye(HEADS_PER_SUM), np.ones((HEAD_SIZE, HEAD_SIZE))), BF16)]
    for d in dirs:
        small += [row(prm["w0"][d]), row(prm["a0"][d]), lora_pad(prm["w2"], d), lora_pad(prm["a2"], d)]
    if final:
        small += [row(prm["ln_w"]), row(prm["ln_b"]), prm["g2"].astype(BF16)]
    in_specs += [_const_spec(x.shape) for x in small]
    args += small
    tokens = jax.ShapeDtypeStruct((b, s, D_RWKV), F32)
    out_specs, out_shape = [tok(D_RWKV)], [tokens]
    if not final:
        out_specs.append(tok(D_RWKV))
        out_shape.append(tokens)
    out_specs += [state_spec] * len(dirs)
    out_shape += [state_shape] * len(dirs)
    n_chunks = tm // CHUNK
    rows = lambda: pltpu.VMEM((tm, D_RWKV), F32)
    scratch = [rows() for _ in range(4)]
    for d in dirs:
        scratch += [rows() for _ in range(4)] + [pltpu.VMEM((seqs, N_PAIRS, CHUNK, PAIR), F32),
                                                  pltpu.VMEM((n_chunks, N_PAIRS, 2 * CHUNK, PAIR), BF16),
                                                  pltpu.VMEM((n_chunks, N_PAIRS, 2 * CHUNK, PAIR), F32)]
    outs = pl.pallas_call(
        functools.partial(_scan_kernel, grid=grid, dirs=dirs, other=other is not None, n_tiles=n_tiles, tm=tm,
                          seqs=seqs),
        grid=(b, n_tiles),
        in_specs=in_specs,
        out_specs=out_specs,
        out_shape=out_shape,
        scratch_shapes=scratch,
        compiler_params=_params(("arbitrary", "arbitrary")),
        name="scan_" + "".join("fb"[d] for d in dirs),
    )(*args)
    unfold = lambda a: a.reshape((n_seq, seq_len) + a.shape[2:])
    states = {d: a.reshape((n_seq,) + a.shape[2:]) for d, a in zip(dirs, outs[-len(dirs):])}
    return tuple(unfold(a) for a in outs[:-len(dirs)]) + (states,)


def _pack_state(s):
    b = s.shape[0]
    h = jnp.swapaxes(s, -1, -2).reshape(b, N_PAIRS, 2, HEAD_SIZE, HEAD_SIZE)
    return jnp.transpose(h, (0, 1, 3, 2, 4)).reshape(b, N_PAIRS, HEAD_SIZE, PAIR)


def _unpack_state(hp):
    b = hp.shape[0]
    h = jnp.transpose(hp.reshape(b, N_PAIRS, HEAD_SIZE, 2, HEAD_SIZE), (0, 1, 3, 2, 4))
    return jnp.swapaxes(h.reshape(b, N_RWKV_HEADS, HEAD_SIZE, HEAD_SIZE), -1, -2)


def _mixer_heads(z, s0_f, s0_b, prm, grid):
    h0 = {0: _pack_state(s0_f), 1: _pack_state(s0_b)}
    if z.shape[1] < SCAN_TILE:
        y, h = _scan(z, h0, prm, (0, 1), grid)
    else:
        y_b, bonus_b, h_b = _scan(z, h0, prm, (1,), grid)
        y, h_f = _scan(z, h0, prm, (0,), grid, other=(y_b, bonus_b))
        h = {**h_f, **h_b}
    return y, _unpack_state(h[0]), _unpack_state(h[1])


def _path(x, mod, mod_per_batch, s0_f, s0_b, grid, w, prm):
    s = x.shape[1]
    x1, u, z = _front(x, mod, mod_per_batch, w["norm_g"], w["gate"], w["up"], w["down"], w["w_fold"],
                      w["w_z"], prm["mu"], TOKEN_TILE, GRID_W if grid else s, grid)
    y_four = _fourier_two_stage(u) if grid else _fourier_dense(u)
    y_rwkv, s_f, s_b = _mixer_heads(z, s0_f, s0_b, prm, grid)
    y = _back(x1, y_four, y_rwkv, mod, mod_per_batch, w["norm_g"], w["final"], w["w_out"],
              w["gate"], w["up"], w["down"], TOKEN_TILE)
    return y, s_f, s_b


def kernel(x_prompt, x_sample, state_fwd, state_bwd, c, c_ctx, w_mod, b_mod, norm_g, ffn_w_gate, ffn_w_up,
           ffn_w_down, w_in, shift_mu, decay_w0, decay_w2, iclr_a0, iclr_a2, gate_g2, k_k, k_a, r_k,
           ln_x_w, ln_x_b, w_out, final_norm):
    depth = w_mod.shape[0]
    assert depth == 1, "the back kernel applies the final norm, so exactly one layer is supported"
    bp = x_prompt.shape[0]
    bs = x_sample.shape[0]
    xp, xs = x_prompt, x_sample
    new_f, new_b = [], []
    for l in range(depth):
        cvec = jnp.zeros((8, D_MODEL), F32).at[:bs].set(c).at[bs].set(c_ctx)
        mod = _modulation(cvec, w_mod[l], b_mod[l]).reshape(8, N_MOD, D_MODEL)
        w = {"norm_g": norm_g[l], "final": final_norm,
             "gate": ffn_w_gate[l].astype(BF16), "up": ffn_w_up[l].astype(BF16), "down": ffn_w_down[l].astype(BF16),
             "w_fold": _fold_group_dft(w_in[l]), "w_z": w_in[l, :, D_FOURIER:].astype(BF16),
             "w_out": w_out[l].astype(BF16)}
        prm = {"mu": shift_mu[l], "w0": decay_w0[l], "w2": decay_w2[l], "a0": iclr_a0[l], "a2": iclr_a2[l],
               "g2": gate_g2[l], "k_k": k_k[l], "k_a": k_a[l], "r_k": r_k[l], "ln_w": ln_x_w[l], "ln_b": ln_x_b[l]}
        zero_state = jnp.zeros((bp, N_RWKV_HEADS, HEAD_SIZE, HEAD_SIZE), F32)
        xp, s_f, s_b = _path(xp, mod[bs:bs + 1], False, zero_state, zero_state, False, w, prm)
        new_f.append(s_f)
        new_b.append(s_b)
        xs, _, _ = _path(xs, mod[:bs], True, state_fwd[:, l], state_bwd[:, l], True, w, prm)
    return xp, xs, jnp.stack(new_f, axis=1), jnp.stack(new_b, axis=1)
```

```python
import functools

import numpy as np
import jax
import jax.numpy as jnp
from jax import lax
from jax.experimental import pallas as pl
from jax.experimental.pallas import tpu as pltpu

D_MODEL = 1024
GRID_W = 64
D_FOURIER = 512
N_FOURIER_GROUPS = 8
FOURIER_GROUP = D_FOURIER // N_FOURIER_GROUPS
D_RWKV = D_MODEL - D_FOURIER
HEAD_SIZE = 64
N_RWKV_HEADS = D_RWKV // HEAD_SIZE
N_PAIRS = N_RWKV_HEADS // 2
PAIR = 2 * HEAD_SIZE
DECAY_LORA = 64
ICLR_LORA = 64
GATE_LORA = 128
D_Z = 3 * D_RWKV + 2 * DECAY_LORA + 2 * ICLR_LORA + GATE_LORA
D_FOLD = 2 * D_FOURIER
D_FF = 2816
N_MOD = 9
RMS_EPS = 1e-6
GN_EPS = 64e-5
CHUNK = 64
TOKEN_TILE = 512
SCAN_TILE = 1024
SHORT_SEQS_PER_TILE = 2
OFF_WD = 3 * D_RWKV
OFF_AD = OFF_WD + 2 * DECAY_LORA
OFF_GD = OFF_AD + 2 * ICLR_LORA
DECAY_SCALE = float(np.exp(-0.5))
V7X_VMEM_LIMIT = 56 * 1024 * 1024

F32 = jnp.float32
BF16 = jnp.bfloat16


def _dot32(a, b):
    a_hi = a.astype(BF16)
    b_hi = b.astype(BF16)
    a_lo = (a - a_hi.astype(F32)).astype(BF16)
    b_lo = (b - b_hi.astype(F32)).astype(BF16)
    dot = lambda x, y: jnp.dot(x, y, preferred_element_type=F32)
    return dot(a_hi, b_hi) + (dot(a_hi, b_lo) + dot(a_lo, b_hi))


def _sigmoid(x):
    return 1.0 / (1.0 + jnp.exp(-x))


def _norm_mod(x, g, shift, scale):
    ms = jnp.mean(x * x, axis=-1, keepdims=True)
    return x * lax.rsqrt(ms + RMS_EPS) * g * (1.0 + scale) + shift


def _swiglu(h, wg_ref, wu_ref, wd_ref):
    hb = h.astype(BF16)
    gate = jnp.dot(hb, wg_ref[...], preferred_element_type=F32)
    up = jnp.dot(hb, wu_ref[...], preferred_element_type=F32)
    act = gate * _sigmoid(gate) * up
    return jnp.dot(act.astype(BF16), wd_ref[...], preferred_element_type=F32)


def _const_spec(shape):
    nd = len(shape)
    return pl.BlockSpec(shape, lambda *_: (0,) * nd, pipeline_mode=pl.Buffered(1))


def _ffn_spec(shape, which):
    return pl.BlockSpec((None,) + shape, lambda *_: (which, 0, 0), pipeline_mode=pl.Buffered(1))


def _params(sem):
    return pltpu.CompilerParams(dimension_semantics=sem, vmem_limit_bytes=V7X_VMEM_LIMIT)


def _mod_kernel(c_ref, w_ref, b_ref, o_ref):
    c = c_ref[...]
    o_ref[...] = _dot32(c * _sigmoid(c), w_ref[...]) + b_ref[...]


def _modulation(cvec, w_mod, b_mod):
    n = w_mod.shape[1]
    tn = n // 8
    return pl.pallas_call(
        _mod_kernel,
        grid=(n // tn,),
        in_specs=[pl.BlockSpec((8, D_MODEL), lambda j: (0, 0)),
                  pl.BlockSpec((D_MODEL, tn), lambda j: (0, j)),
                  pl.BlockSpec((1, tn), lambda j: (0, j))],
        out_specs=pl.BlockSpec((8, tn), lambda j: (0, j)),
        out_shape=jax.ShapeDtypeStruct((8, n), F32),
        compiler_params=_params(("arbitrary",)),
        name="modulation",
    )(cvec, w_mod, b_mod.reshape(1, n))


def _shift_adjacent(z, mu, period, grid):
    tm = z.shape[0]
    col = lax.broadcasted_iota(jnp.int32, (tm, 1), 0) % period
    lane = lax.broadcasted_iota(jnp.int32, (1, z.shape[1]), 1) % 4
    back1 = jnp.where(col == 0, 0.0, pltpu.roll(z, 1, 0))
    fwd1 = jnp.where(col == period - 1, 0.0, pltpu.roll(z, tm - 1, 0))
    if grid:
        return jnp.where(lane < 2, z + mu * (jnp.where(lane == 0, back1, fwd1) - z), z)
    return z + mu * (jnp.where(lane % 2 == 0, back1, fwd1) - z)


def _front_kernel(x_ref, mod_ref, g_ref, wg_ref, wu_ref, wd_ref, wfold_ref, wz_ref, mu_ref,
                  x1_ref, u_ref, z_ref, *, period, grid):
    x = x_ref[...]
    h = _norm_mod(x, g_ref[0:1, :], mod_ref[0:1, :], mod_ref[1:2, :])
    x1 = x + 0.5 * mod_ref[2:3, :] * _swiglu(h, wg_ref, wu_ref, wd_ref)
    x1_ref[...] = x1
    h2 = _norm_mod(x1, g_ref[1:2, :], mod_ref[3:4, :], mod_ref[4:5, :]).astype(BF16)
    u_ref[...] = jnp.dot(h2, wfold_ref[...], preferred_element_type=F32)
    z_ref[...] = _shift_adjacent(jnp.dot(h2, wz_ref[...], preferred_element_type=F32), mu_ref[...], period, grid)


def _front(x, mod, mod_per_batch, norm_g, wg, wu, wd, w_fold, w_z, mu, tm, period, grid):
    b, s, _ = x.shape
    nt = s // tm
    assert tm % period == 0 and s % tm == 0
    tok = lambda w: pl.BlockSpec((None, tm, w), lambda i, j: (i, j, 0))
    mod_map = (lambda i, j: (i, 0, 0)) if mod_per_batch else (lambda i, j: (0, 0, 0))
    return pl.pallas_call(
        functools.partial(_front_kernel, period=period, grid=grid),
        grid=(b, nt),
        in_specs=[tok(D_MODEL),
                  pl.BlockSpec((None, N_MOD, D_MODEL), mod_map),
                  _const_spec((3, D_MODEL)),
                  _ffn_spec((D_MODEL, D_FF), 0), _ffn_spec((D_MODEL, D_FF), 0),
                  _ffn_spec((D_FF, D_MODEL), 0), _const_spec((D_MODEL, D_FOLD)), _const_spec((D_MODEL, D_Z)),
                  _const_spec((1, D_Z))],
        out_specs=[tok(D_MODEL), tok(D_FOLD), tok(D_Z)],
        out_shape=[jax.ShapeDtypeStruct((b, s, D_MODEL), F32),
                   jax.ShapeDtypeStruct((b, s, D_FOLD), F32),
                   jax.ShapeDtypeStruct((b, s, D_Z), F32)],
        compiler_params=_params(("arbitrary", "arbitrary")),
        name="front",
    )(x, mod, norm_g, wg, wu, wd, w_fold, w_z, mu.reshape(1, D_Z))


def _back_kernel(x1_ref, yf_ref, yr_ref, mod_ref, g_ref, fin_ref, wo_ref, wg_ref, wu_ref, wd_ref, o_ref):
    mixed = (jnp.dot(yf_ref[...].astype(BF16), wo_ref[0:D_FOURIER, :], preferred_element_type=F32)
             + jnp.dot(yr_ref[...].astype(BF16), wo_ref[D_FOURIER:, :], preferred_element_type=F32))
    x2 = x1_ref[...] + mod_ref[5:6, :] * mixed
    h = _norm_mod(x2, g_ref[2:3, :], mod_ref[6:7, :], mod_ref[7:8, :])
    x3 = x2 + 0.5 * mod_ref[8:9, :] * _swiglu(h, wg_ref, wu_ref, wd_ref)
    ms = jnp.mean(x3 * x3, axis=-1, keepdims=True)
    o_ref[...] = x3 * lax.rsqrt(ms + RMS_EPS) * fin_ref[...]


def _back(x1, yf, yr, mod, mod_per_batch, norm_g, final_norm, w_out, wg, wu, wd, tm):
    b, s, _ = x1.shape
    nt = s // tm
    tok = lambda w: pl.BlockSpec((None, tm, w), lambda i, j: (i, j, 0))
    mod_map = (lambda i, j: (i, 0, 0)) if mod_per_batch else (lambda i, j: (0, 0, 0))
    return pl.pallas_call(
        _back_kernel,
        grid=(b, nt),
        in_specs=[tok(D_MODEL), tok(D_FOURIER), tok(D_RWKV),
                  pl.BlockSpec((None, N_MOD, D_MODEL), mod_map),
                  _const_spec((3, D_MODEL)), _const_spec((1, D_MODEL)),
                  _const_spec((D_MODEL, D_MODEL)),
                  _ffn_spec((D_MODEL, D_FF), 1), _ffn_spec((D_MODEL, D_FF), 1),
                  _ffn_spec((D_FF, D_MODEL), 1)],
        out_specs=tok(D_MODEL),
        out_shape=jax.ShapeDtypeStruct((b, s, D_MODEL), F32),
        compiler_params=_params(("arbitrary", "arbitrary")),
        name="back",
    )(x1, yf, yr, mod, norm_g, final_norm.reshape(1, D_MODEL), w_out, wg, wu, wd)


def _fold_kernel(w_ref, cs_ref, o_ref):
    o_ref[...] = _dot32(w_ref[...], cs_ref[...]).astype(o_ref.dtype)


def _fold_group_dft(w_in):
    q = np.arange(FOURIER_GROUP)
    ang = 2.0 * np.pi * ((q[:, None] * q[None, :]) % FOURIER_GROUP) / FOURIER_GROUP
    eye = np.eye(N_FOURIER_GROUPS)
    cs = jnp.asarray(np.concatenate([np.kron(eye, np.cos(ang)), np.kron(eye, np.sin(ang))], axis=1), F32)
    return pl.pallas_call(
        _fold_kernel,
        grid=(1,),
        in_specs=[pl.BlockSpec((D_MODEL, D_FOURIER), lambda i: (0, 0)),
                  pl.BlockSpec((D_FOURIER, D_FOLD), lambda i: (0, 0))],
        out_specs=pl.BlockSpec((D_MODEL, D_FOLD), lambda i: (0, 0)),
        out_shape=jax.ShapeDtypeStruct((D_MODEL, D_FOLD), BF16),
        compiler_params=_params(("arbitrary",)),
        name="fold_group_dft",
    )(w_in, cs)


def _stack_cos_sin(x):
    return jnp.concatenate([x[:, :D_FOURIER], x[:, D_FOURIER:]], axis=0).astype(BF16)


DENSE_DFT_SEQS = 8


def _fourier_dense_kernel(x_ref, p_ref, o_ref):
    for i in range(x_ref.shape[0]):
        o_ref[i] = _bdot(p_ref[...], _stack_cos_sin(x_ref[i]))


def _fourier_dense(xcs):
    b, s, _ = xcs.shape
    pos = np.arange(s)
    ang = 2.0 * np.pi * ((pos[:, None] * pos[None, :]) % s) / s
    table = np.concatenate([np.cos(ang), -np.sin(ang)], axis=1) / np.sqrt(s * FOURIER_GROUP)
    per_step = DENSE_DFT_SEQS if b % DENSE_DFT_SEQS == 0 else 1
    return pl.pallas_call(
        _fourier_dense_kernel,
        grid=(b // per_step,),
        in_specs=[pl.BlockSpec((per_step, s, D_FOLD), lambda i: (i, 0, 0)), _const_spec((s, 2 * s))],
        out_specs=pl.BlockSpec((per_step, s, D_FOURIER), lambda i: (i, 0, 0)),
        out_shape=jax.ShapeDtypeStruct((b, s, D_FOURIER), F32),
        compiler_params=_params(("arbitrary",)),
        name="fourier_dense",
    )(xcs, jnp.asarray(table, F32).astype(BF16))


FFT_ROWS = 16


def _fourier_stage1_kernel(x_ref, f_ref, z_ref):
    for j in range(FFT_ROWS):
        z = _bdot(f_ref[...], _stack_cos_sin(x_ref[:, j, :]))
        n = z.shape[0] // 2
        z_ref[:, j, :] = jnp.concatenate([z[:n], z[n:]], axis=1)


def _fourier_stage2_kernel(z_ref, g_ref, o_ref):
    for j in range(FFT_ROWS):
        o_ref[:, j, :] = _bdot(g_ref[j], _stack_cos_sin(z_ref[j]))


def _fourier_two_stage(xcs):
    b, s, _ = xcs.shape
    n = int(round(np.sqrt(s)))
    assert n * n == s and n % FFT_ROWS == 0
    idx = np.arange(n)
    ang = 2.0 * np.pi * ((idx[:, None] * idx[None, :]) % n) / n
    fc, fs = np.cos(ang), np.sin(ang)
    f2 = jnp.asarray(np.block([[fc, -fs], [fs, fc]]), F32).astype(BF16)
    z = pl.pallas_call(
        _fourier_stage1_kernel,
        grid=(b, n // FFT_ROWS),
        in_specs=[pl.BlockSpec((None, n, FFT_ROWS, D_FOLD), lambda i, j: (i, 0, j, 0)),
                  _const_spec((2 * n, 2 * n))],
        out_specs=pl.BlockSpec((None, n, FFT_ROWS, D_FOLD), lambda i, j: (i, 0, j, 0)),
        out_shape=jax.ShapeDtypeStruct((b, n, n, D_FOLD), F32),
        compiler_params=_params(("arbitrary", "arbitrary")),
        name="fourier_stage1",
    )(xcs.reshape(b, n, n, D_FOLD), f2)
    bb, aa, s1 = idx[:, None, None], idx[None, :, None], idx[None, None, :]
    ang2 = 2.0 * np.pi * ((s1 * (n * aa + bb)) % s) / s
    g2 = np.concatenate([np.cos(ang2), -np.sin(ang2)], axis=2) / np.sqrt(s * FOURIER_GROUP)
    out = pl.pallas_call(
        _fourier_stage2_kernel,
        grid=(b, n // FFT_ROWS),
        in_specs=[pl.BlockSpec((None, FFT_ROWS, n, D_FOLD), lambda i, j: (i, j, 0, 0)),
                  pl.BlockSpec((FFT_ROWS, n, 2 * n), lambda i, j: (j, 0, 0))],
        out_specs=pl.BlockSpec((None, n, FFT_ROWS, D_FOURIER), lambda i, j: (i, 0, j, 0)),
        out_shape=jax.ShapeDtypeStruct((b, n, n, D_FOURIER), F32),
        compiler_params=_params(("arbitrary", "arbitrary")),
        name="fourier_stage2",
    )(z, jnp.asarray(g2, F32).astype(BF16))
    return out.reshape(b, s, D_FOURIER)


def _shift_rows(z, up, down, mu):
    lane = lax.broadcasted_iota(jnp.int32, (1, z.shape[1]), 1) % 4
    return z + jnp.where(lane < 2, 0.0, mu) * (jnp.where(lane == 2, up, down) - z)


def _stack_masked(x, head0):
    return jnp.concatenate([jnp.where(head0, x, 0.0), jnp.where(head0, 0.0, x)], axis=0)


NT = (((1,), (1,)), ((), ()))
CUM_PIECES = 2
CHUNKS_PER_STEP = 4


def _split(x, pieces):
    out = []
    for i in range(pieces):
        hi = x.astype(BF16)
        out.append(hi)
        if i + 1 < pieces:
            x = x - hi.astype(F32)
    return out


def _bdot(a, b, dims=None):
    if dims is None:
        return jnp.dot(a, b, preferred_element_type=F32)
    return lax.dot_general(a, b, dims, preferred_element_type=F32)


HEADS_PER_SUM = 4


def _head_sums(x, ones_ref):
    w = HEADS_PER_SUM * HEAD_SIZE
    return jnp.concatenate([_bdot(x[:, i:i + w].astype(BF16), ones_ref[...]) for i in range(0, D_RWKV, w)], axis=1)


def _scan_kernel(*refs, grid, dirs, other, n_tiles, tm, seqs):
    final = other or len(dirs) == 2
    it = iter(refs)
    z_ref = next(it)
    zp_ref = next(it) if grid else None
    zn_ref = next(it) if grid else None
    h0_refs = {d: next(it) for d in dirs}
    if other:
        yo_ref, bo_ref = next(it), next(it)
    mu_ref, kk_ref, ka_ref, rk_ref, ones_ref = (next(it) for _ in range(5))
    lora = {d: tuple(next(it) for _ in range(4)) for d in dirs}
    if final:
        lnw_ref, lnb_ref, g2_ref = next(it), next(it), next(it)
    y_ref = next(it)
    b_ref = None if final else next(it)
    hout_refs = {d: next(it) for d in dirs}
    r_s, v_s, al_s, g_s = (next(it) for _ in range(4))
    per_dir = {d: tuple(next(it) for _ in range(7)) for d in dirs}

    step = pl.program_id(1)
    tile = (n_tiles - 1 - step) if dirs == (1,) else step

    @pl.when(step == 0)
    def _():
        for d in dirs:
            per_dir[d][4][...] = h0_refs[d][...]

    def neighbour_rows(lo, hi):
        parts = []
        if lo < 0:
            parts.append(jnp.where(tile == 0, 0.0, zp_ref[GRID_W + lo:, :]))
        parts.append(z_ref[max(lo, 0):min(hi, tm), :])
        if hi > tm:
            parts.append(jnp.where(tile == n_tiles - 1, 0.0, zn_ref[:hi - tm, :]))
        return parts[0] if len(parts) == 1 else jnp.concatenate(parts, axis=0)

    def prepare(rows):
        z = z_ref[rows, :]
        if grid:
            z = _shift_rows(z, neighbour_rows(rows.start - GRID_W, rows.stop - GRID_W),
                            neighbour_rows(rows.start + GRID_W, rows.stop + GRID_W), mu_ref[...])
        r = z[:, 0:D_RWKV]
        k = z[:, D_RWKV:2 * D_RWKV]
        v = z[:, 2 * D_RWKV:3 * D_RWKV]
        kk = k * kk_ref[...]
        kk = kk * lax.rsqrt(jnp.maximum(_head_sums(kk * kk, ones_ref), 1e-24))
        r_s[rows, :] = r
        v_s[rows, :] = v
        al_s[rows, :] = -kk
        bonus = bo_ref[rows, :] if other else None
        for d in dirs:
            w0_ref, a0_ref, w2_ref, a2_ref = lora[d]
            be_s, kd_s, lw_s = per_dir[d][:3]
            w_raw = w0_ref[...] + _bdot(jnp.tanh(z[:, OFF_WD:OFF_AD]).astype(BF16), w2_ref[...])
            a = _sigmoid(a0_ref[...] + _bdot(z[:, OFF_AD:OFF_GD].astype(BF16), a2_ref[...]))
            kd = k * (1.0 + (a - 1.0) * ka_ref[...])
            bonus_d = _head_sums(r * kd * rk_ref[...], ones_ref) * v
            bonus = bonus_d if bonus is None else bonus + bonus_d
            be_s[rows, :] = a * kk
            kd_s[rows, :] = kd
            lw_s[rows, :] = -DECAY_SCALE * _sigmoid(w_raw)
        if final:
            y_ref[rows, :] = bonus
            g_s[rows, :] = _bdot(_sigmoid(z[:, OFF_GD:]).astype(BF16), g2_ref[...])
        else:
            b_ref[rows, :] = bonus

    def finish(rows):
        y_sum = yo_ref[rows, :] if other else None
        for d in dirs:
            y_d = per_dir[d][3][rows, :]
            y_sum = y_d if y_sum is None else y_sum + y_d
        if final:
            mean = _head_sums(y_sum, ones_ref) * (1.0 / HEAD_SIZE)
            cen = y_sum - mean
            var = _head_sums(cen * cen, ones_ref) * (1.0 / HEAD_SIZE)
            y_gn = cen * lax.rsqrt(var + GN_EPS) * lnw_ref[...] + lnb_ref[...]
            y_ref[rows, :] = (y_gn + y_ref[rows, :]) * g_s[rows, :]
        else:
            y_ref[rows, :] = y_sum

    n_chunks = tm // CHUNK
    per_seq = n_chunks // seqs
    assert per_seq % CHUNKS_PER_STEP == 0, "a group of chunks must not straddle two sequences"
    ri = lax.broadcasted_iota(jnp.int32, (CHUNK, CHUNK), 0)
    ci = lax.broadcasted_iota(jnp.int32, (CHUNK, CHUNK), 1)
    rj = lax.broadcasted_iota(jnp.int32, (CHUNK, PAIR), 0)
    cj = lax.broadcasted_iota(jnp.int32, (CHUNK, PAIR), 1) % CHUNK
    eye = rj == cj
    tri = {0: (ci <= ri).astype(BF16), 1: (ci >= ri).astype(BF16)}
    strict = {0: cj < rj, 1: cj > rj}
    incl = {0: cj <= rj, 1: cj >= rj}
    last = {0: CHUNK - 1, 1: 0}
    head0 = lax.broadcasted_iota(jnp.int32, (1, PAIR), 1) < HEAD_SIZE
    blockdiag = lambda x: _stack_masked(x, head0)

    def head_transpose(x):
        t = blockdiag(x).T
        return t[:CHUNK] + t[CHUNK:]

    same_block = lambda n: (rj & -n) == (cj & -n)
    levels = [2 ** i for i in range(1, int(np.log2(CHUNK)))]
    base_mask = same_block(2)
    pair_masks = {n: same_block(2 * n) & ~same_block(n) for n in levels}

    def chunk_rows(c, d):
        seq, k = divmod(c, per_seq)
        off = (seq * per_seq + ((per_seq - 1 - k) if d == 1 else k)) * CHUNK
        return slice(off, off + CHUNK)

    def local_phase(chunks, d):
        be_s, kd_s, lw_s, _, _, qm_s, yn_s = per_dir[d]
        units = []
        for c in chunks:
            rows = chunk_rows(c, d)
            lw = lw_s[rows, :]
            cum = sum(_bdot(tri[d], piece) for piece in _split(lw, CUM_PIECES))
            tot = cum[last[d]:last[d] + 1, :]
            g_inv = jnp.exp(-cum)
            g_tot = jnp.exp(tot)
            g_end = g_tot * g_inv
            abar = al_s[rows, :] * jnp.exp(cum - lw)
            rbar = r_s[rows, :] * jnp.exp(cum)
            be = be_s[rows, :]
            kdc = kd_s[rows, :]
            btil, ktil, bhat, khat = be * g_inv, kdc * g_inv, be * g_end, kdc * g_end
            vc = v_s[rows, :]
            for p in range(N_PAIRS):
                lanes = slice(p * PAIR, (p + 1) * PAIR)
                cut = lambda x: x[:, lanes]
                units.append(dict(c=c, p=p, a=cut(abar).astype(BF16), r=cut(rbar), v=cut(vc).astype(BF16),
                                  bt=cut(btil).astype(BF16), kt=cut(ktil).astype(BF16),
                                  bh=cut(bhat), kh=cut(khat), g_tot=cut(g_tot)))
        for u in units:
            ar = jnp.concatenate([u["a"], u["r"].astype(BF16)], axis=0)
            u["gb"] = _bdot(ar, blockdiag(u["bt"]), NT)
            u["gk"] = _bdot(ar, blockdiag(u["kt"]), NT)
        for u in units:
            gb, gk = u.pop("gb"), u.pop("gk")
            u["l_ab"] = jnp.where(strict[d], gb[:CHUNK], 0.0)
            u["l_ak"] = jnp.where(strict[d], gk[:CHUNK], 0.0).astype(BF16)
            u["l_rbk"] = jnp.concatenate([jnp.where(incl[d], gb[CHUNK:], 0.0),
                                          jnp.where(incl[d], gk[CHUNK:], 0.0)], axis=1).astype(BF16)
            u["t"] = jnp.where(eye, 1.0, jnp.where(base_mask, u["l_ab"], 0.0))
        for n in levels:
            for u in units:
                e = jnp.where(pair_masks[n], u["l_ab"], 0.0).astype(BF16)
                u["et"] = _bdot(e, blockdiag(u["t"].astype(BF16)))
            for u in units:
                u["t"] = u["t"] + _bdot(u["t"].astype(BF16), blockdiag(u["et"].astype(BF16)))
        for u in units:
            u["x1"] = _bdot(u["l_ak"], blockdiag(u["v"]))
        for u in units:
            rhs = jnp.concatenate([blockdiag(u["a"]), blockdiag(u["x1"].astype(BF16))], axis=1)
            u["pu"] = _bdot(u["t"].astype(BF16), rhs)
        for u in units:
            pu = u["pu"].astype(BF16)
            v_bd = blockdiag(u["v"])
            rhs = jnp.concatenate(
                [jnp.concatenate([blockdiag(pu[:, :PAIR]), blockdiag(pu[:, PAIR:])], axis=1),
                 jnp.concatenate([jnp.zeros_like(v_bd), v_bd], axis=1)], axis=0)
            bkh_t = jnp.concatenate([head_transpose(u["bh"]), head_transpose(u["kh"])], axis=1)
            o = _bdot(jnp.concatenate([u["l_rbk"], bkh_t.astype(BF16)], axis=0), rhs)
            q = u["r"] + o[:CHUNK, :PAIR]
            m = jnp.where(eye, u["g_tot"], 0.0) + o[CHUNK:, :PAIR]
            qm_s[u["c"], u["p"]] = jnp.concatenate([q, m], axis=0).astype(BF16)
            yn_s[u["c"], u["p"]] = o[:, PAIR:]

    def state_phase(chunks, d):
        _, _, _, y_s, h_s, qm_s, yn_s = per_dir[d]
        for c in chunks:
            seq = c // per_seq
            for p in range(N_PAIRS):
                oh = _bdot(qm_s[c, p], blockdiag(h_s[seq, p].astype(BF16))) + yn_s[c, p]
                y_s[chunk_rows(c, d), p * PAIR:(p + 1) * PAIR] = oh[:CHUNK]
                h_s[seq, p] = oh[CHUNK:]

    spans = [list(range(g, g + CHUNKS_PER_STEP)) for g in range(0, n_chunks, CHUNKS_PER_STEP)]
    groups = [(chunks, d) for chunks in spans for d in dirs]

    def group_rows(chunks, d):
        spans = [chunk_rows(c, d) for c in chunks]
        return slice(min(r.start for r in spans), max(r.stop for r in spans))

    for i, (chunks, d) in enumerate(groups):
        if d == dirs[0]:
            prepare(group_rows(chunks, d))
        local_phase(chunks, d)
        if i > 0:
            prev_chunks, prev_d = groups[i - 1]
            state_phase(prev_chunks, prev_d)
            if prev_d == dirs[-1]:
                finish(group_rows(prev_chunks, prev_d))
    state_phase(*groups[-1])
    finish(group_rows(*groups[-1]))

    @pl.when(step == n_tiles - 1)
    def _():
        for d in dirs:
            hout_refs[d][...] = per_dir[d][4][...]


def _scan(z, h0, prm, dirs, grid, other=None):
    n_seq, seq_len, _ = z.shape
    tm = SCAN_TILE if seq_len >= SCAN_TILE else SHORT_SEQS_PER_TILE * seq_len
    seqs = max(tm // seq_len, 1)
    assert (seq_len % tm == 0) if seqs == 1 else (tm == seqs * seq_len and n_seq % seqs == 0 and not grid)
    b, s = n_seq // seqs, seq_len * seqs
    n_tiles = s // tm
    assert len(dirs) == 1 or n_tiles == 1, "both directions in one call need the whole sequence in a tile"
    final = other is not None or len(dirs) == 2
    fold = lambda a: a.reshape((b, seqs * a.shape[1]) + a.shape[2:])
    tile_of = (lambda j: n_tiles - 1 - j) if dirs == (1,) else (lambda j: j)
    tok = lambda w: pl.BlockSpec((None, tm, w), lambda i, j: (i, tile_of(j), 0))
    in_specs, args = [tok(D_Z)], [fold(z)]
    if grid:
        halo_per_tile, n_halo = tm // GRID_W, s // GRID_W
        in_specs += [pl.BlockSpec((None, GRID_W, D_Z),
                                  lambda i, j: (i, jnp.maximum(tile_of(j) * halo_per_tile - 1, 0), 0)),
                     pl.BlockSpec((None, GRID_W, D_Z),
                                  lambda i, j: (i, jnp.minimum((tile_of(j) + 1) * halo_per_tile, n_halo - 1), 0))]
        args += [args[0], args[0]]
    state_spec = pl.BlockSpec((None, seqs, N_PAIRS, HEAD_SIZE, PAIR), lambda i, j: (i, 0, 0, 0, 0))
    state_shape = jax.ShapeDtypeStruct((b, seqs, N_PAIRS, HEAD_SIZE, PAIR), F32)
    for d in dirs:
        in_specs.append(state_spec)
        args.append(h0[d].reshape(state_shape.shape))
    if other is not None:
        in_specs += [tok(D_RWKV), tok(D_RWKV)]
        args += [fold(a) for a in other]
    row = lambda x: x.reshape(1, -1)
    lora_pad = lambda w, d: jnp.zeros((2 * w.shape[1], D_RWKV), BF16).at[d * w.shape[1]:(d + 1) * w.shape[1]].set(
        w[d].astype(BF16))
    small = [row(prm["mu"]), row(prm["k_k"]), row(prm["k_a"]), row(prm["r_k"]),
             jnp.asarray(np.kron(np.eye(HEADS_PER_SUM), np.ones((HEAD_SIZE, HEAD_SIZE))), BF16)]
    for d in dirs:
        small += [row(prm["w0"][d]), row(prm["a0"][d]), lora_pad(prm["w2"], d), lora_pad(prm["a2"], d)]
    if final:
        small += [row(prm["ln_w"]), row(prm["ln_b"]), prm["g2"].astype(BF16)]
    in_specs += [_const_spec(x.shape) for x in small]
    args += small
    tokens = jax.ShapeDtypeStruct((b, s, D_RWKV), F32)
    out_specs, out_shape = [tok(D_RWKV)], [tokens]
    if not final:
        out_specs.append(tok(D_RWKV))
        out_shape.append(tokens)
    out_specs += [state_spec] * len(dirs)
    out_shape += [state_shape] * len(dirs)
    n_chunks = tm // CHUNK
    rows = lambda: pltpu.VMEM((tm, D_RWKV), F32)
    scratch = [rows() for _ in range(4)]
    for d in dirs:
        scratch += [rows() for _ in range(4)] + [pltpu.VMEM((seqs, N_PAIRS, CHUNK, PAIR), F32),
                                                  pltpu.VMEM((n_chunks, N_PAIRS, 2 * CHUNK, PAIR), BF16),
                                                  pltpu.VMEM((n_chunks, N_PAIRS, 2 * CHUNK, PAIR), F32)]
    outs = pl.pallas_call(
        functools.partial(_scan_kernel, grid=grid, dirs=dirs, other=other is not None, n_tiles=n_tiles, tm=tm,
                          seqs=seqs),
        grid=(b, n_tiles),
        in_specs=in_specs,
        out_specs=out_specs,
        out_shape=out_shape,
        scratch_shapes=scratch,
        compiler_params=_params(("arbitrary", "arbitrary")),
        name="scan_" + "".join("fb"[d] for d in dirs),
    )(*args)
    unfold = lambda a: a.reshape((n_seq, seq_len) + a.shape[2:])
    states = {d: a.reshape((n_seq,) + a.shape[2:]) for d, a in zip(dirs, outs[-len(dirs):])}
    return tuple(unfold(a) for a in outs[:-len(dirs)]) + (states,)


def _pack_state(s):
    b = s.shape[0]
    h = jnp.swapaxes(s, -1, -2).reshape(b, N_PAIRS, 2, HEAD_SIZE, HEAD_SIZE)
    return jnp.transpose(h, (0, 1, 3, 2, 4)).reshape(b, N_PAIRS, HEAD_SIZE, PAIR)


def _unpack_state(hp):
    b = hp.shape[0]
    h = jnp.transpose(hp.reshape(b, N_PAIRS, HEAD_SIZE, 2, HEAD_SIZE), (0, 1, 3, 2, 4))
    return jnp.swapaxes(h.reshape(b, N_RWKV_HEADS, HEAD_SIZE, HEAD_SIZE), -1, -2)


def _mixer_heads(z, s0_f, s0_b, prm, grid):
    h0 = {0: _pack_state(s0_f), 1: _pack_state(s0_b)}
    if z.shape[1] < SCAN_TILE:
        y, h = _scan(z, h0, prm, (0, 1), grid)
    else:
        y_b, bonus_b, h_b = _scan(z, h0, prm, (1,), grid)
        y, h_f = _scan(z, h0, prm, (0,), grid, other=(y_b, bonus_b))
        h = {**h_f, **h_b}
    return y, _unpack_state(h[0]), _unpack_state(h[1])


def _path(x, mod, mod_per_batch, s0_f, s0_b, grid, w, prm):
    b, s, _ = x.shape
    rows = (b, s) if mod_per_batch else (1, b * s)
    as_rows = lambda a: a.reshape(rows + a.shape[2:])
    as_seqs = lambda a: a.reshape((b, s) + a.shape[2:])
    x1, u, z = _front(as_rows(x), mod, mod_per_batch, w["norm_g"], w["gate"], w["up"], w["down"], w["w_fold"],
                      w["w_z"], prm["mu"], TOKEN_TILE, GRID_W if grid else s, grid)
    y_four = _fourier_two_stage(as_seqs(u)) if grid else _fourier_dense(as_seqs(u))
    y_rwkv, s_f, s_b = _mixer_heads(as_seqs(z), s0_f, s0_b, prm, grid)
    y = _back(x1, as_rows(y_four), as_rows(y_rwkv), mod, mod_per_batch, w["norm_g"], w["final"], w["w_out"],
              w["gate"], w["up"], w["down"], TOKEN_TILE)
    return as_seqs(y), s_f, s_b


def kernel(x_prompt, x_sample, state_fwd, state_bwd, c, c_ctx, w_mod, b_mod, norm_g, ffn_w_gate, ffn_w_up,
           ffn_w_down, w_in, shift_mu, decay_w0, decay_w2, iclr_a0, iclr_a2, gate_g2, k_k, k_a, r_k,
           ln_x_w, ln_x_b, w_out, final_norm):
    depth = w_mod.shape[0]
    assert depth == 1, "the back kernel applies the final norm, so exactly one layer is supported"
    bp = x_prompt.shape[0]
    bs = x_sample.shape[0]
    xp, xs = x_prompt, x_sample
    new_f, new_b = [], []
    for l in range(depth):
        cvec = jnp.zeros((8, D_MODEL), F32).at[:bs].set(c).at[bs].set(c_ctx)
        mod = _modulation(cvec, w_mod[l], b_mod[l]).reshape(8, N_MOD, D_MODEL)
        w = {"norm_g": norm_g[l], "final": final_norm,
             "gate": ffn_w_gate[l].astype(BF16), "up": ffn_w_up[l].astype(BF16), "down": ffn_w_down[l].astype(BF16),
             "w_fold": _fold_group_dft(w_in[l]), "w_z": w_in[l, :, D_FOURIER:].astype(BF16),
             "w_out": w_out[l].astype(BF16)}
        prm = {"mu": shift_mu[l], "w0": decay_w0[l], "w2": decay_w2[l], "a0": iclr_a0[l], "a2": iclr_a2[l],
               "g2": gate_g2[l], "k_k": k_k[l], "k_a": k_a[l], "r_k": r_k[l], "ln_w": ln_x_w[l], "ln_b": ln_x_b[l]}
        zero_state = jnp.zeros((bp, N_RWKV_HEADS, HEAD_SIZE, HEAD_SIZE), F32)
        xp, s_f, s_b = _path(xp, mod[bs:bs + 1], False, zero_state, zero_state, False, w, prm)
        new_f.append(s_f)
        new_b.append(s_b)
        xs, _, _ = _path(xs, mod[:bs], True, state_fwd[:, l], state_bwd[:, l], True, w, prm)
    return xp, xs, jnp.stack(new_f, axis=1), jnp.stack(new_b, axis=1)
```

```python
import functools

import numpy as np
import jax
import jax.numpy as jnp
from jax import lax
from jax.experimental import pallas as pl
from jax.experimental.pallas import tpu as pltpu

D_MODEL = 1024
GRID_W = 64
D_FOURIER = 512
N_FOURIER_GROUPS = 8
FOURIER_GROUP = D_FOURIER // N_FOURIER_GROUPS
D_RWKV = D_MODEL - D_FOURIER
HEAD_SIZE = 64
N_RWKV_HEADS = D_RWKV // HEAD_SIZE
N_PAIRS = N_RWKV_HEADS // 2
PAIR = 2 * HEAD_SIZE
DECAY_LORA = 64
ICLR_LORA = 64
GATE_LORA = 128
D_Z = 3 * D_RWKV + 2 * DECAY_LORA + 2 * ICLR_LORA + GATE_LORA
D_FOLD = 2 * D_FOURIER
D_FF = 2816
N_MOD = 9
RMS_EPS = 1e-6
GN_EPS = 64e-5
CHUNK = 64
TOKEN_TILE = 512
SCAN_TILE = 1024
SHORT_SEQS_PER_TILE = 2
OFF_WD = 3 * D_RWKV
OFF_AD = OFF_WD + 2 * DECAY_LORA
OFF_GD = OFF_AD + 2 * ICLR_LORA
DECAY_SCALE = float(np.exp(-0.5))
V7X_VMEM_LIMIT = 56 * 1024 * 1024

F32 = jnp.float32
BF16 = jnp.bfloat16


def _dot32(a, b):
    a_hi = a.astype(BF16)
    b_hi = b.astype(BF16)
    a_lo = (a - a_hi.astype(F32)).astype(BF16)
    b_lo = (b - b_hi.astype(F32)).astype(BF16)
    dot = lambda x, y: jnp.dot(x, y, preferred_element_type=F32)
    return dot(a_hi, b_hi) + (dot(a_hi, b_lo) + dot(a_lo, b_hi))


def _sigmoid(x):
    return 1.0 / (1.0 + jnp.exp(-x))


def _norm_mod(x, g, shift, scale):
    ms = jnp.mean(x * x, axis=-1, keepdims=True)
    return x * lax.rsqrt(ms + RMS_EPS) * g * (1.0 + scale) + shift


def _swiglu(h, wg_ref, wu_ref, wd_ref):
    hb = h.astype(BF16)
    gate = jnp.dot(hb, wg_ref[...], preferred_element_type=F32)
    up = jnp.dot(hb, wu_ref[...], preferred_element_type=F32)
    act = gate * _sigmoid(gate) * up
    return jnp.dot(act.astype(BF16), wd_ref[...], preferred_element_type=F32)


def _const_spec(shape):
    nd = len(shape)
    return pl.BlockSpec(shape, lambda *_: (0,) * nd, pipeline_mode=pl.Buffered(1))


def _ffn_spec(shape, which):
    return pl.BlockSpec((None,) + shape, lambda *_: (which, 0, 0), pipeline_mode=pl.Buffered(1))


def _params(sem):
    return pltpu.CompilerParams(dimension_semantics=sem, vmem_limit_bytes=V7X_VMEM_LIMIT)


def _mod_kernel(c_ref, w_ref, b_ref, o_ref):
    c = c_ref[...]
    o_ref[...] = _dot32(c * _sigmoid(c), w_ref[...]) + b_ref[...]


def _modulation(cvec, w_mod, b_mod):
    n = w_mod.shape[1]
    tn = n // 8
    return pl.pallas_call(
        _mod_kernel,
        grid=(n // tn,),
        in_specs=[pl.BlockSpec((8, D_MODEL), lambda j: (0, 0)),
                  pl.BlockSpec((D_MODEL, tn), lambda j: (0, j)),
                  pl.BlockSpec((1, tn), lambda j: (0, j))],
        out_specs=pl.BlockSpec((8, tn), lambda j: (0, j)),
        out_shape=jax.ShapeDtypeStruct((8, n), F32),
        compiler_params=_params(("arbitrary",)),
        name="modulation",
    )(cvec, w_mod, b_mod.reshape(1, n))


def _shift_adjacent(z, mu, period, grid):
    tm = z.shape[0]
    col = lax.broadcasted_iota(jnp.int32, (tm, 1), 0) % period
    lane = lax.broadcasted_iota(jnp.int32, (1, z.shape[1]), 1) % 4
    back1 = jnp.where(col == 0, 0.0, pltpu.roll(z, 1, 0))
    fwd1 = jnp.where(col == period - 1, 0.0, pltpu.roll(z, tm - 1, 0))
    if grid:
        return jnp.where(lane < 2, z + mu * (jnp.where(lane == 0, back1, fwd1) - z), z)
    return z + mu * (jnp.where(lane % 2 == 0, back1, fwd1) - z)


def _front_kernel(x_ref, mod_ref, g_ref, wg_ref, wu_ref, wd_ref, wfold_ref, wz_ref, mu_ref,
                  x1_ref, u_ref, z_ref, *, period, grid):
    x = x_ref[...]
    h = _norm_mod(x, g_ref[0:1, :], mod_ref[0:1, :], mod_ref[1:2, :])
    x1 = x + 0.5 * mod_ref[2:3, :] * _swiglu(h, wg_ref, wu_ref, wd_ref)
    x1_ref[...] = x1
    h2 = _norm_mod(x1, g_ref[1:2, :], mod_ref[3:4, :], mod_ref[4:5, :]).astype(BF16)
    u_ref[...] = jnp.dot(h2, wfold_ref[...], preferred_element_type=F32)
    z_ref[...] = _shift_adjacent(jnp.dot(h2, wz_ref[...], preferred_element_type=F32), mu_ref[...], period, grid)


def _front(x, mod, mod_per_batch, norm_g, wg, wu, wd, w_fold, w_z, mu, tm, period, grid):
    b, s, _ = x.shape
    nt = s // tm
    assert tm % period == 0 and s % tm == 0
    tok = lambda w: pl.BlockSpec((None, tm, w), lambda i, j: (i, j, 0))
    mod_map = (lambda i, j: (i, 0, 0)) if mod_per_batch else (lambda i, j: (0, 0, 0))
    return pl.pallas_call(
        functools.partial(_front_kernel, period=period, grid=grid),
        grid=(b, nt),
        in_specs=[tok(D_MODEL),
                  pl.BlockSpec((None, N_MOD, D_MODEL), mod_map),
                  _const_spec((3, D_MODEL)),
                  _ffn_spec((D_MODEL, D_FF), 0), _ffn_spec((D_MODEL, D_FF), 0),
                  _ffn_spec((D_FF, D_MODEL), 0), _const_spec((D_MODEL, D_FOLD)), _const_spec((D_MODEL, D_Z)),
                  _const_spec((1, D_Z))],
        out_specs=[tok(D_MODEL), tok(D_FOLD), tok(D_Z)],
        out_shape=[jax.ShapeDtypeStruct((b, s, D_MODEL), F32),
                   jax.ShapeDtypeStruct((b, s, D_FOLD), F32),
                   jax.ShapeDtypeStruct((b, s, D_Z), F32)],
        compiler_params=_params(("arbitrary", "arbitrary")),
        name="front",
    )(x, mod, norm_g, wg, wu, wd, w_fold, w_z, mu.reshape(1, D_Z))


def _back_kernel(x1_ref, yf_ref, yr_ref, mod_ref, g_ref, fin_ref, wo_ref, wg_ref, wu_ref, wd_ref, o_ref):
    mixed = (jnp.dot(yf_ref[...].astype(BF16), wo_ref[0:D_FOURIER, :], preferred_element_type=F32)
             + jnp.dot(yr_ref[...].astype(BF16), wo_ref[D_FOURIER:, :], preferred_element_type=F32))
    x2 = x1_ref[...] + mod_ref[5:6, :] * mixed
    h = _norm_mod(x2, g_ref[2:3, :], mod_ref[6:7, :], mod_ref[7:8, :])
    x3 = x2 + 0.5 * mod_ref[8:9, :] * _swiglu(h, wg_ref, wu_ref, wd_ref)
    ms = jnp.mean(x3 * x3, axis=-1, keepdims=True)
    o_ref[...] = x3 * lax.rsqrt(ms + RMS_EPS) * fin_ref[...]


def _back(x1, yf, yr, mod, mod_per_batch, norm_g, final_norm, w_out, wg, wu, wd, tm):
    b, s, _ = x1.shape
    nt = s // tm
    tok = lambda w: pl.BlockSpec((None, tm, w), lambda i, j: (i, j, 0))
    mod_map = (lambda i, j: (i, 0, 0)) if mod_per_batch else (lambda i, j: (0, 0, 0))
    return pl.pallas_call(
        _back_kernel,
        grid=(b, nt),
        in_specs=[tok(D_MODEL), tok(D_FOURIER), tok(D_RWKV),
                  pl.BlockSpec((None, N_MOD, D_MODEL), mod_map),
                  _const_spec((3, D_MODEL)), _const_spec((1, D_MODEL)),
                  _const_spec((D_MODEL, D_MODEL)),
                  _ffn_spec((D_MODEL, D_FF), 1), _ffn_spec((D_MODEL, D_FF), 1),
                  _ffn_spec((D_FF, D_MODEL), 1)],
        out_specs=tok(D_MODEL),
        out_shape=jax.ShapeDtypeStruct((b, s, D_MODEL), F32),
        compiler_params=_params(("arbitrary", "arbitrary")),
        name="back",
    )(x1, yf, yr, mod, norm_g, final_norm.reshape(1, D_MODEL), w_out, wg, wu, wd)


def _fold_kernel(w_ref, cs_ref, o_ref):
    o_ref[...] = _dot32(w_ref[...], cs_ref[...]).astype(o_ref.dtype)


def _fold_group_dft(w_in):
    q = np.arange(FOURIER_GROUP)
    ang = 2.0 * np.pi * ((q[:, None] * q[None, :]) % FOURIER_GROUP) / FOURIER_GROUP
    eye = np.eye(N_FOURIER_GROUPS)
    cs = jnp.asarray(np.concatenate([np.kron(eye, np.cos(ang)), np.kron(eye, np.sin(ang))], axis=1), F32)
    return pl.pallas_call(
        _fold_kernel,
        grid=(1,),
        in_specs=[pl.BlockSpec((D_MODEL, D_FOURIER), lambda i: (0, 0)),
                  pl.BlockSpec((D_FOURIER, D_FOLD), lambda i: (0, 0))],
        out_specs=pl.BlockSpec((D_MODEL, D_FOLD), lambda i: (0, 0)),
        out_shape=jax.ShapeDtypeStruct((D_MODEL, D_FOLD), BF16),
        compiler_params=_params(("arbitrary",)),
        name="fold_group_dft",
    )(w_in, cs)


def _stack_cos_sin(x):
    return jnp.concatenate([x[:, :D_FOURIER], x[:, D_FOURIER:]], axis=0).astype(BF16)


DENSE_DFT_SEQS = 8


def _fourier_dense_kernel(x_ref, p_ref, o_ref):
    for i in range(x_ref.shape[0]):
        o_ref[i] = _bdot(p_ref[...], _stack_cos_sin(x_ref[i]))


def _fourier_dense(xcs):
    b, s, _ = xcs.shape
    pos = np.arange(s)
    ang = 2.0 * np.pi * ((pos[:, None] * pos[None, :]) % s) / s
    table = np.concatenate([np.cos(ang), -np.sin(ang)], axis=1) / np.sqrt(s * FOURIER_GROUP)
    per_step = DENSE_DFT_SEQS if b % DENSE_DFT_SEQS == 0 else 1
    return pl.pallas_call(
        _fourier_dense_kernel,
        grid=(b // per_step,),
        in_specs=[pl.BlockSpec((per_step, s, D_FOLD), lambda i: (i, 0, 0)), _const_spec((s, 2 * s))],
        out_specs=pl.BlockSpec((per_step, s, D_FOURIER), lambda i: (i, 0, 0)),
        out_shape=jax.ShapeDtypeStruct((b, s, D_FOURIER), F32),
        compiler_params=_params(("arbitrary",)),
        name="fourier_dense",
    )(xcs, jnp.asarray(table, F32).astype(BF16))


FFT_ROWS = 16


def _fourier_stage1_kernel(x_ref, f_ref, z_ref):
    for j in range(FFT_ROWS):
        z = _bdot(f_ref[...], _stack_cos_sin(x_ref[:, j, :]))
        n = z.shape[0] // 2
        z_ref[:, j, :] = jnp.concatenate([z[:n], z[n:]], axis=1)


def _fourier_stage2_kernel(z_ref, g_ref, o_ref):
    for j in range(FFT_ROWS):
        o_ref[:, j, :] = _bdot(g_ref[j], _stack_cos_sin(z_ref[j]))


def _fourier_two_stage(xcs):
    b, s, _ = xcs.shape
    n = int(round(np.sqrt(s)))
    assert n * n == s and n % FFT_ROWS == 0
    idx = np.arange(n)
    ang = 2.0 * np.pi * ((idx[:, None] * idx[None, :]) % n) / n
    fc, fs = np.cos(ang), np.sin(ang)
    f2 = jnp.asarray(np.block([[fc, -fs], [fs, fc]]), F32).astype(BF16)
    z = pl.pallas_call(
        _fourier_stage1_kernel,
        grid=(b, n // FFT_ROWS),
        in_specs=[pl.BlockSpec((None, n, FFT_ROWS, D_FOLD), lambda i, j: (i, 0, j, 0)),
                  _const_spec((2 * n, 2 * n))],
        out_specs=pl.BlockSpec((None, n, FFT_ROWS, D_FOLD), lambda i, j: (i, 0, j, 0)),
        out_shape=jax.ShapeDtypeStruct((b, n, n, D_FOLD), F32),
        compiler_params=_params(("arbitrary", "arbitrary")),
        name="fourier_stage1",
    )(xcs.reshape(b, n, n, D_FOLD), f2)
    bb, aa, s1 = idx[:, None, None], idx[None, :, None], idx[None, None, :]
    ang2 = 2.0 * np.pi * ((s1 * (n * aa + bb)) % s) / s
    g2 = np.concatenate([np.cos(ang2), -np.sin(ang2)], axis=2) / np.sqrt(s * FOURIER_GROUP)
    out = pl.pallas_call(
        _fourier_stage2_kernel,
        grid=(b, n // FFT_ROWS),
        in_specs=[pl.BlockSpec((None, FFT_ROWS, n, D_FOLD), lambda i, j: (i, j, 0, 0)),
                  pl.BlockSpec((FFT_ROWS, n, 2 * n), lambda i, j: (j, 0, 0))],
        out_specs=pl.BlockSpec((None, n, FFT_ROWS, D_FOURIER), lambda i, j: (i, 0, j, 0)),
        out_shape=jax.ShapeDtypeStruct((b, n, n, D_FOURIER), F32),
        compiler_params=_params(("arbitrary", "arbitrary")),
        name="fourier_stage2",
    )(z, jnp.asarray(g2, F32).astype(BF16))
    return out.reshape(b, s, D_FOURIER)


def _shift_rows(z, up, down, mu):
    lane = lax.broadcasted_iota(jnp.int32, (1, z.shape[1]), 1) % 4
    return z + jnp.where(lane < 2, 0.0, mu) * (jnp.where(lane == 2, up, down) - z)


def _stack_masked(x, head0):
    return jnp.concatenate([jnp.where(head0, x, 0.0), jnp.where(head0, 0.0, x)], axis=0)


NT = (((1,), (1,)), ((), ()))
CUM_PIECES = 2
CHUNKS_PER_STEP = 4


def _split(x, pieces):
    out = []
    for i in range(pieces):
        hi = x.astype(BF16)
        out.append(hi)
        if i + 1 < pieces:
            x = x - hi.astype(F32)
    return out


def _bdot(a, b, dims=None):
    if dims is None:
        return jnp.dot(a, b, preferred_element_type=F32)
    return lax.dot_general(a, b, dims, preferred_element_type=F32)


HEADS_PER_SUM = 4


def _head_sums(x, ones_ref):
    w = HEADS_PER_SUM * HEAD_SIZE
    return jnp.concatenate([_bdot(x[:, i:i + w].astype(BF16), ones_ref[...]) for i in range(0, D_RWKV, w)], axis=1)


def _scan_kernel(*refs, grid, dirs, other, n_tiles, tm, seqs):
    final = other or len(dirs) == 2
    it = iter(refs)
    z_ref = next(it)
    zp_ref = next(it) if grid else None
    zn_ref = next(it) if grid else None
    h0_refs = {d: next(it) for d in dirs}
    if other:
        yo_ref, bo_ref = next(it), next(it)
    mu_ref, kk_ref, ka_ref, rk_ref, ones_ref = (next(it) for _ in range(5))
    lora = {d: tuple(next(it) for _ in range(4)) for d in dirs}
    if final:
        lnw_ref, lnb_ref, g2_ref = next(it), next(it), next(it)
    y_ref = next(it)
    b_ref = None if final else next(it)
    hout_refs = {d: next(it) for d in dirs}
    r_s, v_s, al_s, g_s, bonus_s = (next(it) for _ in range(5))
    per_dir = {d: tuple(next(it) for _ in range(7)) for d in dirs}

    step = pl.program_id(1)
    tile = (n_tiles - 1 - step) if dirs == (1,) else step

    @pl.when(step == 0)
    def _():
        for d in dirs:
            per_dir[d][4][...] = h0_refs[d][...]

    def neighbour_rows(lo, hi):
        parts = []
        if lo < 0:
            parts.append(jnp.where(tile == 0, 0.0, zp_ref[GRID_W + lo:, :]))
        parts.append(z_ref[max(lo, 0):min(hi, tm), :])
        if hi > tm:
            parts.append(jnp.where(tile == n_tiles - 1, 0.0, zn_ref[:hi - tm, :]))
        return parts[0] if len(parts) == 1 else jnp.concatenate(parts, axis=0)

    def prepare(rows):
        z = z_ref[rows, :]
        if grid:
            z = _shift_rows(z, neighbour_rows(rows.start - GRID_W, rows.stop - GRID_W),
                            neighbour_rows(rows.start + GRID_W, rows.stop + GRID_W), mu_ref[...])
        r = z[:, 0:D_RWKV]
        k = z[:, D_RWKV:2 * D_RWKV]
        v = z[:, 2 * D_RWKV:3 * D_RWKV]
        kk = k * kk_ref[...]
        kk = kk * lax.rsqrt(jnp.maximum(_head_sums(kk * kk, ones_ref), 1e-24))
        r_s[rows, :] = r
        v_s[rows, :] = v
        al_s[rows, :] = -kk
        bonus = bo_ref[rows, :] if other else None
        for d in dirs:
            w0_ref, a0_ref, w2_ref, a2_ref = lora[d]
            be_s, kd_s, lw_s = per_dir[d][:3]
            w_raw = w0_ref[...] + _bdot(jnp.tanh(z[:, OFF_WD:OFF_AD]).astype(BF16), w2_ref[...])
            a = _sigmoid(a0_ref[...] + _bdot(z[:, OFF_AD:OFF_GD].astype(BF16), a2_ref[...]))
            kd = k * (1.0 + (a - 1.0) * ka_ref[...])
            bonus_d = _head_sums(r * kd * rk_ref[...], ones_ref) * v
            bonus = bonus_d if bonus is None else bonus + bonus_d
            be_s[rows, :] = a * kk
            kd_s[rows, :] = kd
            lw_s[rows, :] = -DECAY_SCALE * _sigmoid(w_raw)
        if final:
            bonus_s[rows, :] = bonus
            g_s[rows, :] = _bdot(_sigmoid(z[:, OFF_GD:]).astype(BF16), g2_ref[...])
        else:
            b_ref[rows, :] = bonus

    def finish(rows):
        y_sum = yo_ref[rows, :] if other else None
        for d in dirs:
            y_d = per_dir[d][3][rows, :]
            y_sum = y_d if y_sum is None else y_sum + y_d
        if final:
            mean = _head_sums(y_sum, ones_ref) * (1.0 / HEAD_SIZE)
            cen = y_sum - mean
            var = _head_sums(cen * cen, ones_ref) * (1.0 / HEAD_SIZE)
            y_gn = cen * lax.rsqrt(var + GN_EPS) * lnw_ref[...] + lnb_ref[...]
            y_ref[rows, :] = ((y_gn + bonus_s[rows, :]) * g_s[rows, :]).astype(y_ref.dtype)
        else:
            y_ref[rows, :] = y_sum

    n_chunks = tm // CHUNK
    per_seq = n_chunks // seqs
    assert per_seq % CHUNKS_PER_STEP == 0, "a group of chunks must not straddle two sequences"
    ri = lax.broadcasted_iota(jnp.int32, (CHUNK, CHUNK), 0)
    ci = lax.broadcasted_iota(jnp.int32, (CHUNK, CHUNK), 1)
    rj = lax.broadcasted_iota(jnp.int32, (CHUNK, PAIR), 0)
    cj = lax.broadcasted_iota(jnp.int32, (CHUNK, PAIR), 1) % CHUNK
    eye = rj == cj
    tri = {0: (ci <= ri).astype(BF16), 1: (ci >= ri).astype(BF16)}
    strict = {0: cj < rj, 1: cj > rj}
    incl = {0: cj <= rj, 1: cj >= rj}
    last = {0: CHUNK - 1, 1: 0}
    head0 = lax.broadcasted_iota(jnp.int32, (1, PAIR), 1) < HEAD_SIZE
    blockdiag = lambda x: _stack_masked(x, head0)

    def head_transpose(x):
        t = blockdiag(x).T
        return t[:CHUNK] + t[CHUNK:]

    same_block = lambda n: (rj & -n) == (cj & -n)
    levels = [2 ** i for i in range(1, int(np.log2(CHUNK)))]
    base_mask = same_block(2)
    pair_masks = {n: same_block(2 * n) & ~same_block(n) for n in levels}

    def chunk_rows(c, d):
        seq, k = divmod(c, per_seq)
        off = (seq * per_seq + ((per_seq - 1 - k) if d == 1 else k)) * CHUNK
        return slice(off, off + CHUNK)

    def local_phase(chunks, d):
        be_s, kd_s, lw_s, _, _, qm_s, yn_s = per_dir[d]
        units = []
        for c in chunks:
            rows = chunk_rows(c, d)
            lw = lw_s[rows, :]
            cum = sum(_bdot(tri[d], piece) for piece in _split(lw, CUM_PIECES))
            tot = cum[last[d]:last[d] + 1, :]
            g_inv = jnp.exp(-cum)
            g_tot = jnp.exp(tot)
            g_end = g_tot * g_inv
            abar = al_s[rows, :] * jnp.exp(cum - lw)
            rbar = r_s[rows, :] * jnp.exp(cum)
            be = be_s[rows, :]
            kdc = kd_s[rows, :]
            btil, ktil, bhat, khat = be * g_inv, kdc * g_inv, be * g_end, kdc * g_end
            vc = v_s[rows, :]
            for p in range(N_PAIRS):
                lanes = slice(p * PAIR, (p + 1) * PAIR)
                cut = lambda x: x[:, lanes]
                units.append(dict(c=c, p=p, a=cut(abar).astype(BF16), r=cut(rbar), v=cut(vc).astype(BF16),
                                  bt=cut(btil).astype(BF16), kt=cut(ktil).astype(BF16),
                                  bh=cut(bhat), kh=cut(khat), g_tot=cut(g_tot)))
        for u in units:
            ar = jnp.concatenate([u["a"], u["r"].astype(BF16)], axis=0)
            u["gb"] = _bdot(ar, blockdiag(u["bt"]), NT)
            u["gk"] = _bdot(ar, blockdiag(u["kt"]), NT)
        for u in units:
            gb, gk = u.pop("gb"), u.pop("gk")
            u["l_ab"] = jnp.where(strict[d], gb[:CHUNK], 0.0)
            u["l_ak"] = jnp.where(strict[d], gk[:CHUNK], 0.0).astype(BF16)
            u["l_rbk"] = jnp.concatenate([jnp.where(incl[d], gb[CHUNK:], 0.0),
                                          jnp.where(incl[d], gk[CHUNK:], 0.0)], axis=1).astype(BF16)
            u["t"] = jnp.where(eye, 1.0, jnp.where(base_mask, u["l_ab"], 0.0))
        for n in levels:
            for u in units:
                e = jnp.where(pair_masks[n], u["l_ab"], 0.0).astype(BF16)
                u["et"] = _bdot(e, blockdiag(u["t"].astype(BF16)))
            for u in units:
                u["t"] = u["t"] + _bdot(u["t"].astype(BF16), blockdiag(u["et"].astype(BF16)))
        for u in units:
            u["x1"] = _bdot(u["l_ak"], blockdiag(u["v"]))
        for u in units:
            rhs = jnp.concatenate([blockdiag(u["a"]), blockdiag(u["x1"].astype(BF16))], axis=1)
            u["pu"] = _bdot(u["t"].astype(BF16), rhs)
        for u in units:
            pu = u["pu"].astype(BF16)
            v_bd = blockdiag(u["v"])
            rhs = jnp.concatenate(
                [jnp.concatenate([blockdiag(pu[:, :PAIR]), blockdiag(pu[:, PAIR:])], axis=1),
                 jnp.concatenate([jnp.zeros_like(v_bd), v_bd], axis=1)], axis=0)
            bkh_t = jnp.concatenate([head_transpose(u["bh"]), head_transpose(u["kh"])], axis=1)
            o = _bdot(jnp.concatenate([u["l_rbk"], bkh_t.astype(BF16)], axis=0), rhs)
            q = u["r"] + o[:CHUNK, :PAIR]
            m = jnp.where(eye, u["g_tot"], 0.0) + o[CHUNK:, :PAIR]
            qm_s[u["c"], u["p"]] = jnp.concatenate([q, m], axis=0).astype(BF16)
            yn_s[u["c"], u["p"]] = o[:, PAIR:]

    def state_phase(chunks, d):
        _, _, _, y_s, h_s, qm_s, yn_s = per_dir[d]
        for c in chunks:
            seq = c // per_seq
            for p in range(N_PAIRS):
                oh = _bdot(qm_s[c, p], blockdiag(h_s[seq, p].astype(BF16))) + yn_s[c, p]
                y_s[chunk_rows(c, d), p * PAIR:(p + 1) * PAIR] = oh[:CHUNK]
                h_s[seq, p] = oh[CHUNK:]

    spans = [list(range(g, g + CHUNKS_PER_STEP)) for g in range(0, n_chunks, CHUNKS_PER_STEP)]
    groups = [(chunks, d) for chunks in spans for d in dirs]

    def group_rows(chunks, d):
        spans = [chunk_rows(c, d) for c in chunks]
        return slice(min(r.start for r in spans), max(r.stop for r in spans))

    for i, (chunks, d) in enumerate(groups):
        if d == dirs[0]:
            prepare(group_rows(chunks, d))
        local_phase(chunks, d)
        if i > 0:
            prev_chunks, prev_d = groups[i - 1]
            state_phase(prev_chunks, prev_d)
            if prev_d == dirs[-1]:
                finish(group_rows(prev_chunks, prev_d))
    state_phase(*groups[-1])
    finish(group_rows(*groups[-1]))

    @pl.when(step == n_tiles - 1)
    def _():
        for d in dirs:
            hout_refs[d][...] = per_dir[d][4][...]


def _scan(z, h0, prm, dirs, grid, other=None):
    n_seq, seq_len, _ = z.shape
    tm = SCAN_TILE if seq_len >= SCAN_TILE else SHORT_SEQS_PER_TILE * seq_len
    seqs = max(tm // seq_len, 1)
    assert (seq_len % tm == 0) if seqs == 1 else (tm == seqs * seq_len and n_seq % seqs == 0 and not grid)
    b, s = n_seq // seqs, seq_len * seqs
    n_tiles = s // tm
    assert len(dirs) == 1 or n_tiles == 1, "both directions in one call need the whole sequence in a tile"
    final = other is not None or len(dirs) == 2
    fold = lambda a: a.reshape((b, seqs * a.shape[1]) + a.shape[2:])
    tile_of = (lambda j: n_tiles - 1 - j) if dirs == (1,) else (lambda j: j)
    tok = lambda w: pl.BlockSpec((None, tm, w), lambda i, j: (i, tile_of(j), 0))
    in_specs, args = [tok(D_Z)], [fold(z)]
    if grid:
        halo_per_tile, n_halo = tm // GRID_W, s // GRID_W
        in_specs += [pl.BlockSpec((None, GRID_W, D_Z),
                                  lambda i, j: (i, jnp.maximum(tile_of(j) * halo_per_tile - 1, 0), 0)),
                     pl.BlockSpec((None, GRID_W, D_Z),
                                  lambda i, j: (i, jnp.minimum((tile_of(j) + 1) * halo_per_tile, n_halo - 1), 0))]
        args += [args[0], args[0]]
    state_spec = pl.BlockSpec((None, seqs, N_PAIRS, HEAD_SIZE, PAIR), lambda i, j: (i, 0, 0, 0, 0))
    state_shape = jax.ShapeDtypeStruct((b, seqs, N_PAIRS, HEAD_SIZE, PAIR), F32)
    for d in dirs:
        in_specs.append(state_spec)
        args.append(h0[d].reshape(state_shape.shape))
    if other is not None:
        in_specs += [tok(D_RWKV), tok(D_RWKV)]
        args += [fold(a) for a in other]
    row = lambda x: x.reshape(1, -1)
    lora_pad = lambda w, d: jnp.zeros((2 * w.shape[1], D_RWKV), BF16).at[d * w.shape[1]:(d + 1) * w.shape[1]].set(
        w[d].astype(BF16))
    small = [row(prm["mu"]), row(prm["k_k"]), row(prm["k_a"]), row(prm["r_k"]),
             jnp.asarray(np.kron(np.eye(HEADS_PER_SUM), np.ones((HEAD_SIZE, HEAD_SIZE))), BF16)]
    for d in dirs:
        small += [row(prm["w0"][d]), row(prm["a0"][d]), lora_pad(prm["w2"], d), lora_pad(prm["a2"], d)]
    if final:
        small += [row(prm["ln_w"]), row(prm["ln_b"]), prm["g2"].astype(BF16)]
    in_specs += [_const_spec(x.shape) for x in small]
    args += small
    tokens = jax.ShapeDtypeStruct((b, s, D_RWKV), F32)
    out_specs, out_shape = [tok(D_RWKV)], [jax.ShapeDtypeStruct(tokens.shape, BF16) if final else tokens]
    if not final:
        out_specs.append(tok(D_RWKV))
        out_shape.append(tokens)
    out_specs += [state_spec] * len(dirs)
    out_shape += [state_shape] * len(dirs)
    n_chunks = tm // CHUNK
    rows = lambda: pltpu.VMEM((tm, D_RWKV), F32)
    scratch = [rows() for _ in range(5)]
    for d in dirs:
        scratch += [rows() for _ in range(4)] + [pltpu.VMEM((seqs, N_PAIRS, CHUNK, PAIR), F32),
                                                  pltpu.VMEM((n_chunks, N_PAIRS, 2 * CHUNK, PAIR), BF16),
                                                  pltpu.VMEM((n_chunks, N_PAIRS, 2 * CHUNK, PAIR), F32)]
    outs = pl.pallas_call(
        functools.partial(_scan_kernel, grid=grid, dirs=dirs, other=other is not None, n_tiles=n_tiles, tm=tm,
                          seqs=seqs),
        grid=(b, n_tiles),
        in_specs=in_specs,
        out_specs=out_specs,
        out_shape=out_shape,
        scratch_shapes=scratch,
        compiler_params=_params(("arbitrary", "arbitrary")),
        name="scan_" + "".join("fb"[d] for d in dirs),
    )(*args)
    unfold = lambda a: a.reshape((n_seq, seq_len) + a.shape[2:])
    states = {d: a.reshape((n_seq,) + a.shape[2:]) for d, a in zip(dirs, outs[-len(dirs):])}
    return tuple(unfold(a) for a in outs[:-len(dirs)]) + (states,)


def _pack_state(s):
    b = s.shape[0]
    h = jnp.swapaxes(s, -1, -2).reshape(b, N_PAIRS, 2, HEAD_SIZE, HEAD_SIZE)
    return jnp.transpose(h, (0, 1, 3, 2, 4)).reshape(b, N_PAIRS, HEAD_SIZE, PAIR)


def _unpack_state(hp):
    b = hp.shape[0]
    h = jnp.transpose(hp.reshape(b, N_PAIRS, HEAD_SIZE, 2, HEAD_SIZE), (0, 1, 3, 2, 4))
    return jnp.swapaxes(h.reshape(b, N_RWKV_HEADS, HEAD_SIZE, HEAD_SIZE), -1, -2)


def _mixer_heads(z, s0_f, s0_b, prm, grid):
    h0 = {0: _pack_state(s0_f), 1: _pack_state(s0_b)}
    if z.shape[1] < SCAN_TILE:
        y, h = _scan(z, h0, prm, (0, 1), grid)
    else:
        y_b, bonus_b, h_b = _scan(z, h0, prm, (1,), grid)
        y, h_f = _scan(z, h0, prm, (0,), grid, other=(y_b, bonus_b))
        h = {**h_f, **h_b}
    return y, _unpack_state(h[0]), _unpack_state(h[1])


def _path(x, mod, mod_per_batch, s0_f, s0_b, grid, w, prm):
    b, s, _ = x.shape
    rows = (b, s) if mod_per_batch else (1, b * s)
    as_rows = lambda a: a.reshape(rows + a.shape[2:])
    as_seqs = lambda a: a.reshape((b, s) + a.shape[2:])
    x1, u, z = _front(as_rows(x), mod, mod_per_batch, w["norm_g"], w["gate"], w["up"], w["down"], w["w_fold"],
                      w["w_z"], prm["mu"], TOKEN_TILE, GRID_W if grid else s, grid)
    y_four = _fourier_two_stage(as_seqs(u)) if grid else _fourier_dense(as_seqs(u))
    y_rwkv, s_f, s_b = _mixer_heads(as_seqs(z), s0_f, s0_b, prm, grid)
    y = _back(x1, as_rows(y_four), as_rows(y_rwkv), mod, mod_per_batch, w["norm_g"], w["final"], w["w_out"],
              w["gate"], w["up"], w["down"], TOKEN_TILE)
    return as_seqs(y), s_f, s_b


def kernel(x_prompt, x_sample, state_fwd, state_bwd, c, c_ctx, w_mod, b_mod, norm_g, ffn_w_gate, ffn_w_up,
           ffn_w_down, w_in, shift_mu, decay_w0, decay_w2, iclr_a0, iclr_a2, gate_g2, k_k, k_a, r_k,
           ln_x_w, ln_x_b, w_out, final_norm):
    depth = w_mod.shape[0]
    assert depth == 1, "the back kernel applies the final norm, so exactly one layer is supported"
    bp = x_prompt.shape[0]
    bs = x_sample.shape[0]
    xp, xs = x_prompt, x_sample
    new_f, new_b = [], []
    for l in range(depth):
        cvec = jnp.zeros((8, D_MODEL), F32).at[:bs].set(c).at[bs].set(c_ctx)
        mod = _modulation(cvec, w_mod[l], b_mod[l]).reshape(8, N_MOD, D_MODEL)
        w = {"norm_g": norm_g[l], "final": final_norm,
             "gate": ffn_w_gate[l].astype(BF16), "up": ffn_w_up[l].astype(BF16), "down": ffn_w_down[l].astype(BF16),
             "w_fold": _fold_group_dft(w_in[l]), "w_z": w_in[l, :, D_FOURIER:].astype(BF16),
             "w_out": w_out[l].astype(BF16)}
        prm = {"mu": shift_mu[l], "w0": decay_w0[l], "w2": decay_w2[l], "a0": iclr_a0[l], "a2": iclr_a2[l],
               "g2": gate_g2[l], "k_k": k_k[l], "k_a": k_a[l], "r_k": r_k[l], "ln_w": ln_x_w[l], "ln_b": ln_x_b[l]}
        zero_state = jnp.zeros((bp, N_RWKV_HEADS, HEAD_SIZE, HEAD_SIZE), F32)
        xp, s_f, s_b = _path(xp, mod[bs:bs + 1], False, zero_state, zero_state, False, w, prm)
        new_f.append(s_f)
        new_b.append(s_b)
        xs, _, _ = _path(xs, mod[:bs], True, state_fwd[:, l], state_bwd[:, l], True, w, prm)
    return xp, xs, jnp.stack(new_f, axis=1), jnp.stack(new_b, axis=1)
```
